```python
import math
import jax, jax.numpy as jnp
from jax import lax
import numpy as np

D_MODEL = 1024
BATCH = 2
SEQ = 8192
DEPTH = 1

CHUNK = 64
Q_BLOCK = 128
EPS = 1e-6

DIFF_HEADS = 4
DIFF_HEAD_DIM = 64
DIFF_V_DIM = 2 * DIFF_HEAD_DIM
DIFF_QK_WIDTH = DIFF_HEADS * 2 * DIFF_HEAD_DIM
DIFF_V_WIDTH = DIFF_HEADS * DIFF_V_DIM

GDN_HEADS = 4
GDN_K_DIM = 128
GDN_V_DIM = 128
GDN_K_WIDTH = GDN_HEADS * GDN_K_DIM
GDN_V_WIDTH = GDN_HEADS * GDN_V_DIM
CONV_K = 4
GDN_CONV_WIDTH = 2 * GDN_K_WIDTH + GDN_V_WIDTH

MOE_GROUPS = 4
MOE_EXPERTS_PER_GROUP = 8
MOE_EXPERTS = MOE_GROUPS * MOE_EXPERTS_PER_GROUP
MOE_TOP_K = 2
MOE_HIDDEN = 256

IN_SPLITS = (DIFF_QK_WIDTH, DIFF_QK_WIDTH, DIFF_V_WIDTH, GDN_K_WIDTH, GDN_K_WIDTH, GDN_V_WIDTH, GDN_HEADS, GDN_HEADS, GDN_V_WIDTH, D_MODEL, D_MODEL)
IN_WIDTH = 3 * 512 + 3 * 512 + 2 * GDN_HEADS + GDN_V_WIDTH + 2 * D_MODEL

kernel_name = 'hybrid_diffattn_gdn_hmoe_block'


def rms_norm(x, gain):
    xf = x.astype(jnp.float32)
    y = xf * lax.rsqrt(jnp.mean(xf * xf, axis=-1, keepdims=True) + EPS)
    return (y * gain.astype(jnp.float32)).astype(x.dtype)


def l2_normalize(t):
    return t * lax.rsqrt(jnp.sum(t * t, axis=-1, keepdims=True) + EPS)


def causal_depthwise_conv(x, w):
    s = x.shape[1]
    xp = jnp.pad(x, ((0, 0), (CONV_K - 1, 0), (0, 0)))
    out = xp[:, 0:s] * w[0]
    for j in range(1, CONV_K):
        out = out + xp[:, j:j + s] * w[j]
    return out


def diff_attention(q, k, v, lam, subln_gain, lambda_init):
    b, s = q.shape[0], q.shape[1]
    nb = s // Q_BLOCK
    scale = DIFF_HEAD_DIM ** -0.5
    kf = k.astype(jnp.float32)
    vf = v.astype(jnp.float32)
    qb = q.astype(jnp.float32).reshape(b, nb, Q_BLOCK, DIFF_HEADS, 2, DIFF_HEAD_DIM).transpose(1, 0, 2, 3, 4, 5)
    key_chunk = jnp.arange(s) // CHUNK

    def block(args):
        q_blk, bidx = args
        sc = jnp.einsum('bqhcd,bkhcd->bhcqk', q_blk, kf) * scale
        q_chunk = (bidx * Q_BLOCK + jnp.arange(Q_BLOCK)) // CHUNK
        mask = key_chunk[None, :] <= q_chunk[:, None]
        sc = jnp.where(mask, sc, -jnp.inf)
        p = jax.nn.softmax(sc, axis=-1)
        a = p[:, :, 0] - lam * p[:, :, 1]
        return jnp.einsum('bhqk,bkhv->bqhv', a, vf)

    o = lax.map(block, (qb, jnp.arange(nb)))
    o = o.transpose(1, 0, 2, 3, 4).reshape(b, s, DIFF_HEADS, DIFF_V_DIM)
    o = rms_norm(o, subln_gain) * (1.0 - lambda_init)
    return o.reshape(b, s, DIFF_V_WIDTH)


def gated_delta_net(q, k, v, beta, g, norm_gain, gate):
    b, s = q.shape[0], q.shape[1]
    nc = s // CHUNK
    q = l2_normalize(q) * (GDN_K_DIM ** -0.5)
    k = l2_normalize(k)

    def to_chunks(t):
        return t.reshape(b, nc, CHUNK, GDN_HEADS, -1).transpose(0, 3, 1, 2, 4)

    qc, kc, vc = to_chunks(q), to_chunks(k), to_chunks(v)
    bc = beta.reshape(b, nc, CHUNK, GDN_HEADS).transpose(0, 3, 1, 2)
    gc = g.reshape(b, nc, CHUNK, GDN_HEADS).transpose(0, 3, 1, 2)
    gcum = jnp.cumsum(gc, axis=-1)
    causal = jnp.tril(jnp.ones((CHUNK, CHUNK), dtype=bool))
    strict = jnp.tril(jnp.ones((CHUNK, CHUNK), dtype=bool), -1)
    diff = gcum[..., :, None] - gcum[..., None, :]
    decay_mat = jnp.exp(jnp.where(causal, diff, -jnp.inf))
    kb = kc * bc[..., None]
    vb = vc * bc[..., None]
    a_low = jnp.where(strict, jnp.einsum('bhncd,bhnjd->bhncj', kb, kc) * decay_mat, 0.0)
    eye = jnp.eye(CHUNK, dtype=jnp.float32)
    rhs = jnp.concatenate([vb, kb * jnp.exp(gcum)[..., None]], axis=-1)
    sol = lax.linalg.triangular_solve(eye + a_low, rhs, left_side=True, lower=True, unit_diagonal=True)
    u, w = sol[..., :GDN_V_DIM], sol[..., GDN_V_DIM:]
    intra = jnp.einsum('bhncd,bhnjd->bhncj', qc, kc) * decay_mat
    q_dec = qc * jnp.exp(gcum)[..., None]
    k_dec = kc * jnp.exp(gcum[..., -1:] - gcum)[..., None]
    g_last = jnp.exp(gcum[..., -1])
    xs = tuple(jnp.moveaxis(t, 2, 0) for t in (q_dec, k_dec, u, w, intra, g_last))

    def step(state, inp):
        q_d, k_d, u_c, w_c, a_c, gl = inp
        v_new = u_c - jnp.einsum('bhck,bhkv->bhcv', w_c, state)
        o_c = jnp.einsum('bhck,bhkv->bhcv', q_d, state) + jnp.einsum('bhcj,bhjv->bhcv', a_c, v_new)
        state = state * gl[..., None, None] + jnp.einsum('bhck,bhcv->bhkv', k_d, v_new)
        return state, o_c

    s0 = jnp.zeros((b, GDN_HEADS, GDN_K_DIM, GDN_V_DIM), jnp.float32)
    _, o = lax.scan(step, s0, xs)
    o = o.transpose(1, 0, 3, 2, 4).reshape(b, s, GDN_HEADS, GDN_V_DIM)
    o = rms_norm(o, norm_gain) * jax.nn.silu(gate.astype(jnp.float32)).reshape(b, s, GDN_HEADS, GDN_V_DIM)
    return o.reshape(b, s, GDN_V_WIDTH)


def hier_moe(h, w_group, b_group, w_expert, b_expert, w_gate, w_up, w_down):
    b, s, d = h.shape
    t = h.reshape(b * s, d)
    tf = t.astype(jnp.float32)
    grp_logits = tf @ w_group.astype(jnp.float32) + b_group.astype(jnp.float32)
    grp_prob = jax.nn.softmax(grp_logits, axis=-1)
    grp_idx = jnp.argmax(grp_logits, axis=-1)
    grp_w = jnp.max(grp_prob, axis=-1, keepdims=True)
    exp_logits = (tf @ w_expert.astype(jnp.float32) + b_expert.astype(jnp.float32)).reshape(-1, MOE_GROUPS, MOE_EXPERTS_PER_GROUP)
    in_grp = jnp.einsum('tg,tge->te', jax.nn.one_hot(grp_idx, MOE_GROUPS, dtype=jnp.float32), exp_logits)
    top_val, top_idx = lax.top_k(in_grp, MOE_TOP_K)
    top_w = jax.nn.softmax(top_val, axis=-1) * grp_w
    expert_id = grp_idx[:, None] * MOE_EXPERTS_PER_GROUP + top_idx
    combine = jnp.einsum('tk,tke->te', top_w, jax.nn.one_hot(expert_id, MOE_EXPERTS, dtype=jnp.float32))
    hg = jnp.einsum('td,edf->tef', t, w_gate)
    hu = jnp.einsum('td,edf->tef', t, w_up)
    act = jax.nn.silu(hg) * hu * combine[..., None].astype(t.dtype)
    y = jnp.einsum('tef,efd->td', act, w_down)
    return y.reshape(b, s, d)


def setup_inputs(seed: int = 0) -> dict:
    key = jax.random.key(seed)
    ks = jax.random.split(key, 24)
    nrm = jax.random.normal
    f32 = jnp.float32
    dt = jnp.exp(jax.random.uniform(ks[9], (DEPTH, GDN_HEADS), f32, math.log(1e-3), math.log(1e-1)))
    return {
        'x': nrm(ks[0], (BATCH, SEQ, D_MODEL), f32),
        'norm_mix_gain': 1.0 + 0.02 * nrm(ks[1], (DEPTH, D_MODEL), f32),
        'w_in': nrm(ks[2], (DEPTH, D_MODEL, IN_WIDTH), f32) * D_MODEL ** -0.5,
        'diff_lambda_q1': 0.1 * nrm(ks[3], (DEPTH, DIFF_HEAD_DIM), f32),
        'diff_lambda_k1': 0.1 * nrm(ks[4], (DEPTH, DIFF_HEAD_DIM), f32),
        'diff_lambda_q2': 0.1 * nrm(ks[5], (DEPTH, DIFF_HEAD_DIM), f32),
        'diff_lambda_k2': 0.1 * nrm(ks[6], (DEPTH, DIFF_HEAD_DIM), f32),
        'diff_subln_gain': 1.0 + 0.02 * nrm(ks[7], (DEPTH, DIFF_V_DIM), f32),
        'gdn_conv_w': nrm(ks[8], (DEPTH, CONV_K, GDN_CONV_WIDTH), f32) * CONV_K ** -0.5,
        'gdn_a_log': jnp.log(jax.random.uniform(ks[10], (DEPTH, GDN_HEADS), f32, 1.0, 16.0)),
        'gdn_dt_bias': dt + jnp.log(-jnp.expm1(-dt)),
        'gdn_norm_gain': 1.0 + 0.02 * nrm(ks[11], (DEPTH, GDN_V_DIM), f32),
        'w_branch_attn': nrm(ks[12], (DEPTH, DIFF_V_WIDTH, D_MODEL), f32) * DIFF_V_WIDTH ** -0.5,
        'w_branch_gdn': nrm(ks[13], (DEPTH, GDN_V_WIDTH, D_MODEL), f32) * GDN_V_WIDTH ** -0.5,
        'w_out': nrm(ks[14], (DEPTH, D_MODEL, D_MODEL), f32) * D_MODEL ** -0.5,
        'norm_ffn_gain': 1.0 + 0.02 * nrm(ks[15], (DEPTH, D_MODEL), f32),
        'moe_w_group': nrm(ks[16], (DEPTH, D_MODEL, MOE_GROUPS), f32) * D_MODEL ** -0.5,
        'moe_b_group': 0.01 * nrm(ks[17], (DEPTH, MOE_GROUPS), f32),
        'moe_w_expert': nrm(ks[18], (DEPTH, D_MODEL, MOE_EXPERTS), f32) * D_MODEL ** -0.5,
        'moe_b_expert': 0.01 * nrm(ks[19], (DEPTH, MOE_EXPERTS), f32),
        'moe_w_gate': nrm(ks[20], (DEPTH, MOE_EXPERTS, D_MODEL, MOE_HIDDEN), f32) * D_MODEL ** -0.5,
        'moe_w_up': nrm(ks[21], (DEPTH, MOE_EXPERTS, D_MODEL, MOE_HIDDEN), f32) * D_MODEL ** -0.5,
        'moe_w_down': nrm(ks[22], (DEPTH, MOE_EXPERTS, MOE_HIDDEN, D_MODEL), f32) * MOE_HIDDEN ** -0.5,
        'norm_final_gain': 1.0 + 0.02 * nrm(ks[23], (D_MODEL,), f32),
    }


def reference(x, norm_mix_gain, w_in, diff_lambda_q1, diff_lambda_k1, diff_lambda_q2, diff_lambda_k2, diff_subln_gain, gdn_conv_w, gdn_a_log, gdn_dt_bias, gdn_norm_gain, w_branch_attn, w_branch_gdn, w_out, norm_ffn_gain, moe_w_group, moe_b_group, moe_w_expert, moe_b_expert, moe_w_gate, moe_w_up, moe_w_down, norm_final_gain):
    b, s, _ = x.shape
    points = [int(p) for p in np.cumsum(IN_SPLITS)[:-1]]
    for layer in range(DEPTH):
        lambda_init = 0.8 - 0.6 * math.exp(-0.3 * layer)
        h = rms_norm(x, norm_mix_gain[layer])
        proj = h @ w_in[layer]
        (dq, dk, dv, gq, gk, gv, g_beta, g_dec, g_gate, m_gate_a, m_gate_b) = jnp.split(proj, points, axis=-1)
        lam = (jnp.exp(jnp.sum(diff_lambda_q1[layer].astype(jnp.float32) * diff_lambda_k1[layer].astype(jnp.float32)))
               - jnp.exp(jnp.sum(diff_lambda_q2[layer].astype(jnp.float32) * diff_lambda_k2[layer].astype(jnp.float32)))
               + lambda_init)
        y_a = diff_attention(dq.reshape(b, s, DIFF_HEADS, 2, DIFF_HEAD_DIM),
                             dk.reshape(b, s, DIFF_HEADS, 2, DIFF_HEAD_DIM),
                             dv.reshape(b, s, DIFF_HEADS, DIFF_V_DIM),
                             lam, diff_subln_gain[layer], lambda_init)
        qkv = jax.nn.silu(causal_depthwise_conv(jnp.concatenate([gq, gk, gv], axis=-1).astype(jnp.float32), gdn_conv_w[layer].astype(jnp.float32)))
        cq, ck, cv = jnp.split(qkv, [GDN_K_WIDTH, 2 * GDN_K_WIDTH], axis=-1)
        beta = jax.nn.sigmoid(g_beta.astype(jnp.float32))
        g_log = -jnp.exp(gdn_a_log[layer].astype(jnp.float32)) * jax.nn.softplus(g_dec.astype(jnp.float32) + gdn_dt_bias[layer].astype(jnp.float32))
        y_b = gated_delta_net(cq.reshape(b, s, GDN_HEADS, GDN_K_DIM), ck.reshape(b, s, GDN_HEADS, GDN_K_DIM),
                              cv.reshape(b, s, GDN_HEADS, GDN_V_DIM), beta, g_log, gdn_norm_gain[layer], g_gate)
        merged = (jax.nn.sigmoid(m_gate_a) * (y_a.astype(x.dtype) @ w_branch_attn[layer])
                  + jax.nn.sigmoid(m_gate_b) * (y_b.astype(x.dtype) @ w_branch_gdn[layer]))
        x = x + merged @ w_out[layer]
        h2 = rms_norm(x, norm_ffn_gain[layer])
        x = x + hier_moe(h2, moe_w_group[layer], moe_b_group[layer], moe_w_expert[layer], moe_b_expert[layer],
                         moe_w_gate[layer], moe_w_up[layer], moe_w_down[layer])
    return rms_norm(x, norm_final_gain)
```

```python
import functools
import math

import jax
import jax.numpy as jnp
from jax import lax
from jax.experimental import pallas as pl
from jax.experimental.pallas import tpu as pltpu

F32 = jnp.float32
BF16 = jnp.bfloat16

D_MODEL = 1024
CHUNK = 64
EPS = 1e-6

DIFF_HEADS = 4
DIFF_HEAD_DIM = 64
DIFF_V_DIM = 2 * DIFF_HEAD_DIM
DIFF_WIDTH = DIFF_HEADS * DIFF_V_DIM

GDN_HEADS = 4
GDN_DIM = 128
GDN_WIDTH = GDN_HEADS * GDN_DIM
CONV_K = 4

MOE_GROUPS = 4
MOE_EXPERTS_PER_GROUP = 8
MOE_EXPERTS = MOE_GROUPS * MOE_EXPERTS_PER_GROUP
MOE_HIDDEN = 256

LANES = 128
SUBLANES = 8
VMEM_LIMIT = 56 * 1024 * 1024

LAMBDA_INIT = 0.8 - 0.6 * math.exp(-0.3 * 0)

REST_QKV = 0
REST_GATE = 3 * GDN_WIDTH
REST_MA = REST_GATE + GDN_WIDTH
REST_MB = REST_MA + D_MODEL
REST_SMALL = REST_MB + D_MODEL
REST_WIDTH = REST_SMALL + LANES

NEG_INF = float("-inf")


def _params(*sem):
    return pltpu.CompilerParams(dimension_semantics=sem, vmem_limit_bytes=VMEM_LIMIT)


def _inproj_kernel(x_ref, g_ref, w_ref, o_ref, h_scr):
    @pl.when(pl.program_id(1) == 0)
    def _():
        x = x_ref[...]
        ms = jnp.mean(x * x, axis=-1, keepdims=True)
        h_scr[...] = (x * lax.rsqrt(ms + EPS) * g_ref[...]).astype(BF16)

    o_ref[...] = jnp.dot(h_scr[...], w_ref[...], preferred_element_type=F32).astype(o_ref.dtype)


def _inproj(x2, gain, w, out_dtype, tm, tn):
    t, d = x2.shape
    n = w.shape[1]
    return pl.pallas_call(
        _inproj_kernel,
        out_shape=jax.ShapeDtypeStruct((t, n), out_dtype),
        grid=(t // tm, n // tn),
        in_specs=[
            pl.BlockSpec((tm, d), lambda i, j: (i, 0)),
            pl.BlockSpec((1, d), lambda i, j: (0, 0)),
            pl.BlockSpec((d, tn), lambda i, j: (0, j)),
        ],
        out_specs=pl.BlockSpec((tm, tn), lambda i, j: (i, j)),
        scratch_shapes=[pltpu.VMEM((tm, d), BF16)],
        compiler_params=_params("parallel", "arbitrary"),
        name="inproj",
    )(x2, gain, w)


def _attn_kernel(lam_ref, q_ref, k_ref, v_ref, gain_ref, o_ref, qs_scr, m_scr, l_scr, acc_scr, *, tq):
    i = pl.program_id(2)
    j = pl.program_id(3)

    @pl.when(j == 0)
    def _():
        q = q_ref[0].astype(F32) * (DIFF_HEAD_DIM ** -0.5)
        lane = lax.broadcasted_iota(jnp.int32, q.shape, 1)
        qs_scr[0:tq, :] = jnp.where(lane < DIFF_HEAD_DIM, q, 0.0).astype(BF16)
        qs_scr[tq:2 * tq, :] = jnp.where(lane >= DIFF_HEAD_DIM, q, 0.0).astype(BF16)
        m_scr[...] = jnp.full(m_scr.shape, NEG_INF, F32)
        l_scr[...] = jnp.zeros(l_scr.shape, F32)
        acc_scr[...] = jnp.zeros(acc_scr.shape, F32)

    @pl.when(j <= i)
    def _():
        s = lax.dot_general(qs_scr[...], k_ref[0], (((1,), (1,)), ((), ())), preferred_element_type=F32)
        row = lax.broadcasted_iota(jnp.int32, s.shape, 0)
        col = lax.broadcasted_iota(jnp.int32, s.shape, 1)
        q_chunk = (i * tq + (row & (tq - 1))) // CHUNK
        k_chunk = (j * tq + col) // CHUNK
        s = jnp.where(k_chunk <= q_chunk, s, NEG_INF)
        m_prev = m_scr[...]
        m_new = jnp.maximum(m_prev, jnp.max(s, axis=-1, keepdims=True))
        alpha = jnp.exp(m_prev - m_new)
        p = jnp.exp(s - m_new)
        l_scr[...] = alpha * l_scr[...] + jnp.sum(p, axis=-1, keepdims=True)
        acc_scr[...] = alpha * acc_scr[...] + jnp.dot(p.astype(BF16), v_ref[0], preferred_element_type=F32)
        m_scr[...] = m_new

    @pl.when(j == i)
    def _():
        lp = lam_ref[...]
        lam = (jnp.exp(jnp.sum(lp[0:1] * lp[1:2], axis=-1, keepdims=True))
               - jnp.exp(jnp.sum(lp[2:3] * lp[3:4], axis=-1, keepdims=True)) + LAMBDA_INIT)
        o0 = acc_scr[0:tq, :] / l_scr[0:tq, :]
        o1 = acc_scr[tq:2 * tq, :] / l_scr[tq:2 * tq, :]
        o = o0 - lam * o1
        ms = jnp.mean(o * o, axis=-1, keepdims=True)
        y = (o * lax.rsqrt(ms + EPS) * gain_ref[...]) * (1.0 - LAMBDA_INIT)
        o_ref[0] = y.astype(o_ref.dtype)


def _diff_attention(lam_params, qkv, subln_gain, tq):
    b, s, _ = qkv.shape
    nq = s // tq
    assert tq & (tq - 1) == 0 and tq % CHUNK == 0
    kv_map = lambda off: (lambda bi, h, i, j: (bi, jnp.minimum(j, i), off + h))
    return pl.pallas_call(
        functools.partial(_attn_kernel, tq=tq),
        out_shape=jax.ShapeDtypeStruct((b, s, DIFF_WIDTH), BF16),
        grid=(b, DIFF_HEADS, nq, nq),
        in_specs=[
            pl.BlockSpec((4, DIFF_HEAD_DIM), lambda bi, h, i, j: (0, 0)),
            pl.BlockSpec((1, tq, DIFF_V_DIM), lambda bi, h, i, j: (bi, i, h)),
            pl.BlockSpec((1, tq, DIFF_V_DIM), kv_map(DIFF_HEADS)),
            pl.BlockSpec((1, tq, DIFF_V_DIM), kv_map(2 * DIFF_HEADS)),
            pl.BlockSpec((1, DIFF_V_DIM), lambda bi, h, i, j: (0, 0)),
        ],
        out_specs=pl.BlockSpec((1, tq, DIFF_V_DIM), lambda bi, h, i, j: (bi, i, h)),
        scratch_shapes=[
            pltpu.VMEM((2 * tq, DIFF_V_DIM), BF16),
            pltpu.VMEM((2 * tq, 1), F32),
            pltpu.VMEM((2 * tq, 1), F32),
            pltpu.VMEM((2 * tq, DIFF_V_DIM), F32),
        ],
        compiler_params=_params("parallel", "parallel", "parallel", "arbitrary"),
        name="diff_attn",
    )(lam_params, qkv, qkv, qkv, subln_gain)


def _silu(x):
    return x * (1.0 / (1.0 + jnp.exp(-x)))


def _sigmoid(x):
    return 1.0 / (1.0 + jnp.exp(-x))


def _softplus(x):
    return jnp.maximum(x, 0.0) + jnp.log(1.0 + jnp.exp(-jnp.abs(x)))


def _dot(a, b):
    return jnp.dot(a.astype(BF16), b.astype(BF16), preferred_element_type=F32)


def _dot_nt(a, b):
    return lax.dot_general(a.astype(BF16), b.astype(BF16), (((1,), (1,)), ((), ())), preferred_element_type=F32)


def _gdn_kernel(prev_ref, qkv_ref, gate_ref, small_ref, convw_ref, alog_ref, dtb_ref, ngain_ref, o_ref,
                xp_scr, q_scr, k_scr, v_scr, beta_scr, g_scr, state_scr, *, tb):
    i = pl.program_id(1)

    @pl.when(i == 0)
    def _():
        state_scr[...] = jnp.zeros(state_scr.shape, F32)

    xp_scr[0:SUBLANES, :] = jnp.where(i == 0, 0.0, prev_ref[0])
    xp_scr[SUBLANES:SUBLANES + tb, :] = qkv_ref[0]
    for sec in range(3 * GDN_HEADS):
        cols = slice(sec * LANES, (sec + 1) * LANES)
        acc = None
        for jj in range(CONV_K):
            start = SUBLANES - (CONV_K - 1) + jj
            term = xp_scr[start:start + tb, cols] * convw_ref[jj:jj + 1, cols]
            acc = term if acc is None else acc + term
        y = _silu(acc)
        which, head = divmod(sec, GDN_HEADS)
        hc = slice(head * LANES, (head + 1) * LANES)
        if which == 0:
            q_scr[:, hc] = y * lax.rsqrt(jnp.sum(y * y, axis=-1, keepdims=True) + EPS) * (GDN_DIM ** -0.5)
        elif which == 1:
            k_scr[:, hc] = y * lax.rsqrt(jnp.sum(y * y, axis=-1, keepdims=True) + EPS)
        else:
            v_scr[:, hc] = y

    sm = small_ref[0]
    beta_scr[...] = _sigmoid(sm)
    g_scr[...] = -jnp.exp(alog_ref[...]) * _softplus(sm + dtb_ref[...])

    r_i = lax.broadcasted_iota(jnp.int32, (CHUNK, CHUNK), 0)
    c_i = lax.broadcasted_iota(jnp.int32, (CHUNK, CHUNK), 1)
    causal = c_i <= r_i
    strict = c_i < r_i
    tril = causal.astype(F32)
    eye = (c_i == r_i).astype(F32)

    def chunk_body(c, carry):
        rows = pl.ds(pl.multiple_of(c * CHUNK, CHUNK), CHUNK)
        beta_all = beta_scr[rows, :]
        g_all = g_scr[rows, :]
        for h in range(GDN_HEADS):
            hc = slice(h * LANES, (h + 1) * LANES)
            q = q_scr[rows, hc]
            k = k_scr[rows, hc]
            v = v_scr[rows, hc]
            beta = beta_all[:, h:h + 1]
            gb = jnp.broadcast_to(g_all[:, GDN_HEADS + h:GDN_HEADS + h + 1], (CHUNK, LANES))
            gc = jnp.dot(tril, gb, preferred_element_type=F32, precision=lax.Precision.HIGHEST)
            gc_row = jnp.transpose(gc)[0:CHUNK, :]
            decay = jnp.exp(jnp.where(causal, gc[:, 0:CHUNK] - gc_row, NEG_INF))
            eg = jnp.exp(gc)
            g_last = gc[CHUNK - 1:CHUNK, :]
            kb = k * beta
            vb = v * beta
            a = jnp.where(strict, _dot_nt(kb, k) * decay, 0.0)
            tinv = eye - a
            pw = a
            for _ in range(5):
                pw = _dot(pw, pw)
                tinv = tinv + _dot(tinv, pw)
            u = _dot(tinv, vb)
            w = _dot(tinv, kb * eg)
            intra = _dot_nt(q, k) * decay
            q_dec = q * eg
            k_dec = k * jnp.exp(g_last - gc)
            state = state_scr[h]
            v_new = u - _dot(w, state)
            o = _dot(q_dec, state) + _dot(intra, v_new)
            state_scr[h] = state * jnp.exp(g_last) + _dot(jnp.transpose(k_dec), v_new)
            ms = jnp.mean(o * o, axis=-1, keepdims=True)
            y = o * lax.rsqrt(ms + EPS) * ngain_ref[...] * _silu(gate_ref[0, rows, hc])
            o_ref[0, rows, hc] = y.astype(o_ref.dtype)
        return carry

    lax.fori_loop(0, tb // CHUNK, chunk_body, 0)


def _gdn(rest, conv_w, alog_pad, dtb_pad, norm_gain, tb):
    b, s, _ = rest.shape
    nb = s // tb
    w3 = 3 * GDN_WIDTH
    return pl.pallas_call(
        functools.partial(_gdn_kernel, tb=tb),
        out_shape=jax.ShapeDtypeStruct((b, s, GDN_WIDTH), BF16),
        grid=(b, nb),
        in_specs=[
            pl.BlockSpec((1, SUBLANES, w3), lambda bi, i: (bi, jnp.maximum(i * (tb // SUBLANES) - 1, 0), 0)),
            pl.BlockSpec((1, tb, w3), lambda bi, i: (bi, i, 0)),
            pl.BlockSpec((1, tb, GDN_WIDTH), lambda bi, i: (bi, i, REST_GATE // GDN_WIDTH)),
            pl.BlockSpec((1, tb, LANES), lambda bi, i: (bi, i, REST_SMALL // LANES)),
            pl.BlockSpec((CONV_K, w3), lambda bi, i: (0, 0)),
            pl.BlockSpec((1, LANES), lambda bi, i: (0, 0)),
            pl.BlockSpec((1, LANES), lambda bi, i: (0, 0)),
            pl.BlockSpec((1, GDN_DIM), lambda bi, i: (0, 0)),
        ],
        out_specs=pl.BlockSpec((1, tb, GDN_WIDTH), lambda bi, i: (bi, i, 0)),
        scratch_shapes=[
            pltpu.VMEM((tb + SUBLANES, w3), F32),
            pltpu.VMEM((tb, GDN_WIDTH), F32),
            pltpu.VMEM((tb, GDN_WIDTH), F32),
            pltpu.VMEM((tb, GDN_WIDTH), F32),
            pltpu.VMEM((tb, LANES), F32),
            pltpu.VMEM((tb, LANES), F32),
            pltpu.VMEM((GDN_HEADS, GDN_DIM, GDN_DIM), F32),
        ],
        compiler_params=_params("parallel", "arbitrary"),
        name="gdn",
    )(rest, rest, rest, rest, conv_w, alog_pad, dtb_pad, norm_gain)


def _merge_kernel(x_ref, ya_ref, yb_ref, ma_ref, mb_ref, wa_ref, wb_ref, wo_ref, g2_ref, wr_ref, br_ref,
                  x1_ref, h2_ref, comb_ref):
    pa = jnp.dot(ya_ref[...], wa_ref[...], preferred_element_type=F32)
    pb = jnp.dot(yb_ref[...], wb_ref[...], preferred_element_type=F32)
    merged = _sigmoid(ma_ref[...]) * pa + _sigmoid(mb_ref[...]) * pb
    x1 = x_ref[...] + jnp.dot(merged.astype(BF16), wo_ref[...], preferred_element_type=F32)
    x1_ref[...] = x1
    ms = jnp.mean(x1 * x1, axis=-1, keepdims=True)
    h2 = x1 * lax.rsqrt(ms + EPS) * g2_ref[...]
    h2_ref[...] = h2.astype(BF16)

    logits = jnp.dot(h2, wr_ref[...], preferred_element_type=F32, precision=lax.Precision.HIGHEST) + br_ref[...]
    lane = lax.broadcasted_iota(jnp.int32, logits.shape, 1)
    big = jnp.int32(4 * LANES)
    gl = jnp.where(lane < MOE_GROUPS, logits, NEG_INF)
    gmax = jnp.max(gl, axis=-1, keepdims=True)
    gidx = jnp.min(jnp.where(gl == gmax, lane, big), axis=-1, keepdims=True)
    grp_w = 1.0 / jnp.sum(jnp.exp(gl - gmax), axis=-1, keepdims=True)
    lo = MOE_GROUPS + gidx * MOE_EXPERTS_PER_GROUP
    el = jnp.where((lane >= lo) & (lane < lo + MOE_EXPERTS_PER_GROUP), logits, NEG_INF)
    v1 = jnp.max(el, axis=-1, keepdims=True)
    i1 = jnp.min(jnp.where(el == v1, lane, big), axis=-1, keepdims=True)
    el2 = jnp.where(lane == i1, NEG_INF, el)
    v2 = jnp.max(el2, axis=-1, keepdims=True)
    i2 = jnp.min(jnp.where(el2 == v2, lane, big), axis=-1, keepdims=True)
    e2 = jnp.exp(v2 - v1)
    w1 = grp_w / (1.0 + e2)
    w2 = w1 * e2
    comb_ref[...] = jnp.where(lane == i1, w1, jnp.where(lane == i2, w2, 0.0))


def _merge(x2, ya, yb, rest, wa, wb, wo, g2, wr, br, tm):
    t, d = x2.shape
    row = lambda i: (i, 0)
    const = lambda i: (0, 0)
    return pl.pallas_call(
        _merge_kernel,
        out_shape=(
            jax.ShapeDtypeStruct((t, d), F32),
            jax.ShapeDtypeStruct((t, d), BF16),
            jax.ShapeDtypeStruct((t, LANES), F32),
        ),
        grid=(t // tm,),
        in_specs=[
            pl.BlockSpec((tm, d), row),
            pl.BlockSpec((tm, DIFF_WIDTH), row),
            pl.BlockSpec((tm, GDN_WIDTH), row),
            pl.BlockSpec((tm, d), lambda i: (i, REST_MA // D_MODEL)),
            pl.BlockSpec((tm, d), lambda i: (i, REST_MB // D_MODEL)),
            pl.BlockSpec((DIFF_WIDTH, d), const),
            pl.BlockSpec((GDN_WIDTH, d), const),
            pl.BlockSpec((d, d), const),
            pl.BlockSpec((1, d), const),
            pl.BlockSpec((d, LANES), const),
            pl.BlockSpec((1, LANES), const),
        ],
        out_specs=(pl.BlockSpec((tm, d), row), pl.BlockSpec((tm, d), row), pl.BlockSpec((tm, LANES), row)),
        compiler_params=_params("parallel"),
        name="merge_router",
    )(x2, ya, yb, rest, rest, wa, wb, wo, g2, wr, br)


def _moe_kernel(h2_ref, comb_ref, x1_ref, wg_ref, wu_ref, wd_ref, gf_ref, o_ref, acc_scr):
    e = pl.program_id(1)

    @pl.when(e == 0)
    def _():
        acc_scr[...] = jnp.zeros(acc_scr.shape, F32)

    h = h2_ref[...]
    hg = jnp.dot(h, wg_ref[0], preferred_element_type=F32)
    hu = jnp.dot(h, wu_ref[0], preferred_element_type=F32)
    comb = comb_ref[...]
    lane = lax.broadcasted_iota(jnp.int32, comb.shape, 1)
    cw = jnp.sum(jnp.where(lane == e + MOE_GROUPS, comb, 0.0), axis=-1, keepdims=True)
    act = _silu(hg) * hu * cw
    acc_scr[...] += jnp.dot(act.astype(BF16), wd_ref[0], preferred_element_type=F32)

    @pl.when(e == MOE_EXPERTS - 1)
    def _():
        x2 = x1_ref[...] + acc_scr[...]
        ms = jnp.mean(x2 * x2, axis=-1, keepdims=True)
        o_ref[...] = x2 * lax.rsqrt(ms + EPS) * gf_ref[...]


def _moe(h2, comb, x1, wg, wu, wd, gf, tm):
    t, d = x1.shape
    return pl.pallas_call(
        _moe_kernel,
        out_shape=jax.ShapeDtypeStruct((t, d), F32),
        grid=(t // tm, MOE_EXPERTS),
        in_specs=[
            pl.BlockSpec((tm, d), lambda i, e: (i, 0)),
            pl.BlockSpec((tm, LANES), lambda i, e: (i, 0)),
            pl.BlockSpec((tm, d), lambda i, e: (i, 0)),
            pl.BlockSpec((1, d, MOE_HIDDEN), lambda i, e: (e, 0, 0)),
            pl.BlockSpec((1, d, MOE_HIDDEN), lambda i, e: (e, 0, 0)),
            pl.BlockSpec((1, MOE_HIDDEN, d), lambda i, e: (e, 0, 0)),
            pl.BlockSpec((1, d), lambda i, e: (0, 0)),
        ],
        out_specs=pl.BlockSpec((tm, d), lambda i, e: (i, 0)),
        scratch_shapes=[pltpu.VMEM((tm, d), F32)],
        compiler_params=_params("parallel", "arbitrary"),
        name="moe",
    )(h2, comb, x1, wg, wu, wd, gf)


def _pad_lanes(v, offset):
    return jnp.zeros((1, LANES), F32).at[0, offset:offset + v.shape[0]].set(v.astype(F32))


def kernel(x, norm_mix_gain, w_in, diff_lambda_q1, diff_lambda_k1, diff_lambda_q2, diff_lambda_k2, diff_subln_gain, gdn_conv_w, gdn_a_log, gdn_dt_bias, gdn_norm_gain, w_branch_attn, w_branch_gdn, w_out, norm_ffn_gain, moe_w_group, moe_b_group, moe_w_expert, moe_b_expert, moe_w_gate, moe_w_up, moe_w_down, norm_final_gain):
    b, s, d = x.shape
    t = b * s
    x2 = x.reshape(t, d)

    w = w_in[0]
    n_attn = 3 * DIFF_WIDTH
    small_lo = n_attn + 3 * GDN_WIDTH
    small_hi = small_lo + 2 * GDN_HEADS
    w_attn = w[:, :n_attn].astype(BF16)
    w_rest = jnp.concatenate(
        [w[:, n_attn:small_lo], w[:, small_hi:], w[:, small_lo:small_hi],
         jnp.zeros((d, LANES - 2 * GDN_HEADS), w.dtype)], axis=1).astype(BF16)
    gain1 = norm_mix_gain[0].reshape(1, d)

    tm = min(1024, t)
    qkv = _inproj(x2, gain1, w_attn, BF16, tm, n_attn).reshape(b, s, n_attn)
    rest2 = _inproj(x2, gain1, w_rest, F32, tm, REST_WIDTH // 3)
    rest = rest2.reshape(b, s, REST_WIDTH)

    lam_params = jnp.stack([diff_lambda_q1[0], diff_lambda_k1[0], diff_lambda_q2[0], diff_lambda_k2[0]]).astype(F32)
    ya = _diff_attention(lam_params, qkv, diff_subln_gain[0].reshape(1, DIFF_V_DIM), tq=min(256, s))

    yb = _gdn(rest, gdn_conv_w[0], _pad_lanes(gdn_a_log[0], GDN_HEADS), _pad_lanes(gdn_dt_bias[0], GDN_HEADS),
              gdn_norm_gain[0].reshape(1, GDN_DIM), tb=min(512, s))

    wr = jnp.concatenate([moe_w_group[0], moe_w_expert[0],
                          jnp.zeros((d, LANES - MOE_GROUPS - MOE_EXPERTS), F32)], axis=1)
    br = _pad_lanes(jnp.concatenate([moe_b_group[0], moe_b_expert[0]]), 0)
    x1, h2, comb = _merge(x2, ya.reshape(t, DIFF_WIDTH), yb.reshape(t, GDN_WIDTH), rest2,
                          w_branch_attn[0].astype(BF16), w_branch_gdn[0].astype(BF16), w_out[0].astype(BF16),
                          norm_ffn_gain[0].reshape(1, d), wr, br, tm=min(512, t))

    out = _moe(h2, comb, x1, moe_w_gate[0].astype(BF16), moe_w_up[0].astype(BF16), moe_w_down[0].astype(BF16),
               norm_final_gain.reshape(1, d), tm=min(1024, t))
    return out.reshape(b, s, d)
```

```python
import functools
import math

import jax
import jax.numpy as jnp
from jax import lax
from jax.experimental import pallas as pl
from jax.experimental.pallas import tpu as pltpu

F32 = jnp.float32
BF16 = jnp.bfloat16

D_MODEL = 1024
CHUNK = 64
EPS = 1e-6

DIFF_HEADS = 4
DIFF_HEAD_DIM = 64
DIFF_V_DIM = 2 * DIFF_HEAD_DIM
DIFF_WIDTH = DIFF_HEADS * DIFF_V_DIM

GDN_HEADS = 4
GDN_DIM = 128
GDN_WIDTH = GDN_HEADS * GDN_DIM
CONV_K = 4

MOE_GROUPS = 4
MOE_EXPERTS_PER_GROUP = 8
MOE_EXPERTS = MOE_GROUPS * MOE_EXPERTS_PER_GROUP
MOE_HIDDEN = 256

LANES = 128
SUBLANES = 8
VMEM_LIMIT = 56 * 1024 * 1024

LAMBDA_INIT = 0.8 - 0.6 * math.exp(-0.3 * 0)

REST_QKV = 0
REST_GATE = 3 * GDN_WIDTH
REST_MA = REST_GATE + GDN_WIDTH
REST_MB = REST_MA + D_MODEL
REST_SMALL = REST_MB + D_MODEL
REST_WIDTH = REST_SMALL + LANES

NEG_INF = float("-inf")
LOG2_E = math.log2(math.e)


def _params(*sem):
    return pltpu.CompilerParams(dimension_semantics=sem, vmem_limit_bytes=VMEM_LIMIT)


def _inproj_kernel(x_ref, g_ref, w_ref, o_ref, h_scr):
    @pl.when(pl.program_id(1) == 0)
    def _():
        x = x_ref[...]
        ms = jnp.mean(x * x, axis=-1, keepdims=True)
        h_scr[...] = (x * lax.rsqrt(ms + EPS) * g_ref[...]).astype(BF16)

    o_ref[...] = jnp.dot(h_scr[...], w_ref[...], preferred_element_type=F32).astype(o_ref.dtype)


def _inproj(x2, gain, w, out_dtype, tm, tn):
    t, d = x2.shape
    n = w.shape[1]
    return pl.pallas_call(
        _inproj_kernel,
        out_shape=jax.ShapeDtypeStruct((t, n), out_dtype),
        grid=(t // tm, n // tn),
        in_specs=[
            pl.BlockSpec((tm, d), lambda i, j: (i, 0)),
            pl.BlockSpec((1, d), lambda i, j: (0, 0)),
            pl.BlockSpec((d, tn), lambda i, j: (0, j)),
        ],
        out_specs=pl.BlockSpec((tm, tn), lambda i, j: (i, j)),
        scratch_shapes=[pltpu.VMEM((tm, d), BF16)],
        compiler_params=_params("parallel", "arbitrary"),
        name="inproj",
    )(x2, gain, w)


ATTN_TQ = 256
ATTN_TK = 512
ATTN_ROWS = 64
MASK_BIAS = -1e30


def _attn_kernel(lam_ref, q_ref, k_ref, v_ref, gain_ref, o_ref,
                 qs_scr, s0_scr, s1_scr, p0_scr, p1_scr, a0_scr, a1_scr, m_scr, acc_scr, *, tq, tk):
    i = pl.program_id(2)
    m_rows = 2 * tq
    ratio = tk // tq
    t_last = i // ratio
    n_pairs = (t_last + 2) // 2

    q = q_ref[0].astype(F32) * (DIFF_HEAD_DIM ** -0.5 * LOG2_E)
    lane = lax.broadcasted_iota(jnp.int32, (tq, LANES), 1)
    row = lax.broadcasted_iota(jnp.int32, (tq, LANES), 0)
    qs_scr[0:tq, 0:LANES] = jnp.where(lane < DIFF_HEAD_DIM, q, 0.0).astype(BF16)
    qs_scr[tq:m_rows, 0:LANES] = jnp.where(lane >= DIFF_HEAD_DIM, q, 0.0).astype(BF16)
    q_slot = row // CHUNK + (i % ratio) * (tq // CHUNK)
    onehot = jnp.where(lane == q_slot, 1.0, 0.0).astype(BF16)
    qs_scr[0:tq, LANES:2 * LANES] = onehot
    qs_scr[tq:m_rows, LANES:2 * LANES] = onehot

    k_lane = lax.broadcasted_iota(jnp.int32, (tk, LANES), 1)
    k_chunk = lax.broadcasted_iota(jnp.int32, (tk, LANES), 0) // CHUNK
    n_slots = tk // CHUNK
    diag_bias = jnp.where((k_lane < n_slots) & (k_chunk > k_lane), MASK_BIAS, 0.0).astype(BF16)
    full_bias = jnp.where(k_lane < n_slots, MASK_BIAS, 0.0).astype(BF16)
    zero_bias = jnp.zeros((tk, LANES), BF16)
    ones_v = jnp.ones((tk, DIFF_V_DIM), BF16)

    m_scr[...] = jnp.full(m_scr.shape, NEG_INF, F32)
    acc_scr[...] = jnp.zeros(acc_scr.shape, F32)
    p1_scr[...] = jnp.zeros(p1_scr.shape, BF16)
    a1_scr[...] = jnp.ones(a1_scr.shape, F32)

    def key_rows(t):
        return pl.ds(pl.multiple_of(jnp.minimum(t, t_last) * tk, tk), tk)

    def scores(t, s_scr):
        bias = jnp.where(t == t_last, diag_bias, jnp.where(t > t_last, full_bias, zero_bias))
        k_aug = jnp.concatenate([k_ref[0, key_rows(t), :], bias], axis=1)
        s_scr[...] = lax.dot_general(qs_scr[...], k_aug, (((1,), (1,)), ((), ())), preferred_element_type=F32)

    def softmax(s_scr, p_scr, a_scr):
        for r0 in range(0, m_rows, ATTN_ROWS):
            rs = slice(r0, r0 + ATTN_ROWS)
            s = s_scr[rs, :]
            tiles = [s[:, c * LANES:(c + 1) * LANES] for c in range(tk // LANES)]
            mx = functools.reduce(jnp.maximum, tiles)
            m_prev = m_scr[rs, :]
            m_new = jnp.maximum(m_prev, jnp.max(mx, axis=-1, keepdims=True))
            a_scr[rs, :] = jnp.exp2(m_prev - m_new)
            m_scr[rs, :] = m_new
            p_scr[rs, :] = jnp.concatenate([jnp.exp2(tl - m_new) for tl in tiles], axis=1).astype(BF16)

    def values(t, p_scr, a_scr):
        v_ext = jnp.concatenate([v_ref[0, key_rows(t), :], ones_v], axis=1)
        pv = jnp.dot(p_scr[...], v_ext, preferred_element_type=F32)
        alpha = a_scr[...]
        acc_scr[...] = jnp.concatenate([alpha, alpha], axis=1) * acc_scr[...] + pv

    scores(0, s0_scr)

    def pair_step(g, carry):
        t0 = 2 * g
        scores(t0 + 1, s1_scr)
        softmax(s0_scr, p0_scr, a0_scr)
        values(jnp.maximum(t0 - 1, 0), p1_scr, a1_scr)
        scores(t0 + 2, s0_scr)
        softmax(s1_scr, p1_scr, a1_scr)
        values(t0, p0_scr, a0_scr)
        return carry

    lax.fori_loop(0, n_pairs, pair_step, 0)
    values(2 * n_pairs - 1, p1_scr, a1_scr)

    lp = lam_ref[...]
    lam = (jnp.exp(jnp.sum(lp[0:1] * lp[1:2], axis=-1, keepdims=True))
           - jnp.exp(jnp.sum(lp[2:3] * lp[3:4], axis=-1, keepdims=True)) + LAMBDA_INIT)
    o0 = acc_scr[0:tq, 0:DIFF_V_DIM] / acc_scr[0:tq, DIFF_V_DIM:2 * DIFF_V_DIM]
    o1 = acc_scr[tq:m_rows, 0:DIFF_V_DIM] / acc_scr[tq:m_rows, DIFF_V_DIM:2 * DIFF_V_DIM]
    o = o0 - lam * o1
    ms = jnp.mean(o * o, axis=-1, keepdims=True)
    y = (o * lax.rsqrt(ms + EPS) * gain_ref[...]) * (1.0 - LAMBDA_INIT)
    o_ref[0] = y.astype(o_ref.dtype)


def _diff_attention(lam_params, qkv, subln_gain):
    b, s, _ = qkv.shape
    tq = min(ATTN_TQ, s)
    tk = min(ATTN_TK, s)
    assert s % tk == 0 and tk % tq == 0 and tq % CHUNK == 0 and tk // CHUNK <= LANES
    m_rows = 2 * tq
    return pl.pallas_call(
        functools.partial(_attn_kernel, tq=tq, tk=tk),
        out_shape=jax.ShapeDtypeStruct((b, s, DIFF_WIDTH), BF16),
        grid=(b, DIFF_HEADS, s // tq),
        in_specs=[
            pl.BlockSpec((4, DIFF_HEAD_DIM), lambda bi, h, i: (0, 0)),
            pl.BlockSpec((1, tq, DIFF_V_DIM), lambda bi, h, i: (bi, i, h)),
            pl.BlockSpec((1, s, DIFF_V_DIM), lambda bi, h, i: (bi, 0, DIFF_HEADS + h)),
            pl.BlockSpec((1, s, DIFF_V_DIM), lambda bi, h, i: (bi, 0, 2 * DIFF_HEADS + h)),
            pl.BlockSpec((1, DIFF_V_DIM), lambda bi, h, i: (0, 0)),
        ],
        out_specs=pl.BlockSpec((1, tq, DIFF_V_DIM), lambda bi, h, i: (bi, i, h)),
        scratch_shapes=[
            pltpu.VMEM((m_rows, 2 * LANES), BF16),
            pltpu.VMEM((m_rows, tk), F32),
            pltpu.VMEM((m_rows, tk), F32),
            pltpu.VMEM((m_rows, tk), BF16),
            pltpu.VMEM((m_rows, tk), BF16),
            pltpu.VMEM((m_rows, LANES), F32),
            pltpu.VMEM((m_rows, LANES), F32),
            pltpu.VMEM((m_rows, LANES), F32),
            pltpu.VMEM((m_rows, 2 * DIFF_V_DIM), F32),
        ],
        compiler_params=_params("parallel", "parallel", "arbitrary"),
        name="diff_attn",
    )(lam_params, qkv, qkv, qkv, subln_gain)


def _silu(x):
    return x * (1.0 / (1.0 + jnp.exp(-x)))


def _sigmoid(x):
    return 1.0 / (1.0 + jnp.exp(-x))


def _softplus(x):
    return jnp.maximum(x, 0.0) + jnp.log(1.0 + jnp.exp(-jnp.abs(x)))


def _dot(a, b):
    return jnp.dot(a.astype(BF16), b.astype(BF16), preferred_element_type=F32)


def _dot_nt(a, b):
    return lax.dot_general(a.astype(BF16), b.astype(BF16), (((1,), (1,)), ((), ())), preferred_element_type=F32)


def _gdn_kernel(prev_ref, qkv_ref, gate_ref, small_ref, convw_ref, alog_ref, dtb_ref, ngain_ref, o_ref,
                xp_scr, q_scr, k_scr, v_scr, pq_scr, n_scr, oc_scr, gl_scr, oraw_scr, state_scr, *, tb):
    i = pl.program_id(1)

    @pl.when(i == 0)
    def _():
        state_scr[...] = jnp.zeros(state_scr.shape, F32)

    xp_scr[0:SUBLANES, :] = jnp.where(i == 0, 0.0, prev_ref[0])
    xp_scr[SUBLANES:SUBLANES + tb, :] = qkv_ref[0]
    for sec in range(3 * GDN_HEADS):
        cols = slice(sec * LANES, (sec + 1) * LANES)
        acc = None
        for jj in range(CONV_K):
            start = SUBLANES - (CONV_K - 1) + jj
            term = xp_scr[start:start + tb, cols] * convw_ref[jj:jj + 1, cols]
            acc = term if acc is None else acc + term
        y = _silu(acc)
        which, head = divmod(sec, GDN_HEADS)
        hc = slice(head * LANES, (head + 1) * LANES)
        if which == 0:
            q_scr[:, hc] = y * lax.rsqrt(jnp.sum(y * y, axis=-1, keepdims=True) + EPS) * (GDN_DIM ** -0.5)
        elif which == 1:
            k_scr[:, hc] = y * lax.rsqrt(jnp.sum(y * y, axis=-1, keepdims=True) + EPS)
        else:
            v_scr[:, hc] = y

    sm = small_ref[0]
    beta_all = _sigmoid(sm)
    g_all = -jnp.exp(alog_ref[...]) * _softplus(sm + dtb_ref[...])

    br = lax.broadcasted_iota(jnp.int32, (tb, tb), 0)
    bc = lax.broadcasted_iota(jnp.int32, (tb, tb), 1)
    block_tril = jnp.where((br // CHUNK == bc // CHUNK) & (bc <= br), 1.0, 0.0)
    gc_all = jnp.dot(block_tril, g_all, preferred_element_type=F32, precision=lax.Precision.HIGHEST)
    gct_all = jnp.transpose(gc_all)

    ur = lax.broadcasted_iota(jnp.int32, (GDN_UNIT, GDN_UNIT), 0)
    uc = lax.broadcasted_iota(jnp.int32, (GDN_UNIT, GDN_UNIT), 1)
    same = (ur // CHUNK) == (uc // CHUNK)
    causal = same & (uc <= ur)
    strict = same & (uc < ur)
    eye = jnp.where(ur == uc, 1.0, 0.0)
    first_rows = ur < CHUNK
    first_cols = uc < CHUNK
    n_units = tb // GDN_UNIT
    units = [(h, u) for h in range(GDN_HEADS) for u in range(n_units)]

    def lane_bcast(x, lane):
        return jnp.broadcast_to(x[:, lane:lane + 1], (GDN_UNIT, GDN_UNIT))

    def row_bcast(x, r):
        return jnp.broadcast_to(x[r:r + 1, :], (GDN_UNIT, GDN_UNIT))

    def rows_of(u):
        return slice(u * GDN_UNIT, (u + 1) * GDN_UNIT)

    def cols_of(h):
        return slice(h * LANES, (h + 1) * LANES)

    qs = [q_scr[rows_of(u), cols_of(h)] for h, u in units]
    ks = [k_scr[rows_of(u), cols_of(h)] for h, u in units]
    vs = [v_scr[rows_of(u), cols_of(h)] for h, u in units]
    betas = [lane_bcast(beta_all[rows_of(u)], h) for h, u in units]
    gcb = [lane_bcast(gc_all[rows_of(u)], GDN_HEADS + h) for h, u in units]
    gcr = [row_bcast(gct_all[:, rows_of(u)], GDN_HEADS + h) for h, u in units]
    decay = [jnp.exp(jnp.where(causal, a - b, NEG_INF)) for a, b in zip(gcb, gcr)]
    eg = [jnp.exp(a) for a in gcb]
    g_last = [jnp.where(first_rows, row_bcast(a, CHUNK - 1), row_bcast(a, GDN_UNIT - 1)) for a in gcb]
    kb = [k * b for k, b in zip(ks, betas)]
    vb = [v * b for v, b in zip(vs, betas)]
    a_low = [jnp.where(strict, _dot_nt(x, k) * d, 0.0) for x, k, d in zip(kb, ks, decay)]
    tinv = [eye - a for a in a_low]
    pw = a_low
    for _ in range(5):
        pw = [_dot(x, x) for x in pw]
        tinv = [t + _dot(t, x) for t, x in zip(tinv, pw)]
    uw = [_dot(t, jnp.concatenate([v, x * e], axis=1)) for t, v, x, e in zip(tinv, vb, kb, eg)]
    intra = [_dot_nt(q, k) * d for q, k, d in zip(qs, ks, decay)]
    iuw = [_dot(a, x) for a, x in zip(intra, uw)]
    k_dec_t = [jnp.transpose(k * jnp.exp(gl - a)) for k, gl, a in zip(ks, g_last, gcb)]
    wu = [jnp.concatenate([x[:, GDN_DIM:], x[:, :GDN_DIM]], axis=1) for x in uw]
    pn0 = [_dot(jnp.where(first_cols, kt, 0.0), x) for kt, x in zip(k_dec_t, wu)]
    pn1 = [_dot(jnp.where(first_cols, 0.0, kt), x) for kt, x in zip(k_dec_t, wu)]
    for n in range(len(units)):
        q_eff = (qs[n] * eg[n] - iuw[n][:, GDN_DIM:]).astype(BF16)
        for half, pn in enumerate((pn0[n], pn1[n])):
            hr = slice(half * CHUNK, (half + 1) * CHUNK)
            pq_scr[n, half, 0:GDN_DIM, :] = pn[:, :GDN_DIM].astype(BF16)
            pq_scr[n, half, GDN_DIM:GDN_DIM + CHUNK, :] = q_eff[hr]
            n_scr[n, half] = pn[:, GDN_DIM:]
            oc_scr[n, half] = iuw[n][hr, :GDN_DIM]
            gl_scr[n, half] = jnp.exp(gcb[n][(half + 1) * CHUNK - 1:(half + 1) * CHUNK, :])

    def chunk_body(c, carry):
        u = c // 2
        half = c % 2
        rows = pl.ds(pl.multiple_of(c * CHUNK, CHUNK), CHUNK)
        states = [state_scr[h] for h in range(GDN_HEADS)]
        res = [jnp.dot(pq_scr[h * n_units + u, half], states[h].astype(BF16), preferred_element_type=F32)
               for h in range(GDN_HEADS)]
        for h in range(GDN_HEADS):
            n = h * n_units + u
            state_scr[h] = gl_scr[n, half] * states[h] - res[h][0:GDN_DIM] + n_scr[n, half]
            oraw_scr[h, rows, :] = res[h][GDN_DIM:GDN_DIM + CHUNK] + oc_scr[n, half]
        return carry

    lax.fori_loop(0, tb // CHUNK, chunk_body, 0)

    for h in range(GDN_HEADS):
        o = oraw_scr[h]
        ms = jnp.mean(o * o, axis=-1, keepdims=True)
        y = o * lax.rsqrt(ms + EPS) * ngain_ref[...] * _silu(gate_ref[0, :, cols_of(h)])
        o_ref[0, :, cols_of(h)] = y.astype(o_ref.dtype)


GDN_UNIT = 2 * CHUNK


def _gdn(rest, conv_w, alog_pad, dtb_pad, norm_gain, tb):
    b, s, _ = rest.shape
    nb = s // tb
    w3 = 3 * GDN_WIDTH
    assert tb % GDN_UNIT == 0
    n_hu = GDN_HEADS * (tb // GDN_UNIT)
    return pl.pallas_call(
        functools.partial(_gdn_kernel, tb=tb),
        out_shape=jax.ShapeDtypeStruct((b, s, GDN_WIDTH), BF16),
        grid=(b, nb),
        in_specs=[
            pl.BlockSpec((1, SUBLANES, w3), lambda bi, i: (bi, jnp.maximum(i * (tb // SUBLANES) - 1, 0), 0)),
            pl.BlockSpec((1, tb, w3), lambda bi, i: (bi, i, 0)),
            pl.BlockSpec((1, tb, GDN_WIDTH), lambda bi, i: (bi, i, REST_GATE // GDN_WIDTH)),
            pl.BlockSpec((1, tb, LANES), lambda bi, i: (bi, i, REST_SMALL // LANES)),
            pl.BlockSpec((CONV_K, w3), lambda bi, i: (0, 0)),
            pl.BlockSpec((1, LANES), lambda bi, i: (0, 0)),
            pl.BlockSpec((1, LANES), lambda bi, i: (0, 0)),
            pl.BlockSpec((1, GDN_DIM), lambda bi, i: (0, 0)),
        ],
        out_specs=pl.BlockSpec((1, tb, GDN_WIDTH), lambda bi, i: (bi, i, 0)),
        scratch_shapes=[
            pltpu.VMEM((tb + SUBLANES, w3), F32),
            pltpu.VMEM((tb, GDN_WIDTH), F32),
            pltpu.VMEM((tb, GDN_WIDTH), F32),
            pltpu.VMEM((tb, GDN_WIDTH), F32),
            pltpu.VMEM((n_hu, 2, GDN_DIM + CHUNK, GDN_DIM), BF16),
            pltpu.VMEM((n_hu, 2, GDN_DIM, GDN_DIM), F32),
            pltpu.VMEM((n_hu, 2, CHUNK, GDN_DIM), F32),
            pltpu.VMEM((n_hu, 2, 1, GDN_DIM), F32),
            pltpu.VMEM((GDN_HEADS, tb, GDN_DIM), F32),
            pltpu.VMEM((GDN_HEADS, GDN_DIM, GDN_DIM), F32),
        ],
        compiler_params=_params("parallel", "arbitrary"),
        name="gdn",
    )(rest, rest, rest, rest, conv_w, alog_pad, dtb_pad, norm_gain)


def _merge_kernel(x_ref, ya_ref, yb_ref, ma_ref, mb_ref, wa_ref, wb_ref, wo_ref, g2_ref, wr_ref, br_ref,
                  x1_ref, h2_ref, comb_ref):
    pa = jnp.dot(ya_ref[...], wa_ref[...], preferred_element_type=F32)
    pb = jnp.dot(yb_ref[...], wb_ref[...], preferred_element_type=F32)
    merged = _sigmoid(ma_ref[...]) * pa + _sigmoid(mb_ref[...]) * pb
    x1 = x_ref[...] + jnp.dot(merged.astype(BF16), wo_ref[...], preferred_element_type=F32)
    x1_ref[...] = x1
    ms = jnp.mean(x1 * x1, axis=-1, keepdims=True)
    h2 = x1 * lax.rsqrt(ms + EPS) * g2_ref[...]
    h2_ref[...] = h2.astype(BF16)

    logits = jnp.dot(h2, wr_ref[...], preferred_element_type=F32, precision=lax.Precision.HIGHEST) + br_ref[...]
    lane = lax.broadcasted_iota(jnp.int32, logits.shape, 1)
    big = jnp.int32(4 * LANES)
    gl = jnp.where(lane < MOE_GROUPS, logits, NEG_INF)
    gmax = jnp.max(gl, axis=-1, keepdims=True)
    gidx = jnp.min(jnp.where(gl == gmax, lane, big), axis=-1, keepdims=True)
    grp_w = 1.0 / jnp.sum(jnp.exp(gl - gmax), axis=-1, keepdims=True)
    lo = MOE_GROUPS + gidx * MOE_EXPERTS_PER_GROUP
    el = jnp.where((lane >= lo) & (lane < lo + MOE_EXPERTS_PER_GROUP), logits, NEG_INF)
    v1 = jnp.max(el, axis=-1, keepdims=True)
    i1 = jnp.min(jnp.where(el == v1, lane, big), axis=-1, keepdims=True)
    el2 = jnp.where(lane == i1, NEG_INF, el)
    v2 = jnp.max(el2, axis=-1, keepdims=True)
    i2 = jnp.min(jnp.where(el2 == v2, lane, big), axis=-1, keepdims=True)
    e2 = jnp.exp(v2 - v1)
    w1 = grp_w / (1.0 + e2)
    w2 = w1 * e2
    comb_ref[...] = jnp.where(lane == i1, w1, jnp.where(lane == i2, w2, 0.0))


def _merge(x2, ya, yb, rest, wa, wb, wo, g2, wr, br, tm):
    t, d = x2.shape
    row = lambda i: (i, 0)
    const = lambda i: (0, 0)
    return pl.pallas_call(
        _merge_kernel,
        out_shape=(
            jax.ShapeDtypeStruct((t, d), F32),
            jax.ShapeDtypeStruct((t, d), BF16),
            jax.ShapeDtypeStruct((t, LANES), F32),
        ),
        grid=(t // tm,),
        in_specs=[
            pl.BlockSpec((tm, d), row),
            pl.BlockSpec((tm, DIFF_WIDTH), row),
            pl.BlockSpec((tm, GDN_WIDTH), row),
            pl.BlockSpec((tm, d), lambda i: (i, REST_MA // D_MODEL)),
            pl.BlockSpec((tm, d), lambda i: (i, REST_MB // D_MODEL)),
            pl.BlockSpec((DIFF_WIDTH, d), const),
            pl.BlockSpec((GDN_WIDTH, d), const),
            pl.BlockSpec((d, d), const),
            pl.BlockSpec((1, d), const),
            pl.BlockSpec((d, LANES), const),
            pl.BlockSpec((1, LANES), const),
        ],
        out_specs=(pl.BlockSpec((tm, d), row), pl.BlockSpec((tm, d), row), pl.BlockSpec((tm, LANES), row)),
        compiler_params=_params("parallel"),
        name="merge_router",
    )(x2, ya, yb, rest, rest, wa, wb, wo, g2, wr, br)


def _moe_kernel(h2_ref, comb_ref, x1_ref, wg_ref, wu_ref, wd_ref, gf_ref, o_ref, acc_scr):
    e = pl.program_id(1)

    @pl.when(e == 0)
    def _():
        acc_scr[...] = jnp.zeros(acc_scr.shape, F32)

    h = h2_ref[...]
    hg = jnp.dot(h, wg_ref[0], preferred_element_type=F32)
    hu = jnp.dot(h, wu_ref[0], preferred_element_type=F32)
    comb = comb_ref[...]
    lane = lax.broadcasted_iota(jnp.int32, comb.shape, 1)
    cw = jnp.sum(jnp.where(lane == e + MOE_GROUPS, comb, 0.0), axis=-1, keepdims=True)
    act = _silu(hg) * hu * cw
    acc_scr[...] += jnp.dot(act.astype(BF16), wd_ref[0], preferred_element_type=F32)

    @pl.when(e == MOE_EXPERTS - 1)
    def _():
        x2 = x1_ref[...] + acc_scr[...]
        ms = jnp.mean(x2 * x2, axis=-1, keepdims=True)
        o_ref[...] = x2 * lax.rsqrt(ms + EPS) * gf_ref[...]


def _moe(h2, comb, x1, wg, wu, wd, gf, tm):
    t, d = x1.shape
    return pl.pallas_call(
        _moe_kernel,
        out_shape=jax.ShapeDtypeStruct((t, d), F32),
        grid=(t // tm, MOE_EXPERTS),
        in_specs=[
            pl.BlockSpec((tm, d), lambda i, e: (i, 0)),
            pl.BlockSpec((tm, LANES), lambda i, e: (i, 0)),
            pl.BlockSpec((tm, d), lambda i, e: (i, 0)),
            pl.BlockSpec((1, d, MOE_HIDDEN), lambda i, e: (e, 0, 0)),
            pl.BlockSpec((1, d, MOE_HIDDEN), lambda i, e: (e, 0, 0)),
            pl.BlockSpec((1, MOE_HIDDEN, d), lambda i, e: (e, 0, 0)),
            pl.BlockSpec((1, d), lambda i, e: (0, 0)),
        ],
        out_specs=pl.BlockSpec((tm, d), lambda i, e: (i, 0)),
        scratch_shapes=[pltpu.VMEM((tm, d), F32)],
        compiler_params=_params("parallel", "arbitrary"),
        name="moe",
    )(h2, comb, x1, wg, wu, wd, gf)


def _pad_lanes(v, offset):
    return jnp.zeros((1, LANES), F32).at[0, offset:offset + v.shape[0]].set(v.astype(F32))


def kernel(x, norm_mix_gain, w_in, diff_lambda_q1, diff_lambda_k1, diff_lambda_q2, diff_lambda_k2, diff_subln_gain, gdn_conv_w, gdn_a_log, gdn_dt_bias, gdn_norm_gain, w_branch_attn, w_branch_gdn, w_out, norm_ffn_gain, moe_w_group, moe_b_group, moe_w_expert, moe_b_expert, moe_w_gate, moe_w_up, moe_w_down, norm_final_gain):
    b, s, d = x.shape
    t = b * s
    x2 = x.reshape(t, d)

    w = w_in[0]
    n_attn = 3 * DIFF_WIDTH
    small_lo = n_attn + 3 * GDN_WIDTH
    small_hi = small_lo + 2 * GDN_HEADS
    w_attn = w[:, :n_attn].astype(BF16)
    w_rest = jnp.concatenate(
        [w[:, n_attn:small_lo], w[:, small_hi:], w[:, small_lo:small_hi],
         jnp.zeros((d, LANES - 2 * GDN_HEADS), w.dtype)], axis=1).astype(BF16)
    gain1 = norm_mix_gain[0].reshape(1, d)

    tm = min(1024, t)
    qkv = _inproj(x2, gain1, w_attn, BF16, tm, n_attn).reshape(b, s, n_attn)
    rest2 = _inproj(x2, gain1, w_rest, F32, tm, REST_WIDTH // 3)
    rest = rest2.reshape(b, s, REST_WIDTH)

    lam_params = jnp.stack([diff_lambda_q1[0], diff_lambda_k1[0], diff_lambda_q2[0], diff_lambda_k2[0]]).astype(F32)
    ya = _diff_attention(lam_params, qkv, diff_subln_gain[0].reshape(1, DIFF_V_DIM))

    yb = _gdn(rest, gdn_conv_w[0], _pad_lanes(gdn_a_log[0], GDN_HEADS), _pad_lanes(gdn_dt_bias[0], GDN_HEADS),
              gdn_norm_gain[0].reshape(1, GDN_DIM), tb=min(512, s))

    wr = jnp.concatenate([moe_w_group[0], moe_w_expert[0],
                          jnp.zeros((d, LANES - MOE_GROUPS - MOE_EXPERTS), F32)], axis=1)
    br = _pad_lanes(jnp.concatenate([moe_b_group[0], moe_b_expert[0]]), 0)
    x1, h2, comb = _merge(x2, ya.reshape(t, DIFF_WIDTH), yb.reshape(t, GDN_WIDTH), rest2,
                          w_branch_attn[0].astype(BF16), w_branch_gdn[0].astype(BF16), w_out[0].astype(BF16),
                          norm_ffn_gain[0].reshape(1, d), wr, br, tm=min(512, t))

    out = _moe(h2, comb, x1, moe_w_gate[0].astype(BF16), moe_w_up[0].astype(BF16), moe_w_down[0].astype(BF16),
               norm_final_gain.reshape(1, d), tm=min(1024, t))
    return out.reshape(b, s, d)
```

```python
import functools
import math

import jax
import jax.numpy as jnp
from jax import lax
from jax.experimental import pallas as pl
from jax.experimental.pallas import tpu as pltpu

F32 = jnp.float32
BF16 = jnp.bfloat16

D_MODEL = 1024
CHUNK = 64
EPS = 1e-6

DIFF_HEADS = 4
DIFF_HEAD_DIM = 64
DIFF_V_DIM = 2 * DIFF_HEAD_DIM
DIFF_WIDTH = DIFF_HEADS * DIFF_V_DIM

GDN_HEADS = 4
GDN_DIM = 128
GDN_WIDTH = GDN_HEADS * GDN_DIM
CONV_K = 4

MOE_GROUPS = 4
MOE_EXPERTS_PER_GROUP = 8
MOE_EXPERTS = MOE_GROUPS * MOE_EXPERTS_PER_GROUP
MOE_HIDDEN = 256

LANES = 128
SUBLANES = 8
VMEM_LIMIT = 56 * 1024 * 1024

LAMBDA_INIT = 0.8 - 0.6 * math.exp(-0.3 * 0)

REST_QKV = 0
REST_GATE = 3 * GDN_WIDTH
REST_MA = REST_GATE + GDN_WIDTH
REST_MB = REST_MA + D_MODEL
REST_SMALL = REST_MB + D_MODEL
REST_WIDTH = REST_SMALL + LANES

NEG_INF = float("-inf")
LOG2_E = math.log2(math.e)


def _params(*sem):
    return pltpu.CompilerParams(dimension_semantics=sem, vmem_limit_bytes=VMEM_LIMIT)


def _inproj_kernel(x_ref, g_ref, w_ref, o_ref, h_scr):
    @pl.when(pl.program_id(1) == 0)
    def _():
        x = x_ref[...]
        ms = jnp.mean(x * x, axis=-1, keepdims=True)
        h_scr[...] = (x * lax.rsqrt(ms + EPS) * g_ref[...]).astype(BF16)

    o_ref[...] = jnp.dot(h_scr[...], w_ref[...], preferred_element_type=F32).astype(o_ref.dtype)


def _inproj(x2, gain, w, out_dtype, tm, tn):
    t, d = x2.shape
    n = w.shape[1]
    return pl.pallas_call(
        _inproj_kernel,
        out_shape=jax.ShapeDtypeStruct((t, n), out_dtype),
        grid=(t // tm, n // tn),
        in_specs=[
            pl.BlockSpec((tm, d), lambda i, j: (i, 0)),
            pl.BlockSpec((1, d), lambda i, j: (0, 0)),
            pl.BlockSpec((d, tn), lambda i, j: (0, j)),
        ],
        out_specs=pl.BlockSpec((tm, tn), lambda i, j: (i, j)),
        scratch_shapes=[pltpu.VMEM((tm, d), BF16)],
        compiler_params=_params("parallel", "arbitrary"),
        name="inproj",
    )(x2, gain, w)


ATTN_TQ = 256
ATTN_TK = 512
ATTN_ROWS = 64
MASK_BIAS = -1e30


def _attn_kernel(lam_ref, q_ref, k_ref, v_ref, gain_ref, o_ref,
                 qs_scr, s0_scr, s1_scr, p0_scr, p1_scr, a0_scr, a1_scr, m_scr, acc_scr, *, tq, tk):
    i = pl.program_id(2)
    m_rows = 2 * tq
    ratio = tk // tq
    t_last = i // ratio
    n_pairs = (t_last + 2) // 2

    q = q_ref[0].astype(F32) * (DIFF_HEAD_DIM ** -0.5 * LOG2_E)
    lane = lax.broadcasted_iota(jnp.int32, (tq, LANES), 1)
    row = lax.broadcasted_iota(jnp.int32, (tq, LANES), 0)
    qs_scr[0:tq, 0:LANES] = jnp.where(lane < DIFF_HEAD_DIM, q, 0.0).astype(BF16)
    qs_scr[tq:m_rows, 0:LANES] = jnp.where(lane >= DIFF_HEAD_DIM, q, 0.0).astype(BF16)
    q_slot = row // CHUNK + (i % ratio) * (tq // CHUNK)
    onehot = jnp.where(lane == q_slot, 1.0, 0.0).astype(BF16)
    qs_scr[0:tq, LANES:2 * LANES] = onehot
    qs_scr[tq:m_rows, LANES:2 * LANES] = onehot

    k_lane = lax.broadcasted_iota(jnp.int32, (tk, LANES), 1)
    k_chunk = lax.broadcasted_iota(jnp.int32, (tk, LANES), 0) // CHUNK
    n_slots = tk // CHUNK
    diag_bias = jnp.where((k_lane < n_slots) & (k_chunk > k_lane), MASK_BIAS, 0.0).astype(BF16)
    full_bias = jnp.where(k_lane < n_slots, MASK_BIAS, 0.0).astype(BF16)
    zero_bias = jnp.zeros((tk, LANES), BF16)
    ones_v = jnp.ones((tk, DIFF_V_DIM), BF16)

    m_scr[...] = jnp.full(m_scr.shape, NEG_INF, F32)
    acc_scr[...] = jnp.zeros(acc_scr.shape, F32)
    p1_scr[...] = jnp.zeros(p1_scr.shape, BF16)
    a1_scr[...] = jnp.ones(a1_scr.shape, F32)

    def key_rows(t):
        return pl.ds(pl.multiple_of(jnp.minimum(t, t_last) * tk, tk), tk)

    def scores(t, s_scr):
        bias = jnp.where(t == t_last, diag_bias, jnp.where(t > t_last, full_bias, zero_bias))
        k_aug = jnp.concatenate([k_ref[0, key_rows(t), :], bias], axis=1)
        s_scr[...] = lax.dot_general(qs_scr[...], k_aug, (((1,), (1,)), ((), ())), preferred_element_type=F32)

    def softmax(s_scr, p_scr, a_scr):
        for r0 in range(0, m_rows, ATTN_ROWS):
            rs = slice(r0, r0 + ATTN_ROWS)
            s = s_scr[rs, :]
            tiles = [s[:, c * LANES:(c + 1) * LANES] for c in range(tk // LANES)]
            mx = functools.reduce(jnp.maximum, tiles)
            m_prev = m_scr[rs, :]
            m_new = jnp.maximum(m_prev, jnp.max(mx, axis=-1, keepdims=True))
            a_scr[rs, :] = jnp.exp2(m_prev - m_new)
            m_scr[rs, :] = m_new
            p_scr[rs, :] = jnp.concatenate([jnp.exp2(tl - m_new) for tl in tiles], axis=1).astype(BF16)

    def values(t, p_scr, a_scr):
        v_ext = jnp.concatenate([v_ref[0, key_rows(t), :], ones_v], axis=1)
        pv = jnp.dot(p_scr[...], v_ext, preferred_element_type=F32)
        alpha = a_scr[...]
        acc_scr[...] = jnp.concatenate([alpha, alpha], axis=1) * acc_scr[...] + pv

    scores(0, s0_scr)

    def pair_step(g, carry):
        t0 = 2 * g
        scores(t0 + 1, s1_scr)
        softmax(s0_scr, p0_scr, a0_scr)
        values(jnp.maximum(t0 - 1, 0), p1_scr, a1_scr)
        scores(t0 + 2, s0_scr)
        softmax(s1_scr, p1_scr, a1_scr)
        values(t0, p0_scr, a0_scr)
        return carry

    lax.fori_loop(0, n_pairs, pair_step, 0)
    values(2 * n_pairs - 1, p1_scr, a1_scr)

    lp = lam_ref[...]
    lam = (jnp.exp(jnp.sum(lp[0:1] * lp[1:2], axis=-1, keepdims=True))
           - jnp.exp(jnp.sum(lp[2:3] * lp[3:4], axis=-1, keepdims=True)) + LAMBDA_INIT)
    o0 = acc_scr[0:tq, 0:DIFF_V_DIM] / acc_scr[0:tq, DIFF_V_DIM:2 * DIFF_V_DIM]
    o1 = acc_scr[tq:m_rows, 0:DIFF_V_DIM] / acc_scr[tq:m_rows, DIFF_V_DIM:2 * DIFF_V_DIM]
    o = o0 - lam * o1
    ms = jnp.mean(o * o, axis=-1, keepdims=True)
    y = (o * lax.rsqrt(ms + EPS) * gain_ref[...]) * (1.0 - LAMBDA_INIT)
    o_ref[0] = y.astype(o_ref.dtype)


def _diff_attention(lam_params, qkv, subln_gain):
    b, s, _ = qkv.shape
    tq = min(ATTN_TQ, s)
    tk = min(ATTN_TK, s)
    assert s % tk == 0 and tk % tq == 0 and tq % CHUNK == 0 and tk // CHUNK <= LANES
    m_rows = 2 * tq
    return pl.pallas_call(
        functools.partial(_attn_kernel, tq=tq, tk=tk),
        out_shape=jax.ShapeDtypeStruct((b, s, DIFF_WIDTH), BF16),
        grid=(b, DIFF_HEADS, s // tq),
        in_specs=[
            pl.BlockSpec((4, DIFF_HEAD_DIM), lambda bi, h, i: (0, 0)),
            pl.BlockSpec((1, tq, DIFF_V_DIM), lambda bi, h, i: (bi, i, h)),
            pl.BlockSpec((1, s, DIFF_V_DIM), lambda bi, h, i: (bi, 0, DIFF_HEADS + h)),
            pl.BlockSpec((1, s, DIFF_V_DIM), lambda bi, h, i: (bi, 0, 2 * DIFF_HEADS + h)),
            pl.BlockSpec((1, DIFF_V_DIM), lambda bi, h, i: (0, 0)),
        ],
        out_specs=pl.BlockSpec((1, tq, DIFF_V_DIM), lambda bi, h, i: (bi, i, h)),
        scratch_shapes=[
            pltpu.VMEM((m_rows, 2 * LANES), BF16),
            pltpu.VMEM((m_rows, tk), F32),
            pltpu.VMEM((m_rows, tk), F32),
            pltpu.VMEM((m_rows, tk), BF16),
            pltpu.VMEM((m_rows, tk), BF16),
            pltpu.VMEM((m_rows, LANES), F32),
            pltpu.VMEM((m_rows, LANES), F32),
            pltpu.VMEM((m_rows, LANES), F32),
            pltpu.VMEM((m_rows, 2 * DIFF_V_DIM), F32),
        ],
        compiler_params=_params("parallel", "parallel", "arbitrary"),
        name="diff_attn",
    )(lam_params, qkv, qkv, qkv, subln_gain)


def _silu(x):
    return x * (1.0 / (1.0 + jnp.exp(-x)))


def _sigmoid(x):
    return 1.0 / (1.0 + jnp.exp(-x))


def _softplus(x):
    return jnp.maximum(x, 0.0) + jnp.log(1.0 + jnp.exp(-jnp.abs(x)))


def _dot(a, b):
    return jnp.dot(a.astype(BF16), b.astype(BF16), preferred_element_type=F32)


def _dot_nt(a, b):
    return lax.dot_general(a.astype(BF16), b.astype(BF16), (((1,), (1,)), ((), ())), preferred_element_type=F32)


def _gdn_kernel(prev_ref, qkv_ref, gate_ref, small_ref, convw_ref, alog_ref, dtb_ref, ngain_ref, o_ref,
                xp_scr, q_scr, k_scr, v_scr, pq_scr, n_scr, oc_scr, gl_scr, oraw_scr, state_scr, *, tb):
    i = pl.program_id(1)

    @pl.when(i == 0)
    def _():
        state_scr[...] = jnp.zeros(state_scr.shape, F32)

    xp_scr[0:SUBLANES, :] = jnp.where(i == 0, 0.0, prev_ref[0])
    xp_scr[SUBLANES:SUBLANES + tb, :] = qkv_ref[0]
    for sec in range(3 * GDN_HEADS):
        cols = slice(sec * LANES, (sec + 1) * LANES)
        acc = None
        for jj in range(CONV_K):
            start = SUBLANES - (CONV_K - 1) + jj
            term = xp_scr[start:start + tb, cols] * convw_ref[jj:jj + 1, cols]
            acc = term if acc is None else acc + term
        y = _silu(acc)
        which, head = divmod(sec, GDN_HEADS)
        hc = slice(head * LANES, (head + 1) * LANES)
        if which == 0:
            q_scr[:, hc] = y * lax.rsqrt(jnp.sum(y * y, axis=-1, keepdims=True) + EPS) * (GDN_DIM ** -0.5)
        elif which == 1:
            k_scr[:, hc] = y * lax.rsqrt(jnp.sum(y * y, axis=-1, keepdims=True) + EPS)
        else:
            v_scr[:, hc] = y

    sm = small_ref[0]
    beta_all = _sigmoid(sm)
    g_all = -jnp.exp(alog_ref[...]) * _softplus(sm + dtb_ref[...])

    br = lax.broadcasted_iota(jnp.int32, (tb, tb), 0)
    bc = lax.broadcasted_iota(jnp.int32, (tb, tb), 1)
    block_tril = jnp.where((br // CHUNK == bc // CHUNK) & (bc <= br), 1.0, 0.0)
    gc_all = jnp.dot(block_tril, g_all, preferred_element_type=F32, precision=lax.Precision.HIGHEST)
    gct_all = jnp.transpose(gc_all)

    ur = lax.broadcasted_iota(jnp.int32, (GDN_UNIT, GDN_UNIT), 0)
    uc = lax.broadcasted_iota(jnp.int32, (GDN_UNIT, GDN_UNIT), 1)
    same = (ur // CHUNK) == (uc // CHUNK)
    causal = same & (uc <= ur)
    strict = same & (uc < ur)
    eye = jnp.where(ur == uc, 1.0, 0.0)
    first_rows = ur < CHUNK
    first_cols = uc < CHUNK
    n_units = tb // GDN_UNIT
    units = [(h, u) for h in range(GDN_HEADS) for u in range(n_units)]

    def lane_bcast(x, lane):
        return jnp.broadcast_to(x[:, lane:lane + 1], (GDN_UNIT, GDN_UNIT))

    def row_bcast(x, r):
        return jnp.broadcast_to(x[r:r + 1, :], (GDN_UNIT, GDN_UNIT))

    def rows_of(u):
        return slice(u * GDN_UNIT, (u + 1) * GDN_UNIT)

    def cols_of(h):
        return slice(h * LANES, (h + 1) * LANES)

    qs = [q_scr[rows_of(u), cols_of(h)] for h, u in units]
    ks = [k_scr[rows_of(u), cols_of(h)] for h, u in units]
    vs = [v_scr[rows_of(u), cols_of(h)] for h, u in units]
    betas = [lane_bcast(beta_all[rows_of(u)], h) for h, u in units]
    gcb = [lane_bcast(gc_all[rows_of(u)], GDN_HEADS + h) for h, u in units]
    gcr = [row_bcast(gct_all[:, rows_of(u)], GDN_HEADS + h) for h, u in units]
    decay = [jnp.exp(jnp.where(causal, a - b, NEG_INF)) for a, b in zip(gcb, gcr)]
    eg = [jnp.exp(a) for a in gcb]
    g_last = [jnp.where(first_rows, row_bcast(a, CHUNK - 1), row_bcast(a, GDN_UNIT - 1)) for a in gcb]
    kb = [k * b for k, b in zip(ks, betas)]
    vb = [v * b for v, b in zip(vs, betas)]
    a_low = [jnp.where(strict, _dot_nt(x, k) * d, 0.0) for x, k, d in zip(kb, ks, decay)]
    tinv = [eye - a for a in a_low]
    pw = a_low
    for _ in range(5):
        pw = [_dot(x, x) for x in pw]
        tinv = [t + _dot(t, x) for t, x in zip(tinv, pw)]
    uw = [_dot(t, jnp.concatenate([v, x * e], axis=1)) for t, v, x, e in zip(tinv, vb, kb, eg)]
    intra = [_dot_nt(q, k) * d for q, k, d in zip(qs, ks, decay)]
    iuw = [_dot(a, x) for a, x in zip(intra, uw)]
    k_dec_t = [jnp.transpose(k * jnp.exp(gl - a)) for k, gl, a in zip(ks, g_last, gcb)]
    wu = [jnp.concatenate([x[:, GDN_DIM:], x[:, :GDN_DIM]], axis=1) for x in uw]
    pn0 = [_dot(jnp.where(first_cols, kt, 0.0), x) for kt, x in zip(k_dec_t, wu)]
    pn1 = [_dot(jnp.where(first_cols, 0.0, kt), x) for kt, x in zip(k_dec_t, wu)]
    for n in range(len(units)):
        q_eff = (qs[n] * eg[n] - iuw[n][:, GDN_DIM:]).astype(BF16)
        for half, pn in enumerate((pn0[n], pn1[n])):
            hr = slice(half * CHUNK, (half + 1) * CHUNK)
            pq_scr[n, half, 0:GDN_DIM, :] = pn[:, :GDN_DIM].astype(BF16)
            pq_scr[n, half, GDN_DIM:GDN_DIM + CHUNK, :] = q_eff[hr]
            n_scr[n, half] = pn[:, GDN_DIM:]
            oc_scr[n, half] = iuw[n][hr, :GDN_DIM]
            gl_scr[n, half] = jnp.exp(gcb[n][(half + 1) * CHUNK - 1:(half + 1) * CHUNK, :])

    def chunk_body(c, carry):
        u = c // 2
        half = c % 2
        rows = pl.ds(pl.multiple_of(c * CHUNK, CHUNK), CHUNK)
        states = [state_scr[h] for h in range(GDN_HEADS)]
        res = [jnp.dot(pq_scr[h * n_units + u, half], states[h].astype(BF16), preferred_element_type=F32)
               for h in range(GDN_HEADS)]
        for h in range(GDN_HEADS):
            n = h * n_units + u
            state_scr[h] = gl_scr[n, half] * states[h] - res[h][0:GDN_DIM] + n_scr[n, half]
            oraw_scr[h, rows, :] = res[h][GDN_DIM:GDN_DIM + CHUNK] + oc_scr[n, half]
        return carry

    lax.fori_loop(0, tb // CHUNK, chunk_body, 0)

    for h in range(GDN_HEADS):
        o = oraw_scr[h]
        ms = jnp.mean(o * o, axis=-1, keepdims=True)
        y = o * lax.rsqrt(ms + EPS) * ngain_ref[...] * _silu(gate_ref[0, :, cols_of(h)])
        o_ref[0, :, cols_of(h)] = y.astype(o_ref.dtype)


GDN_UNIT = 2 * CHUNK


def _gdn(rest, conv_w, alog_pad, dtb_pad, norm_gain, tb):
    b, s, _ = rest.shape
    nb = s // tb
    w3 = 3 * GDN_WIDTH
    assert tb % GDN_UNIT == 0
    n_hu = GDN_HEADS * (tb // GDN_UNIT)
    return pl.pallas_call(
        functools.partial(_gdn_kernel, tb=tb),
        out_shape=jax.ShapeDtypeStruct((b, s, GDN_WIDTH), BF16),
        grid=(b, nb),
        in_specs=[
            pl.BlockSpec((1, SUBLANES, w3), lambda bi, i: (bi, jnp.maximum(i * (tb // SUBLANES) - 1, 0), 0)),
            pl.BlockSpec((1, tb, w3), lambda bi, i: (bi, i, 0)),
            pl.BlockSpec((1, tb, GDN_WIDTH), lambda bi, i: (bi, i, REST_GATE // GDN_WIDTH)),
            pl.BlockSpec((1, tb, LANES), lambda bi, i: (bi, i, REST_SMALL // LANES)),
            pl.BlockSpec((CONV_K, w3), lambda bi, i: (0, 0)),
            pl.BlockSpec((1, LANES), lambda bi, i: (0, 0)),
            pl.BlockSpec((1, LANES), lambda bi, i: (0, 0)),
            pl.BlockSpec((1, GDN_DIM), lambda bi, i: (0, 0)),
        ],
        out_specs=pl.BlockSpec((1, tb, GDN_WIDTH), lambda bi, i: (bi, i, 0)),
        scratch_shapes=[
            pltpu.VMEM((tb + SUBLANES, w3), F32),
            pltpu.VMEM((tb, GDN_WIDTH), F32),
            pltpu.VMEM((tb, GDN_WIDTH), F32),
            pltpu.VMEM((tb, GDN_WIDTH), F32),
            pltpu.VMEM((n_hu, 2, GDN_DIM + CHUNK, GDN_DIM), BF16),
            pltpu.VMEM((n_hu, 2, GDN_DIM, GDN_DIM), F32),
            pltpu.VMEM((n_hu, 2, CHUNK, GDN_DIM), F32),
            pltpu.VMEM((n_hu, 2, 1, GDN_DIM), F32),
            pltpu.VMEM((GDN_HEADS, tb, GDN_DIM), F32),
            pltpu.VMEM((GDN_HEADS, GDN_DIM, GDN_DIM), F32),
        ],
        compiler_params=_params("parallel", "arbitrary"),
        name="gdn",
    )(rest, rest, rest, rest, conv_w, alog_pad, dtb_pad, norm_gain)


def _merge_kernel(x_ref, ya_ref, yb_ref, ma_ref, mb_ref, wa_ref, wb_ref, wo_ref, g2_ref, wr_ref, br_ref,
                  x1_ref, h2_ref, route_ref, cnt_ref):
    pa = jnp.dot(ya_ref[...], wa_ref[...], preferred_element_type=F32)
    pb = jnp.dot(yb_ref[...], wb_ref[...], preferred_element_type=F32)
    merged = _sigmoid(ma_ref[...]) * pa + _sigmoid(mb_ref[...]) * pb
    x1 = x_ref[...] + jnp.dot(merged.astype(BF16), wo_ref[...], preferred_element_type=F32)
    x1_ref[...] = x1
    ms = jnp.mean(x1 * x1, axis=-1, keepdims=True)
    h2 = x1 * lax.rsqrt(ms + EPS) * g2_ref[...]
    h2_ref[...] = h2

    logits = jnp.dot(h2, wr_ref[...], preferred_element_type=F32, precision=lax.Precision.HIGHEST) + br_ref[...]
    lane = lax.broadcasted_iota(jnp.int32, logits.shape, 1)
    big = jnp.int32(4 * LANES)
    gl = jnp.where(lane < MOE_GROUPS, logits, NEG_INF)
    gmax = jnp.max(gl, axis=-1, keepdims=True)
    gidx = jnp.min(jnp.where(gl == gmax, lane, big), axis=-1, keepdims=True)
    grp_w = 1.0 / jnp.sum(jnp.exp(gl - gmax), axis=-1, keepdims=True)
    lo = MOE_GROUPS + gidx * MOE_EXPERTS_PER_GROUP
    el = jnp.where((lane >= lo) & (lane < lo + MOE_EXPERTS_PER_GROUP), logits, NEG_INF)
    v1 = jnp.max(el, axis=-1, keepdims=True)
    i1 = jnp.min(jnp.where(el == v1, lane, big), axis=-1, keepdims=True)
    el2 = jnp.where(lane == i1, NEG_INF, el)
    v2 = jnp.max(el2, axis=-1, keepdims=True)
    i2 = jnp.min(jnp.where(el2 == v2, lane, big), axis=-1, keepdims=True)
    e2 = jnp.exp(v2 - v1)
    w1 = grp_w / (1.0 + e2)
    w2 = w1 * e2
    id1 = i1 - MOE_GROUPS
    id2 = i2 - MOE_GROUPS
    route_ref[...] = jnp.where(lane == 0, id1.astype(F32), jnp.where(lane == 1, id2.astype(F32),
                               jnp.where(lane == 2, w1, jnp.where(lane == 3, w2, 0.0))))

    @pl.when(pl.program_id(0) == 0)
    def _():
        cnt_ref[...] = jnp.zeros(cnt_ref.shape, F32)

    hits = jnp.where((lane == id1) | (lane == id2), 1.0, 0.0)
    cnt_ref[...] += jnp.broadcast_to(jnp.sum(hits, axis=0, keepdims=True), cnt_ref.shape)


def _merge(x2, ya, yb, rest, wa, wb, wo, g2, wr, br, tm):
    t, d = x2.shape
    row = lambda i: (i, 0)
    const = lambda i: (0, 0)
    return pl.pallas_call(
        _merge_kernel,
        out_shape=(
            jax.ShapeDtypeStruct((t, d), F32),
            jax.ShapeDtypeStruct((t, d), F32),
            jax.ShapeDtypeStruct((t, LANES), F32),
            jax.ShapeDtypeStruct((SUBLANES, LANES), F32),
        ),
        grid=(t // tm,),
        in_specs=[
            pl.BlockSpec((tm, d), row),
            pl.BlockSpec((tm, DIFF_WIDTH), row),
            pl.BlockSpec((tm, GDN_WIDTH), row),
            pl.BlockSpec((tm, d), lambda i: (i, REST_MA // D_MODEL)),
            pl.BlockSpec((tm, d), lambda i: (i, REST_MB // D_MODEL)),
            pl.BlockSpec((DIFF_WIDTH, d), const),
            pl.BlockSpec((GDN_WIDTH, d), const),
            pl.BlockSpec((d, d), const),
            pl.BlockSpec((1, d), const),
            pl.BlockSpec((d, LANES), const),
            pl.BlockSpec((1, LANES), const),
        ],
        out_specs=(pl.BlockSpec((tm, d), row), pl.BlockSpec((tm, d), row), pl.BlockSpec((tm, LANES), row),
                   pl.BlockSpec((SUBLANES, LANES), const)),
        compiler_params=_params("arbitrary"),
        name="merge_router",
    )(x2, ya, yb, rest, rest, wa, wb, wo, g2, wr, br)


MOE_BLK = 256
MOE_META_LANES = 256
MOE_ROUTE_TILE = 512
MOE_DMA_TILE = 256


def _route_kernel(route_ref, cnt_ref, pos_ref, meta_ref, run_scr, *, n_rows):
    i = pl.program_id(0)
    tp = route_ref.shape[0]

    @pl.when(i == 0)
    def _():
        cnt = cnt_ref[...]
        jr = lax.broadcasted_iota(jnp.int32, (LANES, LANES), 0)
        jc = lax.broadcasted_iota(jnp.int32, (LANES, LANES), 1)
        upper = jnp.where(jr < jc, 1.0, 0.0)
        hi_prec = dict(preferred_element_type=F32, precision=lax.Precision.HIGHEST)
        off = jnp.dot(cnt, upper, **hi_prec)
        run_scr[...] = off[0:1]
        blk = float(MOE_BLK)
        first_tile = jnp.floor(off / blk)
        last_tile = jnp.floor((off + cnt - 1.0) / blk)
        n_it = jnp.where(cnt > 0.0, last_tile - first_tile + 1.0, 0.0)
        it_start = jnp.dot(n_it, upper, **hi_prec)
        it_end = it_start + n_it
        lane8 = lax.broadcasted_iota(jnp.int32, cnt.shape, 1)
        e_max = jnp.max(jnp.where(cnt > 0.0, lane8, 0), axis=-1, keepdims=True).astype(F32)[0:1]
        sub8 = lax.broadcasted_iota(jnp.int32, cnt.shape, 0)
        table = jnp.where(sub8 == 0, first_tile, jnp.where(sub8 == 1, it_start, jnp.where(
            sub8 == 2, off, jnp.where(sub8 == 3, cnt, it_end))))
        cols = jnp.transpose(table)
        shape = (LANES, MOE_META_LANES)
        e_sub = lax.broadcasted_iota(jnp.int32, shape, 0)
        w_lane = lax.broadcasted_iota(jnp.int32, shape, 1).astype(F32)
        col = lambda k: jnp.broadcast_to(cols[:, k:k + 1], shape)
        e_w = jnp.sum(jnp.where((e_sub < MOE_EXPERTS) & (col(4) <= w_lane), 1.0, 0.0), axis=0, keepdims=True)
        valid = e_w < float(MOE_EXPERTS)
        e_w = jnp.minimum(e_w, e_max)
        sel = e_sub.astype(F32) == e_w
        pick = lambda k: jnp.sum(jnp.where(sel, col(k), 0.0), axis=0, keepdims=True)
        w_row = w_lane[0:1]
        tile_w = jnp.where(valid, pick(0) + (w_row - pick(1)), float(n_rows // MOE_BLK - 1))
        lo_w = jnp.maximum(pick(2) - tile_w * blk, 0.0)
        hi_w = jnp.minimum(pick(2) + pick(3) - tile_w * blk, blk)
        lo_w = jnp.where(valid, lo_w, 0.0)
        hi_w = jnp.where(valid, hi_w, 0.0)
        sub_m = lax.broadcasted_iota(jnp.int32, meta_ref.shape, 0)
        bc = lambda v: jnp.broadcast_to(v, meta_ref.shape)
        meta_ref[...] = jnp.where(sub_m == 0, bc(e_w), jnp.where(sub_m == 1, bc(tile_w), jnp.where(
            sub_m == 2, bc(lo_w), bc(hi_w)))).astype(jnp.int32)

    r = route_ref[...]
    lane = lax.broadcasted_iota(jnp.int32, r.shape, 1)
    lane_f = lane.astype(F32)
    oh1 = lane_f == r[:, 0:1]
    oh2 = lane_f == r[:, 1:2]
    hits = jnp.where(oh1 | oh2, 1.0, 0.0)
    tr = lax.broadcasted_iota(jnp.int32, (tp, tp), 0)
    tc = lax.broadcasted_iota(jnp.int32, (tp, tp), 1)
    earlier = jnp.where(tc < tr, 1.0, 0.0).astype(BF16)
    rank = jnp.dot(earlier, hits.astype(BF16), preferred_element_type=F32)
    base = run_scr[...] + rank
    p1 = jnp.sum(jnp.where(oh1, base, 0.0), axis=-1, keepdims=True)
    p2 = jnp.sum(jnp.where(oh2, base, 0.0), axis=-1, keepdims=True)
    pos_ref[...] = jnp.where(lane == 0, p1, jnp.where(lane == 1, p2, 0.0)).astype(jnp.int32)
    run_scr[...] += jnp.sum(hits, axis=0, keepdims=True)


def _route_positions(route, cnt):
    t = route.shape[0]
    tp = min(MOE_ROUTE_TILE, t)
    n_rows = 2 * t
    assert n_rows % MOE_BLK == 0 and n_rows // MOE_BLK + MOE_EXPERTS <= MOE_META_LANES and n_rows < 2 ** 24
    return pl.pallas_call(
        functools.partial(_route_kernel, n_rows=n_rows),
        out_shape=(jax.ShapeDtypeStruct((t, LANES), jnp.int32),
                   jax.ShapeDtypeStruct((SUBLANES, MOE_META_LANES), jnp.int32)),
        grid=(t // tp,),
        in_specs=[pl.BlockSpec((tp, LANES), lambda i: (i, 0)),
                  pl.BlockSpec((SUBLANES, LANES), lambda i: (0, 0))],
        out_specs=(pl.BlockSpec((tp, LANES), lambda i: (i, 0)),
                   pl.BlockSpec((SUBLANES, MOE_META_LANES), lambda i: (0, 0))),
        scratch_shapes=[pltpu.VMEM((1, LANES), F32)],
        compiler_params=_params("arbitrary"),
        name="route_positions",
    )(route, cnt)


def _scatter_kernel(pos_ref, h2_ref, xs_ref, sem):
    ts = h2_ref.shape[0]

    def body(tok, carry):
        src = h2_ref.at[pl.ds(tok, 1), :]
        pltpu.make_async_copy(src, xs_ref.at[pl.ds(pos_ref[0, 0, 2 * tok], 1), :], sem).start()
        pltpu.make_async_copy(src, xs_ref.at[pl.ds(pos_ref[0, 0, 2 * tok + 1], 1), :], sem).start()
        return carry

    lax.fori_loop(0, ts, body, 0, unroll=8)
    for _ in range(2):
        pltpu.make_async_copy(h2_ref, xs_ref.at[pl.ds(0, ts), :], sem).wait()


def _scatter_rows(pos3, h2):
    t, d = h2.shape
    ts = pos3.shape[2] // 2
    return pl.pallas_call(
        _scatter_kernel,
        out_shape=jax.ShapeDtypeStruct((2 * t, d), F32),
        grid=(t // ts,),
        in_specs=[pl.BlockSpec((1, 1, 2 * ts), lambda i: (i, 0, 0), memory_space=pltpu.SMEM),
                  pl.BlockSpec((ts, d), lambda i: (i, 0))],
        out_specs=pl.BlockSpec(memory_space=pl.ANY),
        scratch_shapes=[pltpu.SemaphoreType.DMA],
        compiler_params=pltpu.CompilerParams(dimension_semantics=("arbitrary",), vmem_limit_bytes=VMEM_LIMIT,
                                             disable_bounds_checks=True),
        name="scatter_rows",
    )(pos3, h2)


def _expert_kernel(ie_ref, it_ref, lo_ref, hi_ref, xs_ref, wg_ref, wu_ref, wd_ref, y_ref, wg_scr, wu_scr, wd_scr):
    w = pl.program_id(0)
    prev = jnp.maximum(w - 1, 0)

    @pl.when((w == 0) | (ie_ref[w] != ie_ref[prev]))
    def _():
        wg_scr[...] = wg_ref[0].astype(BF16)
        wu_scr[...] = wu_ref[0].astype(BF16)
        wd_scr[...] = wd_ref[0].astype(BF16)

    @pl.when((w == 0) | (it_ref[w] != it_ref[prev]))
    def _():
        y_ref[...] = jnp.zeros(y_ref.shape, F32)

    lo = lo_ref[w]
    hi = hi_ref[w]

    @pl.when(hi > lo)
    def _():
        x = xs_ref[...].astype(BF16)
        hg = jnp.dot(x, wg_scr[...], preferred_element_type=F32)
        hu = jnp.dot(x, wu_scr[...], preferred_element_type=F32)
        act = (_silu(hg) * hu).astype(BF16)
        yp = jnp.dot(act, wd_scr[...], preferred_element_type=F32)
        row = lax.broadcasted_iota(jnp.int32, yp.shape, 0)
        y_ref[...] += jnp.where((row >= lo) & (row < hi), yp, 0.0)


def _experts(meta, xs, wg, wu, wd):
    n_rows, d = xs.shape
    n_items = n_rows // MOE_BLK + MOE_EXPERTS
    ie, it, lo, hi = (meta[k, :n_items] for k in range(4))
    return pl.pallas_call(
        _expert_kernel,
        out_shape=jax.ShapeDtypeStruct((n_rows, d), F32),
        grid_spec=pltpu.PrefetchScalarGridSpec(
            num_scalar_prefetch=4,
            grid=(n_items,),
            in_specs=[
                pl.BlockSpec((MOE_BLK, d), lambda w, ie, it, lo, hi: (it[w], 0)),
                pl.BlockSpec((1, d, MOE_HIDDEN), lambda w, ie, it, lo, hi: (ie[w], 0, 0)),
                pl.BlockSpec((1, d, MOE_HIDDEN), lambda w, ie, it, lo, hi: (ie[w], 0, 0)),
                pl.BlockSpec((1, MOE_HIDDEN, d), lambda w, ie, it, lo, hi: (ie[w], 0, 0)),
            ],
            out_specs=pl.BlockSpec((MOE_BLK, d), lambda w, ie, it, lo, hi: (it[w], 0)),
            scratch_shapes=[pltpu.VMEM((d, MOE_HIDDEN), BF16), pltpu.VMEM((d, MOE_HIDDEN), BF16),
                            pltpu.VMEM((MOE_HIDDEN, d), BF16)],
        ),
        compiler_params=_params("arbitrary"),
        name="experts",
    )(ie, it, lo, hi, xs, wg, wu, wd)


def _combine_kernel(pos_ref, posn_ref, x1_ref, route_ref, gf_ref, y_ref, o_ref, ybuf, sem):
    i = pl.program_id(0)
    n = pl.num_programs(0)
    tc = x1_ref.shape[0]
    slot = i % 2

    def issue(p_ref, s):
        def body(tok, carry):
            for k in range(2):
                pltpu.make_async_copy(y_ref.at[pl.ds(p_ref[0, 0, 2 * tok + k], 1), :],
                                      ybuf.at[s, k, pl.ds(tok, 1), :], sem.at[s]).start()
            return carry

        lax.fori_loop(0, tc, body, 0, unroll=8)

    @pl.when(i == 0)
    def _():
        issue(pos_ref, 0)

    @pl.when(i + 1 < n)
    def _():
        issue(posn_ref, 1 - slot)

    for k in range(2):
        pltpu.make_async_copy(y_ref.at[pl.ds(0, tc), :], ybuf.at[slot, k], sem.at[slot]).wait()
    r = route_ref[...]
    x2 = x1_ref[...] + r[:, 2:3] * ybuf[slot, 0] + r[:, 3:4] * ybuf[slot, 1]
    ms = jnp.mean(x2 * x2, axis=-1, keepdims=True)
    o_ref[...] = x2 * lax.rsqrt(ms + EPS) * gf_ref[...]


def _combine(pos3, x1, route, gf, y):
    t, d = x1.shape
    tc = pos3.shape[2] // 2
    n = t // tc
    return pl.pallas_call(
        _combine_kernel,
        out_shape=jax.ShapeDtypeStruct((t, d), F32),
        grid=(n,),
        in_specs=[
            pl.BlockSpec((1, 1, 2 * tc), lambda i: (i, 0, 0), memory_space=pltpu.SMEM),
            pl.BlockSpec((1, 1, 2 * tc), lambda i: (jnp.minimum(i + 1, n - 1), 0, 0), memory_space=pltpu.SMEM),
            pl.BlockSpec((tc, d), lambda i: (i, 0)),
            pl.BlockSpec((tc, LANES), lambda i: (i, 0)),
            pl.BlockSpec((1, d), lambda i: (0, 0)),
            pl.BlockSpec(memory_space=pl.ANY),
        ],
        out_specs=pl.BlockSpec((tc, d), lambda i: (i, 0)),
        scratch_shapes=[pltpu.VMEM((2, 2, tc, d), F32), pltpu.SemaphoreType.DMA((2,))],
        compiler_params=pltpu.CompilerParams(dimension_semantics=("arbitrary",), vmem_limit_bytes=VMEM_LIMIT,
                                             disable_bounds_checks=True),
        name="combine_norm",
    )(pos3, pos3, x1, route, gf, y)


def _pad_lanes(v, offset):
    return jnp.zeros((1, LANES), F32).at[0, offset:offset + v.shape[0]].set(v.astype(F32))


def kernel(x, norm_mix_gain, w_in, diff_lambda_q1, diff_lambda_k1, diff_lambda_q2, diff_lambda_k2, diff_subln_gain, gdn_conv_w, gdn_a_log, gdn_dt_bias, gdn_norm_gain, w_branch_attn, w_branch_gdn, w_out, norm_ffn_gain, moe_w_group, moe_b_group, moe_w_expert, moe_b_expert, moe_w_gate, moe_w_up, moe_w_down, norm_final_gain):
    b, s, d = x.shape
    t = b * s
    x2 = x.reshape(t, d)

    w = w_in[0]
    n_attn = 3 * DIFF_WIDTH
    small_lo = n_attn + 3 * GDN_WIDTH
    small_hi = small_lo + 2 * GDN_HEADS
    w_attn = w[:, :n_attn].astype(BF16)
    w_rest = jnp.concatenate(
        [w[:, n_attn:small_lo], w[:, small_hi:], w[:, small_lo:small_hi],
         jnp.zeros((d, LANES - 2 * GDN_HEADS), w.dtype)], axis=1).astype(BF16)
    gain1 = norm_mix_gain[0].reshape(1, d)

    tm = min(1024, t)
    qkv = _inproj(x2, gain1, w_attn, BF16, tm, n_attn).reshape(b, s, n_attn)
    rest2 = _inproj(x2, gain1, w_rest, F32, tm, REST_WIDTH // 3)
    rest = rest2.reshape(b, s, REST_WIDTH)

    lam_params = jnp.stack([diff_lambda_q1[0], diff_lambda_k1[0], diff_lambda_q2[0], diff_lambda_k2[0]]).astype(F32)
    ya = _diff_attention(lam_params, qkv, diff_subln_gain[0].reshape(1, DIFF_V_DIM))

    yb = _gdn(rest, gdn_conv_w[0], _pad_lanes(gdn_a_log[0], GDN_HEADS), _pad_lanes(gdn_dt_bias[0], GDN_HEADS),
              gdn_norm_gain[0].reshape(1, GDN_DIM), tb=min(512, s))

    wr = jnp.concatenate([moe_w_group[0], moe_w_expert[0],
                          jnp.zeros((d, LANES - MOE_GROUPS - MOE_EXPERTS), F32)], axis=1)
    br = _pad_lanes(jnp.concatenate([moe_b_group[0], moe_b_expert[0]]), 0)
    x1, h2, route, cnt = _merge(x2, ya.reshape(t, DIFF_WIDTH), yb.reshape(t, GDN_WIDTH), rest2,
                                w_branch_attn[0].astype(BF16), w_branch_gdn[0].astype(BF16),
                                w_out[0].astype(BF16), norm_ffn_gain[0].reshape(1, d), wr, br, tm=min(512, t))

    pos, meta = _route_positions(route, cnt)
    ts = min(MOE_DMA_TILE, t)
    pos3 = pos[:, :2].reshape(t // ts, 1, 2 * ts)
    xs = _scatter_rows(pos3, h2)
    y = _experts(meta, xs, moe_w_gate[0], moe_w_up[0], moe_w_down[0])
    out = _combine(pos3, x1, route, norm_final_gain.reshape(1, d), y)
    return out.reshape(b, s, d)
```

```python
import functools
import math

import jax
import jax.numpy as jnp
from jax import lax
from jax.experimental import pallas as pl
from jax.experimental.pallas import tpu as pltpu

F32 = jnp.float32
BF16 = jnp.bfloat16

D_MODEL = 1024
CHUNK = 64
EPS = 1e-6

DIFF_HEADS = 4
DIFF_HEAD_DIM = 64
DIFF_V_DIM = 2 * DIFF_HEAD_DIM
DIFF_WIDTH = DIFF_HEADS * DIFF_V_DIM

GDN_HEADS = 4
GDN_DIM = 128
GDN_WIDTH = GDN_HEADS * GDN_DIM
CONV_K = 4

MOE_GROUPS = 4
MOE_EXPERTS_PER_GROUP = 8
MOE_EXPERTS = MOE_GROUPS * MOE_EXPERTS_PER_GROUP
MOE_HIDDEN = 256

LANES = 128
SUBLANES = 8
BF16_SUBLANES = 16
VMEM_LIMIT = 56 * 1024 * 1024

LAMBDA_INIT = 0.8 - 0.6 * math.exp(-0.3 * 0)

PROJ_ATTN = 0
PROJ_GDN = 3 * DIFF_WIDTH
PROJ_MA = PROJ_GDN + 3 * GDN_WIDTH
PROJ_MB = PROJ_MA + D_MODEL
PROJ_GATE = PROJ_MB + D_MODEL
PROJ_WIDTH = PROJ_GATE + GDN_WIDTH

NEG_INF = float("-inf")
LOG2_E = math.log2(math.e)


def _params(*sem):
    return pltpu.CompilerParams(dimension_semantics=sem, vmem_limit_bytes=VMEM_LIMIT)


def _inproj_kernel(x_ref, g_ref, w_ref, ws_ref, o_ref, os_ref, h_scr):
    @pl.when(pl.program_id(1) == 0)
    def _():
        x = x_ref[...]
        ms = jnp.mean(x * x, axis=-1, keepdims=True)
        h_scr[...] = (x * lax.rsqrt(ms + EPS) * g_ref[...]).astype(BF16)
        os_ref[...] = jnp.dot(h_scr[...], ws_ref[...], preferred_element_type=F32)

    o_ref[...] = jnp.dot(h_scr[...], w_ref[...], preferred_element_type=F32).astype(o_ref.dtype)


def _inproj(x2, gain, w, w_small, tm, tn):
    t, d = x2.shape
    n = w.shape[1]
    return pl.pallas_call(
        _inproj_kernel,
        out_shape=(jax.ShapeDtypeStruct((t, n), BF16), jax.ShapeDtypeStruct((t, LANES), F32)),
        grid=(t // tm, n // tn),
        in_specs=[
            pl.BlockSpec((tm, d), lambda i, j: (i, 0)),
            pl.BlockSpec((1, d), lambda i, j: (0, 0)),
            pl.BlockSpec((d, tn), lambda i, j: (0, j)),
            pl.BlockSpec((d, LANES), lambda i, j: (0, 0)),
        ],
        out_specs=(pl.BlockSpec((tm, tn), lambda i, j: (i, j)), pl.BlockSpec((tm, LANES), lambda i, j: (i, 0))),
        scratch_shapes=[pltpu.VMEM((tm, d), BF16)],
        compiler_params=_params("parallel", "arbitrary"),
        name="inproj",
    )(x2, gain, w, w_small)


ATTN_TQ = 512
ATTN_TK = 512
ATTN_ROWS = 64
MASK_BIAS = -1e30


def _attn_kernel(lam_ref, q_ref, k_ref, v_ref, gain_ref, o_ref,
                 qs_scr, s0_scr, s1_scr, p0_scr, p1_scr, a0_scr, a1_scr, m_scr, acc_scr, *, tq, tk):
    i = pl.program_id(2)
    m_rows = 2 * tq
    ratio = tk // tq
    t_last = i // ratio
    n_pairs = (t_last + 2) // 2

    q = q_ref[0].astype(F32) * (DIFF_HEAD_DIM ** -0.5 * LOG2_E)
    lane = lax.broadcasted_iota(jnp.int32, (tq, LANES), 1)
    row = lax.broadcasted_iota(jnp.int32, (tq, LANES), 0)
    qs_scr[0:tq, 0:LANES] = jnp.where(lane < DIFF_HEAD_DIM, q, 0.0).astype(BF16)
    qs_scr[tq:m_rows, 0:LANES] = jnp.where(lane >= DIFF_HEAD_DIM, q, 0.0).astype(BF16)
    q_slot = row // CHUNK + (i % ratio) * (tq // CHUNK)
    onehot = jnp.where(lane == q_slot, 1.0, 0.0).astype(BF16)
    qs_scr[0:tq, LANES:2 * LANES] = onehot
    qs_scr[tq:m_rows, LANES:2 * LANES] = onehot

    k_lane = lax.broadcasted_iota(jnp.int32, (tk, LANES), 1)
    k_chunk = lax.broadcasted_iota(jnp.int32, (tk, LANES), 0) // CHUNK
    n_slots = tk // CHUNK
    diag_bias = jnp.where((k_lane < n_slots) & (k_chunk > k_lane), MASK_BIAS, 0.0).astype(BF16)
    full_bias = jnp.where(k_lane < n_slots, MASK_BIAS, 0.0).astype(BF16)
    zero_bias = jnp.zeros((tk, LANES), BF16)
    ones_v = jnp.ones((tk, DIFF_V_DIM), BF16)

    m_scr[...] = jnp.full(m_scr.shape, NEG_INF, F32)
    acc_scr[...] = jnp.zeros(acc_scr.shape, F32)
    p1_scr[...] = jnp.zeros(p1_scr.shape, BF16)
    a1_scr[...] = jnp.ones(a1_scr.shape, F32)

    def key_rows(t):
        return pl.ds(pl.multiple_of(jnp.minimum(t, t_last) * tk, tk), tk)

    def scores(t, s_scr):
        bias = jnp.where(t == t_last, diag_bias, jnp.where(t > t_last, full_bias, zero_bias))
        k_aug = jnp.concatenate([k_ref[0, key_rows(t), :], bias], axis=1)
        s_scr[...] = lax.dot_general(qs_scr[...], k_aug, (((1,), (1,)), ((), ())), preferred_element_type=F32)

    def softmax(s_scr, p_scr, a_scr):
        for r0 in range(0, m_rows, ATTN_ROWS):
            rs = slice(r0, r0 + ATTN_ROWS)
            s = s_scr[rs, :]
            tiles = [s[:, c * LANES:(c + 1) * LANES] for c in range(tk // LANES)]
            mx = functools.reduce(jnp.maximum, tiles)
            m_prev = m_scr[rs, :]
            m_new = jnp.maximum(m_prev, jnp.max(mx, axis=-1, keepdims=True))
            a_scr[rs, :] = jnp.exp2(m_prev - m_new)
            m_scr[rs, :] = m_new
            p_scr[rs, :] = jnp.concatenate([jnp.exp2(tl - m_new) for tl in tiles], axis=1).astype(BF16)

    def values(t, p_scr, a_scr):
        v_ext = jnp.concatenate([v_ref[0, key_rows(t), :], ones_v], axis=1)
        pv = jnp.dot(p_scr[...], v_ext, preferred_element_type=F32)
        alpha = a_scr[...]
        acc_scr[...] = jnp.concatenate([alpha, alpha], axis=1) * acc_scr[...] + pv

    scores(0, s0_scr)

    def pair_step(g, carry):
        t0 = 2 * g
        scores(t0 + 1, s1_scr)
        softmax(s0_scr, p0_scr, a0_scr)
        values(jnp.maximum(t0 - 1, 0), p1_scr, a1_scr)
        scores(t0 + 2, s0_scr)
        softmax(s1_scr, p1_scr, a1_scr)
        values(t0, p0_scr, a0_scr)
        return carry

    lax.fori_loop(0, n_pairs, pair_step, 0)
    values(2 * n_pairs - 1, p1_scr, a1_scr)

    lp = lam_ref[...]
    lam = (jnp.exp(jnp.sum(lp[0:1] * lp[1:2], axis=-1, keepdims=True))
           - jnp.exp(jnp.sum(lp[2:3] * lp[3:4], axis=-1, keepdims=True)) + LAMBDA_INIT)
    o0 = acc_scr[0:tq, 0:DIFF_V_DIM] / acc_scr[0:tq, DIFF_V_DIM:2 * DIFF_V_DIM]
    o1 = acc_scr[tq:m_rows, 0:DIFF_V_DIM] / acc_scr[tq:m_rows, DIFF_V_DIM:2 * DIFF_V_DIM]
    o = o0 - lam * o1
    ms = jnp.mean(o * o, axis=-1, keepdims=True)
    y = (o * lax.rsqrt(ms + EPS) * gain_ref[...]) * (1.0 - LAMBDA_INIT)
    o_ref[0] = y.astype(o_ref.dtype)


def _diff_attention(lam_params, qkv, subln_gain):
    b, s, _ = qkv.shape
    tq = min(ATTN_TQ, s)
    tk = min(ATTN_TK, s)
    assert s % tk == 0 and tk % tq == 0 and tq % CHUNK == 0 and tk // CHUNK <= LANES
    m_rows = 2 * tq
    return pl.pallas_call(
        functools.partial(_attn_kernel, tq=tq, tk=tk),
        out_shape=jax.ShapeDtypeStruct((b, s, DIFF_WIDTH), BF16),
        grid=(b, DIFF_HEADS, s // tq),
        in_specs=[
            pl.BlockSpec((4, DIFF_HEAD_DIM), lambda bi, h, i: (0, 0)),
            pl.BlockSpec((1, tq, DIFF_V_DIM), lambda bi, h, i: (bi, i, h)),
            pl.BlockSpec((1, s, DIFF_V_DIM), lambda bi, h, i: (bi, 0, DIFF_HEADS + h)),
            pl.BlockSpec((1, s, DIFF_V_DIM), lambda bi, h, i: (bi, 0, 2 * DIFF_HEADS + h)),
            pl.BlockSpec((1, DIFF_V_DIM), lambda bi, h, i: (0, 0)),
        ],
        out_specs=pl.BlockSpec((1, tq, DIFF_V_DIM), lambda bi, h, i: (bi, i, h)),
        scratch_shapes=[
            pltpu.VMEM((m_rows, 2 * LANES), BF16),
            pltpu.VMEM((m_rows, tk), F32),
            pltpu.VMEM((m_rows, tk), F32),
            pltpu.VMEM((m_rows, tk), BF16),
            pltpu.VMEM((m_rows, tk), BF16),
            pltpu.VMEM((m_rows, LANES), F32),
            pltpu.VMEM((m_rows, LANES), F32),
            pltpu.VMEM((m_rows, LANES), F32),
            pltpu.VMEM((m_rows, 2 * DIFF_V_DIM), F32),
        ],
        compiler_params=_params("parallel", "parallel", "arbitrary"),
        name="diff_attn",
    )(lam_params, qkv, qkv, qkv, subln_gain)


def _silu(x):
    return x * (1.0 / (1.0 + jnp.exp(-x)))


def _sigmoid(x):
    return 1.0 / (1.0 + jnp.exp(-x))


def _softplus(x):
    return jnp.maximum(x, 0.0) + jnp.log(1.0 + jnp.exp(-jnp.abs(x)))


def _dot(a, b):
    return jnp.dot(a.astype(BF16), b.astype(BF16), preferred_element_type=F32)


def _dot_nt(a, b):
    return lax.dot_general(a.astype(BF16), b.astype(BF16), (((1,), (1,)), ((), ())), preferred_element_type=F32)


def _gdn_kernel(prev_ref, qkv_ref, gate_ref, small_ref, convw_ref, alog_ref, dtb_ref, ngain_ref, o_ref,
                xp_scr, q_scr, k_scr, v_scr, pq_scr, n_scr, oc_scr, gl_scr, oraw_scr, state_scr, *, tb):
    i = pl.program_id(1)

    @pl.when(i == 0)
    def _():
        state_scr[...] = jnp.zeros(state_scr.shape, F32)

    prev = prev_ref[0, BF16_SUBLANES - SUBLANES:BF16_SUBLANES, :].astype(F32)
    xp_scr[0:SUBLANES, :] = jnp.where(i == 0, 0.0, prev)
    xp_scr[SUBLANES:SUBLANES + tb, :] = qkv_ref[0].astype(F32)
    for sec in range(3 * GDN_HEADS):
        cols = slice(sec * LANES, (sec + 1) * LANES)
        acc = None
        for jj in range(CONV_K):
            start = SUBLANES - (CONV_K - 1) + jj
            term = xp_scr[start:start + tb, cols] * convw_ref[jj:jj + 1, cols]
            acc = term if acc is None else acc + term
        y = _silu(acc)
        which, head = divmod(sec, GDN_HEADS)
        hc = slice(head * LANES, (head + 1) * LANES)
        if which == 0:
            q_scr[:, hc] = y * lax.rsqrt(jnp.sum(y * y, axis=-1, keepdims=True) + EPS) * (GDN_DIM ** -0.5)
        elif which == 1:
            k_scr[:, hc] = y * lax.rsqrt(jnp.sum(y * y, axis=-1, keepdims=True) + EPS)
        else:
            v_scr[:, hc] = y

    sm = small_ref[0]
    beta_all = _sigmoid(sm)
    g_all = -jnp.exp(alog_ref[...]) * _softplus(sm + dtb_ref[...])

    br = lax.broadcasted_iota(jnp.int32, (tb, tb), 0)
    bc = lax.broadcasted_iota(jnp.int32, (tb, tb), 1)
    block_tril = jnp.where((br // CHUNK == bc // CHUNK) & (bc <= br), 1.0, 0.0)
    gc_all = jnp.dot(block_tril, g_all, preferred_element_type=F32, precision=lax.Precision.HIGHEST)
    gct_all = jnp.transpose(gc_all)

    ur = lax.broadcasted_iota(jnp.int32, (GDN_UNIT, GDN_UNIT), 0)
    uc = lax.broadcasted_iota(jnp.int32, (GDN_UNIT, GDN_UNIT), 1)
    same = (ur // CHUNK) == (uc // CHUNK)
    causal = same & (uc <= ur)
    strict = same & (uc < ur)
    eye = jnp.where(ur == uc, 1.0, 0.0)
    first_rows = ur < CHUNK
    first_cols = uc < CHUNK
    n_units = tb // GDN_UNIT
    units = [(h, u) for h in range(GDN_HEADS) for u in range(n_units)]

    def lane_bcast(x, lane):
        return jnp.broadcast_to(x[:, lane:lane + 1], (GDN_UNIT, GDN_UNIT))

    def row_bcast(x, r):
        return jnp.broadcast_to(x[r:r + 1, :], (GDN_UNIT, GDN_UNIT))

    def rows_of(u):
        return slice(u * GDN_UNIT, (u + 1) * GDN_UNIT)

    def cols_of(h):
        return slice(h * LANES, (h + 1) * LANES)

    qs = [q_scr[rows_of(u), cols_of(h)] for h, u in units]
    ks = [k_scr[rows_of(u), cols_of(h)] for h, u in units]
    vs = [v_scr[rows_of(u), cols_of(h)] for h, u in units]
    betas = [lane_bcast(beta_all[rows_of(u)], h) for h, u in units]
    gcb = [lane_bcast(gc_all[rows_of(u)], GDN_HEADS + h) for h, u in units]
    gcr = [row_bcast(gct_all[:, rows_of(u)], GDN_HEADS + h) for h, u in units]
    decay = [jnp.exp(jnp.where(causal, a - b, NEG_INF)) for a, b in zip(gcb, gcr)]
    eg = [jnp.exp(a) for a in gcb]
    g_last = [jnp.where(first_rows, row_bcast(a, CHUNK - 1), row_bcast(a, GDN_UNIT - 1)) for a in gcb]
    kb = [k * b for k, b in zip(ks, betas)]
    vb = [v * b for v, b in zip(vs, betas)]
    a_low = [jnp.where(strict, _dot_nt(x, k) * d, 0.0) for x, k, d in zip(kb, ks, decay)]
    tinv = [eye - a for a in a_low]
    pw = a_low
    for _ in range(5):
        pw = [_dot(x, x) for x in pw]
        tinv = [t + _dot(t, x) for t, x in zip(tinv, pw)]
    uw = [_dot(t, jnp.concatenate([v, x * e], axis=1)) for t, v, x, e in zip(tinv, vb, kb, eg)]
    intra = [_dot_nt(q, k) * d for q, k, d in zip(qs, ks, decay)]
    iuw = [_dot(a, x) for a, x in zip(intra, uw)]
    k_dec_t = [jnp.transpose(k * jnp.exp(gl - a)) for k, gl, a in zip(ks, g_last, gcb)]
    wu = [jnp.concatenate([x[:, GDN_DIM:], x[:, :GDN_DIM]], axis=1) for x in uw]
    pn0 = [_dot(jnp.where(first_cols, kt, 0.0), x) for kt, x in zip(k_dec_t, wu)]
    pn1 = [_dot(jnp.where(first_cols, 0.0, kt), x) for kt, x in zip(k_dec_t, wu)]
    for n in range(len(units)):
        q_eff = (qs[n] * eg[n] - iuw[n][:, GDN_DIM:]).astype(BF16)
        for half, pn in enumerate((pn0[n], pn1[n])):
            hr = slice(half * CHUNK, (half + 1) * CHUNK)
            pq_scr[n, half, 0:GDN_DIM, :] = pn[:, :GDN_DIM].astype(BF16)
            pq_scr[n, half, GDN_DIM:GDN_DIM + CHUNK, :] = q_eff[hr]
            n_scr[n, half] = pn[:, GDN_DIM:]
            oc_scr[n, half] = iuw[n][hr, :GDN_DIM]
            gl_scr[n, half] = jnp.exp(gcb[n][(half + 1) * CHUNK - 1:(half + 1) * CHUNK, :])

    def chunk_body(c, carry):
        u = c // 2
        half = c % 2
        rows = pl.ds(pl.multiple_of(c * CHUNK, CHUNK), CHUNK)
        states = [state_scr[h] for h in range(GDN_HEADS)]
        res = [jnp.dot(pq_scr[h * n_units + u, half], states[h].astype(BF16), preferred_element_type=F32)
               for h in range(GDN_HEADS)]
        for h in range(GDN_HEADS):
            n = h * n_units + u
            state_scr[h] = gl_scr[n, half] * states[h] - res[h][0:GDN_DIM] + n_scr[n, half]
            oraw_scr[h, rows, :] = res[h][GDN_DIM:GDN_DIM + CHUNK] + oc_scr[n, half]
        return carry

    lax.fori_loop(0, tb // CHUNK, chunk_body, 0)

    for h in range(GDN_HEADS):
        o = oraw_scr[h]
        ms = jnp.mean(o * o, axis=-1, keepdims=True)
        y = o * lax.rsqrt(ms + EPS) * ngain_ref[...] * _silu(gate_ref[0, :, cols_of(h)].astype(F32))
        o_ref[0, :, cols_of(h)] = y.astype(o_ref.dtype)


GDN_UNIT = 2 * CHUNK


def _gdn(proj, small, conv_w, alog_pad, dtb_pad, norm_gain, tb):
    b, s, _ = proj.shape
    nb = s // tb
    w3 = 3 * GDN_WIDTH
    assert tb % GDN_UNIT == 0
    n_hu = GDN_HEADS * (tb // GDN_UNIT)
    return pl.pallas_call(
        functools.partial(_gdn_kernel, tb=tb),
        out_shape=jax.ShapeDtypeStruct((b, s, GDN_WIDTH), BF16),
        grid=(b, nb),
        in_specs=[
            pl.BlockSpec((1, BF16_SUBLANES, w3),
                         lambda bi, i: (bi, jnp.maximum(i * (tb // BF16_SUBLANES) - 1, 0), PROJ_GDN // w3)),
            pl.BlockSpec((1, tb, w3), lambda bi, i: (bi, i, PROJ_GDN // w3)),
            pl.BlockSpec((1, tb, GDN_WIDTH), lambda bi, i: (bi, i, PROJ_GATE // GDN_WIDTH)),
            pl.BlockSpec((1, tb, LANES), lambda bi, i: (bi, i, 0)),
            pl.BlockSpec((CONV_K, w3), lambda bi, i: (0, 0)),
            pl.BlockSpec((1, LANES), lambda bi, i: (0, 0)),
            pl.BlockSpec((1, LANES), lambda bi, i: (0, 0)),
            pl.BlockSpec((1, GDN_DIM), lambda bi, i: (0, 0)),
        ],
        out_specs=pl.BlockSpec((1, tb, GDN_WIDTH), lambda bi, i: (bi, i, 0)),
        scratch_shapes=[
            pltpu.VMEM((tb + SUBLANES, w3), F32),
            pltpu.VMEM((tb, GDN_WIDTH), F32),
            pltpu.VMEM((tb, GDN_WIDTH), F32),
            pltpu.VMEM((tb, GDN_WIDTH), F32),
            pltpu.VMEM((n_hu, 2, GDN_DIM + CHUNK, GDN_DIM), BF16),
            pltpu.VMEM((n_hu, 2, GDN_DIM, GDN_DIM), F32),
            pltpu.VMEM((n_hu, 2, CHUNK, GDN_DIM), F32),
            pltpu.VMEM((n_hu, 2, 1, GDN_DIM), F32),
            pltpu.VMEM((GDN_HEADS, tb, GDN_DIM), F32),
            pltpu.VMEM((GDN_HEADS, GDN_DIM, GDN_DIM), F32),
        ],
        compiler_params=_params("parallel", "arbitrary"),
        name="gdn",
    )(proj, proj, proj, small, conv_w, alog_pad, dtb_pad, norm_gain)


def _merge_kernel(x_ref, ya_ref, yb_ref, ma_ref, mb_ref, wa_ref, wb_ref, wo_ref, g2_ref, wr_ref, br_ref,
                  x1_ref, h2_ref, route_ref, cnt_ref):
    tm = x_ref.shape[0]
    subs = [slice(r, r + MERGE_SUB) for r in range(0, tm, MERGE_SUB)]
    pa = [jnp.dot(ya_ref[rs, :], wa_ref[...], preferred_element_type=F32) for rs in subs]
    pb = [jnp.dot(yb_ref[rs, :], wb_ref[...], preferred_element_type=F32) for rs in subs]
    merged = [(_sigmoid(ma_ref[rs, :].astype(F32)) * a + _sigmoid(mb_ref[rs, :].astype(F32)) * b).astype(BF16)
              for rs, a, b in zip(subs, pa, pb)]
    x1s = [x_ref[rs, :] + jnp.dot(m, wo_ref[...], preferred_element_type=F32) for rs, m in zip(subs, merged)]
    h2s = [v * lax.rsqrt(jnp.mean(v * v, axis=-1, keepdims=True) + EPS) * g2_ref[...] for v in x1s]
    logit_s = [jnp.dot(v, wr_ref[...], preferred_element_type=F32, precision=lax.Precision.HIGHEST) for v in h2s]
    for rs, v, hh in zip(subs, x1s, h2s):
        x1_ref[rs, :] = v
        h2_ref[rs, :] = hh
    logits = jnp.concatenate(logit_s, axis=0) + br_ref[...]
    lane = lax.broadcasted_iota(jnp.int32, logits.shape, 1)
    big = jnp.int32(4 * LANES)
    gl = jnp.where(lane < MOE_GROUPS, logits, NEG_INF)
    gmax = jnp.max(gl, axis=-1, keepdims=True)
    gidx = jnp.min(jnp.where(gl == gmax, lane, big), axis=-1, keepdims=True)
    grp_w = 1.0 / jnp.sum(jnp.exp(gl - gmax), axis=-1, keepdims=True)
    lo = MOE_GROUPS + gidx * MOE_EXPERTS_PER_GROUP
    el = jnp.where((lane >= lo) & (lane < lo + MOE_EXPERTS_PER_GROUP), logits, NEG_INF)
    v1 = jnp.max(el, axis=-1, keepdims=True)
    i1 = jnp.min(jnp.where(el == v1, lane, big), axis=-1, keepdims=True)
    el2 = jnp.where(lane == i1, NEG_INF, el)
    v2 = jnp.max(el2, axis=-1, keepdims=True)
    i2 = jnp.min(jnp.where(el2 == v2, lane, big), axis=-1, keepdims=True)
    e2 = jnp.exp(v2 - v1)
    w1 = grp_w / (1.0 + e2)
    w2 = w1 * e2
    id1 = i1 - MOE_GROUPS
    id2 = i2 - MOE_GROUPS
    route_ref[...] = jnp.where(lane == 0, id1.astype(F32), jnp.where(lane == 1, id2.astype(F32),
                               jnp.where(lane == 2, w1, jnp.where(lane == 3, w2, 0.0))))

    @pl.when(pl.program_id(0) == 0)
    def _():
        cnt_ref[...] = jnp.zeros(cnt_ref.shape, F32)

    hits = jnp.where((lane == id1) | (lane == id2), 1.0, 0.0)
    cnt_ref[...] += jnp.broadcast_to(jnp.sum(hits, axis=0, keepdims=True), cnt_ref.shape)


MERGE_SUB = 256


def _merge(x2, ya, yb, proj, wa, wb, wo, g2, wr, br, tm):
    t, d = x2.shape
    row = lambda i: (i, 0)
    const = lambda i: (0, 0)
    return pl.pallas_call(
        _merge_kernel,
        out_shape=(
            jax.ShapeDtypeStruct((t, d), F32),
            jax.ShapeDtypeStruct((t, d), F32),
            jax.ShapeDtypeStruct((t, LANES), F32),
            jax.ShapeDtypeStruct((SUBLANES, LANES), F32),
        ),
        grid=(t // tm,),
        in_specs=[
            pl.BlockSpec((tm, d), row),
            pl.BlockSpec((tm, DIFF_WIDTH), row),
            pl.BlockSpec((tm, GDN_WIDTH), row),
            pl.BlockSpec((tm, d), lambda i: (i, PROJ_MA // D_MODEL)),
            pl.BlockSpec((tm, d), lambda i: (i, PROJ_MB // D_MODEL)),
            pl.BlockSpec((DIFF_WIDTH, d), const),
            pl.BlockSpec((GDN_WIDTH, d), const),
            pl.BlockSpec((d, d), const),
            pl.BlockSpec((1, d), const),
            pl.BlockSpec((d, LANES), const),
            pl.BlockSpec((1, LANES), const),
        ],
        out_specs=(pl.BlockSpec((tm, d), row), pl.BlockSpec((tm, d), row), pl.BlockSpec((tm, LANES), row),
                   pl.BlockSpec((SUBLANES, LANES), const)),
        compiler_params=_params("arbitrary"),
        name="merge_router",
    )(x2, ya, yb, proj, proj, wa, wb, wo, g2, wr, br)


MOE_BLK = 256
MOE_META_LANES = 256
MOE_ROUTE_TILE = 512
MOE_DMA_TILE = 256


def _route_kernel(route_ref, cnt_ref, pos_ref, meta_ref, run_scr, *, n_rows):
    i = pl.program_id(0)
    tp = route_ref.shape[0]

    @pl.when(i == 0)
    def _():
        cnt = cnt_ref[...]
        jr = lax.broadcasted_iota(jnp.int32, (LANES, LANES), 0)
        jc = lax.broadcasted_iota(jnp.int32, (LANES, LANES), 1)
        upper = jnp.where(jr < jc, 1.0, 0.0)
        hi_prec = dict(preferred_element_type=F32, precision=lax.Precision.HIGHEST)
        off = jnp.dot(cnt, upper, **hi_prec)
        run_scr[...] = off[0:1]
        blk = float(MOE_BLK)
        first_tile = jnp.floor(off / blk)
        last_tile = jnp.floor((off + cnt - 1.0) / blk)
        n_it = jnp.where(cnt > 0.0, last_tile - first_tile + 1.0, 0.0)
        it_start = jnp.dot(n_it, upper, **hi_prec)
        it_end = it_start + n_it
        lane8 = lax.broadcasted_iota(jnp.int32, cnt.shape, 1)
        e_max = jnp.max(jnp.where(cnt > 0.0, lane8, 0), axis=-1, keepdims=True).astype(F32)[0:1]
        sub8 = lax.broadcasted_iota(jnp.int32, cnt.shape, 0)
        table = jnp.where(sub8 == 0, first_tile, jnp.where(sub8 == 1, it_start, jnp.where(
            sub8 == 2, off, jnp.where(sub8 == 3, cnt, it_end))))
        cols = jnp.transpose(table)
        shape = (LANES, MOE_META_LANES)
        e_sub = lax.broadcasted_iota(jnp.int32, shape, 0)
        w_lane = lax.broadcasted_iota(jnp.int32, shape, 1).astype(F32)
        col = lambda k: jnp.broadcast_to(cols[:, k:k + 1], shape)
        e_w = jnp.sum(jnp.where((e_sub < MOE_EXPERTS) & (col(4) <= w_lane), 1.0, 0.0), axis=0, keepdims=True)
        valid = e_w < float(MOE_EXPERTS)
        e_w = jnp.minimum(e_w, e_max)
        sel = e_sub.astype(F32) == e_w
        pick = lambda k: jnp.sum(jnp.where(sel, col(k), 0.0), axis=0, keepdims=True)
        w_row = w_lane[0:1]
        tile_w = jnp.where(valid, pick(0) + (w_row - pick(1)), float(n_rows // MOE_BLK - 1))
        lo_w = jnp.maximum(pick(2) - tile_w * blk, 0.0)
        hi_w = jnp.minimum(pick(2) + pick(3) - tile_w * blk, blk)
        lo_w = jnp.where(valid, lo_w, 0.0)
        hi_w = jnp.where(valid, hi_w, 0.0)
        sub_m = lax.broadcasted_iota(jnp.int32, meta_ref.shape, 0)
        bc = lambda v: jnp.broadcast_to(v, meta_ref.shape)
        meta_ref[...] = jnp.where(sub_m == 0, bc(e_w), jnp.where(sub_m == 1, bc(tile_w), jnp.where(
            sub_m == 2, bc(lo_w), bc(hi_w)))).astype(jnp.int32)

    r = route_ref[...]
    lane = lax.broadcasted_iota(jnp.int32, r.shape, 1)
    lane_f = lane.astype(F32)
    oh1 = lane_f == r[:, 0:1]
    oh2 = lane_f == r[:, 1:2]
    hits = jnp.where(oh1 | oh2, 1.0, 0.0)
    tr = lax.broadcasted_iota(jnp.int32, (tp, tp), 0)
    tc = lax.broadcasted_iota(jnp.int32, (tp, tp), 1)
    earlier = jnp.where(tc < tr, 1.0, 0.0).astype(BF16)
    rank = jnp.dot(earlier, hits.astype(BF16), preferred_element_type=F32)
    base = run_scr[...] + rank
    p1 = jnp.sum(jnp.where(oh1, base, 0.0), axis=-1, keepdims=True)
    p2 = jnp.sum(jnp.where(oh2, base, 0.0), axis=-1, keepdims=True)
    pos_ref[...] = jnp.where(lane == 0, p1, jnp.where(lane == 1, p2, 0.0)).astype(jnp.int32)
    run_scr[...] += jnp.sum(hits, axis=0, keepdims=True)


def _route_positions(route, cnt):
    t = route.shape[0]
    tp = min(MOE_ROUTE_TILE, t)
    n_rows = 2 * t
    assert n_rows % MOE_BLK == 0 and n_rows // MOE_BLK + MOE_EXPERTS <= MOE_META_LANES and n_rows < 2 ** 24
    return pl.pallas_call(
        functools.partial(_route_kernel, n_rows=n_rows),
        out_shape=(jax.ShapeDtypeStruct((t, LANES), jnp.int32),
                   jax.ShapeDtypeStruct((SUBLANES, MOE_META_LANES), jnp.int32)),
        grid=(t // tp,),
        in_specs=[pl.BlockSpec((tp, LANES), lambda i: (i, 0)),
                  pl.BlockSpec((SUBLANES, LANES), lambda i: (0, 0))],
        out_specs=(pl.BlockSpec((tp, LANES), lambda i: (i, 0)),
                   pl.BlockSpec((SUBLANES, MOE_META_LANES), lambda i: (0, 0))),
        scratch_shapes=[pltpu.VMEM((1, LANES), F32)],
        compiler_params=_params("arbitrary"),
        name="route_positions",
    )(route, cnt)


def _scatter_kernel(pos_ref, h2_ref, xs_ref, sem):
    ts = h2_ref.shape[0]

    def body(tok, carry):
        src = h2_ref.at[pl.ds(tok, 1), :]
        for k in range(2):
            pltpu.make_async_copy(src, xs_ref.at[pl.ds(pos_ref[0, 0, 2 * tok + k], 1), :], sem).start(priority=k)
        return carry

    lax.fori_loop(0, ts, body, 0, unroll=8)
    for _ in range(2):
        pltpu.make_async_copy(h2_ref, xs_ref.at[pl.ds(0, ts), :], sem).wait()


def _scatter_rows(pos3, h2):
    t, d = h2.shape
    ts = pos3.shape[2] // 2
    return pl.pallas_call(
        _scatter_kernel,
        out_shape=jax.ShapeDtypeStruct((2 * t, d), F32),
        grid=(t // ts,),
        in_specs=[pl.BlockSpec((1, 1, 2 * ts), lambda i: (i, 0, 0), memory_space=pltpu.SMEM),
                  pl.BlockSpec((ts, d), lambda i: (i, 0))],
        out_specs=pl.BlockSpec(memory_space=pl.ANY),
        scratch_shapes=[pltpu.SemaphoreType.DMA],
        compiler_params=pltpu.CompilerParams(dimension_semantics=("arbitrary",), vmem_limit_bytes=VMEM_LIMIT,
                                             disable_bounds_checks=True),
        name="scatter_rows",
    )(pos3, h2)


def _expert_kernel(ie_ref, it_ref, lo_ref, hi_ref, xs_ref, wg_ref, wu_ref, wd_ref, y_ref, wg_scr, wu_scr, wd_scr):
    w = pl.program_id(0)
    prev = jnp.maximum(w - 1, 0)

    @pl.when((w == 0) | (ie_ref[w] != ie_ref[prev]))
    def _():
        wg_scr[...] = wg_ref[0].astype(BF16)
        wu_scr[...] = wu_ref[0].astype(BF16)
        wd_scr[...] = wd_ref[0].astype(BF16)

    @pl.when((w == 0) | (it_ref[w] != it_ref[prev]))
    def _():
        y_ref[...] = jnp.zeros(y_ref.shape, F32)

    lo = lo_ref[w]
    hi = hi_ref[w]

    @pl.when(hi > lo)
    def _():
        x = xs_ref[...].astype(BF16)
        hg = jnp.dot(x, wg_scr[...], preferred_element_type=F32)
        hu = jnp.dot(x, wu_scr[...], preferred_element_type=F32)
        act = (_silu(hg) * hu).astype(BF16)
        yp = jnp.dot(act, wd_scr[...], preferred_element_type=F32)
        row = lax.broadcasted_iota(jnp.int32, yp.shape, 0)
        y_ref[...] += jnp.where((row >= lo) & (row < hi), yp, 0.0)


def _experts(meta, xs, wg, wu, wd):
    n_rows, d = xs.shape
    n_items = n_rows // MOE_BLK + MOE_EXPERTS
    ie, it, lo, hi = (meta[k, :n_items] for k in range(4))
    return pl.pallas_call(
        _expert_kernel,
        out_shape=jax.ShapeDtypeStruct((n_rows, d), F32),
        grid_spec=pltpu.PrefetchScalarGridSpec(
            num_scalar_prefetch=4,
            grid=(n_items,),
            in_specs=[
                pl.BlockSpec((MOE_BLK, d), lambda w, ie, it, lo, hi: (it[w], 0)),
                pl.BlockSpec((1, d, MOE_HIDDEN), lambda w, ie, it, lo, hi: (ie[w], 0, 0)),
                pl.BlockSpec((1, d, MOE_HIDDEN), lambda w, ie, it, lo, hi: (ie[w], 0, 0)),
                pl.BlockSpec((1, MOE_HIDDEN, d), lambda w, ie, it, lo, hi: (ie[w], 0, 0)),
            ],
            out_specs=pl.BlockSpec((MOE_BLK, d), lambda w, ie, it, lo, hi: (it[w], 0)),
            scratch_shapes=[pltpu.VMEM((d, MOE_HIDDEN), BF16), pltpu.VMEM((d, MOE_HIDDEN), BF16),
                            pltpu.VMEM((MOE_HIDDEN, d), BF16)],
        ),
        compiler_params=_params("arbitrary"),
        name="experts",
    )(ie, it, lo, hi, xs, wg, wu, wd)


def _combine_kernel(pos_ref, posn_ref, x1_ref, route_ref, gf_ref, y_ref, o_ref, ybuf, sem):
    i = pl.program_id(0)
    n = pl.num_programs(0)
    tc = x1_ref.shape[0]
    slot = i % 2

    def issue(p_ref, s):
        def body(tok, carry):
            for k in range(2):
                pltpu.make_async_copy(y_ref.at[pl.ds(p_ref[0, 0, 2 * tok + k], 1), :],
                                      ybuf.at[s, k, pl.ds(tok, 1), :], sem.at[s]).start(priority=k)
            return carry

        lax.fori_loop(0, tc, body, 0, unroll=8)

    @pl.when(i == 0)
    def _():
        issue(pos_ref, 0)

    @pl.when(i + 1 < n)
    def _():
        issue(posn_ref, 1 - slot)

    for k in range(2):
        pltpu.make_async_copy(y_ref.at[pl.ds(0, tc), :], ybuf.at[slot, k], sem.at[slot]).wait()
    r = route_ref[...]
    x2 = x1_ref[...] + r[:, 2:3] * ybuf[slot, 0] + r[:, 3:4] * ybuf[slot, 1]
    ms = jnp.mean(x2 * x2, axis=-1, keepdims=True)
    o_ref[...] = x2 * lax.rsqrt(ms + EPS) * gf_ref[...]


def _combine(pos3, x1, route, gf, y):
    t, d = x1.shape
    tc = pos3.shape[2] // 2
    n = t // tc
    return pl.pallas_call(
        _combine_kernel,
        out_shape=jax.ShapeDtypeStruct((t, d), F32),
        grid=(n,),
        in_specs=[
            pl.BlockSpec((1, 1, 2 * tc), lambda i: (i, 0, 0), memory_space=pltpu.SMEM),
            pl.BlockSpec((1, 1, 2 * tc), lambda i: (jnp.minimum(i + 1, n - 1), 0, 0), memory_space=pltpu.SMEM),
            pl.BlockSpec((tc, d), lambda i: (i, 0)),
            pl.BlockSpec((tc, LANES), lambda i: (i, 0)),
            pl.BlockSpec((1, d), lambda i: (0, 0)),
            pl.BlockSpec(memory_space=pl.ANY),
        ],
        out_specs=pl.BlockSpec((tc, d), lambda i: (i, 0)),
        scratch_shapes=[pltpu.VMEM((2, 2, tc, d), F32), pltpu.SemaphoreType.DMA((2,))],
        compiler_params=pltpu.CompilerParams(dimension_semantics=("arbitrary",), vmem_limit_bytes=VMEM_LIMIT,
                                             disable_bounds_checks=True),
        name="combine_norm",
    )(pos3, pos3, x1, route, gf, y)


def _pad_lanes(v, offset):
    return jnp.zeros((1, LANES), F32).at[0, offset:offset + v.shape[0]].set(v.astype(F32))


def kernel(x, norm_mix_gain, w_in, diff_lambda_q1, diff_lambda_k1, diff_lambda_q2, diff_lambda_k2, diff_subln_gain, gdn_conv_w, gdn_a_log, gdn_dt_bias, gdn_norm_gain, w_branch_attn, w_branch_gdn, w_out, norm_ffn_gain, moe_w_group, moe_b_group, moe_w_expert, moe_b_expert, moe_w_gate, moe_w_up, moe_w_down, norm_final_gain):
    b, s, d = x.shape
    t = b * s
    x2 = x.reshape(t, d)

    w = w_in[0]
    small_lo = PROJ_GDN + 3 * GDN_WIDTH
    small_hi = small_lo + 2 * GDN_HEADS
    gate_hi = small_hi + GDN_WIDTH
    w_main = jnp.concatenate([w[:, :small_lo], w[:, gate_hi:], w[:, small_hi:gate_hi]], axis=1).astype(BF16)
    w_small = jnp.concatenate([w[:, small_lo:small_hi], jnp.zeros((d, LANES - 2 * GDN_HEADS), w.dtype)],
                              axis=1).astype(BF16)
    gain1 = norm_mix_gain[0].reshape(1, d)

    proj2, small2 = _inproj(x2, gain1, w_main, w_small, min(1024, t), PROJ_WIDTH // 4)
    proj = proj2.reshape(b, s, PROJ_WIDTH)

    lam_params = jnp.stack([diff_lambda_q1[0], diff_lambda_k1[0], diff_lambda_q2[0], diff_lambda_k2[0]]).astype(F32)
    ya = _diff_attention(lam_params, proj, diff_subln_gain[0].reshape(1, DIFF_V_DIM))

    yb = _gdn(proj, small2.reshape(b, s, LANES), gdn_conv_w[0], _pad_lanes(gdn_a_log[0], GDN_HEADS),
              _pad_lanes(gdn_dt_bias[0], GDN_HEADS), gdn_norm_gain[0].reshape(1, GDN_DIM), tb=min(512, s))

    wr = jnp.concatenate([moe_w_group[0], moe_w_expert[0],
                          jnp.zeros((d, LANES - MOE_GROUPS - MOE_EXPERTS), F32)], axis=1)
    br = _pad_lanes(jnp.concatenate([moe_b_group[0], moe_b_expert[0]]), 0)
    x1, h2, route, cnt = _merge(x2, ya.reshape(t, DIFF_WIDTH), yb.reshape(t, GDN_WIDTH), proj2,
                                w_branch_attn[0].astype(BF16), w_branch_gdn[0].astype(BF16),
                                w_out[0].astype(BF16), norm_ffn_gain[0].reshape(1, d), wr, br, tm=min(512, t))

    pos, meta = _route_positions(route, cnt)
    ts = min(MOE_DMA_TILE, t)
    pos3 = pos[:, :2].reshape(t // ts, 1, 2 * ts)
    xs = _scatter_rows(pos3, h2)
    y = _experts(meta, xs, moe_w_gate[0], moe_w_up[0], moe_w_down[0])
    out = _combine(pos3, x1, route, norm_final_gain.reshape(1, d), y)
    return out.reshape(b, s, d)
```

```python
import functools
import math

import jax
import jax.numpy as jnp
from jax import lax
from jax.experimental import pallas as pl
from jax.experimental.pallas import tpu as pltpu

F32 = jnp.float32
BF16 = jnp.bfloat16

D_MODEL = 1024
CHUNK = 64
EPS = 1e-6

DIFF_HEADS = 4
DIFF_HEAD_DIM = 64
DIFF_V_DIM = 2 * DIFF_HEAD_DIM
DIFF_WIDTH = DIFF_HEADS * DIFF_V_DIM

GDN_HEADS = 4
GDN_DIM = 128
GDN_WIDTH = GDN_HEADS * GDN_DIM
CONV_K = 4

MOE_GROUPS = 4
MOE_EXPERTS_PER_GROUP = 8
MOE_EXPERTS = MOE_GROUPS * MOE_EXPERTS_PER_GROUP
MOE_HIDDEN = 256

LANES = 128
SUBLANES = 8
BF16_SUBLANES = 16
VMEM_LIMIT = 56 * 1024 * 1024

LAMBDA_INIT = 0.8 - 0.6 * math.exp(-0.3 * 0)

PROJ_ATTN = 0
PROJ_GDN = 3 * DIFF_WIDTH
PROJ_MA = PROJ_GDN + 3 * GDN_WIDTH
PROJ_MB = PROJ_MA + D_MODEL
PROJ_GATE = PROJ_MB + D_MODEL
PROJ_WIDTH = PROJ_GATE + GDN_WIDTH

NEG_INF = float("-inf")
LOG2_E = math.log2(math.e)


def _params(*sem):
    return pltpu.CompilerParams(dimension_semantics=sem, vmem_limit_bytes=VMEM_LIMIT)


def _inproj_kernel(x_ref, g_ref, w_ref, ws_ref, o_ref, os_ref, h_scr):
    @pl.when(pl.program_id(1) == 0)
    def _():
        x = x_ref[...]
        ms = jnp.mean(x * x, axis=-1, keepdims=True)
        h_scr[...] = (x * lax.rsqrt(ms + EPS) * g_ref[...]).astype(BF16)
        os_ref[...] = jnp.dot(h_scr[...], ws_ref[...], preferred_element_type=F32)

    o_ref[...] = jnp.dot(h_scr[...], w_ref[...], preferred_element_type=F32).astype(o_ref.dtype)


def _inproj(x2, gain, w, w_small, tm, tn):
    t, d = x2.shape
    n = w.shape[1]
    return pl.pallas_call(
        _inproj_kernel,
        out_shape=(jax.ShapeDtypeStruct((t, n), BF16), jax.ShapeDtypeStruct((t, LANES), F32)),
        grid=(t // tm, n // tn),
        in_specs=[
            pl.BlockSpec((tm, d), lambda i, j: (i, 0)),
            pl.BlockSpec((1, d), lambda i, j: (0, 0)),
            pl.BlockSpec((d, tn), lambda i, j: (0, j)),
            pl.BlockSpec((d, LANES), lambda i, j: (0, 0)),
        ],
        out_specs=(pl.BlockSpec((tm, tn), lambda i, j: (i, j)), pl.BlockSpec((tm, LANES), lambda i, j: (i, 0))),
        scratch_shapes=[pltpu.VMEM((tm, d), BF16)],
        compiler_params=_params("parallel", "arbitrary"),
        name="inproj",
    )(x2, gain, w, w_small)


ATTN_TQ = 512
ATTN_TK = 512
MASK_BIAS = -1e30


ATTN_V_ROWS = DIFF_V_DIM + BF16_SUBLANES
ATTN_COLS = 128


def _attn_kernel(lam_ref, q_ref, k_ref, v_ref, gain_ref, o_ref,
                   qt_scr, vt_scr, s0_scr, s1_scr, p0_scr, p1_scr, a0_scr, a1_scr, m_scr, acc_scr, *, tq, tk):
    i = pl.program_id(2)
    n_q = 2 * tq
    ratio = tk // tq
    t_last = i // ratio
    n_pairs = (t_last + 2) // 2

    @pl.when(i == 0)
    def _():
        for c in range(v_ref.shape[1] // tk):
            v_t = jnp.transpose(v_ref[0, c * tk:(c + 1) * tk, :].astype(F32)).astype(BF16)
            vt_scr[c] = jnp.concatenate([v_t, jnp.ones((BF16_SUBLANES, tk), BF16)], axis=0)

    q_t = jnp.transpose(q_ref[0].astype(F32) * (DIFF_HEAD_DIM ** -0.5 * LOG2_E))
    dim = lax.broadcasted_iota(jnp.int32, (LANES, tq), 0)
    qcol = lax.broadcasted_iota(jnp.int32, (LANES, tq), 1)
    qt_scr[0:LANES, 0:tq] = jnp.where(dim < DIFF_HEAD_DIM, q_t, 0.0).astype(BF16)
    qt_scr[0:LANES, tq:n_q] = jnp.where(dim >= DIFF_HEAD_DIM, q_t, 0.0).astype(BF16)
    q_slot = qcol // CHUNK + (i % ratio) * (tq // CHUNK)
    onehot = jnp.where(dim == q_slot, 1.0, 0.0).astype(BF16)
    qt_scr[LANES:2 * LANES, 0:tq] = onehot
    qt_scr[LANES:2 * LANES, tq:n_q] = onehot

    k_lane = lax.broadcasted_iota(jnp.int32, (tk, LANES), 1)
    k_chunk = lax.broadcasted_iota(jnp.int32, (tk, LANES), 0) // CHUNK
    n_slots = tk // CHUNK
    diag_bias = jnp.where((k_lane < n_slots) & (k_chunk > k_lane), MASK_BIAS, 0.0).astype(BF16)
    full_bias = jnp.where(k_lane < n_slots, MASK_BIAS, 0.0).astype(BF16)
    zero_bias = jnp.zeros((tk, LANES), BF16)

    m_scr[...] = jnp.full(m_scr.shape, NEG_INF, F32)
    acc_scr[...] = jnp.zeros(acc_scr.shape, F32)
    p1_scr[...] = jnp.zeros(p1_scr.shape, BF16)
    a1_scr[...] = jnp.ones(a1_scr.shape, F32)

    def scores(t, s_scr):
        bias = jnp.where(t == t_last, diag_bias, jnp.where(t > t_last, full_bias, zero_bias))
        rows = pl.ds(pl.multiple_of(jnp.minimum(t, t_last) * tk, tk), tk)
        k_aug = jnp.concatenate([k_ref[0, rows, :], bias], axis=1)
        s_scr[...] = jnp.dot(k_aug, qt_scr[...], preferred_element_type=F32)

    def softmax(s_scr, p_scr, a_scr):
        for c0 in range(0, n_q, ATTN_COLS):
            cs = slice(c0, c0 + ATTN_COLS)
            s = s_scr[:, cs]
            m_prev = m_scr[:, cs]
            m_new = jnp.maximum(m_prev, jnp.max(s, axis=0, keepdims=True))
            a_scr[:, cs] = jnp.exp2(m_prev - m_new)
            m_scr[:, cs] = m_new
            p_scr[:, cs] = jnp.exp2(s - m_new).astype(BF16)

    def values(t, p_scr, a_scr):
        pv = jnp.dot(vt_scr[jnp.minimum(t, t_last)], p_scr[...], preferred_element_type=F32)
        acc_scr[...] = a_scr[...] * acc_scr[...] + pv

    scores(0, s0_scr)

    def pair_step(g, carry):
        t0 = 2 * g
        scores(t0 + 1, s1_scr)
        softmax(s0_scr, p0_scr, a0_scr)
        values(jnp.maximum(t0 - 1, 0), p1_scr, a1_scr)
        scores(t0 + 2, s0_scr)
        softmax(s1_scr, p1_scr, a1_scr)
        values(t0, p0_scr, a0_scr)
        return carry

    lax.fori_loop(0, n_pairs, pair_step, 0)
    values(2 * n_pairs - 1, p1_scr, a1_scr)

    lp = lam_ref[...]
    lam = (jnp.exp(jnp.sum(lp[0:1] * lp[1:2], axis=-1, keepdims=True))
           - jnp.exp(jnp.sum(lp[2:3] * lp[3:4], axis=-1, keepdims=True)) + LAMBDA_INIT)
    o_t = acc_scr[0:DIFF_V_DIM, :] / acc_scr[DIFF_V_DIM:DIFF_V_DIM + 1, :]
    o = jnp.transpose(o_t[:, 0:tq] - lam * o_t[:, tq:n_q])
    ms = jnp.mean(o * o, axis=-1, keepdims=True)
    y = (o * lax.rsqrt(ms + EPS) * gain_ref[...]) * (1.0 - LAMBDA_INIT)
    o_ref[0] = y.astype(o_ref.dtype)


def _diff_attention(lam_params, qkv, subln_gain):
    b, s, _ = qkv.shape
    tq = min(ATTN_TQ, s)
    tk = min(ATTN_TK, s)
    assert s % tk == 0 and tk % tq == 0 and tq % CHUNK == 0 and tk // CHUNK <= LANES
    n_q = 2 * tq
    return pl.pallas_call(
        functools.partial(_attn_kernel, tq=tq, tk=tk),
        out_shape=jax.ShapeDtypeStruct((b, s, DIFF_WIDTH), BF16),
        grid=(b, DIFF_HEADS, s // tq),
        in_specs=[
            pl.BlockSpec((4, DIFF_HEAD_DIM), lambda bi, h, i: (0, 0)),
            pl.BlockSpec((1, tq, DIFF_V_DIM), lambda bi, h, i: (bi, i, h)),
            pl.BlockSpec((1, s, DIFF_V_DIM), lambda bi, h, i: (bi, 0, DIFF_HEADS + h)),
            pl.BlockSpec((1, s, DIFF_V_DIM), lambda bi, h, i: (bi, 0, 2 * DIFF_HEADS + h)),
            pl.BlockSpec((1, DIFF_V_DIM), lambda bi, h, i: (0, 0)),
        ],
        out_specs=pl.BlockSpec((1, tq, DIFF_V_DIM), lambda bi, h, i: (bi, i, h)),
        scratch_shapes=[
            pltpu.VMEM((2 * LANES, n_q), BF16),
            pltpu.VMEM((s // tk, ATTN_V_ROWS, tk), BF16),
            pltpu.VMEM((tk, n_q), F32),
            pltpu.VMEM((tk, n_q), F32),
            pltpu.VMEM((tk, n_q), BF16),
            pltpu.VMEM((tk, n_q), BF16),
            pltpu.VMEM((1, n_q), F32),
            pltpu.VMEM((1, n_q), F32),
            pltpu.VMEM((1, n_q), F32),
            pltpu.VMEM((ATTN_V_ROWS, n_q), F32),
        ],
        compiler_params=_params("parallel", "parallel", "arbitrary"),
        name="diff_attn",
    )(lam_params, qkv, qkv, qkv, subln_gain)


def _silu(x):
    return x * (1.0 / (1.0 + jnp.exp(-x)))


def _sigmoid(x):
    return 1.0 / (1.0 + jnp.exp(-x))


def _softplus(x):
    return jnp.maximum(x, 0.0) + jnp.log(1.0 + jnp.exp(-jnp.abs(x)))


def _dot(a, b):
    return jnp.dot(a.astype(BF16), b.astype(BF16), preferred_element_type=F32)


def _dot_nt(a, b):
    return lax.dot_general(a.astype(BF16), b.astype(BF16), (((1,), (1,)), ((), ())), preferred_element_type=F32)


def _gdn_kernel(prev_ref, qkv_ref, gate_ref, small_ref, convw_ref, alog_ref, dtb_ref, ngain_ref, o_ref,
                xp_scr, q_scr, k_scr, v_scr, pq_scr, n_scr, oc_scr, gl_scr, oraw_scr, state_scr, *, tb):
    i = pl.program_id(1)

    @pl.when(i == 0)
    def _():
        state_scr[...] = jnp.zeros(state_scr.shape, F32)

    prev = prev_ref[0, BF16_SUBLANES - SUBLANES:BF16_SUBLANES, :].astype(F32)
    xp_scr[0:SUBLANES, :] = jnp.where(i == 0, 0.0, prev)
    xp_scr[SUBLANES:SUBLANES + tb, :] = qkv_ref[0].astype(F32)
    for sec in range(3 * GDN_HEADS):
        cols = slice(sec * LANES, (sec + 1) * LANES)
        acc = None
        for jj in range(CONV_K):
            start = SUBLANES - (CONV_K - 1) + jj
            term = xp_scr[start:start + tb, cols] * convw_ref[jj:jj + 1, cols]
            acc = term if acc is None else acc + term
        y = _silu(acc)
        which, head = divmod(sec, GDN_HEADS)
        hc = slice(head * LANES, (head + 1) * LANES)
        if which == 0:
            q_scr[:, hc] = y * lax.rsqrt(jnp.sum(y * y, axis=-1, keepdims=True) + EPS) * (GDN_DIM ** -0.5)
        elif which == 1:
            k_scr[:, hc] = y * lax.rsqrt(jnp.sum(y * y, axis=-1, keepdims=True) + EPS)
        else:
            v_scr[:, hc] = y

    sm = small_ref[0]
    beta_all = _sigmoid(sm)
    g_all = -jnp.exp(alog_ref[...]) * _softplus(sm + dtb_ref[...])

    br = lax.broadcasted_iota(jnp.int32, (tb, tb), 0)
    bc = lax.broadcasted_iota(jnp.int32, (tb, tb), 1)
    block_tril = jnp.where((br // CHUNK == bc // CHUNK) & (bc <= br), 1.0, 0.0)
    gc_all = jnp.dot(block_tril, g_all, preferred_element_type=F32, precision=lax.Precision.HIGHEST)
    gct_all = jnp.transpose(gc_all)

    ur = lax.broadcasted_iota(jnp.int32, (GDN_UNIT, GDN_UNIT), 0)
    uc = lax.broadcasted_iota(jnp.int32, (GDN_UNIT, GDN_UNIT), 1)
    same = (ur // CHUNK) == (uc // CHUNK)
    causal = same & (uc <= ur)
    strict = same & (uc < ur)
    eye = jnp.where(ur == uc, 1.0, 0.0)
    first_rows = ur < CHUNK
    first_cols = uc < CHUNK
    n_units = tb // GDN_UNIT
    units = [(h, u) for h in range(GDN_HEADS) for u in range(n_units)]

    def lane_bcast(x, lane):
        return jnp.broadcast_to(x[:, lane:lane + 1], (GDN_UNIT, GDN_UNIT))

    def row_bcast(x, r):
        return jnp.broadcast_to(x[r:r + 1, :], (GDN_UNIT, GDN_UNIT))

    def rows_of(u):
        return slice(u * GDN_UNIT, (u + 1) * GDN_UNIT)

    def cols_of(h):
        return slice(h * LANES, (h + 1) * LANES)

    qs = [q_scr[rows_of(u), cols_of(h)] for h, u in units]
    ks = [k_scr[rows_of(u), cols_of(h)] for h, u in units]
    vs = [v_scr[rows_of(u), cols_of(h)] for h, u in units]
    betas = [lane_bcast(beta_all[rows_of(u)], h) for h, u in units]
    gcb = [lane_bcast(gc_all[rows_of(u)], GDN_HEADS + h) for h, u in units]
    gcr = [row_bcast(gct_all[:, rows_of(u)], GDN_HEADS + h) for h, u in units]
    decay = [jnp.exp(jnp.where(causal, a - b, NEG_INF)) for a, b in zip(gcb, gcr)]
    eg = [jnp.exp(a) for a in gcb]
    g_last = [jnp.where(first_rows, row_bcast(a, CHUNK - 1), row_bcast(a, GDN_UNIT - 1)) for a in gcb]
    kb = [k * b for k, b in zip(ks, betas)]
    vb = [v * b for v, b in zip(vs, betas)]
    a_low = [jnp.where(strict, _dot_nt(x, k) * d, 0.0) for x, k, d in zip(kb, ks, decay)]
    tinv = [eye - a for a in a_low]
    pw = a_low
    for _ in range(5):
        pw = [_dot(x, x) for x in pw]
        tinv = [t + _dot(t, x) for t, x in zip(tinv, pw)]
    uw = [_dot(t, jnp.concatenate([v, x * e], axis=1)) for t, v, x, e in zip(tinv, vb, kb, eg)]
    intra = [_dot_nt(q, k) * d for q, k, d in zip(qs, ks, decay)]
    iuw = [_dot(a, x) for a, x in zip(intra, uw)]
    k_dec_t = [jnp.transpose(k * jnp.exp(gl - a)) for k, gl, a in zip(ks, g_last, gcb)]
    wu = [jnp.concatenate([x[:, GDN_DIM:], x[:, :GDN_DIM]], axis=1) for x in uw]
    pn0 = [_dot(jnp.where(first_cols, kt, 0.0), x) for kt, x in zip(k_dec_t, wu)]
    pn1 = [_dot(jnp.where(first_cols, 0.0, kt), x) for kt, x in zip(k_dec_t, wu)]
    for n in range(len(units)):
        q_eff = (qs[n] * eg[n] - iuw[n][:, GDN_DIM:]).astype(BF16)
        for half, pn in enumerate((pn0[n], pn1[n])):
            hr = slice(half * CHUNK, (half + 1) * CHUNK)
            pq_scr[n, half, 0:GDN_DIM, :] = pn[:, :GDN_DIM].astype(BF16)
            pq_scr[n, half, GDN_DIM:GDN_DIM + CHUNK, :] = q_eff[hr]
            n_scr[n, half] = pn[:, GDN_DIM:]
            oc_scr[n, half] = iuw[n][hr, :GDN_DIM]
            gl_scr[n, half] = jnp.exp(gcb[n][(half + 1) * CHUNK - 1:(half + 1) * CHUNK, :])

    def chunk_body(c, carry):
        u = c // 2
        half = c % 2
        rows = pl.ds(pl.multiple_of(c * CHUNK, CHUNK), CHUNK)
        states = [state_scr[h] for h in range(GDN_HEADS)]
        res = [jnp.dot(pq_scr[h * n_units + u, half], states[h].astype(BF16), preferred_element_type=F32)
               for h in range(GDN_HEADS)]
        for h in range(GDN_HEADS):
            n = h * n_units + u
            state_scr[h] = gl_scr[n, half] * states[h] - res[h][0:GDN_DIM] + n_scr[n, half]
            oraw_scr[h, rows, :] = res[h][GDN_DIM:GDN_DIM + CHUNK] + oc_scr[n, half]
        return carry

    lax.fori_loop(0, tb // CHUNK, chunk_body, 0)

    for h in range(GDN_HEADS):
        o = oraw_scr[h]
        ms = jnp.mean(o * o, axis=-1, keepdims=True)
        y = o * lax.rsqrt(ms + EPS) * ngain_ref[...] * _silu(gate_ref[0, :, cols_of(h)].astype(F32))
        o_ref[0, :, cols_of(h)] = y.astype(o_ref.dtype)


GDN_UNIT = 2 * CHUNK


def _gdn(proj, small, conv_w, alog_pad, dtb_pad, norm_gain, tb):
    b, s, _ = proj.shape
    nb = s // tb
    w3 = 3 * GDN_WIDTH
    assert tb % GDN_UNIT == 0
    n_hu = GDN_HEADS * (tb // GDN_UNIT)
    return pl.pallas_call(
        functools.partial(_gdn_kernel, tb=tb),
        out_shape=jax.ShapeDtypeStruct((b, s, GDN_WIDTH), BF16),
        grid=(b, nb),
        in_specs=[
            pl.BlockSpec((1, BF16_SUBLANES, w3),
                         lambda bi, i: (bi, jnp.maximum(i * (tb // BF16_SUBLANES) - 1, 0), PROJ_GDN // w3)),
            pl.BlockSpec((1, tb, w3), lambda bi, i: (bi, i, PROJ_GDN // w3)),
            pl.BlockSpec((1, tb, GDN_WIDTH), lambda bi, i: (bi, i, PROJ_GATE // GDN_WIDTH)),
            pl.BlockSpec((1, tb, LANES), lambda bi, i: (bi, i, 0)),
            pl.BlockSpec((CONV_K, w3), lambda bi, i: (0, 0)),
            pl.BlockSpec((1, LANES), lambda bi, i: (0, 0)),
            pl.BlockSpec((1, LANES), lambda bi, i: (0, 0)),
            pl.BlockSpec((1, GDN_DIM), lambda bi, i: (0, 0)),
        ],
        out_specs=pl.BlockSpec((1, tb, GDN_WIDTH), lambda bi, i: (bi, i, 0)),
        scratch_shapes=[
            pltpu.VMEM((tb + SUBLANES, w3), F32),
            pltpu.VMEM((tb, GDN_WIDTH), F32),
            pltpu.VMEM((tb, GDN_WIDTH), F32),
            pltpu.VMEM((tb, GDN_WIDTH), F32),
            pltpu.VMEM((n_hu, 2, GDN_DIM + CHUNK, GDN_DIM), BF16),
            pltpu.VMEM((n_hu, 2, GDN_DIM, GDN_DIM), F32),
            pltpu.VMEM((n_hu, 2, CHUNK, GDN_DIM), F32),
            pltpu.VMEM((n_hu, 2, 1, GDN_DIM), F32),
            pltpu.VMEM((GDN_HEADS, tb, GDN_DIM), F32),
            pltpu.VMEM((GDN_HEADS, GDN_DIM, GDN_DIM), F32),
        ],
        compiler_params=_params("parallel", "arbitrary"),
        name="gdn",
    )(proj, proj, proj, small, conv_w, alog_pad, dtb_pad, norm_gain)


def _merge_kernel(x_ref, ya_ref, yb_ref, ma_ref, mb_ref, wa_ref, wb_ref, wo_ref, g2_ref, wr_ref, br_ref,
                  x1_ref, h2_ref, route_ref, cnt_ref):
    tm = x_ref.shape[0]
    subs = [slice(r, r + MERGE_SUB) for r in range(0, tm, MERGE_SUB)]
    pa = [jnp.dot(ya_ref[rs, :], wa_ref[...], preferred_element_type=F32) for rs in subs]
    pb = [jnp.dot(yb_ref[rs, :], wb_ref[...], preferred_element_type=F32) for rs in subs]
    merged = [(_sigmoid(ma_ref[rs, :].astype(F32)) * a + _sigmoid(mb_ref[rs, :].astype(F32)) * b).astype(BF16)
              for rs, a, b in zip(subs, pa, pb)]
    x1s = [x_ref[rs, :] + jnp.dot(m, wo_ref[...], preferred_element_type=F32) for rs, m in zip(subs, merged)]
    h2s = [v * lax.rsqrt(jnp.mean(v * v, axis=-1, keepdims=True) + EPS) * g2_ref[...] for v in x1s]
    logit_s = [jnp.dot(v, wr_ref[...], preferred_element_type=F32, precision=lax.Precision.HIGHEST) for v in h2s]
    for rs, v, hh in zip(subs, x1s, h2s):
        x1_ref[rs, :] = v
        h2_ref[rs, :] = hh
    logits = jnp.concatenate(logit_s, axis=0) + br_ref[...]
    lane = lax.broadcasted_iota(jnp.int32, logits.shape, 1)
    big = jnp.int32(4 * LANES)
    gl = jnp.where(lane < MOE_GROUPS, logits, NEG_INF)
    gmax = jnp.max(gl, axis=-1, keepdims=True)
    gidx = jnp.min(jnp.where(gl == gmax, lane, big), axis=-1, keepdims=True)
    grp_w = 1.0 / jnp.sum(jnp.exp(gl - gmax), axis=-1, keepdims=True)
    lo = MOE_GROUPS + gidx * MOE_EXPERTS_PER_GROUP
    el = jnp.where((lane >= lo) & (lane < lo + MOE_EXPERTS_PER_GROUP), logits, NEG_INF)
    v1 = jnp.max(el, axis=-1, keepdims=True)
    i1 = jnp.min(jnp.where(el == v1, lane, big), axis=-1, keepdims=True)
    el2 = jnp.where(lane == i1, NEG_INF, el)
    v2 = jnp.max(el2, axis=-1, keepdims=True)
    i2 = jnp.min(jnp.where(el2 == v2, lane, big), axis=-1, keepdims=True)
    e2 = jnp.exp(v2 - v1)
    w1 = grp_w / (1.0 + e2)
    w2 = w1 * e2
    id1 = i1 - MOE_GROUPS
    id2 = i2 - MOE_GROUPS
    route_ref[...] = jnp.where(lane == 0, id1.astype(F32), jnp.where(lane == 1, id2.astype(F32),
                               jnp.where(lane == 2, w1, jnp.where(lane == 3, w2, 0.0))))

    @pl.when(pl.program_id(0) == 0)
    def _():
        cnt_ref[...] = jnp.zeros(cnt_ref.shape, F32)

    hits = jnp.where((lane == id1) | (lane == id2), 1.0, 0.0)
    cnt_ref[...] += jnp.broadcast_to(jnp.sum(hits, axis=0, keepdims=True), cnt_ref.shape)


MERGE_SUB = 256


def _merge(x2, ya, yb, proj, wa, wb, wo, g2, wr, br, tm):
    t, d = x2.shape
    row = lambda i: (i, 0)
    const = lambda i: (0, 0)
    return pl.pallas_call(
        _merge_kernel,
        out_shape=(
            jax.ShapeDtypeStruct((t, d), F32),
            jax.ShapeDtypeStruct((t, d), F32),
            jax.ShapeDtypeStruct((t, LANES), F32),
            jax.ShapeDtypeStruct((SUBLANES, LANES), F32),
        ),
        grid=(t // tm,),
        in_specs=[
            pl.BlockSpec((tm, d), row),
            pl.BlockSpec((tm, DIFF_WIDTH), row),
            pl.BlockSpec((tm, GDN_WIDTH), row),
            pl.BlockSpec((tm, d), lambda i: (i, PROJ_MA // D_MODEL)),
            pl.BlockSpec((tm, d), lambda i: (i, PROJ_MB // D_MODEL)),
            pl.BlockSpec((DIFF_WIDTH, d), const),
            pl.BlockSpec((GDN_WIDTH, d), const),
            pl.BlockSpec((d, d), const),
            pl.BlockSpec((1, d), const),
            pl.BlockSpec((d, LANES), const),
            pl.BlockSpec((1, LANES), const),
        ],
        out_specs=(pl.BlockSpec((tm, d), row), pl.BlockSpec((tm, d), row), pl.BlockSpec((tm, LANES), row),
                   pl.BlockSpec((SUBLANES, LANES), const)),
        compiler_params=_params("arbitrary"),
        name="merge_router",
    )(x2, ya, yb, proj, proj, wa, wb, wo, g2, wr, br)


MOE_BLK = 256
MOE_META_LANES = 256
MOE_ROUTE_TILE = 512
MOE_DMA_TILE = 256


def _route_kernel(route_ref, cnt_ref, pos_ref, meta_ref, run_scr, *, n_rows):
    i = pl.program_id(0)
    tp = route_ref.shape[0]

    @pl.when(i == 0)
    def _():
        cnt = cnt_ref[...]
        jr = lax.broadcasted_iota(jnp.int32, (LANES, LANES), 0)
        jc = lax.broadcasted_iota(jnp.int32, (LANES, LANES), 1)
        upper = jnp.where(jr < jc, 1.0, 0.0)
        hi_prec = dict(preferred_element_type=F32, precision=lax.Precision.HIGHEST)
        off = jnp.dot(cnt, upper, **hi_prec)
        run_scr[...] = off[0:1]
        blk = float(MOE_BLK)
        first_tile = jnp.floor(off / blk)
        last_tile = jnp.floor((off + cnt - 1.0) / blk)
        n_it = jnp.where(cnt > 0.0, last_tile - first_tile + 1.0, 0.0)
        it_start = jnp.dot(n_it, upper, **hi_prec)
        it_end = it_start + n_it
        lane8 = lax.broadcasted_iota(jnp.int32, cnt.shape, 1)
        e_max = jnp.max(jnp.where(cnt > 0.0, lane8, 0), axis=-1, keepdims=True).astype(F32)[0:1]
        sub8 = lax.broadcasted_iota(jnp.int32, cnt.shape, 0)
        table = jnp.where(sub8 == 0, first_tile, jnp.where(sub8 == 1, it_start, jnp.where(
            sub8 == 2, off, jnp.where(sub8 == 3, cnt, it_end))))
        cols = jnp.transpose(table)
        shape = (LANES, MOE_META_LANES)
        e_sub = lax.broadcasted_iota(jnp.int32, shape, 0)
        w_lane = lax.broadcasted_iota(jnp.int32, shape, 1).astype(F32)
        col = lambda k: jnp.broadcast_to(cols[:, k:k + 1], shape)
        e_w = jnp.sum(jnp.where((e_sub < MOE_EXPERTS) & (col(4) <= w_lane), 1.0, 0.0), axis=0, keepdims=True)
        valid = e_w < float(MOE_EXPERTS)
        e_w = jnp.minimum(e_w, e_max)
        sel = e_sub.astype(F32) == e_w
        pick = lambda k: jnp.sum(jnp.where(sel, col(k), 0.0), axis=0, keepdims=True)
        w_row = w_lane[0:1]
        tile_w = jnp.where(valid, pick(0) + (w_row - pick(1)), float(n_rows // MOE_BLK - 1))
        lo_w = jnp.maximum(pick(2) - tile_w * blk, 0.0)
        hi_w = jnp.minimum(pick(2) + pick(3) - tile_w * blk, blk)
        lo_w = jnp.where(valid, lo_w, 0.0)
        hi_w = jnp.where(valid, hi_w, 0.0)
        sub_m = lax.broadcasted_iota(jnp.int32, meta_ref.shape, 0)
        bc = lambda v: jnp.broadcast_to(v, meta_ref.shape)
        meta_ref[...] = jnp.where(sub_m == 0, bc(e_w), jnp.where(sub_m == 1, bc(tile_w), jnp.where(
            sub_m == 2, bc(lo_w), bc(hi_w)))).astype(jnp.int32)

    r = route_ref[...]
    lane = lax.broadcasted_iota(jnp.int32, r.shape, 1)
    lane_f = lane.astype(F32)
    oh1 = lane_f == r[:, 0:1]
    oh2 = lane_f == r[:, 1:2]
    hits = jnp.where(oh1 | oh2, 1.0, 0.0)
    tr = lax.broadcasted_iota(jnp.int32, (tp, tp), 0)
    tc = lax.broadcasted_iota(jnp.int32, (tp, tp), 1)
    earlier = jnp.where(tc < tr, 1.0, 0.0).astype(BF16)
    rank = jnp.dot(earlier, hits.astype(BF16), preferred_element_type=F32)
    base = run_scr[...] + rank
    p1 = jnp.sum(jnp.where(oh1, base, 0.0), axis=-1, keepdims=True)
    p2 = jnp.sum(jnp.where(oh2, base, 0.0), axis=-1, keepdims=True)
    pos_ref[...] = jnp.where(lane == 0, p1, jnp.where(lane == 1, p2, 0.0)).astype(jnp.int32)
    run_scr[...] += jnp.sum(hits, axis=0, keepdims=True)


def _route_positions(route, cnt):
    t = route.shape[0]
    tp = min(MOE_ROUTE_TILE, t)
    n_rows = 2 * t
    assert n_rows % MOE_BLK == 0 and n_rows // MOE_BLK + MOE_EXPERTS <= MOE_META_LANES and n_rows < 2 ** 24
    return pl.pallas_call(
        functools.partial(_route_kernel, n_rows=n_rows),
        out_shape=(jax.ShapeDtypeStruct((t, LANES), jnp.int32),
                   jax.ShapeDtypeStruct((SUBLANES, MOE_META_LANES), jnp.int32)),
        grid=(t // tp,),
        in_specs=[pl.BlockSpec((tp, LANES), lambda i: (i, 0)),
                  pl.BlockSpec((SUBLANES, LANES), lambda i: (0, 0))],
        out_specs=(pl.BlockSpec((tp, LANES), lambda i: (i, 0)),
                   pl.BlockSpec((SUBLANES, MOE_META_LANES), lambda i: (0, 0))),
        scratch_shapes=[pltpu.VMEM((1, LANES), F32)],
        compiler_params=_params("arbitrary"),
        name="route_positions",
    )(route, cnt)


def _scatter_kernel(pos_ref, h2_ref, xs_ref, sem):
    ts = h2_ref.shape[0]

    def body(tok, carry):
        src = h2_ref.at[pl.ds(tok, 1), :]
        for k in range(2):
            pltpu.make_async_copy(src, xs_ref.at[pl.ds(pos_ref[0, 0, 2 * tok + k], 1), :], sem).start()
        return carry

    lax.fori_loop(0, ts, body, 0, unroll=8)
    for _ in range(2):
        pltpu.make_async_copy(h2_ref, xs_ref.at[pl.ds(0, ts), :], sem).wait()


def _scatter_rows(pos3, h2):
    t, d = h2.shape
    ts = pos3.shape[2] // 2
    return pl.pallas_call(
        _scatter_kernel,
        out_shape=jax.ShapeDtypeStruct((2 * t, d), F32),
        grid=(t // ts,),
        in_specs=[pl.BlockSpec((1, 1, 2 * ts), lambda i: (i, 0, 0), memory_space=pltpu.SMEM),
                  pl.BlockSpec((ts, d), lambda i: (i, 0))],
        out_specs=pl.BlockSpec(memory_space=pl.ANY),
        scratch_shapes=[pltpu.SemaphoreType.DMA],
        compiler_params=pltpu.CompilerParams(dimension_semantics=("arbitrary",), vmem_limit_bytes=VMEM_LIMIT,
                                             disable_bounds_checks=True),
        name="scatter_rows",
    )(pos3, h2)


def _expert_kernel(ie_ref, it_ref, lo_ref, hi_ref, xs_ref, wg_ref, wu_ref, wd_ref, y_ref, wg_scr, wu_scr, wd_scr):
    w = pl.program_id(0)
    prev = jnp.maximum(w - 1, 0)

    @pl.when((w == 0) | (ie_ref[w] != ie_ref[prev]))
    def _():
        wg_scr[...] = wg_ref[0].astype(BF16)
        wu_scr[...] = wu_ref[0].astype(BF16)
        wd_scr[...] = wd_ref[0].astype(BF16)

    @pl.when((w == 0) | (it_ref[w] != it_ref[prev]))
    def _():
        y_ref[...] = jnp.zeros(y_ref.shape, F32)

    lo = lo_ref[w]
    hi = hi_ref[w]

    @pl.when(hi > lo)
    def _():
        x = xs_ref[...].astype(BF16)
        hg = jnp.dot(x, wg_scr[...], preferred_element_type=F32)
        hu = jnp.dot(x, wu_scr[...], preferred_element_type=F32)
        act = (_silu(hg) * hu).astype(BF16)
        yp = jnp.dot(act, wd_scr[...], preferred_element_type=F32)
        row = lax.broadcasted_iota(jnp.int32, yp.shape, 0)
        y_ref[...] += jnp.where((row >= lo) & (row < hi), yp, 0.0)


def _experts(meta, xs, wg, wu, wd):
    n_rows, d = xs.shape
    n_items = n_rows // MOE_BLK + MOE_EXPERTS
    ie, it, lo, hi = (meta[k, :n_items] for k in range(4))
    return pl.pallas_call(
        _expert_kernel,
        out_shape=jax.ShapeDtypeStruct((n_rows, d), F32),
        grid_spec=pltpu.PrefetchScalarGridSpec(
            num_scalar_prefetch=4,
            grid=(n_items,),
            in_specs=[
                pl.BlockSpec((MOE_BLK, d), lambda w, ie, it, lo, hi: (it[w], 0)),
                pl.BlockSpec((1, d, MOE_HIDDEN), lambda w, ie, it, lo, hi: (ie[w], 0, 0)),
                pl.BlockSpec((1, d, MOE_HIDDEN), lambda w, ie, it, lo, hi: (ie[w], 0, 0)),
                pl.BlockSpec((1, MOE_HIDDEN, d), lambda w, ie, it, lo, hi: (ie[w], 0, 0)),
            ],
            out_specs=pl.BlockSpec((MOE_BLK, d), lambda w, ie, it, lo, hi: (it[w], 0)),
            scratch_shapes=[pltpu.VMEM((d, MOE_HIDDEN), BF16), pltpu.VMEM((d, MOE_HIDDEN), BF16),
                            pltpu.VMEM((MOE_HIDDEN, d), BF16)],
        ),
        compiler_params=_params("arbitrary"),
        name="experts",
    )(ie, it, lo, hi, xs, wg, wu, wd)


def _combine_kernel(pos_ref, posn_ref, x1_ref, route_ref, gf_ref, y_ref, o_ref, ybuf, sem):
    i = pl.program_id(0)
    n = pl.num_programs(0)
    tc = x1_ref.shape[0]
    slot = i % 2

    def issue(p_ref, s):
        def body(tok, carry):
            for k in range(2):
                pltpu.make_async_copy(y_ref.at[pl.ds(p_ref[0, 0, 2 * tok + k], 1), :],
                                      ybuf.at[s, k, pl.ds(tok, 1), :], sem.at[s]).start()
            return carry

        lax.fori_loop(0, tc, body, 0, unroll=8)

    @pl.when(i == 0)
    def _():
        issue(pos_ref, 0)

    @pl.when(i + 1 < n)
    def _():
        issue(posn_ref, 1 - slot)

    for k in range(2):
        pltpu.make_async_copy(y_ref.at[pl.ds(0, tc), :], ybuf.at[slot, k], sem.at[slot]).wait()
    r = route_ref[...]
    x2 = x1_ref[...] + r[:, 2:3] * ybuf[slot, 0] + r[:, 3:4] * ybuf[slot, 1]
    ms = jnp.mean(x2 * x2, axis=-1, keepdims=True)
    o_ref[...] = x2 * lax.rsqrt(ms + EPS) * gf_ref[...]


def _combine(pos3, x1, route, gf, y):
    t, d = x1.shape
    tc = pos3.shape[2] // 2
    n = t // tc
    return pl.pallas_call(
        _combine_kernel,
        out_shape=jax.ShapeDtypeStruct((t, d), F32),
        grid=(n,),
        in_specs=[
            pl.BlockSpec((1, 1, 2 * tc), lambda i: (i, 0, 0), memory_space=pltpu.SMEM),
            pl.BlockSpec((1, 1, 2 * tc), lambda i: (jnp.minimum(i + 1, n - 1), 0, 0), memory_space=pltpu.SMEM),
            pl.BlockSpec((tc, d), lambda i: (i, 0)),
            pl.BlockSpec((tc, LANES), lambda i: (i, 0)),
            pl.BlockSpec((1, d), lambda i: (0, 0)),
            pl.BlockSpec(memory_space=pl.ANY),
        ],
        out_specs=pl.BlockSpec((tc, d), lambda i: (i, 0)),
        scratch_shapes=[pltpu.VMEM((2, 2, tc, d), F32), pltpu.SemaphoreType.DMA((2,))],
        compiler_params=pltpu.CompilerParams(dimension_semantics=("arbitrary",), vmem_limit_bytes=VMEM_LIMIT,
                                             disable_bounds_checks=True),
        name="combine_norm",
    )(pos3, pos3, x1, route, gf, y)


def _pad_lanes(v, offset):
    return jnp.zeros((1, LANES), F32).at[0, offset:offset + v.shape[0]].set(v.astype(F32))


def kernel(x, norm_mix_gain, w_in, diff_lambda_q1, diff_lambda_k1, diff_lambda_q2, diff_lambda_k2, diff_subln_gain, gdn_conv_w, gdn_a_log, gdn_dt_bias, gdn_norm_gain, w_branch_attn, w_branch_gdn, w_out, norm_ffn_gain, moe_w_group, moe_b_group, moe_w_expert, moe_b_expert, moe_w_gate, moe_w_up, moe_w_down, norm_final_gain):
    b, s, d = x.shape
    t = b * s
    x2 = x.reshape(t, d)

    w = w_in[0]
    small_lo = PROJ_GDN + 3 * GDN_WIDTH
    small_hi = small_lo + 2 * GDN_HEADS
    gate_hi = small_hi + GDN_WIDTH
    w_main = jnp.concatenate([w[:, :small_lo], w[:, gate_hi:], w[:, small_hi:gate_hi]], axis=1).astype(BF16)
    w_small = jnp.concatenate([w[:, small_lo:small_hi], jnp.zeros((d, LANES - 2 * GDN_HEADS), w.dtype)],
                              axis=1).astype(BF16)
    gain1 = norm_mix_gain[0].reshape(1, d)

    proj2, small2 = _inproj(x2, gain1, w_main, w_small, min(1024, t), PROJ_WIDTH // 2)
    proj = proj2.reshape(b, s, PROJ_WIDTH)

    lam_params = jnp.stack([diff_lambda_q1[0], diff_lambda_k1[0], diff_lambda_q2[0], diff_lambda_k2[0]]).astype(F32)
    ya = _diff_attention(lam_params, proj, diff_subln_gain[0].reshape(1, DIFF_V_DIM))

    yb = _gdn(proj, small2.reshape(b, s, LANES), gdn_conv_w[0], _pad_lanes(gdn_a_log[0], GDN_HEADS),
              _pad_lanes(gdn_dt_bias[0], GDN_HEADS), gdn_norm_gain[0].reshape(1, GDN_DIM), tb=min(512, s))

    wr = jnp.concatenate([moe_w_group[0], moe_w_expert[0],
                          jnp.zeros((d, LANES - MOE_GROUPS - MOE_EXPERTS), F32)], axis=1)
    br = _pad_lanes(jnp.concatenate([moe_b_group[0], moe_b_expert[0]]), 0)
    x1, h2, route, cnt = _merge(x2, ya.reshape(t, DIFF_WIDTH), yb.reshape(t, GDN_WIDTH), proj2,
                                w_branch_attn[0].astype(BF16), w_branch_gdn[0].astype(BF16),
                                w_out[0].astype(BF16), norm_ffn_gain[0].reshape(1, d), wr, br, tm=min(512, t))

    pos, meta = _route_positions(route, cnt)
    ts = min(MOE_DMA_TILE, t)
    pos3 = pos[:, :2].reshape(t // ts, 1, 2 * ts)
    xs = _scatter_rows(pos3, h2)
    y = _experts(meta, xs, moe_w_gate[0], moe_w_up[0], moe_w_down[0])
    out = _combine(pos3, x1, route, norm_final_gain.reshape(1, d), y)
    return out.reshape(b, s, d)
```

```python
import functools
import math

import jax
import jax.numpy as jnp
from jax import lax
from jax.experimental import pallas as pl
from jax.experimental.pallas import tpu as pltpu

F32 = jnp.float32
BF16 = jnp.bfloat16

D_MODEL = 1024
CHUNK = 64
EPS = 1e-6

DIFF_HEADS = 4
DIFF_HEAD_DIM = 64
DIFF_V_DIM = 2 * DIFF_HEAD_DIM
DIFF_WIDTH = DIFF_HEADS * DIFF_V_DIM

GDN_HEADS = 4
GDN_DIM = 128
GDN_WIDTH = GDN_HEADS * GDN_DIM
CONV_K = 4

MOE_GROUPS = 4
MOE_EXPERTS_PER_GROUP = 8
MOE_EXPERTS = MOE_GROUPS * MOE_EXPERTS_PER_GROUP
MOE_HIDDEN = 256

LANES = 128
SUBLANES = 8
BF16_SUBLANES = 16
VMEM_LIMIT = 56 * 1024 * 1024

LAMBDA_INIT = 0.8 - 0.6 * math.exp(-0.3 * 0)

PROJ_ATTN = 0
PROJ_GDN = 3 * DIFF_WIDTH
PROJ_MA = PROJ_GDN + 3 * GDN_WIDTH
PROJ_MB = PROJ_MA + D_MODEL
PROJ_GATE = PROJ_MB + D_MODEL
PROJ_WIDTH = PROJ_GATE + GDN_WIDTH

NEG_INF = float("-inf")
LOG2_E = math.log2(math.e)


def _params(*sem):
    return pltpu.CompilerParams(dimension_semantics=sem, vmem_limit_bytes=VMEM_LIMIT)


def _inproj_kernel(x_ref, g_ref, w_ref, ws_ref, o_ref, os_ref, h_scr):
    @pl.when(pl.program_id(1) == 0)
    def _():
        x = x_ref[...]
        ms = jnp.mean(x * x, axis=-1, keepdims=True)
        h_scr[...] = (x * lax.rsqrt(ms + EPS) * g_ref[...]).astype(BF16)
        os_ref[...] = jnp.dot(h_scr[...], ws_ref[...], preferred_element_type=F32)

    o_ref[...] = jnp.dot(h_scr[...], w_ref[...], preferred_element_type=F32).astype(o_ref.dtype)


def _inproj(x2, gain, w, w_small, tm, tn):
    t, d = x2.shape
    n = w.shape[1]
    return pl.pallas_call(
        _inproj_kernel,
        out_shape=(jax.ShapeDtypeStruct((t, n), BF16), jax.ShapeDtypeStruct((t, LANES), F32)),
        grid=(t // tm, n // tn),
        in_specs=[
            pl.BlockSpec((tm, d), lambda i, j: (i, 0)),
            pl.BlockSpec((1, d), lambda i, j: (0, 0)),
            pl.BlockSpec((d, tn), lambda i, j: (0, j)),
            pl.BlockSpec((d, LANES), lambda i, j: (0, 0)),
        ],
        out_specs=(pl.BlockSpec((tm, tn), lambda i, j: (i, j)), pl.BlockSpec((tm, LANES), lambda i, j: (i, 0))),
        scratch_shapes=[pltpu.VMEM((tm, d), BF16)],
        compiler_params=_params("parallel", "arbitrary"),
        name="inproj",
    )(x2, gain, w, w_small)


ATTN_TQ = 512
ATTN_TK = 512
MASK_BIAS = -1e30


ATTN_V_ROWS = DIFF_V_DIM + BF16_SUBLANES
ATTN_COLS = 128


def _attn_kernel(lam_ref, q_ref, k_ref, v_ref, gain_ref, o_ref,
                   qt_scr, vt_scr, s0_scr, s1_scr, p0_scr, p1_scr, a0_scr, a1_scr, m_scr, acc_scr, *, tq, tk):
    i = pl.program_id(2)
    n_q = 2 * tq
    ratio = tk // tq
    t_last = i // ratio
    n_pairs = (t_last + 2) // 2

    @pl.when(i == 0)
    def _():
        for c in range(v_ref.shape[1] // tk):
            v_t = jnp.transpose(v_ref[0, c * tk:(c + 1) * tk, :].astype(F32)).astype(BF16)
            vt_scr[c] = jnp.concatenate([v_t, jnp.ones((BF16_SUBLANES, tk), BF16)], axis=0)

    q_t = jnp.transpose(q_ref[0].astype(F32) * (DIFF_HEAD_DIM ** -0.5 * LOG2_E))
    dim = lax.broadcasted_iota(jnp.int32, (LANES, tq), 0)
    qcol = lax.broadcasted_iota(jnp.int32, (LANES, tq), 1)
    qt_scr[0:LANES, 0:tq] = jnp.where(dim < DIFF_HEAD_DIM, q_t, 0.0).astype(BF16)
    qt_scr[0:LANES, tq:n_q] = jnp.where(dim >= DIFF_HEAD_DIM, q_t, 0.0).astype(BF16)
    q_slot = qcol // CHUNK + (i % ratio) * (tq // CHUNK)
    onehot = jnp.where(dim == q_slot, 1.0, 0.0).astype(BF16)
    qt_scr[LANES:2 * LANES, 0:tq] = onehot
    qt_scr[LANES:2 * LANES, tq:n_q] = onehot

    k_lane = lax.broadcasted_iota(jnp.int32, (tk, LANES), 1)
    k_chunk = lax.broadcasted_iota(jnp.int32, (tk, LANES), 0) // CHUNK
    n_slots = tk // CHUNK
    diag_bias = jnp.where((k_lane < n_slots) & (k_chunk > k_lane), MASK_BIAS, 0.0).astype(BF16)
    full_bias = jnp.where(k_lane < n_slots, MASK_BIAS, 0.0).astype(BF16)
    zero_bias = jnp.zeros((tk, LANES), BF16)

    m_scr[...] = jnp.full(m_scr.shape, NEG_INF, F32)
    acc_scr[...] = jnp.zeros(acc_scr.shape, F32)
    p1_scr[...] = jnp.zeros(p1_scr.shape, BF16)
    a1_scr[...] = jnp.ones(a1_scr.shape, F32)

    def scores(t, s_scr):
        bias = jnp.where(t == t_last, diag_bias, jnp.where(t > t_last, full_bias, zero_bias))
        rows = pl.ds(pl.multiple_of(jnp.minimum(t, t_last) * tk, tk), tk)
        k_aug = jnp.concatenate([k_ref[0, rows, :], bias], axis=1)
        s = jnp.dot(k_aug, qt_scr[...], preferred_element_type=F32)
        for c in range(n_q // ATTN_COLS):
            s_scr[c] = s[:, c * ATTN_COLS:(c + 1) * ATTN_COLS]

    def softmax(s_scr, p_scr, a_scr):
        for c in range(n_q // ATTN_COLS):
            cs = slice(c * ATTN_COLS, (c + 1) * ATTN_COLS)
            s = s_scr[c]
            m_prev = m_scr[:, cs]
            m_new = jnp.maximum(m_prev, jnp.max(s, axis=0, keepdims=True))
            a_scr[:, cs] = jnp.exp2(m_prev - m_new)
            m_scr[:, cs] = m_new
            p_scr[c] = jnp.exp2(s - m_new).astype(BF16)

    def values(t, p_scr, a_scr):
        p = jnp.concatenate([p_scr[c] for c in range(n_q // ATTN_COLS)], axis=1)
        pv = jnp.dot(vt_scr[jnp.minimum(t, t_last)], p, preferred_element_type=F32)
        acc_scr[...] = a_scr[...] * acc_scr[...] + pv

    scores(0, s0_scr)

    def pair_step(g, carry):
        t0 = 2 * g
        scores(t0 + 1, s1_scr)
        softmax(s0_scr, p0_scr, a0_scr)
        values(jnp.maximum(t0 - 1, 0), p1_scr, a1_scr)
        scores(t0 + 2, s0_scr)
        softmax(s1_scr, p1_scr, a1_scr)
        values(t0, p0_scr, a0_scr)
        return carry

    lax.fori_loop(0, n_pairs, pair_step, 0)
    values(2 * n_pairs - 1, p1_scr, a1_scr)

    lp = lam_ref[...]
    lam = (jnp.exp(jnp.sum(lp[0:1] * lp[1:2], axis=-1, keepdims=True))
           - jnp.exp(jnp.sum(lp[2:3] * lp[3:4], axis=-1, keepdims=True)) + LAMBDA_INIT)
    o_t = acc_scr[0:DIFF_V_DIM, :] / acc_scr[DIFF_V_DIM:DIFF_V_DIM + 1, :]
    o = jnp.transpose(o_t[:, 0:tq] - lam * o_t[:, tq:n_q])
    ms = jnp.mean(o * o, axis=-1, keepdims=True)
    y = (o * lax.rsqrt(ms + EPS) * gain_ref[...]) * (1.0 - LAMBDA_INIT)
    o_ref[0] = y.astype(o_ref.dtype)


def _diff_attention(lam_params, qkv, subln_gain):
    b, s, _ = qkv.shape
    tq = min(ATTN_TQ, s)
    tk = min(ATTN_TK, s)
    assert s % tk == 0 and tk % tq == 0 and tq % CHUNK == 0 and tk // CHUNK <= LANES
    n_q = 2 * tq
    return pl.pallas_call(
        functools.partial(_attn_kernel, tq=tq, tk=tk),
        out_shape=jax.ShapeDtypeStruct((b, s, DIFF_WIDTH), BF16),
        grid=(b, DIFF_HEADS, s // tq),
        in_specs=[
            pl.BlockSpec((4, DIFF_HEAD_DIM), lambda bi, h, i: (0, 0)),
            pl.BlockSpec((1, tq, DIFF_V_DIM), lambda bi, h, i: (bi, i, h)),
            pl.BlockSpec((1, s, DIFF_V_DIM), lambda bi, h, i: (bi, 0, DIFF_HEADS + h)),
            pl.BlockSpec((1, s, DIFF_V_DIM), lambda bi, h, i: (bi, 0, 2 * DIFF_HEADS + h)),
            pl.BlockSpec((1, DIFF_V_DIM), lambda bi, h, i: (0, 0)),
        ],
        out_specs=pl.BlockSpec((1, tq, DIFF_V_DIM), lambda bi, h, i: (bi, i, h)),
        scratch_shapes=[
            pltpu.VMEM((2 * LANES, n_q), BF16),
            pltpu.VMEM((s // tk, ATTN_V_ROWS, tk), BF16),
            pltpu.VMEM((n_q // ATTN_COLS, tk, ATTN_COLS), F32),
            pltpu.VMEM((n_q // ATTN_COLS, tk, ATTN_COLS), F32),
            pltpu.VMEM((n_q // ATTN_COLS, tk, ATTN_COLS), BF16),
            pltpu.VMEM((n_q // ATTN_COLS, tk, ATTN_COLS), BF16),
            pltpu.VMEM((1, n_q), F32),
            pltpu.VMEM((1, n_q), F32),
            pltpu.VMEM((1, n_q), F32),
            pltpu.VMEM((ATTN_V_ROWS, n_q), F32),
        ],
        compiler_params=_params("parallel", "parallel", "arbitrary"),
        name="diff_attn",
    )(lam_params, qkv, qkv, qkv, subln_gain)


def _silu(x):
    return x * (1.0 / (1.0 + jnp.exp(-x)))


def _sigmoid(x):
    return 1.0 / (1.0 + jnp.exp(-x))


def _softplus(x):
    return jnp.maximum(x, 0.0) + jnp.log(1.0 + jnp.exp(-jnp.abs(x)))


def _dot(a, b):
    return jnp.dot(a.astype(BF16), b.astype(BF16), preferred_element_type=F32)


def _dot_nt(a, b):
    return lax.dot_general(a.astype(BF16), b.astype(BF16), (((1,), (1,)), ((), ())), preferred_element_type=F32)


def _gdn_kernel(prev_ref, qkv_ref, gate_ref, small_ref, convw_ref, alog_ref, dtb_ref, ngain_ref, o_ref,
                xp_scr, q_scr, k_scr, v_scr, pq_scr, n_scr, oc_scr, gl_scr, oraw_scr, state_scr, *, tb):
    i = pl.program_id(1)

    @pl.when(i == 0)
    def _():
        state_scr[...] = jnp.zeros(state_scr.shape, F32)

    prev = prev_ref[0, BF16_SUBLANES - SUBLANES:BF16_SUBLANES, :].astype(F32)
    xp_scr[0:SUBLANES, :] = jnp.where(i == 0, 0.0, prev)
    xp_scr[SUBLANES:SUBLANES + tb, :] = qkv_ref[0].astype(F32)
    for sec in range(3 * GDN_HEADS):
        cols = slice(sec * LANES, (sec + 1) * LANES)
        acc = None
        for jj in range(CONV_K):
            start = SUBLANES - (CONV_K - 1) + jj
            term = xp_scr[start:start + tb, cols] * convw_ref[jj:jj + 1, cols]
            acc = term if acc is None else acc + term
        y = _silu(acc)
        which, head = divmod(sec, GDN_HEADS)
        hc = slice(head * LANES, (head + 1) * LANES)
        if which == 0:
            q_scr[:, hc] = y * lax.rsqrt(jnp.sum(y * y, axis=-1, keepdims=True) + EPS) * (GDN_DIM ** -0.5)
        elif which == 1:
            k_scr[:, hc] = y * lax.rsqrt(jnp.sum(y * y, axis=-1, keepdims=True) + EPS)
        else:
            v_scr[:, hc] = y

    sm = small_ref[0]
    beta_all = _sigmoid(sm)
    g_all = -jnp.exp(alog_ref[...]) * _softplus(sm + dtb_ref[...])

    br = lax.broadcasted_iota(jnp.int32, (tb, tb), 0)
    bc = lax.broadcasted_iota(jnp.int32, (tb, tb), 1)
    block_tril = jnp.where((br // CHUNK == bc // CHUNK) & (bc <= br), 1.0, 0.0)
    gc_all = jnp.dot(block_tril, g_all, preferred_element_type=F32, precision=lax.Precision.HIGHEST)
    gct_all = jnp.transpose(gc_all)

    ur = lax.broadcasted_iota(jnp.int32, (GDN_UNIT, GDN_UNIT), 0)
    uc = lax.broadcasted_iota(jnp.int32, (GDN_UNIT, GDN_UNIT), 1)
    same = (ur // CHUNK) == (uc // CHUNK)
    causal = same & (uc <= ur)
    strict = same & (uc < ur)
    eye = jnp.where(ur == uc, 1.0, 0.0)
    first_rows = ur < CHUNK
    first_cols = uc < CHUNK
    n_units = tb // GDN_UNIT
    units = [(h, u) for h in range(GDN_HEADS) for u in range(n_units)]

    def lane_bcast(x, lane):
        return jnp.broadcast_to(x[:, lane:lane + 1], (GDN_UNIT, GDN_UNIT))

    def row_bcast(x, r):
        return jnp.broadcast_to(x[r:r + 1, :], (GDN_UNIT, GDN_UNIT))

    def rows_of(u):
        return slice(u * GDN_UNIT, (u + 1) * GDN_UNIT)

    def cols_of(h):
        return slice(h * LANES, (h + 1) * LANES)

    qs = [q_scr[rows_of(u), cols_of(h)] for h, u in units]
    ks = [k_scr[rows_of(u), cols_of(h)] for h, u in units]
    vs = [v_scr[rows_of(u), cols_of(h)] for h, u in units]
    betas = [lane_bcast(beta_all[rows_of(u)], h) for h, u in units]
    gcb = [lane_bcast(gc_all[rows_of(u)], GDN_HEADS + h) for h, u in units]
    gcr = [row_bcast(gct_all[:, rows_of(u)], GDN_HEADS + h) for h, u in units]
    decay = [jnp.exp(jnp.where(causal, a - b, NEG_INF)) for a, b in zip(gcb, gcr)]
    eg = [jnp.exp(a) for a in gcb]
    g_last = [jnp.where(first_rows, row_bcast(a, CHUNK - 1), row_bcast(a, GDN_UNIT - 1)) for a in gcb]
    kb = [k * b for k, b in zip(ks, betas)]
    vb = [v * b for v, b in zip(vs, betas)]
    a_low = [jnp.where(strict, _dot_nt(x, k) * d, 0.0) for x, k, d in zip(kb, ks, decay)]
    tinv = [eye - a for a in a_low]
    pw = a_low
    for _ in range(5):
        pw = [_dot(x, x) for x in pw]
        tinv = [t + _dot(t, x) for t, x in zip(tinv, pw)]
    uw = [_dot(t, jnp.concatenate([v, x * e], axis=1)) for t, v, x, e in zip(tinv, vb, kb, eg)]
    intra = [_dot_nt(q, k) * d for q, k, d in zip(qs, ks, decay)]
    iuw = [_dot(a, x) for a, x in zip(intra, uw)]
    k_dec_t = [jnp.transpose(k * jnp.exp(gl - a)) for k, gl, a in zip(ks, g_last, gcb)]
    wu = [jnp.concatenate([x[:, GDN_DIM:], x[:, :GDN_DIM]], axis=1) for x in uw]
    pn0 = [_dot(jnp.where(first_cols, kt, 0.0), x) for kt, x in zip(k_dec_t, wu)]
    pn1 = [_dot(jnp.where(first_cols, 0.0, kt), x) for kt, x in zip(k_dec_t, wu)]
    for n in range(len(units)):
        q_eff = (qs[n] * eg[n] - iuw[n][:, GDN_DIM:]).astype(BF16)
        for half, pn in enumerate((pn0[n], pn1[n])):
            hr = slice(half * CHUNK, (half + 1) * CHUNK)
            pq_scr[n, half, 0:GDN_DIM, :] = pn[:, :GDN_DIM].astype(BF16)
            pq_scr[n, half, GDN_DIM:GDN_DIM + CHUNK, :] = q_eff[hr]
            n_scr[n, half] = pn[:, GDN_DIM:]
            oc_scr[n, half] = iuw[n][hr, :GDN_DIM]
            gl_scr[n, half] = jnp.exp(gcb[n][(half + 1) * CHUNK - 1:(half + 1) * CHUNK, :])

    def chunk_body(c, carry):
        u = c // 2
        half = c % 2
        rows = pl.ds(pl.multiple_of(c * CHUNK, CHUNK), CHUNK)
        states = [state_scr[h] for h in range(GDN_HEADS)]
        res = [jnp.dot(pq_scr[h * n_units + u, half], states[h].astype(BF16), preferred_element_type=F32)
               for h in range(GDN_HEADS)]
        for h in range(GDN_HEADS):
            n = h * n_units + u
            state_scr[h] = gl_scr[n, half] * states[h] - res[h][0:GDN_DIM] + n_scr[n, half]
            oraw_scr[h, rows, :] = res[h][GDN_DIM:GDN_DIM + CHUNK] + oc_scr[n, half]
        return carry

    lax.fori_loop(0, tb // CHUNK, chunk_body, 0)

    for h in range(GDN_HEADS):
        o = oraw_scr[h]
        ms = jnp.mean(o * o, axis=-1, keepdims=True)
        y = o * lax.rsqrt(ms + EPS) * ngain_ref[...] * _silu(gate_ref[0, :, cols_of(h)].astype(F32))
        o_ref[0, :, cols_of(h)] = y.astype(o_ref.dtype)


GDN_UNIT = 2 * CHUNK


def _gdn(proj, small, conv_w, alog_pad, dtb_pad, norm_gain, tb):
    b, s, _ = proj.shape
    nb = s // tb
    w3 = 3 * GDN_WIDTH
    assert tb % GDN_UNIT == 0
    n_hu = GDN_HEADS * (tb // GDN_UNIT)
    return pl.pallas_call(
        functools.partial(_gdn_kernel, tb=tb),
        out_shape=jax.ShapeDtypeStruct((b, s, GDN_WIDTH), BF16),
        grid=(b, nb),
        in_specs=[
            pl.BlockSpec((1, BF16_SUBLANES, w3),
                         lambda bi, i: (bi, jnp.maximum(i * (tb // BF16_SUBLANES) - 1, 0), PROJ_GDN // w3)),
            pl.BlockSpec((1, tb, w3), lambda bi, i: (bi, i, PROJ_GDN // w3)),
            pl.BlockSpec((1, tb, GDN_WIDTH), lambda bi, i: (bi, i, PROJ_GATE // GDN_WIDTH)),
            pl.BlockSpec((1, tb, LANES), lambda bi, i: (bi, i, 0)),
            pl.BlockSpec((CONV_K, w3), lambda bi, i: (0, 0)),
            pl.BlockSpec((1, LANES), lambda bi, i: (0, 0)),
            pl.BlockSpec((1, LANES), lambda bi, i: (0, 0)),
            pl.BlockSpec((1, GDN_DIM), lambda bi, i: (0, 0)),
        ],
        out_specs=pl.BlockSpec((1, tb, GDN_WIDTH), lambda bi, i: (bi, i, 0)),
        scratch_shapes=[
            pltpu.VMEM((tb + SUBLANES, w3), F32),
            pltpu.VMEM((tb, GDN_WIDTH), F32),
            pltpu.VMEM((tb, GDN_WIDTH), F32),
            pltpu.VMEM((tb, GDN_WIDTH), F32),
            pltpu.VMEM((n_hu, 2, GDN_DIM + CHUNK, GDN_DIM), BF16),
            pltpu.VMEM((n_hu, 2, GDN_DIM, GDN_DIM), F32),
            pltpu.VMEM((n_hu, 2, CHUNK, GDN_DIM), F32),
            pltpu.VMEM((n_hu, 2, 1, GDN_DIM), F32),
            pltpu.VMEM((GDN_HEADS, tb, GDN_DIM), F32),
            pltpu.VMEM((GDN_HEADS, GDN_DIM, GDN_DIM), F32),
        ],
        compiler_params=_params("parallel", "arbitrary"),
        name="gdn",
    )(proj, proj, proj, small, conv_w, alog_pad, dtb_pad, norm_gain)


def _merge_kernel(x_ref, ya_ref, yb_ref, ma_ref, mb_ref, wa_ref, wb_ref, wo_ref, g2_ref, wr_ref, br_ref,
                  x1_ref, h2_ref, route_ref, cnt_ref):
    tm = x_ref.shape[0]
    subs = [slice(r, r + MERGE_SUB) for r in range(0, tm, MERGE_SUB)]
    pa = [jnp.dot(ya_ref[rs, :], wa_ref[...], preferred_element_type=F32) for rs in subs]
    pb = [jnp.dot(yb_ref[rs, :], wb_ref[...], preferred_element_type=F32) for rs in subs]
    merged = [(_sigmoid(ma_ref[rs, :].astype(F32)) * a + _sigmoid(mb_ref[rs, :].astype(F32)) * b).astype(BF16)
              for rs, a, b in zip(subs, pa, pb)]
    x1s = [x_ref[rs, :] + jnp.dot(m, wo_ref[...], preferred_element_type=F32) for rs, m in zip(subs, merged)]
    h2s = [v * lax.rsqrt(jnp.mean(v * v, axis=-1, keepdims=True) + EPS) * g2_ref[...] for v in x1s]
    logit_s = [jnp.dot(v, wr_ref[...], preferred_element_type=F32, precision=lax.Precision.HIGHEST) for v in h2s]
    for rs, v, hh in zip(subs, x1s, h2s):
        x1_ref[rs, :] = v
        h2_ref[rs, :] = hh
    logits = jnp.concatenate(logit_s, axis=0) + br_ref[...]
    lane = lax.broadcasted_iota(jnp.int32, logits.shape, 1)
    big = jnp.int32(4 * LANES)
    gl = jnp.where(lane < MOE_GROUPS, logits, NEG_INF)
    gmax = jnp.max(gl, axis=-1, keepdims=True)
    gidx = jnp.min(jnp.where(gl == gmax, lane, big), axis=-1, keepdims=True)
    grp_w = 1.0 / jnp.sum(jnp.exp(gl - gmax), axis=-1, keepdims=True)
    lo = MOE_GROUPS + gidx * MOE_EXPERTS_PER_GROUP
    el = jnp.where((lane >= lo) & (lane < lo + MOE_EXPERTS_PER_GROUP), logits, NEG_INF)
    v1 = jnp.max(el, axis=-1, keepdims=True)
    i1 = jnp.min(jnp.where(el == v1, lane, big), axis=-1, keepdims=True)
    el2 = jnp.where(lane == i1, NEG_INF, el)
    v2 = jnp.max(el2, axis=-1, keepdims=True)
    i2 = jnp.min(jnp.where(el2 == v2, lane, big), axis=-1, keepdims=True)
    e2 = jnp.exp(v2 - v1)
    w1 = grp_w / (1.0 + e2)
    w2 = w1 * e2
    id1 = i1 - MOE_GROUPS
    id2 = i2 - MOE_GROUPS
    route_ref[...] = jnp.where(lane == 0, id1.astype(F32), jnp.where(lane == 1, id2.astype(F32),
                               jnp.where(lane == 2, w1, jnp.where(lane == 3, w2, 0.0))))

    @pl.when(pl.program_id(0) == 0)
    def _():
        cnt_ref[...] = jnp.zeros(cnt_ref.shape, F32)

    hits = jnp.where((lane == id1) | (lane == id2), 1.0, 0.0)
    cnt_ref[...] += jnp.broadcast_to(jnp.sum(hits, axis=0, keepdims=True), cnt_ref.shape)


MERGE_SUB = 256


def _merge(x2, ya, yb, proj, wa, wb, wo, g2, wr, br, tm):
    t, d = x2.shape
    row = lambda i: (i, 0)
    const = lambda i: (0, 0)
    return pl.pallas_call(
        _merge_kernel,
        out_shape=(
            jax.ShapeDtypeStruct((t, d), F32),
            jax.ShapeDtypeStruct((t, d), F32),
            jax.ShapeDtypeStruct((t, LANES), F32),
            jax.ShapeDtypeStruct((SUBLANES, LANES), F32),
        ),
        grid=(t // tm,),
        in_specs=[
            pl.BlockSpec((tm, d), row),
            pl.BlockSpec((tm, DIFF_WIDTH), row),
            pl.BlockSpec((tm, GDN_WIDTH), row),
            pl.BlockSpec((tm, d), lambda i: (i, PROJ_MA // D_MODEL)),
            pl.BlockSpec((tm, d), lambda i: (i, PROJ_MB // D_MODEL)),
            pl.BlockSpec((DIFF_WIDTH, d), const),
            pl.BlockSpec((GDN_WIDTH, d), const),
            pl.BlockSpec((d, d), const),
            pl.BlockSpec((1, d), const),
            pl.BlockSpec((d, LANES), const),
            pl.BlockSpec((1, LANES), const),
        ],
        out_specs=(pl.BlockSpec((tm, d), row), pl.BlockSpec((tm, d), row), pl.BlockSpec((tm, LANES), row),
                   pl.BlockSpec((SUBLANES, LANES), const)),
        compiler_params=_params("arbitrary"),
        name="merge_router",
    )(x2, ya, yb, proj, proj, wa, wb, wo, g2, wr, br)


MOE_BLK = 512
MOE_SUB = 256
MOE_META_LANES = 256
MOE_ROUTE_TILE = 512
MOE_DMA_TILE = 256


def _route_kernel(route_ref, cnt_ref, pos_ref, meta_ref, run_scr, *, n_rows):
    i = pl.program_id(0)
    tp = route_ref.shape[0]

    @pl.when(i == 0)
    def _():
        cnt = cnt_ref[...]
        jr = lax.broadcasted_iota(jnp.int32, (LANES, LANES), 0)
        jc = lax.broadcasted_iota(jnp.int32, (LANES, LANES), 1)
        upper = jnp.where(jr < jc, 1.0, 0.0)
        hi_prec = dict(preferred_element_type=F32, precision=lax.Precision.HIGHEST)
        off = jnp.dot(cnt, upper, **hi_prec)
        run_scr[...] = off[0:1]
        blk = float(MOE_BLK)
        first_tile = jnp.floor(off / blk)
        last_tile = jnp.floor((off + cnt - 1.0) / blk)
        n_it = jnp.where(cnt > 0.0, last_tile - first_tile + 1.0, 0.0)
        it_start = jnp.dot(n_it, upper, **hi_prec)
        it_end = it_start + n_it
        lane8 = lax.broadcasted_iota(jnp.int32, cnt.shape, 1)
        e_max = jnp.max(jnp.where(cnt > 0.0, lane8, 0), axis=-1, keepdims=True).astype(F32)[0:1]
        sub8 = lax.broadcasted_iota(jnp.int32, cnt.shape, 0)
        table = jnp.where(sub8 == 0, first_tile, jnp.where(sub8 == 1, it_start, jnp.where(
            sub8 == 2, off, jnp.where(sub8 == 3, cnt, it_end))))
        cols = jnp.transpose(table)
        shape = (LANES, MOE_META_LANES)
        e_sub = lax.broadcasted_iota(jnp.int32, shape, 0)
        w_lane = lax.broadcasted_iota(jnp.int32, shape, 1).astype(F32)
        col = lambda k: jnp.broadcast_to(cols[:, k:k + 1], shape)
        e_w = jnp.sum(jnp.where((e_sub < MOE_EXPERTS) & (col(4) <= w_lane), 1.0, 0.0), axis=0, keepdims=True)
        valid = e_w < float(MOE_EXPERTS)
        e_w = jnp.minimum(e_w, e_max)
        sel = e_sub.astype(F32) == e_w
        pick = lambda k: jnp.sum(jnp.where(sel, col(k), 0.0), axis=0, keepdims=True)
        w_row = w_lane[0:1]
        tile_w = jnp.where(valid, pick(0) + (w_row - pick(1)), float(n_rows // MOE_BLK - 1))
        lo_w = jnp.maximum(pick(2) - tile_w * blk, 0.0)
        hi_w = jnp.minimum(pick(2) + pick(3) - tile_w * blk, blk)
        lo_w = jnp.where(valid, lo_w, 0.0)
        hi_w = jnp.where(valid, hi_w, 0.0)
        sub_m = lax.broadcasted_iota(jnp.int32, meta_ref.shape, 0)
        bc = lambda v: jnp.broadcast_to(v, meta_ref.shape)
        meta_ref[...] = jnp.where(sub_m == 0, bc(e_w), jnp.where(sub_m == 1, bc(tile_w), jnp.where(
            sub_m == 2, bc(lo_w), bc(hi_w)))).astype(jnp.int32)

    r = route_ref[...]
    lane = lax.broadcasted_iota(jnp.int32, r.shape, 1)
    lane_f = lane.astype(F32)
    oh1 = lane_f == r[:, 0:1]
    oh2 = lane_f == r[:, 1:2]
    hits = jnp.where(oh1 | oh2, 1.0, 0.0)
    tr = lax.broadcasted_iota(jnp.int32, (tp, tp), 0)
    tc = lax.broadcasted_iota(jnp.int32, (tp, tp), 1)
    earlier = jnp.where(tc < tr, 1.0, 0.0).astype(BF16)
    rank = jnp.dot(earlier, hits.astype(BF16), preferred_element_type=F32)
    base = run_scr[...] + rank
    p1 = jnp.sum(jnp.where(oh1, base, 0.0), axis=-1, keepdims=True)
    p2 = jnp.sum(jnp.where(oh2, base, 0.0), axis=-1, keepdims=True)
    pos_ref[...] = jnp.where(lane == 0, p1, jnp.where(lane == 1, p2, 0.0)).astype(jnp.int32)
    run_scr[...] += jnp.sum(hits, axis=0, keepdims=True)


def _route_positions(route, cnt):
    t = route.shape[0]
    tp = min(MOE_ROUTE_TILE, t)
    n_rows = 2 * t
    assert n_rows % MOE_BLK == 0 and n_rows // MOE_BLK + MOE_EXPERTS <= MOE_META_LANES and n_rows < 2 ** 24
    return pl.pallas_call(
        functools.partial(_route_kernel, n_rows=n_rows),
        out_shape=(jax.ShapeDtypeStruct((t, LANES), jnp.int32),
                   jax.ShapeDtypeStruct((SUBLANES, MOE_META_LANES), jnp.int32)),
        grid=(t // tp,),
        in_specs=[pl.BlockSpec((tp, LANES), lambda i: (i, 0)),
                  pl.BlockSpec((SUBLANES, LANES), lambda i: (0, 0))],
        out_specs=(pl.BlockSpec((tp, LANES), lambda i: (i, 0)),
                   pl.BlockSpec((SUBLANES, MOE_META_LANES), lambda i: (0, 0))),
        scratch_shapes=[pltpu.VMEM((1, LANES), F32)],
        compiler_params=_params("arbitrary"),
        name="route_positions",
    )(route, cnt)


def _scatter_kernel(pos_ref, h2_ref, xs_ref, sem):
    ts = h2_ref.shape[0]

    def body(tok, carry):
        src = h2_ref.at[pl.ds(tok, 1), :]
        for k in range(2):
            pltpu.make_async_copy(src, xs_ref.at[pl.ds(pos_ref[0, 0, 2 * tok + k], 1), :], sem).start()
        return carry

    lax.fori_loop(0, ts, body, 0, unroll=8)
    for _ in range(2):
        pltpu.make_async_copy(h2_ref, xs_ref.at[pl.ds(0, ts), :], sem).wait()


def _scatter_rows(pos3, h2):
    t, d = h2.shape
    ts = pos3.shape[2] // 2
    return pl.pallas_call(
        _scatter_kernel,
        out_shape=jax.ShapeDtypeStruct((2 * t, d), F32),
        grid=(t // ts,),
        in_specs=[pl.BlockSpec((1, 1, 2 * ts), lambda i: (i, 0, 0), memory_space=pltpu.SMEM),
                  pl.BlockSpec((ts, d), lambda i: (i, 0))],
        out_specs=pl.BlockSpec(memory_space=pl.ANY),
        scratch_shapes=[pltpu.SemaphoreType.DMA],
        compiler_params=pltpu.CompilerParams(dimension_semantics=("arbitrary",), vmem_limit_bytes=VMEM_LIMIT,
                                             disable_bounds_checks=True),
        name="scatter_rows",
    )(pos3, h2)


def _expert_kernel(ie_ref, it_ref, lo_ref, hi_ref, xs_ref, wg_ref, wu_ref, wd_ref, y_ref, wg_scr, wu_scr, wd_scr):
    w = pl.program_id(0)
    prev = jnp.maximum(w - 1, 0)

    @pl.when((w == 0) | (ie_ref[w] != ie_ref[prev]))
    def _():
        wg_scr[...] = wg_ref[0].astype(BF16)
        wu_scr[...] = wu_ref[0].astype(BF16)
        wd_scr[...] = wd_ref[0].astype(BF16)

    @pl.when((w == 0) | (it_ref[w] != it_ref[prev]))
    def _():
        y_ref[...] = jnp.zeros(y_ref.shape, F32)

    lo = lo_ref[w]
    hi = hi_ref[w]

    @pl.when(hi > lo)
    def _():
        subs = [slice(r, r + MOE_SUB) for r in range(0, xs_ref.shape[0], MOE_SUB)]
        xs = [xs_ref[rs, :].astype(BF16) for rs in subs]
        hg = [jnp.dot(x, wg_scr[...], preferred_element_type=F32) for x in xs]
        hu = [jnp.dot(x, wu_scr[...], preferred_element_type=F32) for x in xs]
        act = [(_silu(g) * u).astype(BF16) for g, u in zip(hg, hu)]
        yp = [jnp.dot(a, wd_scr[...], preferred_element_type=F32) for a in act]
        for rs, v in zip(subs, yp):
            row = rs.start + lax.broadcasted_iota(jnp.int32, v.shape, 0)
            y_ref[rs, :] += jnp.where((row >= lo) & (row < hi), v, 0.0)


def _experts(meta, xs, wg, wu, wd):
    n_rows, d = xs.shape
    n_items = n_rows // MOE_BLK + MOE_EXPERTS
    ie, it, lo, hi = (meta[k, :n_items] for k in range(4))
    return pl.pallas_call(
        _expert_kernel,
        out_shape=jax.ShapeDtypeStruct((n_rows, d), F32),
        grid_spec=pltpu.PrefetchScalarGridSpec(
            num_scalar_prefetch=4,
            grid=(n_items,),
            in_specs=[
                pl.BlockSpec((MOE_BLK, d), lambda w, ie, it, lo, hi: (it[w], 0)),
                pl.BlockSpec((1, d, MOE_HIDDEN), lambda w, ie, it, lo, hi: (ie[w], 0, 0)),
                pl.BlockSpec((1, d, MOE_HIDDEN), lambda w, ie, it, lo, hi: (ie[w], 0, 0)),
                pl.BlockSpec((1, MOE_HIDDEN, d), lambda w, ie, it, lo, hi: (ie[w], 0, 0)),
            ],
            out_specs=pl.BlockSpec((MOE_BLK, d), lambda w, ie, it, lo, hi: (it[w], 0)),
            scratch_shapes=[pltpu.VMEM((d, MOE_HIDDEN), BF16), pltpu.VMEM((d, MOE_HIDDEN), BF16),
                            pltpu.VMEM((MOE_HIDDEN, d), BF16)],
        ),
        compiler_params=_params("arbitrary"),
        name="experts",
    )(ie, it, lo, hi, xs, wg, wu, wd)


def _combine_kernel(pos_ref, posn_ref, x1_ref, route_ref, gf_ref, y_ref, o_ref, ybuf, sem):
    i = pl.program_id(0)
    n = pl.num_programs(0)
    tc = x1_ref.shape[0]
    slot = i % 2

    def issue(p_ref, s):
        def body(tok, carry):
            for k in range(2):
                pltpu.make_async_copy(y_ref.at[pl.ds(p_ref[0, 0, 2 * tok + k], 1), :],
                                      ybuf.at[s, k, pl.ds(tok, 1), :], sem.at[s]).start()
            return carry

        lax.fori_loop(0, tc, body, 0, unroll=8)

    @pl.when(i == 0)
    def _():
        issue(pos_ref, 0)

    @pl.when(i + 1 < n)
    def _():
        issue(posn_ref, 1 - slot)

    for k in range(2):
        pltpu.make_async_copy(y_ref.at[pl.ds(0, tc), :], ybuf.at[slot, k], sem.at[slot]).wait()
    r = route_ref[...]
    x2 = x1_ref[...] + r[:, 2:3] * ybuf[slot, 0] + r[:, 3:4] * ybuf[slot, 1]
    ms = jnp.mean(x2 * x2, axis=-1, keepdims=True)
    o_ref[...] = x2 * lax.rsqrt(ms + EPS) * gf_ref[...]


def _combine(pos3, x1, route, gf, y):
    t, d = x1.shape
    tc = pos3.shape[2] // 2
    n = t // tc
    return pl.pallas_call(
        _combine_kernel,
        out_shape=jax.ShapeDtypeStruct((t, d), F32),
        grid=(n,),
        in_specs=[
            pl.BlockSpec((1, 1, 2 * tc), lambda i: (i, 0, 0), memory_space=pltpu.SMEM),
            pl.BlockSpec((1, 1, 2 * tc), lambda i: (jnp.minimum(i + 1, n - 1), 0, 0), memory_space=pltpu.SMEM),
            pl.BlockSpec((tc, d), lambda i: (i, 0)),
            pl.BlockSpec((tc, LANES), lambda i: (i, 0)),
            pl.BlockSpec((1, d), lambda i: (0, 0)),
            pl.BlockSpec(memory_space=pl.ANY),
        ],
        out_specs=pl.BlockSpec((tc, d), lambda i: (i, 0)),
        scratch_shapes=[pltpu.VMEM((2, 2, tc, d), F32), pltpu.SemaphoreType.DMA((2,))],
        compiler_params=pltpu.CompilerParams(dimension_semantics=("arbitrary",), vmem_limit_bytes=VMEM_LIMIT,
                                             disable_bounds_checks=True),
        name="combine_norm",
    )(pos3, pos3, x1, route, gf, y)


def _pad_lanes(v, offset):
    return jnp.zeros((1, LANES), F32).at[0, offset:offset + v.shape[0]].set(v.astype(F32))


def kernel(x, norm_mix_gain, w_in, diff_lambda_q1, diff_lambda_k1, diff_lambda_q2, diff_lambda_k2, diff_subln_gain, gdn_conv_w, gdn_a_log, gdn_dt_bias, gdn_norm_gain, w_branch_attn, w_branch_gdn, w_out, norm_ffn_gain, moe_w_group, moe_b_group, moe_w_expert, moe_b_expert, moe_w_gate, moe_w_up, moe_w_down, norm_final_gain):
    b, s, d = x.shape
    t = b * s
    x2 = x.reshape(t, d)

    w = w_in[0]
    small_lo = PROJ_GDN + 3 * GDN_WIDTH
    small_hi = small_lo + 2 * GDN_HEADS
    gate_hi = small_hi + GDN_WIDTH
    w_main = jnp.concatenate([w[:, :small_lo], w[:, gate_hi:], w[:, small_hi:gate_hi]], axis=1).astype(BF16)
    w_small = jnp.concatenate([w[:, small_lo:small_hi], jnp.zeros((d, LANES - 2 * GDN_HEADS), w.dtype)],
                              axis=1).astype(BF16)
    gain1 = norm_mix_gain[0].reshape(1, d)

    proj2, small2 = _inproj(x2, gain1, w_main, w_small, min(1024, t), PROJ_WIDTH // 2)
    proj = proj2.reshape(b, s, PROJ_WIDTH)

    lam_params = jnp.stack([diff_lambda_q1[0], diff_lambda_k1[0], diff_lambda_q2[0], diff_lambda_k2[0]]).astype(F32)
    ya = _diff_attention(lam_params, proj, diff_subln_gain[0].reshape(1, DIFF_V_DIM))

    yb = _gdn(proj, small2.reshape(b, s, LANES), gdn_conv_w[0], _pad_lanes(gdn_a_log[0], GDN_HEADS),
              _pad_lanes(gdn_dt_bias[0], GDN_HEADS), gdn_norm_gain[0].reshape(1, GDN_DIM), tb=min(512, s))

    wr = jnp.concatenate([moe_w_group[0], moe_w_expert[0],
                          jnp.zeros((d, LANES - MOE_GROUPS - MOE_EXPERTS), F32)], axis=1)
    br = _pad_lanes(jnp.concatenate([moe_b_group[0], moe_b_expert[0]]), 0)
    x1, h2, route, cnt = _merge(x2, ya.reshape(t, DIFF_WIDTH), yb.reshape(t, GDN_WIDTH), proj2,
                                w_branch_attn[0].astype(BF16), w_branch_gdn[0].astype(BF16),
                                w_out[0].astype(BF16), norm_ffn_gain[0].reshape(1, d), wr, br, tm=min(512, t))

    pos, meta = _route_positions(route, cnt)
    ts = min(MOE_DMA_TILE, t)
    pos3 = pos[:, :2].reshape(t // ts, 1, 2 * ts)
    xs = _scatter_rows(pos3, h2)
    y = _experts(meta, xs, moe_w_gate[0], moe_w_up[0], moe_w_down[0])
    out = _combine(pos3, x1, route, norm_final_gain.reshape(1, d), y)
    return out.reshape(b, s, d)
```

```python
import functools
import math

import jax
import jax.numpy as jnp
from jax import lax
from jax.experimental import pallas as pl
from jax.experimental.pallas import tpu as pltpu

F32 = jnp.float32
BF16 = jnp.bfloat16

D_MODEL = 1024
CHUNK = 64
EPS = 1e-6

DIFF_HEADS = 4
DIFF_HEAD_DIM = 64
DIFF_V_DIM = 2 * DIFF_HEAD_DIM
DIFF_WIDTH = DIFF_HEADS * DIFF_V_DIM

GDN_HEADS = 4
GDN_DIM = 128
GDN_WIDTH = GDN_HEADS * GDN_DIM
CONV_K = 4

MOE_GROUPS = 4
MOE_EXPERTS_PER_GROUP = 8
MOE_EXPERTS = MOE_GROUPS * MOE_EXPERTS_PER_GROUP
MOE_HIDDEN = 256

LANES = 128
SUBLANES = 8
BF16_SUBLANES = 16
VMEM_LIMIT = 56 * 1024 * 1024

LAMBDA_INIT = 0.8 - 0.6 * math.exp(-0.3 * 0)

PROJ_ATTN = 0
PROJ_GDN = 3 * DIFF_WIDTH
PROJ_MA = PROJ_GDN + 3 * GDN_WIDTH
PROJ_MB = PROJ_MA + D_MODEL
PROJ_GATE = PROJ_MB + D_MODEL
PROJ_WIDTH = PROJ_GATE + GDN_WIDTH

NEG_INF = float("-inf")
LOG2_E = math.log2(math.e)


def _params(*sem):
    return pltpu.CompilerParams(dimension_semantics=sem, vmem_limit_bytes=VMEM_LIMIT)


def _inproj_kernel(x_ref, g_ref, w_ref, ws_ref, o_ref, os_ref, h_scr):
    @pl.when(pl.program_id(1) == 0)
    def _():
        x = x_ref[...]
        ms = jnp.mean(x * x, axis=-1, keepdims=True)
        h_scr[...] = (x * lax.rsqrt(ms + EPS) * g_ref[...]).astype(BF16)
        os_ref[...] = jnp.dot(h_scr[...], ws_ref[...], preferred_element_type=F32)

    o_ref[...] = jnp.dot(h_scr[...], w_ref[...], preferred_element_type=F32).astype(o_ref.dtype)


def _inproj(x2, gain, w, w_small, tm, tn):
    t, d = x2.shape
    n = w.shape[1]
    return pl.pallas_call(
        _inproj_kernel,
        out_shape=(jax.ShapeDtypeStruct((t, n), BF16), jax.ShapeDtypeStruct((t, LANES), F32)),
        grid=(t // tm, n // tn),
        in_specs=[
            pl.BlockSpec((tm, d), lambda i, j: (i, 0)),
            pl.BlockSpec((1, d), lambda i, j: (0, 0)),
            pl.BlockSpec((d, tn), lambda i, j: (0, j)),
            pl.BlockSpec((d, LANES), lambda i, j: (0, 0)),
        ],
        out_specs=(pl.BlockSpec((tm, tn), lambda i, j: (i, j)), pl.BlockSpec((tm, LANES), lambda i, j: (i, 0))),
        scratch_shapes=[pltpu.VMEM((tm, d), BF16)],
        compiler_params=_params("parallel", "arbitrary"),
        name="inproj",
    )(x2, gain, w, w_small)


ATTN_TQ = 512
ATTN_TK = 512
MASK_BIAS = -1e30


ATTN_V_ROWS = DIFF_V_DIM + BF16_SUBLANES
ATTN_COLS = 128


def _attn_kernel(lam_ref, q_ref, k_ref, v_ref, gain_ref, o_ref,
                   qt_scr, vt_scr, s0_scr, s1_scr, p0_scr, p1_scr, a0_scr, a1_scr, m_scr, acc_scr, *, tq, tk):
    i = pl.program_id(2)
    n_q = 2 * tq
    ratio = tk // tq
    t_last = i // ratio
    n_pairs = (t_last + 2) // 2

    n_blk = n_q // ATTN_COLS
    half = tq // ATTN_COLS

    @pl.when(i == 0)
    def _():
        for c in range(v_ref.shape[1] // tk):
            v_t = jnp.transpose(v_ref[0, c * tk:(c + 1) * tk, :].astype(F32)).astype(BF16)
            for j in range(tk // LANES):
                vt_scr[c, j] = jnp.concatenate(
                    [v_t[:, j * LANES:(j + 1) * LANES], jnp.ones((BF16_SUBLANES, LANES), BF16)], axis=0)

    q_t = jnp.transpose(q_ref[0].astype(F32) * (DIFF_HEAD_DIM ** -0.5 * LOG2_E))
    dim = lax.broadcasted_iota(jnp.int32, (LANES, tq), 0)
    qcol = lax.broadcasted_iota(jnp.int32, (LANES, tq), 1)
    q_comp = (jnp.where(dim < DIFF_HEAD_DIM, q_t, 0.0).astype(BF16),
              jnp.where(dim >= DIFF_HEAD_DIM, q_t, 0.0).astype(BF16))
    q_slot = qcol // CHUNK + (i % ratio) * (tq // CHUNK)
    onehot = jnp.where(dim == q_slot, 1.0, 0.0).astype(BF16)
    for c in range(n_blk):
        qs = slice((c % half) * ATTN_COLS, (c % half + 1) * ATTN_COLS)
        qt_scr[c, 0:LANES, :] = q_comp[c // half][:, qs]
        qt_scr[c, LANES:2 * LANES, :] = onehot[:, qs]

    k_lane = lax.broadcasted_iota(jnp.int32, (tk, LANES), 1)
    k_chunk = lax.broadcasted_iota(jnp.int32, (tk, LANES), 0) // CHUNK
    n_slots = tk // CHUNK
    diag_bias = jnp.where((k_lane < n_slots) & (k_chunk > k_lane), MASK_BIAS, 0.0).astype(BF16)
    full_bias = jnp.where(k_lane < n_slots, MASK_BIAS, 0.0).astype(BF16)
    zero_bias = jnp.zeros((tk, LANES), BF16)

    m_scr[...] = jnp.full(m_scr.shape, NEG_INF, F32)
    acc_scr[...] = jnp.zeros(acc_scr.shape, F32)
    p1_scr[...] = jnp.zeros(p1_scr.shape, BF16)
    a1_scr[...] = jnp.ones(a1_scr.shape, F32)

    def scores(t, s_scr):
        bias = jnp.where(t == t_last, diag_bias, jnp.where(t > t_last, full_bias, zero_bias))
        rows = pl.ds(pl.multiple_of(jnp.minimum(t, t_last) * tk, tk), tk)
        k_aug = jnp.concatenate([k_ref[0, rows, :], bias], axis=1)
        q_all = jnp.concatenate([qt_scr[c] for c in range(n_blk)], axis=1)
        s = jnp.dot(k_aug, q_all, preferred_element_type=F32)
        for c in range(n_blk):
            s_scr[c] = s[:, c * ATTN_COLS:(c + 1) * ATTN_COLS]

    def softmax(s_scr, p_scr, a_scr):
        for c in range(n_blk):
            cs = slice(c * ATTN_COLS, (c + 1) * ATTN_COLS)
            s = s_scr[c]
            m_prev = m_scr[:, cs]
            m_new = jnp.maximum(m_prev, jnp.max(s, axis=0, keepdims=True))
            a_scr[:, cs] = jnp.exp2(m_prev - m_new)
            m_scr[:, cs] = m_new
            p_scr[c] = jnp.exp2(s - m_new).astype(BF16)

    def values(t, p_scr, a_scr):
        tile = jnp.minimum(t, t_last)
        v_all = jnp.concatenate([vt_scr[tile, j] for j in range(tk // LANES)], axis=1)
        p = jnp.concatenate([p_scr[c] for c in range(n_blk)], axis=1)
        pv = jnp.dot(v_all, p, preferred_element_type=F32)
        for c in range(n_blk):
            cs = slice(c * ATTN_COLS, (c + 1) * ATTN_COLS)
            acc_scr[c] = a_scr[:, cs] * acc_scr[c] + pv[:, cs]

    scores(0, s0_scr)

    def pair_step(g, carry):
        t0 = 2 * g
        scores(t0 + 1, s1_scr)
        softmax(s0_scr, p0_scr, a0_scr)
        values(jnp.maximum(t0 - 1, 0), p1_scr, a1_scr)
        scores(t0 + 2, s0_scr)
        softmax(s1_scr, p1_scr, a1_scr)
        values(t0, p0_scr, a0_scr)
        return carry

    lax.fori_loop(0, n_pairs, pair_step, 0)
    values(2 * n_pairs - 1, p1_scr, a1_scr)

    lp = lam_ref[...]
    lam = (jnp.exp(jnp.sum(lp[0:1] * lp[1:2], axis=-1, keepdims=True))
           - jnp.exp(jnp.sum(lp[2:3] * lp[3:4], axis=-1, keepdims=True)) + LAMBDA_INIT)
    for c in range(half):
        o0_t = acc_scr[c, 0:DIFF_V_DIM, :] / acc_scr[c, DIFF_V_DIM:DIFF_V_DIM + 1, :]
        o1_t = acc_scr[half + c, 0:DIFF_V_DIM, :] / acc_scr[half + c, DIFF_V_DIM:DIFF_V_DIM + 1, :]
        o = jnp.transpose(o0_t - lam * o1_t)
        ms = jnp.mean(o * o, axis=-1, keepdims=True)
        y = (o * lax.rsqrt(ms + EPS) * gain_ref[...]) * (1.0 - LAMBDA_INIT)
        o_ref[0, c * ATTN_COLS:(c + 1) * ATTN_COLS, :] = y.astype(o_ref.dtype)


def _diff_attention(lam_params, qkv, subln_gain):
    b, s, _ = qkv.shape
    tq = min(ATTN_TQ, s)
    tk = min(ATTN_TK, s)
    assert s % tk == 0 and tk % tq == 0 and tq % CHUNK == 0 and tk // CHUNK <= LANES
    n_q = 2 * tq
    return pl.pallas_call(
        functools.partial(_attn_kernel, tq=tq, tk=tk),
        out_shape=jax.ShapeDtypeStruct((b, s, DIFF_WIDTH), BF16),
        grid=(b, DIFF_HEADS, s // tq),
        in_specs=[
            pl.BlockSpec((4, DIFF_HEAD_DIM), lambda bi, h, i: (0, 0)),
            pl.BlockSpec((1, tq, DIFF_V_DIM), lambda bi, h, i: (bi, i, h)),
            pl.BlockSpec((1, s, DIFF_V_DIM), lambda bi, h, i: (bi, 0, DIFF_HEADS + h)),
            pl.BlockSpec((1, s, DIFF_V_DIM), lambda bi, h, i: (bi, 0, 2 * DIFF_HEADS + h)),
            pl.BlockSpec((1, DIFF_V_DIM), lambda bi, h, i: (0, 0)),
        ],
        out_specs=pl.BlockSpec((1, tq, DIFF_V_DIM), lambda bi, h, i: (bi, i, h)),
        scratch_shapes=[
            pltpu.VMEM((n_q // ATTN_COLS, 2 * LANES, ATTN_COLS), BF16),
            pltpu.VMEM((s // tk, tk // LANES, ATTN_V_ROWS, LANES), BF16),
            pltpu.VMEM((n_q // ATTN_COLS, tk, ATTN_COLS), F32),
            pltpu.VMEM((n_q // ATTN_COLS, tk, ATTN_COLS), F32),
            pltpu.VMEM((n_q // ATTN_COLS, tk, ATTN_COLS), BF16),
            pltpu.VMEM((n_q // ATTN_COLS, tk, ATTN_COLS), BF16),
            pltpu.VMEM((1, n_q), F32),
            pltpu.VMEM((1, n_q), F32),
            pltpu.VMEM((1, n_q), F32),
            pltpu.VMEM((n_q // ATTN_COLS, ATTN_V_ROWS, ATTN_COLS), F32),
        ],
        compiler_params=_params("parallel", "parallel", "arbitrary"),
        name="diff_attn",
    )(lam_params, qkv, qkv, qkv, subln_gain)


def _silu(x):
    return x * (1.0 / (1.0 + jnp.exp(-x)))


def _sigmoid(x):
    return 1.0 / (1.0 + jnp.exp(-x))


def _softplus(x):
    return jnp.maximum(x, 0.0) + jnp.log(1.0 + jnp.exp(-jnp.abs(x)))


def _dot(a, b):
    return jnp.dot(a.astype(BF16), b.astype(BF16), preferred_element_type=F32)


def _dot_nt(a, b):
    return lax.dot_general(a.astype(BF16), b.astype(BF16), (((1,), (1,)), ((), ())), preferred_element_type=F32)


def _gdn_kernel(prev_ref, qkv_ref, gate_ref, small_ref, convw_ref, alog_ref, dtb_ref, ngain_ref, o_ref,
                xp_scr, q_scr, k_scr, v_scr, pq_scr, n_scr, oc_scr, gl_scr, oraw_scr, state_scr, *, tb):
    i = pl.program_id(1)

    @pl.when(i == 0)
    def _():
        state_scr[...] = jnp.zeros(state_scr.shape, F32)

    n_sec = 3 * GDN_HEADS
    prev = jnp.where(i == 0, 0.0, prev_ref[0, BF16_SUBLANES - SUBLANES:BF16_SUBLANES, :].astype(F32))
    for sec in range(n_sec):
        xp_scr[sec, 0:SUBLANES, :] = prev[:, sec * LANES:(sec + 1) * LANES]
    for r0 in range(0, tb, GDN_UNIT):
        blk = qkv_ref[0, r0:r0 + GDN_UNIT, :].astype(F32)
        for sec in range(n_sec):
            xp_scr[sec, SUBLANES + r0:SUBLANES + r0 + GDN_UNIT, :] = blk[:, sec * LANES:(sec + 1) * LANES]
    for sec in range(n_sec):
        cols = slice(sec * LANES, (sec + 1) * LANES)
        acc = None
        for jj in range(CONV_K):
            start = SUBLANES - (CONV_K - 1) + jj
            term = xp_scr[sec, start:start + tb, :] * convw_ref[jj:jj + 1, cols]
            acc = term if acc is None else acc + term
        y = _silu(acc)
        which, head = divmod(sec, GDN_HEADS)
        if which == 0:
            q_scr[head] = y * lax.rsqrt(jnp.sum(y * y, axis=-1, keepdims=True) + EPS) * (GDN_DIM ** -0.5)
        elif which == 1:
            k_scr[head] = y * lax.rsqrt(jnp.sum(y * y, axis=-1, keepdims=True) + EPS)
        else:
            v_scr[head] = y

    sm = small_ref[0]
    beta_all = _sigmoid(sm)
    g_all = -jnp.exp(alog_ref[...]) * _softplus(sm + dtb_ref[...])

    br = lax.broadcasted_iota(jnp.int32, (tb, tb), 0)
    bc = lax.broadcasted_iota(jnp.int32, (tb, tb), 1)
    block_tril = jnp.where((br // CHUNK == bc // CHUNK) & (bc <= br), 1.0, 0.0)
    gc_all = jnp.dot(block_tril, g_all, preferred_element_type=F32, precision=lax.Precision.HIGHEST)
    gct_all = jnp.transpose(gc_all)

    ur = lax.broadcasted_iota(jnp.int32, (GDN_UNIT, GDN_UNIT), 0)
    uc = lax.broadcasted_iota(jnp.int32, (GDN_UNIT, GDN_UNIT), 1)
    same = (ur // CHUNK) == (uc // CHUNK)
    causal = same & (uc <= ur)
    strict = same & (uc < ur)
    eye = jnp.where(ur == uc, 1.0, 0.0)
    first_rows = ur < CHUNK
    first_cols = uc < CHUNK
    n_units = tb // GDN_UNIT
    units = [(h, u) for h in range(GDN_HEADS) for u in range(n_units)]

    def lane_bcast(x, lane):
        return jnp.broadcast_to(x[:, lane:lane + 1], (GDN_UNIT, GDN_UNIT))

    def row_bcast(x, r):
        return jnp.broadcast_to(x[r:r + 1, :], (GDN_UNIT, GDN_UNIT))

    def rows_of(u):
        return slice(u * GDN_UNIT, (u + 1) * GDN_UNIT)

    def cols_of(h):
        return slice(h * LANES, (h + 1) * LANES)

    qs = [q_scr[h, rows_of(u), :] for h, u in units]
    ks = [k_scr[h, rows_of(u), :] for h, u in units]
    vs = [v_scr[h, rows_of(u), :] for h, u in units]
    betas = [lane_bcast(beta_all[rows_of(u)], h) for h, u in units]
    gcb = [lane_bcast(gc_all[rows_of(u)], GDN_HEADS + h) for h, u in units]
    gcr = [row_bcast(gct_all[:, rows_of(u)], GDN_HEADS + h) for h, u in units]
    decay = [jnp.exp(jnp.where(causal, a - b, NEG_INF)) for a, b in zip(gcb, gcr)]
    eg = [jnp.exp(a) for a in gcb]
    g_last = [jnp.where(first_rows, row_bcast(a, CHUNK - 1), row_bcast(a, GDN_UNIT - 1)) for a in gcb]
    kb = [k * b for k, b in zip(ks, betas)]
    vb = [v * b for v, b in zip(vs, betas)]
    a_low = [jnp.where(strict, _dot_nt(x, k) * d, 0.0) for x, k, d in zip(kb, ks, decay)]
    tinv = [eye - a for a in a_low]
    pw = a_low
    for _ in range(5):
        pw = [_dot(x, x) for x in pw]
        tinv = [t + _dot(t, x) for t, x in zip(tinv, pw)]
    uw = [_dot(t, jnp.concatenate([v, x * e], axis=1)) for t, v, x, e in zip(tinv, vb, kb, eg)]
    intra = [_dot_nt(q, k) * d for q, k, d in zip(qs, ks, decay)]
    iuw = [_dot(a, x) for a, x in zip(intra, uw)]
    k_dec_t = [jnp.transpose(k * jnp.exp(gl - a)) for k, gl, a in zip(ks, g_last, gcb)]
    wu = [jnp.concatenate([x[:, GDN_DIM:], x[:, :GDN_DIM]], axis=1) for x in uw]
    pn0 = [_dot(jnp.where(first_cols, kt, 0.0), x) for kt, x in zip(k_dec_t, wu)]
    pn1 = [_dot(jnp.where(first_cols, 0.0, kt), x) for kt, x in zip(k_dec_t, wu)]
    for n in range(len(units)):
        q_eff = (qs[n] * eg[n] - iuw[n][:, GDN_DIM:]).astype(BF16)
        for half, pn in enumerate((pn0[n], pn1[n])):
            hr = slice(half * CHUNK, (half + 1) * CHUNK)
            pq_scr[n, half, 0:GDN_DIM, :] = pn[:, :GDN_DIM].astype(BF16)
            pq_scr[n, half, GDN_DIM:GDN_DIM + CHUNK, :] = q_eff[hr]
            n_scr[n, half] = pn[:, GDN_DIM:]
            oc_scr[n, half] = iuw[n][hr, :GDN_DIM]
            gl_scr[n, half] = jnp.exp(gcb[n][(half + 1) * CHUNK - 1:(half + 1) * CHUNK, :])

    def chunk_body(c, carry):
        u = c // 2
        half = c % 2
        rows = pl.ds(pl.multiple_of(c * CHUNK, CHUNK), CHUNK)
        states = [state_scr[h] for h in range(GDN_HEADS)]
        res = [jnp.dot(pq_scr[h * n_units + u, half], states[h].astype(BF16), preferred_element_type=F32)
               for h in range(GDN_HEADS)]
        for h in range(GDN_HEADS):
            n = h * n_units + u
            state_scr[h] = gl_scr[n, half] * states[h] - res[h][0:GDN_DIM] + n_scr[n, half]
            oraw_scr[h, rows, :] = res[h][GDN_DIM:GDN_DIM + CHUNK] + oc_scr[n, half]
        return carry

    lax.fori_loop(0, tb // CHUNK, chunk_body, 0)

    for h in range(GDN_HEADS):
        o = oraw_scr[h]
        ms = jnp.mean(o * o, axis=-1, keepdims=True)
        y = o * lax.rsqrt(ms + EPS) * ngain_ref[...] * _silu(gate_ref[0, :, cols_of(h)].astype(F32))
        o_ref[0, :, cols_of(h)] = y.astype(o_ref.dtype)


GDN_UNIT = 2 * CHUNK


def _gdn(proj, small, conv_w, alog_pad, dtb_pad, norm_gain, tb):
    b, s, _ = proj.shape
    nb = s // tb
    w3 = 3 * GDN_WIDTH
    assert tb % GDN_UNIT == 0
    n_hu = GDN_HEADS * (tb // GDN_UNIT)
    return pl.pallas_call(
        functools.partial(_gdn_kernel, tb=tb),
        out_shape=jax.ShapeDtypeStruct((b, s, GDN_WIDTH), BF16),
        grid=(b, nb),
        in_specs=[
            pl.BlockSpec((1, BF16_SUBLANES, w3),
                         lambda bi, i: (bi, jnp.maximum(i * (tb // BF16_SUBLANES) - 1, 0), PROJ_GDN // w3)),
            pl.BlockSpec((1, tb, w3), lambda bi, i: (bi, i, PROJ_GDN // w3)),
            pl.BlockSpec((1, tb, GDN_WIDTH), lambda bi, i: (bi, i, PROJ_GATE // GDN_WIDTH)),
            pl.BlockSpec((1, tb, LANES), lambda bi, i: (bi, i, 0)),
            pl.BlockSpec((CONV_K, w3), lambda bi, i: (0, 0)),
            pl.BlockSpec((1, LANES), lambda bi, i: (0, 0)),
            pl.BlockSpec((1, LANES), lambda bi, i: (0, 0)),
            pl.BlockSpec((1, GDN_DIM), lambda bi, i: (0, 0)),
        ],
        out_specs=pl.BlockSpec((1, tb, GDN_WIDTH), lambda bi, i: (bi, i, 0)),
        scratch_shapes=[
            pltpu.VMEM((w3 // LANES, tb + SUBLANES, LANES), F32),
            pltpu.VMEM((GDN_HEADS, tb, GDN_DIM), F32),
            pltpu.VMEM((GDN_HEADS, tb, GDN_DIM), F32),
            pltpu.VMEM((GDN_HEADS, tb, GDN_DIM), F32),
            pltpu.VMEM((n_hu, 2, GDN_DIM + CHUNK, GDN_DIM), BF16),
            pltpu.VMEM((n_hu, 2, GDN_DIM, GDN_DIM), F32),
            pltpu.VMEM((n_hu, 2, CHUNK, GDN_DIM), F32),
            pltpu.VMEM((n_hu, 2, 1, GDN_DIM), F32),
            pltpu.VMEM((GDN_HEADS, tb, GDN_DIM), F32),
            pltpu.VMEM((GDN_HEADS, GDN_DIM, GDN_DIM), F32),
        ],
        compiler_params=_params("parallel", "arbitrary"),
        name="gdn",
    )(proj, proj, proj, small, conv_w, alog_pad, dtb_pad, norm_gain)


def _merge_kernel(x_ref, ya_ref, yb_ref, ma_ref, mb_ref, wa_ref, wb_ref, wo_ref, g2_ref, wr_ref, br_ref,
                  x1_ref, h2_ref, route_ref, cnt_ref):
    tm = x_ref.shape[0]
    subs = [slice(r, r + MERGE_SUB) for r in range(0, tm, MERGE_SUB)]
    pa = [jnp.dot(ya_ref[rs, :], wa_ref[...], preferred_element_type=F32) for rs in subs]
    pb = [jnp.dot(yb_ref[rs, :], wb_ref[...], preferred_element_type=F32) for rs in subs]
    merged = [(_sigmoid(ma_ref[rs, :].astype(F32)) * a + _sigmoid(mb_ref[rs, :].astype(F32)) * b).astype(BF16)
              for rs, a, b in zip(subs, pa, pb)]
    x1s = [x_ref[rs, :] + jnp.dot(m, wo_ref[...], preferred_element_type=F32) for rs, m in zip(subs, merged)]
    h2s = [v * lax.rsqrt(jnp.mean(v * v, axis=-1, keepdims=True) + EPS) * g2_ref[...] for v in x1s]
    logit_s = [jnp.dot(v, wr_ref[...], preferred_element_type=F32, precision=lax.Precision.HIGHEST) for v in h2s]
    for rs, v, hh in zip(subs, x1s, h2s):
        x1_ref[rs, :] = v
        h2_ref[rs, :] = hh
    logits = jnp.concatenate(logit_s, axis=0) + br_ref[...]
    lane = lax.broadcasted_iota(jnp.int32, logits.shape, 1)
    big = jnp.int32(4 * LANES)
    gl = jnp.where(lane < MOE_GROUPS, logits, NEG_INF)
    gmax = jnp.max(gl, axis=-1, keepdims=True)
    gidx = jnp.min(jnp.where(gl == gmax, lane, big), axis=-1, keepdims=True)
    grp_w = 1.0 / jnp.sum(jnp.exp(gl - gmax), axis=-1, keepdims=True)
    lo = MOE_GROUPS + gidx * MOE_EXPERTS_PER_GROUP
    el = jnp.where((lane >= lo) & (lane < lo + MOE_EXPERTS_PER_GROUP), logits, NEG_INF)
    v1 = jnp.max(el, axis=-1, keepdims=True)
    i1 = jnp.min(jnp.where(el == v1, lane, big), axis=-1, keepdims=True)
    el2 = jnp.where(lane == i1, NEG_INF, el)
    v2 = jnp.max(el2, axis=-1, keepdims=True)
    i2 = jnp.min(jnp.where(el2 == v2, lane, big), axis=-1, keepdims=True)
    e2 = jnp.exp(v2 - v1)
    w1 = grp_w / (1.0 + e2)
    w2 = w1 * e2
    id1 = i1 - MOE_GROUPS
    id2 = i2 - MOE_GROUPS
    route_ref[...] = jnp.where(lane == 0, id1.astype(F32), jnp.where(lane == 1, id2.astype(F32),
                               jnp.where(lane == 2, w1, jnp.where(lane == 3, w2, 0.0))))

    @pl.when(pl.program_id(0) == 0)
    def _():
        cnt_ref[...] = jnp.zeros(cnt_ref.shape, F32)

    hits = jnp.where((lane == id1) | (lane == id2), 1.0, 0.0)
    cnt_ref[...] += jnp.broadcast_to(jnp.sum(hits, axis=0, keepdims=True), cnt_ref.shape)


MERGE_SUB = 256


def _merge(x2, ya, yb, proj, wa, wb, wo, g2, wr, br, tm):
    t, d = x2.shape
    row = lambda i: (i, 0)
    const = lambda i: (0, 0)
    return pl.pallas_call(
        _merge_kernel,
        out_shape=(
            jax.ShapeDtypeStruct((t, d), F32),
            jax.ShapeDtypeStruct((t, d), F32),
            jax.ShapeDtypeStruct((t, LANES), F32),
            jax.ShapeDtypeStruct((SUBLANES, LANES), F32),
        ),
        grid=(t // tm,),
        in_specs=[
            pl.BlockSpec((tm, d), row),
            pl.BlockSpec((tm, DIFF_WIDTH), row),
            pl.BlockSpec((tm, GDN_WIDTH), row),
            pl.BlockSpec((tm, d), lambda i: (i, PROJ_MA // D_MODEL)),
            pl.BlockSpec((tm, d), lambda i: (i, PROJ_MB // D_MODEL)),
            pl.BlockSpec((DIFF_WIDTH, d), const),
            pl.BlockSpec((GDN_WIDTH, d), const),
            pl.BlockSpec((d, d), const),
            pl.BlockSpec((1, d), const),
            pl.BlockSpec((d, LANES), const),
            pl.BlockSpec((1, LANES), const),
        ],
        out_specs=(pl.BlockSpec((tm, d), row), pl.BlockSpec((tm, d), row), pl.BlockSpec((tm, LANES), row),
                   pl.BlockSpec((SUBLANES, LANES), const)),
        compiler_params=_params("arbitrary"),
        name="merge_router",
    )(x2, ya, yb, proj, proj, wa, wb, wo, g2, wr, br)


MOE_BLK = 512
MOE_SUB = 256
MOE_META_LANES = 256
MOE_ROUTE_TILE = 512
MOE_DMA_TILE = 256


def _route_kernel(route_ref, cnt_ref, pos_ref, meta_ref, run_scr, *, n_rows):
    i = pl.program_id(0)
    tp = route_ref.shape[0]

    @pl.when(i == 0)
    def _():
        cnt = cnt_ref[...]
        jr = lax.broadcasted_iota(jnp.int32, (LANES, LANES), 0)
        jc = lax.broadcasted_iota(jnp.int32, (LANES, LANES), 1)
        upper = jnp.where(jr < jc, 1.0, 0.0)
        hi_prec = dict(preferred_element_type=F32, precision=lax.Precision.HIGHEST)
        off = jnp.dot(cnt, upper, **hi_prec)
        run_scr[...] = off[0:1]
        blk = float(MOE_BLK)
        first_tile = jnp.floor(off / blk)
        last_tile = jnp.floor((off + cnt - 1.0) / blk)
        n_it = jnp.where(cnt > 0.0, last_tile - first_tile + 1.0, 0.0)
        it_start = jnp.dot(n_it, upper, **hi_prec)
        it_end = it_start + n_it
        lane8 = lax.broadcasted_iota(jnp.int32, cnt.shape, 1)
        e_max = jnp.max(jnp.where(cnt > 0.0, lane8, 0), axis=-1, keepdims=True).astype(F32)[0:1]
        sub8 = lax.broadcasted_iota(jnp.int32, cnt.shape, 0)
        table = jnp.where(sub8 == 0, first_tile, jnp.where(sub8 == 1, it_start, jnp.where(
            sub8 == 2, off, jnp.where(sub8 == 3, cnt, it_end))))
        cols = jnp.transpose(table)
        shape = (LANES, MOE_META_LANES)
        e_sub = lax.broadcasted_iota(jnp.int32, shape, 0)
        w_lane = lax.broadcasted_iota(jnp.int32, shape, 1).astype(F32)
        col = lambda k: jnp.broadcast_to(cols[:, k:k + 1], shape)
        e_w = jnp.sum(jnp.where((e_sub < MOE_EXPERTS) & (col(4) <= w_lane), 1.0, 0.0), axis=0, keepdims=True)
        valid = e_w < float(MOE_EXPERTS)
        e_w = jnp.minimum(e_w, e_max)
        sel = e_sub.astype(F32) == e_w
        pick = lambda k: jnp.sum(jnp.where(sel, col(k), 0.0), axis=0, keepdims=True)
        w_row = w_lane[0:1]
        tile_w = jnp.where(valid, pick(0) + (w_row - pick(1)), float(n_rows // MOE_BLK - 1))
        lo_w = jnp.maximum(pick(2) - tile_w * blk, 0.0)
        hi_w = jnp.minimum(pick(2) + pick(3) - tile_w * blk, blk)
        lo_w = jnp.where(valid, lo_w, 0.0)
        hi_w = jnp.where(valid, hi_w, 0.0)
        sub_m = lax.broadcasted_iota(jnp.int32, meta_ref.shape, 0)
        bc = lambda v: jnp.broadcast_to(v, meta_ref.shape)
        meta_ref[...] = jnp.where(sub_m == 0, bc(e_w), jnp.where(sub_m == 1, bc(tile_w), jnp.where(
            sub_m == 2, bc(lo_w), bc(hi_w)))).astype(jnp.int32)

    r = route_ref[...]
    lane = lax.broadcasted_iota(jnp.int32, r.shape, 1)
    lane_f = lane.astype(F32)
    oh1 = lane_f == r[:, 0:1]
    oh2 = lane_f == r[:, 1:2]
    hits = jnp.where(oh1 | oh2, 1.0, 0.0)
    tr = lax.broadcasted_iota(jnp.int32, (tp, tp), 0)
    tc = lax.broadcasted_iota(jnp.int32, (tp, tp), 1)
    earlier = jnp.where(tc < tr, 1.0, 0.0).astype(BF16)
    rank = jnp.dot(earlier, hits.astype(BF16), preferred_element_type=F32)
    base = run_scr[...] + rank
    p1 = jnp.sum(jnp.where(oh1, base, 0.0), axis=-1, keepdims=True)
    p2 = jnp.sum(jnp.where(oh2, base, 0.0), axis=-1, keepdims=True)
    pos_ref[...] = jnp.where(lane == 0, p1, jnp.where(lane == 1, p2, 0.0)).astype(jnp.int32)
    run_scr[...] += jnp.sum(hits, axis=0, keepdims=True)


def _route_positions(route, cnt):
    t = route.shape[0]
    tp = min(MOE_ROUTE_TILE, t)
    n_rows = 2 * t
    assert n_rows % MOE_BLK == 0 and n_rows // MOE_BLK + MOE_EXPERTS <= MOE_META_LANES and n_rows < 2 ** 24
    return pl.pallas_call(
        functools.partial(_route_kernel, n_rows=n_rows),
        out_shape=(jax.ShapeDtypeStruct((t, LANES), jnp.int32),
                   jax.ShapeDtypeStruct((SUBLANES, MOE_META_LANES), jnp.int32)),
        grid=(t // tp,),
        in_specs=[pl.BlockSpec((tp, LANES), lambda i: (i, 0)),
                  pl.BlockSpec((SUBLANES, LANES), lambda i: (0, 0))],
        out_specs=(pl.BlockSpec((tp, LANES), lambda i: (i, 0)),
                   pl.BlockSpec((SUBLANES, MOE_META_LANES), lambda i: (0, 0))),
        scratch_shapes=[pltpu.VMEM((1, LANES), F32)],
        compiler_params=_params("arbitrary"),
        name="route_positions",
    )(route, cnt)


def _scatter_kernel(pos_ref, h2_ref, xs_ref, sem):
    ts = h2_ref.shape[0]

    def body(tok, carry):
        src = h2_ref.at[pl.ds(tok, 1), :]
        for k in range(2):
            pltpu.make_async_copy(src, xs_ref.at[pl.ds(pos_ref[0, 0, 2 * tok + k], 1), :], sem).start()
        return carry

    lax.fori_loop(0, ts, body, 0, unroll=8)
    for _ in range(2):
        pltpu.make_async_copy(h2_ref, xs_ref.at[pl.ds(0, ts), :], sem).wait()


def _scatter_rows(pos3, h2):
    t, d = h2.shape
    ts = pos3.shape[2] // 2
    return pl.pallas_call(
        _scatter_kernel,
        out_shape=jax.ShapeDtypeStruct((2 * t, d), F32),
        grid=(t // ts,),
        in_specs=[pl.BlockSpec((1, 1, 2 * ts), lambda i: (i, 0, 0), memory_space=pltpu.SMEM),
                  pl.BlockSpec((ts, d), lambda i: (i, 0))],
        out_specs=pl.BlockSpec(memory_space=pl.ANY),
        scratch_shapes=[pltpu.SemaphoreType.DMA],
        compiler_params=pltpu.CompilerParams(dimension_semantics=("arbitrary",), vmem_limit_bytes=VMEM_LIMIT,
                                             disable_bounds_checks=True),
        name="scatter_rows",
    )(pos3, h2)


def _expert_kernel(ie_ref, it_ref, lo_ref, hi_ref, xs_ref, wg_ref, wu_ref, wd_ref, y_ref, wg_scr, wu_scr, wd_scr):
    w = pl.program_id(0)
    prev = jnp.maximum(w - 1, 0)

    @pl.when((w == 0) | (ie_ref[w] != ie_ref[prev]))
    def _():
        wg_scr[...] = wg_ref[0].astype(BF16)
        wu_scr[...] = wu_ref[0].astype(BF16)
        wd_scr[...] = wd_ref[0].astype(BF16)

    @pl.when((w == 0) | (it_ref[w] != it_ref[prev]))
    def _():
        y_ref[...] = jnp.zeros(y_ref.shape, F32)

    lo = lo_ref[w]
    hi = hi_ref[w]

    @pl.when(hi > lo)
    def _():
        subs = [slice(r, r + MOE_SUB) for r in range(0, xs_ref.shape[0], MOE_SUB)]
        xs = [xs_ref[rs, :].astype(BF16) for rs in subs]
        hg = [jnp.dot(x, wg_scr[...], preferred_element_type=F32) for x in xs]
        hu = [jnp.dot(x, wu_scr[...], preferred_element_type=F32) for x in xs]
        act = [(_silu(g) * u).astype(BF16) for g, u in zip(hg, hu)]
        yp = [jnp.dot(a, wd_scr[...], preferred_element_type=F32) for a in act]
        for rs, v in zip(subs, yp):
            row = rs.start + lax.broadcasted_iota(jnp.int32, v.shape, 0)
            y_ref[rs, :] += jnp.where((row >= lo) & (row < hi), v, 0.0)


def _experts(meta, xs, wg, wu, wd):
    n_rows, d = xs.shape
    n_items = n_rows // MOE_BLK + MOE_EXPERTS
    ie, it, lo, hi = (meta[k, :n_items] for k in range(4))
    return pl.pallas_call(
        _expert_kernel,
        out_shape=jax.ShapeDtypeStruct((n_rows, d), F32),
        grid_spec=pltpu.PrefetchScalarGridSpec(
            num_scalar_prefetch=4,
            grid=(n_items,),
            in_specs=[
                pl.BlockSpec((MOE_BLK, d), lambda w, ie, it, lo, hi: (it[w], 0)),
                pl.BlockSpec((1, d, MOE_HIDDEN), lambda w, ie, it, lo, hi: (ie[w], 0, 0)),
                pl.BlockSpec((1, d, MOE_HIDDEN), lambda w, ie, it, lo, hi: (ie[w], 0, 0)),
                pl.BlockSpec((1, MOE_HIDDEN, d), lambda w, ie, it, lo, hi: (ie[w], 0, 0)),
            ],
            out_specs=pl.BlockSpec((MOE_BLK, d), lambda w, ie, it, lo, hi: (it[w], 0)),
            scratch_shapes=[pltpu.VMEM((d, MOE_HIDDEN), BF16), pltpu.VMEM((d, MOE_HIDDEN), BF16),
                            pltpu.VMEM((MOE_HIDDEN, d), BF16)],
        ),
        compiler_params=_params("arbitrary"),
        name="experts",
    )(ie, it, lo, hi, xs, wg, wu, wd)


def _combine_kernel(pos_ref, posn_ref, x1_ref, route_ref, gf_ref, y_ref, o_ref, ybuf, sem):
    i = pl.program_id(0)
    n = pl.num_programs(0)
    tc = x1_ref.shape[0]
    slot = i % 2

    def issue(p_ref, s):
        def body(tok, carry):
            for k in range(2):
                pltpu.make_async_copy(y_ref.at[pl.ds(p_ref[0, 0, 2 * tok + k], 1), :],
                                      ybuf.at[s, k, pl.ds(tok, 1), :], sem.at[s]).start()
            return carry

        lax.fori_loop(0, tc, body, 0, unroll=8)

    @pl.when(i == 0)
    def _():
        issue(pos_ref, 0)

    @pl.when(i + 1 < n)
    def _():
        issue(posn_ref, 1 - slot)

    for k in range(2):
        pltpu.make_async_copy(y_ref.at[pl.ds(0, tc), :], ybuf.at[slot, k], sem.at[slot]).wait()
    r = route_ref[...]
    x2 = x1_ref[...] + r[:, 2:3] * ybuf[slot, 0] + r[:, 3:4] * ybuf[slot, 1]
    ms = jnp.mean(x2 * x2, axis=-1, keepdims=True)
    o_ref[...] = x2 * lax.rsqrt(ms + EPS) * gf_ref[...]


def _combine(pos3, x1, route, gf, y):
    t, d = x1.shape
    tc = pos3.shape[2] // 2
    n = t // tc
    return pl.pallas_call(
        _combine_kernel,
        out_shape=jax.ShapeDtypeStruct((t, d), F32),
        grid=(n,),
        in_specs=[
            pl.BlockSpec((1, 1, 2 * tc), lambda i: (i, 0, 0), memory_space=pltpu.SMEM),
            pl.BlockSpec((1, 1, 2 * tc), lambda i: (jnp.minimum(i + 1, n - 1), 0, 0), memory_space=pltpu.SMEM),
            pl.BlockSpec((tc, d), lambda i: (i, 0)),
            pl.BlockSpec((tc, LANES), lambda i: (i, 0)),
            pl.BlockSpec((1, d), lambda i: (0, 0)),
            pl.BlockSpec(memory_space=pl.ANY),
        ],
        out_specs=pl.BlockSpec((tc, d), lambda i: (i, 0)),
        scratch_shapes=[pltpu.VMEM((2, 2, tc, d), F32), pltpu.SemaphoreType.DMA((2,))],
        compiler_params=pltpu.CompilerParams(dimension_semantics=("arbitrary",), vmem_limit_bytes=VMEM_LIMIT,
                                             disable_bounds_checks=True),
        name="combine_norm",
    )(pos3, pos3, x1, route, gf, y)


def _pad_lanes(v, offset):
    return jnp.zeros((1, LANES), F32).at[0, offset:offset + v.shape[0]].set(v.astype(F32))


def kernel(x, norm_mix_gain, w_in, diff_lambda_q1, diff_lambda_k1, diff_lambda_q2, diff_lambda_k2, diff_subln_gain, gdn_conv_w, gdn_a_log, gdn_dt_bias, gdn_norm_gain, w_branch_attn, w_branch_gdn, w_out, norm_ffn_gain, moe_w_group, moe_b_group, moe_w_expert, moe_b_expert, moe_w_gate, moe_w_up, moe_w_down, norm_final_gain):
    b, s, d = x.shape
    t = b * s
    x2 = x.reshape(t, d)

    w = w_in[0]
    small_lo = PROJ_GDN + 3 * GDN_WIDTH
    small_hi = small_lo + 2 * GDN_HEADS
    gate_hi = small_hi + GDN_WIDTH
    w_main = jnp.concatenate([w[:, :small_lo], w[:, gate_hi:], w[:, small_hi:gate_hi]], axis=1).astype(BF16)
    w_small = jnp.concatenate([w[:, small_lo:small_hi], jnp.zeros((d, LANES - 2 * GDN_HEADS), w.dtype)],
                              axis=1).astype(BF16)
    gain1 = norm_mix_gain[0].reshape(1, d)

    proj2, small2 = _inproj(x2, gain1, w_main, w_small, min(1024, t), PROJ_WIDTH // 2)
    proj = proj2.reshape(b, s, PROJ_WIDTH)

    lam_params = jnp.stack([diff_lambda_q1[0], diff_lambda_k1[0], diff_lambda_q2[0], diff_lambda_k2[0]]).astype(F32)
    ya = _diff_attention(lam_params, proj, diff_subln_gain[0].reshape(1, DIFF_V_DIM))

    yb = _gdn(proj, small2.reshape(b, s, LANES), gdn_conv_w[0], _pad_lanes(gdn_a_log[0], GDN_HEADS),
              _pad_lanes(gdn_dt_bias[0], GDN_HEADS), gdn_norm_gain[0].reshape(1, GDN_DIM), tb=min(512, s))

    wr = jnp.concatenate([moe_w_group[0], moe_w_expert[0],
                          jnp.zeros((d, LANES - MOE_GROUPS - MOE_EXPERTS), F32)], axis=1)
    br = _pad_lanes(jnp.concatenate([moe_b_group[0], moe_b_expert[0]]), 0)
    x1, h2, route, cnt = _merge(x2, ya.reshape(t, DIFF_WIDTH), yb.reshape(t, GDN_WIDTH), proj2,
                                w_branch_attn[0].astype(BF16), w_branch_gdn[0].astype(BF16),
                                w_out[0].astype(BF16), norm_ffn_gain[0].reshape(1, d), wr, br, tm=min(512, t))

    pos, meta = _route_positions(route, cnt)
    ts = min(MOE_DMA_TILE, t)
    pos3 = pos[:, :2].reshape(t // ts, 1, 2 * ts)
    xs = _scatter_rows(pos3, h2)
    y = _experts(meta, xs, moe_w_gate[0], moe_w_up[0], moe_w_down[0])
    out = _combine(pos3, x1, route, norm_final_gain.reshape(1, d), y)
    return out.reshape(b, s, d)
```

```python
import functools
import math

import jax
import jax.numpy as jnp
from jax import lax
from jax.experimental import pallas as pl
from jax.experimental.pallas import tpu as pltpu

F32 = jnp.float32
BF16 = jnp.bfloat16

D_MODEL = 1024
CHUNK = 64
EPS = 1e-6

DIFF_HEADS = 4
DIFF_HEAD_DIM = 64
DIFF_V_DIM = 2 * DIFF_HEAD_DIM
DIFF_WIDTH = DIFF_HEADS * DIFF_V_DIM

GDN_HEADS = 4
GDN_DIM = 128
GDN_WIDTH = GDN_HEADS * GDN_DIM
CONV_K = 4

MOE_GROUPS = 4
MOE_EXPERTS_PER_GROUP = 8
MOE_EXPERTS = MOE_GROUPS * MOE_EXPERTS_PER_GROUP
MOE_HIDDEN = 256

LANES = 128
SUBLANES = 8
BF16_SUBLANES = 16
VMEM_LIMIT = 56 * 1024 * 1024

LAMBDA_INIT = 0.8 - 0.6 * math.exp(-0.3 * 0)

PROJ_ATTN = 0
PROJ_GDN = 3 * DIFF_WIDTH
PROJ_MA = PROJ_GDN + 3 * GDN_WIDTH
PROJ_MB = PROJ_MA + D_MODEL
PROJ_GATE = PROJ_MB + D_MODEL
PROJ_WIDTH = PROJ_GATE + GDN_WIDTH

NEG_INF = float("-inf")
LOG2_E = math.log2(math.e)


def _params(*sem):
    return pltpu.CompilerParams(dimension_semantics=sem, vmem_limit_bytes=VMEM_LIMIT)


def _inproj_kernel(x_ref, g_ref, w_ref, ws_ref, o_ref, os_ref, h_scr):
    @pl.when(pl.program_id(1) == 0)
    def _():
        x = x_ref[...]
        ms = jnp.mean(x * x, axis=-1, keepdims=True)
        h_scr[...] = (x * lax.rsqrt(ms + EPS) * g_ref[...]).astype(BF16)
        os_ref[...] = jnp.dot(h_scr[...], ws_ref[...], preferred_element_type=F32)

    o_ref[...] = jnp.dot(h_scr[...], w_ref[...], preferred_element_type=F32).astype(o_ref.dtype)


def _inproj(x2, gain, w, w_small, tm, tn):
    t, d = x2.shape
    n = w.shape[1]
    return pl.pallas_call(
        _inproj_kernel,
        out_shape=(jax.ShapeDtypeStruct((t, n), BF16), jax.ShapeDtypeStruct((t, LANES), F32)),
        grid=(t // tm, n // tn),
        in_specs=[
            pl.BlockSpec((tm, d), lambda i, j: (i, 0)),
            pl.BlockSpec((1, d), lambda i, j: (0, 0)),
            pl.BlockSpec((d, tn), lambda i, j: (0, j)),
            pl.BlockSpec((d, LANES), lambda i, j: (0, 0)),
        ],
        out_specs=(pl.BlockSpec((tm, tn), lambda i, j: (i, j)), pl.BlockSpec((tm, LANES), lambda i, j: (i, 0))),
        scratch_shapes=[pltpu.VMEM((tm, d), BF16)],
        compiler_params=_params("parallel", "arbitrary"),
        name="inproj",
    )(x2, gain, w, w_small)


ATTN_TQ = 1024
ATTN_TK = 512
MASK_BIAS = -1e30
ATTN_V_ROWS = DIFF_V_DIM + BF16_SUBLANES
ATTN_COLS = 128


def _attn_kernel(lam_ref, q_ref, k_ref, v_ref, gain_ref, o_ref,
                 qt_scr, oh_scr, vt_scr, s0_scr, s1_scr, p0_scr, p1_scr, a0_scr, a1_scr, m_scr, acc_scr, *, tq, tk):
    i = pl.program_id(2)
    n_q = 2 * tq
    span = tq // tk
    t_last = span * (i + 1) - 1
    n_pairs = (t_last + 2) // 2
    n_slots = tk // CHUNK

    n_blk = n_q // ATTN_COLS
    half = tq // ATTN_COLS

    @pl.when(i == 0)
    def _():
        for c in range(v_ref.shape[1] // tk):
            v_t = jnp.transpose(v_ref[0, c * tk:(c + 1) * tk, :].astype(F32)).astype(BF16)
            for j in range(tk // LANES):
                vt_scr[c, j] = jnp.concatenate(
                    [v_t[:, j * LANES:(j + 1) * LANES], jnp.ones((BF16_SUBLANES, LANES), BF16)], axis=0)
        slot = lax.broadcasted_iota(jnp.int32, (LANES, ATTN_COLS), 0)
        lane = lax.broadcasted_iota(jnp.int32, (LANES, ATTN_COLS), 1)
        for ver in range(span + 2):
            for c in range(n_blk):
                rel = (lane + (c % half) * ATTN_COLS) // CHUNK - (ver - 1) * n_slots
                if ver == 0:
                    rel = rel * 0 + 2 * n_slots
                elif ver == span + 1:
                    rel = rel * 0 - 1
                want = jnp.where(rel < 0, n_slots, jnp.where(rel < n_slots, rel, -1))
                oh_scr[ver, c] = jnp.where(slot == want, 1.0, 0.0).astype(BF16)

    q_t = jnp.transpose(q_ref[0].astype(F32) * (DIFF_HEAD_DIM ** -0.5 * LOG2_E))
    dim = lax.broadcasted_iota(jnp.int32, (LANES, tq), 0)
    q_comp = (jnp.where(dim < DIFF_HEAD_DIM, q_t, 0.0).astype(BF16),
              jnp.where(dim >= DIFF_HEAD_DIM, q_t, 0.0).astype(BF16))
    for c in range(n_blk):
        qt_scr[c] = q_comp[c // half][:, (c % half) * ATTN_COLS:(c % half + 1) * ATTN_COLS]

    k_lane = lax.broadcasted_iota(jnp.int32, (tk, LANES), 1)
    k_chunk = lax.broadcasted_iota(jnp.int32, (tk, LANES), 0) // CHUNK
    bias = jnp.where((k_lane <= n_slots) & ((k_chunk > k_lane) | (k_lane == n_slots)), MASK_BIAS, 0.0).astype(BF16)

    m_scr[...] = jnp.full(m_scr.shape, NEG_INF, F32)
    acc_scr[...] = jnp.zeros(acc_scr.shape, F32)
    p1_scr[...] = jnp.zeros(p1_scr.shape, BF16)
    a1_scr[...] = jnp.ones(a1_scr.shape, F32)

    def scores(t, s_scr):
        ver = jnp.clip(t - span * i + 1, 0, span + 1)
        rows = pl.ds(pl.multiple_of(jnp.minimum(t, t_last) * tk, tk), tk)
        k_aug = jnp.concatenate([k_ref[0, rows, :], bias], axis=1)
        q_all = jnp.concatenate(
            [jnp.concatenate([qt_scr[c], oh_scr[ver, c]], axis=0) for c in range(n_blk)], axis=1)
        s = jnp.dot(k_aug, q_all, preferred_element_type=F32)
        for c in range(n_blk):
            s_scr[c] = s[:, c * ATTN_COLS:(c + 1) * ATTN_COLS]

    def softmax(s_scr, p_scr, a_scr):
        for c in range(n_blk):
            cs = slice(c * ATTN_COLS, (c + 1) * ATTN_COLS)
            s = s_scr[c]
            m_prev = m_scr[:, cs]
            m_new = jnp.maximum(m_prev, jnp.max(s, axis=0, keepdims=True))
            a_scr[:, cs] = jnp.exp2(m_prev - m_new)
            m_scr[:, cs] = m_new
            p_scr[c] = jnp.exp2(s - m_new).astype(BF16)

    def values(t, p_scr, a_scr):
        tile = jnp.minimum(t, t_last)
        v_all = jnp.concatenate([vt_scr[tile, j] for j in range(tk // LANES)], axis=1)
        p = jnp.concatenate([p_scr[c] for c in range(n_blk)], axis=1)
        pv = jnp.dot(v_all, p, preferred_element_type=F32)
        for c in range(n_blk):
            cs = slice(c * ATTN_COLS, (c + 1) * ATTN_COLS)
            acc_scr[c] = a_scr[:, cs] * acc_scr[c] + pv[:, cs]

    scores(0, s0_scr)

    def pair_step(g, carry):
        t0 = 2 * g
        scores(t0 + 1, s1_scr)
        softmax(s0_scr, p0_scr, a0_scr)
        values(jnp.maximum(t0 - 1, 0), p1_scr, a1_scr)
        scores(t0 + 2, s0_scr)
        softmax(s1_scr, p1_scr, a1_scr)
        values(t0, p0_scr, a0_scr)
        return carry

    lax.fori_loop(0, n_pairs, pair_step, 0)
    values(2 * n_pairs - 1, p1_scr, a1_scr)

    lp = lam_ref[...]
    lam = (jnp.exp(jnp.sum(lp[0:1] * lp[1:2], axis=-1, keepdims=True))
           - jnp.exp(jnp.sum(lp[2:3] * lp[3:4], axis=-1, keepdims=True)) + LAMBDA_INIT)
    for c in range(half):
        o0_t = acc_scr[c, 0:DIFF_V_DIM, :] / acc_scr[c, DIFF_V_DIM:DIFF_V_DIM + 1, :]
        o1_t = acc_scr[half + c, 0:DIFF_V_DIM, :] / acc_scr[half + c, DIFF_V_DIM:DIFF_V_DIM + 1, :]
        o = jnp.transpose(o0_t - lam * o1_t)
        ms = jnp.mean(o * o, axis=-1, keepdims=True)
        y = (o * lax.rsqrt(ms + EPS) * gain_ref[...]) * (1.0 - LAMBDA_INIT)
        o_ref[0, c * ATTN_COLS:(c + 1) * ATTN_COLS, :] = y.astype(o_ref.dtype)


def _diff_attention(lam_params, qkv, subln_gain):
    b, s, _ = qkv.shape
    tq = min(ATTN_TQ, s)
    tk = min(ATTN_TK, s)
    assert s % tq == 0 and tq % tk == 0 and tk % LANES == 0 and tk // CHUNK < LANES
    n_q = 2 * tq
    return pl.pallas_call(
        functools.partial(_attn_kernel, tq=tq, tk=tk),
        out_shape=jax.ShapeDtypeStruct((b, s, DIFF_WIDTH), BF16),
        grid=(b, DIFF_HEADS, s // tq),
        in_specs=[
            pl.BlockSpec((4, DIFF_HEAD_DIM), lambda bi, h, i: (0, 0)),
            pl.BlockSpec((1, tq, DIFF_V_DIM), lambda bi, h, i: (bi, i, h)),
            pl.BlockSpec((1, s, DIFF_V_DIM), lambda bi, h, i: (bi, 0, DIFF_HEADS + h)),
            pl.BlockSpec((1, s, DIFF_V_DIM), lambda bi, h, i: (bi, 0, 2 * DIFF_HEADS + h)),
            pl.BlockSpec((1, DIFF_V_DIM), lambda bi, h, i: (0, 0)),
        ],
        out_specs=pl.BlockSpec((1, tq, DIFF_V_DIM), lambda bi, h, i: (bi, i, h)),
        scratch_shapes=[
            pltpu.VMEM((n_q // ATTN_COLS, LANES, ATTN_COLS), BF16),
            pltpu.VMEM((tq // tk + 2, n_q // ATTN_COLS, LANES, ATTN_COLS), BF16),
            pltpu.VMEM((s // tk, tk // LANES, ATTN_V_ROWS, LANES), BF16),
            pltpu.VMEM((n_q // ATTN_COLS, tk, ATTN_COLS), F32),
            pltpu.VMEM((n_q // ATTN_COLS, tk, ATTN_COLS), F32),
            pltpu.VMEM((n_q // ATTN_COLS, tk, ATTN_COLS), BF16),
            pltpu.VMEM((n_q // ATTN_COLS, tk, ATTN_COLS), BF16),
            pltpu.VMEM((1, n_q), F32),
            pltpu.VMEM((1, n_q), F32),
            pltpu.VMEM((1, n_q), F32),
            pltpu.VMEM((n_q // ATTN_COLS, ATTN_V_ROWS, ATTN_COLS), F32),
        ],
        compiler_params=_params("parallel", "parallel", "arbitrary"),
        name="diff_attn",
    )(lam_params, qkv, qkv, qkv, subln_gain)


def _silu(x):
    return x * (1.0 / (1.0 + jnp.exp(-x)))


def _sigmoid(x):
    return 1.0 / (1.0 + jnp.exp(-x))


def _softplus(x):
    return jnp.maximum(x, 0.0) + jnp.log(1.0 + jnp.exp(-jnp.abs(x)))


def _split_bf16(x):
    hi = x.astype(BF16)
    return hi, (x - hi.astype(F32)).astype(BF16)


def _dot_bf16x3(a_parts, b_parts):
    (a_hi, a_lo), (b_hi, b_lo) = a_parts, b_parts
    d = lambda x, y: jnp.dot(x, y, preferred_element_type=F32)
    return d(a_hi, b_hi) + (d(a_lo, b_hi) + d(a_hi, b_lo))


def _dot_exact_lhs(lhs, x):
    x1 = x.astype(BF16)
    r1 = x - x1.astype(F32)
    x2 = r1.astype(BF16)
    x3 = (r1 - x2.astype(F32)).astype(BF16)
    d = lambda y: jnp.dot(lhs, y, preferred_element_type=F32)
    return d(x1) + (d(x2) + d(x3))


def _dot(a, b):
    return jnp.dot(a.astype(BF16), b.astype(BF16), preferred_element_type=F32)


def _dot_nt(a, b):
    return lax.dot_general(a.astype(BF16), b.astype(BF16), (((1,), (1,)), ((), ())), preferred_element_type=F32)


GDN_UNIT = 2 * CHUNK


def _gdn_kernel(prev_ref, qkv_ref, gate_ref, small_ref, convw_ref, alog_ref, dtb_ref, ngain_ref, o_ref,
                xp_scr, q_scr, k_scr, v_scr, pq_scr, n_scr, oc_scr, gl_scr, oraw_scr, state_scr, *, tb):
    i = pl.program_id(1)

    @pl.when(i == 0)
    def _():
        state_scr[...] = jnp.zeros(state_scr.shape, F32)

    n_sec = 3 * GDN_HEADS
    prev = jnp.where(i == 0, 0.0, prev_ref[0, BF16_SUBLANES - SUBLANES:BF16_SUBLANES, :].astype(F32))
    for sec in range(n_sec):
        xp_scr[sec, 0:SUBLANES, :] = prev[:, sec * LANES:(sec + 1) * LANES]
    for r0 in range(0, tb, GDN_UNIT):
        blk = qkv_ref[0, r0:r0 + GDN_UNIT, :].astype(F32)
        for sec in range(n_sec):
            xp_scr[sec, SUBLANES + r0:SUBLANES + r0 + GDN_UNIT, :] = blk[:, sec * LANES:(sec + 1) * LANES]
    for sec in range(n_sec):
        cols = slice(sec * LANES, (sec + 1) * LANES)
        acc = None
        for jj in range(CONV_K):
            start = SUBLANES - (CONV_K - 1) + jj
            term = xp_scr[sec, start:start + tb, :] * convw_ref[jj:jj + 1, cols]
            acc = term if acc is None else acc + term
        y = _silu(acc)
        which, head = divmod(sec, GDN_HEADS)
        if which == 0:
            q_scr[head] = y * lax.rsqrt(jnp.sum(y * y, axis=-1, keepdims=True) + EPS) * (GDN_DIM ** -0.5)
        elif which == 1:
            k_scr[head] = y * lax.rsqrt(jnp.sum(y * y, axis=-1, keepdims=True) + EPS)
        else:
            v_scr[head] = y

    sm = small_ref[0]
    beta_all = _sigmoid(sm)
    g_all = -jnp.exp(alog_ref[...]) * _softplus(sm + dtb_ref[...])

    br = lax.broadcasted_iota(jnp.int32, (tb, tb), 0)
    bc = lax.broadcasted_iota(jnp.int32, (tb, tb), 1)
    block_tril = jnp.where((br // CHUNK == bc // CHUNK) & (bc <= br), 1.0, 0.0).astype(BF16)
    gc_all = _dot_exact_lhs(block_tril, g_all)
    gct_all = jnp.transpose(gc_all)

    ur = lax.broadcasted_iota(jnp.int32, (GDN_UNIT, GDN_UNIT), 0)
    uc = lax.broadcasted_iota(jnp.int32, (GDN_UNIT, GDN_UNIT), 1)
    same = (ur // CHUNK) == (uc // CHUNK)
    causal = same & (uc <= ur)
    strict = same & (uc < ur)
    eye = jnp.where(ur == uc, 1.0, 0.0)
    first_rows = ur < CHUNK
    first_cols = uc < CHUNK
    n_units = tb // GDN_UNIT
    units = [(h, u) for h in range(GDN_HEADS) for u in range(n_units)]

    def lane_bcast(x, lane):
        return jnp.broadcast_to(x[:, lane:lane + 1], (GDN_UNIT, GDN_UNIT))

    def row_bcast(x, r):
        return jnp.broadcast_to(x[r:r + 1, :], (GDN_UNIT, GDN_UNIT))

    def rows_of(u):
        return slice(u * GDN_UNIT, (u + 1) * GDN_UNIT)

    def cols_of(h):
        return slice(h * LANES, (h + 1) * LANES)

    qs = [q_scr[h, rows_of(u), :] for h, u in units]
    ks = [k_scr[h, rows_of(u), :] for h, u in units]
    vs = [v_scr[h, rows_of(u), :] for h, u in units]
    betas = [lane_bcast(beta_all[rows_of(u)], h) for h, u in units]
    gcb = [lane_bcast(gc_all[rows_of(u)], GDN_HEADS + h) for h, u in units]
    gcr = [row_bcast(gct_all[:, rows_of(u)], GDN_HEADS + h) for h, u in units]
    decay = [jnp.exp(jnp.where(causal, a - b, NEG_INF)) for a, b in zip(gcb, gcr)]
    eg = [jnp.exp(a) for a in gcb]
    g_last = [jnp.where(first_rows, row_bcast(a, CHUNK - 1), row_bcast(a, GDN_UNIT - 1)) for a in gcb]
    kb = [k * b for k, b in zip(ks, betas)]
    vb = [v * b for v, b in zip(vs, betas)]
    a_low = [jnp.where(strict, _dot_nt(x, k) * d, 0.0) for x, k, d in zip(kb, ks, decay)]
    tinv = [eye - a for a in a_low]
    pw = a_low
    for _ in range(5):
        pw = [_dot(x, x) for x in pw]
        tinv = [t + _dot(t, x) for t, x in zip(tinv, pw)]
    uw = [_dot(t, jnp.concatenate([v, x * e], axis=1)) for t, v, x, e in zip(tinv, vb, kb, eg)]
    intra = [_dot_nt(q, k) * d for q, k, d in zip(qs, ks, decay)]
    iuw = [_dot(a, x) for a, x in zip(intra, uw)]
    k_dec_t = [jnp.transpose(k * jnp.exp(gl - a)) for k, gl, a in zip(ks, g_last, gcb)]
    wu = [jnp.concatenate([x[:, GDN_DIM:], x[:, :GDN_DIM]], axis=1) for x in uw]
    pn0 = [_dot(jnp.where(first_cols, kt, 0.0), x) for kt, x in zip(k_dec_t, wu)]
    pn1 = [_dot(jnp.where(first_cols, 0.0, kt), x) for kt, x in zip(k_dec_t, wu)]
    for n in range(len(units)):
        q_eff = (qs[n] * eg[n] - iuw[n][:, GDN_DIM:]).astype(BF16)
        for half, pn in enumerate((pn0[n], pn1[n])):
            hr = slice(half * CHUNK, (half + 1) * CHUNK)
            pq_scr[n, half, 0:GDN_DIM, :] = pn[:, :GDN_DIM].astype(BF16)
            pq_scr[n, half, GDN_DIM:GDN_DIM + CHUNK, :] = q_eff[hr]
            n_scr[n, half] = pn[:, GDN_DIM:]
            oc_scr[n, half] = iuw[n][hr, :GDN_DIM]
            gl_scr[n, half] = jnp.exp(gcb[n][(half + 1) * CHUNK - 1:(half + 1) * CHUNK, :])

    def chunk_body(c, carry):
        u = c // 2
        half = c % 2
        rows = pl.ds(pl.multiple_of(c * CHUNK, CHUNK), CHUNK)
        states = [state_scr[h] for h in range(GDN_HEADS)]
        res = [jnp.dot(pq_scr[h * n_units + u, half], states[h].astype(BF16), preferred_element_type=F32)
               for h in range(GDN_HEADS)]
        for h in range(GDN_HEADS):
            n = h * n_units + u
            state_scr[h] = gl_scr[n, half] * states[h] - res[h][0:GDN_DIM] + n_scr[n, half]
            oraw_scr[h, rows, :] = res[h][GDN_DIM:GDN_DIM + CHUNK] + oc_scr[n, half]
        return carry

    lax.fori_loop(0, tb // CHUNK, chunk_body, 0)

    for h in range(GDN_HEADS):
        o = oraw_scr[h]
        ms = jnp.mean(o * o, axis=-1, keepdims=True)
        y = o * lax.rsqrt(ms + EPS) * ngain_ref[...] * _silu(gate_ref[0, :, cols_of(h)].astype(F32))
        o_ref[0, :, cols_of(h)] = y.astype(o_ref.dtype)


def _gdn(proj, small, conv_w, alog_pad, dtb_pad, norm_gain, tb):
    b, s, _ = proj.shape
    nb = s // tb
    w3 = 3 * GDN_WIDTH
    assert tb % GDN_UNIT == 0
    n_hu = GDN_HEADS * (tb // GDN_UNIT)
    return pl.pallas_call(
        functools.partial(_gdn_kernel, tb=tb),
        out_shape=jax.ShapeDtypeStruct((b, s, GDN_WIDTH), BF16),
        grid=(b, nb),
        in_specs=[
            pl.BlockSpec((1, BF16_SUBLANES, w3),
                         lambda bi, i: (bi, jnp.maximum(i * (tb // BF16_SUBLANES) - 1, 0), PROJ_GDN // w3)),
            pl.BlockSpec((1, tb, w3), lambda bi, i: (bi, i, PROJ_GDN // w3)),
            pl.BlockSpec((1, tb, GDN_WIDTH), lambda bi, i: (bi, i, PROJ_GATE // GDN_WIDTH)),
            pl.BlockSpec((1, tb, LANES), lambda bi, i: (bi, i, 0)),
            pl.BlockSpec((CONV_K, w3), lambda bi, i: (0, 0)),
            pl.BlockSpec((1, LANES), lambda bi, i: (0, 0)),
            pl.BlockSpec((1, LANES), lambda bi, i: (0, 0)),
            pl.BlockSpec((1, GDN_DIM), lambda bi, i: (0, 0)),
        ],
        out_specs=pl.BlockSpec((1, tb, GDN_WIDTH), lambda bi, i: (bi, i, 0)),
        scratch_shapes=[
            pltpu.VMEM((w3 // LANES, tb + SUBLANES, LANES), F32),
            pltpu.VMEM((GDN_HEADS, tb, GDN_DIM), F32),
            pltpu.VMEM((GDN_HEADS, tb, GDN_DIM), F32),
            pltpu.VMEM((GDN_HEADS, tb, GDN_DIM), F32),
            pltpu.VMEM((n_hu, 2, GDN_DIM + CHUNK, GDN_DIM), BF16),
            pltpu.VMEM((n_hu, 2, GDN_DIM, GDN_DIM), F32),
            pltpu.VMEM((n_hu, 2, CHUNK, GDN_DIM), F32),
            pltpu.VMEM((n_hu, 2, 1, GDN_DIM), F32),
            pltpu.VMEM((GDN_HEADS, tb, GDN_DIM), F32),
            pltpu.VMEM((GDN_HEADS, GDN_DIM, GDN_DIM), F32),
        ],
        compiler_params=_params("parallel", "arbitrary"),
        name="gdn",
    )(proj, proj, proj, small, conv_w, alog_pad, dtb_pad, norm_gain)


MERGE_SUB = 256


def _merge_kernel(x_ref, ya_ref, yb_ref, ma_ref, mb_ref, wa_ref, wb_ref, wo_ref, g2_ref, wr_ref, br_ref,
                  x1_ref, h2_ref, route_ref, cnt_ref):
    tm = x_ref.shape[0]
    subs = [slice(r, r + MERGE_SUB) for r in range(0, tm, MERGE_SUB)]
    pa = [jnp.dot(ya_ref[rs, :], wa_ref[...], preferred_element_type=F32) for rs in subs]
    pb = [jnp.dot(yb_ref[rs, :], wb_ref[...], preferred_element_type=F32) for rs in subs]
    merged = [(_sigmoid(ma_ref[rs, :].astype(F32)) * a + _sigmoid(mb_ref[rs, :].astype(F32)) * b).astype(BF16)
              for rs, a, b in zip(subs, pa, pb)]
    x1s = [x_ref[rs, :] + jnp.dot(m, wo_ref[...], preferred_element_type=F32) for rs, m in zip(subs, merged)]
    h2s = [v * lax.rsqrt(jnp.mean(v * v, axis=-1, keepdims=True) + EPS) * g2_ref[...] for v in x1s]
    w_parts = _split_bf16(wr_ref[...])
    logit_s = [_dot_bf16x3(_split_bf16(v), w_parts) for v in h2s]
    for rs, v, hh in zip(subs, x1s, h2s):
        x1_ref[rs, :] = v
        h2_ref[rs, :] = hh
    logits = jnp.concatenate(logit_s, axis=0) + br_ref[...]
    lane = lax.broadcasted_iota(jnp.int32, logits.shape, 1)
    big = jnp.int32(4 * LANES)
    gl = jnp.where(lane < MOE_GROUPS, logits, NEG_INF)
    gmax = jnp.max(gl, axis=-1, keepdims=True)
    gidx = jnp.min(jnp.where(gl == gmax, lane, big), axis=-1, keepdims=True)
    grp_w = 1.0 / jnp.sum(jnp.exp(gl - gmax), axis=-1, keepdims=True)
    lo = MOE_GROUPS + gidx * MOE_EXPERTS_PER_GROUP
    el = jnp.where((lane >= lo) & (lane < lo + MOE_EXPERTS_PER_GROUP), logits, NEG_INF)
    v1 = jnp.max(el, axis=-1, keepdims=True)
    i1 = jnp.min(jnp.where(el == v1, lane, big), axis=-1, keepdims=True)
    el2 = jnp.where(lane == i1, NEG_INF, el)
    v2 = jnp.max(el2, axis=-1, keepdims=True)
    i2 = jnp.min(jnp.where(el2 == v2, lane, big), axis=-1, keepdims=True)
    e2 = jnp.exp(v2 - v1)
    w1 = grp_w / (1.0 + e2)
    w2 = w1 * e2
    id1 = i1 - MOE_GROUPS
    id2 = i2 - MOE_GROUPS
    route_ref[...] = jnp.where(lane == 0, id1.astype(F32), jnp.where(lane == 1, id2.astype(F32),
                               jnp.where(lane == 2, w1, jnp.where(lane == 3, w2, 0.0))))

    @pl.when(pl.program_id(0) == 0)
    def _():
        cnt_ref[...] = jnp.zeros(cnt_ref.shape, F32)

    hits = jnp.where((lane == id1) | (lane == id2), 1.0, 0.0)
    cnt_ref[...] += jnp.broadcast_to(jnp.sum(hits, axis=0, keepdims=True), cnt_ref.shape)


def _merge(x2, ya, yb, proj, wa, wb, wo, g2, wr, br, tm):
    t, d = x2.shape
    row = lambda i: (i, 0)
    const = lambda i: (0, 0)
    return pl.pallas_call(
        _merge_kernel,
        out_shape=(
            jax.ShapeDtypeStruct((t, d), F32),
            jax.ShapeDtypeStruct((t, d), F32),
            jax.ShapeDtypeStruct((t, LANES), F32),
            jax.ShapeDtypeStruct((SUBLANES, LANES), F32),
        ),
        grid=(t // tm,),
        in_specs=[
            pl.BlockSpec((tm, d), row),
            pl.BlockSpec((tm, DIFF_WIDTH), row),
            pl.BlockSpec((tm, GDN_WIDTH), row),
            pl.BlockSpec((tm, d), lambda i: (i, PROJ_MA // D_MODEL)),
            pl.BlockSpec((tm, d), lambda i: (i, PROJ_MB // D_MODEL)),
            pl.BlockSpec((DIFF_WIDTH, d), const),
            pl.BlockSpec((GDN_WIDTH, d), const),
            pl.BlockSpec((d, d), const),
            pl.BlockSpec((1, d), const),
            pl.BlockSpec((d, LANES), const),
            pl.BlockSpec((1, LANES), const),
        ],
        out_specs=(pl.BlockSpec((tm, d), row), pl.BlockSpec((tm, d), row), pl.BlockSpec((tm, LANES), row),
                   pl.BlockSpec((SUBLANES, LANES), const)),
        compiler_params=_params("arbitrary"),
        name="merge_router",
    )(x2, ya, yb, proj, proj, wa, wb, wo, g2, wr, br)


MOE_BLK = 512
MOE_SUB = 256
MOE_META_LANES = 256
MOE_ROUTE_TILE = 512
MOE_DMA_TILE = 256


def _route_kernel(route_ref, cnt_ref, pos_ref, meta_ref, run_scr, *, n_rows):
    i = pl.program_id(0)
    tp = route_ref.shape[0]

    @pl.when(i == 0)
    def _():
        cnt = cnt_ref[...]
        jr = lax.broadcasted_iota(jnp.int32, (LANES, LANES), 0)
        jc = lax.broadcasted_iota(jnp.int32, (LANES, LANES), 1)
        upper = jnp.where(jr < jc, 1.0, 0.0)
        hi_prec = dict(preferred_element_type=F32, precision=lax.Precision.HIGHEST)
        off = jnp.dot(cnt, upper, **hi_prec)
        run_scr[...] = off[0:1]
        blk = float(MOE_BLK)
        first_tile = jnp.floor(off / blk)
        last_tile = jnp.floor((off + cnt - 1.0) / blk)
        n_it = jnp.where(cnt > 0.0, last_tile - first_tile + 1.0, 0.0)
        it_start = jnp.dot(n_it, upper, **hi_prec)
        it_end = it_start + n_it
        lane8 = lax.broadcasted_iota(jnp.int32, cnt.shape, 1)
        e_max = jnp.max(jnp.where(cnt > 0.0, lane8, 0), axis=-1, keepdims=True).astype(F32)[0:1]
        sub8 = lax.broadcasted_iota(jnp.int32, cnt.shape, 0)
        table = jnp.where(sub8 == 0, first_tile, jnp.where(sub8 == 1, it_start, jnp.where(
            sub8 == 2, off, jnp.where(sub8 == 3, cnt, it_end))))
        cols = jnp.transpose(table)
        shape = (LANES, MOE_META_LANES)
        e_sub = lax.broadcasted_iota(jnp.int32, shape, 0)
        w_lane = lax.broadcasted_iota(jnp.int32, shape, 1).astype(F32)
        col = lambda k: jnp.broadcast_to(cols[:, k:k + 1], shape)
        e_w = jnp.sum(jnp.where((e_sub < MOE_EXPERTS) & (col(4) <= w_lane), 1.0, 0.0), axis=0, keepdims=True)
        valid = e_w < float(MOE_EXPERTS)
        e_w = jnp.minimum(e_w, e_max)
        sel = e_sub.astype(F32) == e_w
        pick = lambda k: jnp.sum(jnp.where(sel, col(k), 0.0), axis=0, keepdims=True)
        w_row = w_lane[0:1]
        tile_w = jnp.where(valid, pick(0) + (w_row - pick(1)), float(n_rows // MOE_BLK - 1))
        lo_w = jnp.maximum(pick(2) - tile_w * blk, 0.0)
        hi_w = jnp.minimum(pick(2) + pick(3) - tile_w * blk, blk)
        lo_w = jnp.where(valid, lo_w, 0.0)
        hi_w = jnp.where(valid, hi_w, 0.0)
        sub_m = lax.broadcasted_iota(jnp.int32, meta_ref.shape, 0)
        bc = lambda v: jnp.broadcast_to(v, meta_ref.shape)
        meta_ref[...] = jnp.where(sub_m == 0, bc(e_w), jnp.where(sub_m == 1, bc(tile_w), jnp.where(
            sub_m == 2, bc(lo_w), bc(hi_w)))).astype(jnp.int32)

    r = route_ref[...]
    lane = lax.broadcasted_iota(jnp.int32, r.shape, 1)
    lane_f = lane.astype(F32)
    oh1 = lane_f == r[:, 0:1]
    oh2 = lane_f == r[:, 1:2]
    hits = jnp.where(oh1 | oh2, 1.0, 0.0)
    tr = lax.broadcasted_iota(jnp.int32, (tp, tp), 0)
    tc = lax.broadcasted_iota(jnp.int32, (tp, tp), 1)
    earlier = jnp.where(tc < tr, 1.0, 0.0).astype(BF16)
    rank = jnp.dot(earlier, hits.astype(BF16), preferred_element_type=F32)
    base = run_scr[...] + rank
    p1 = jnp.sum(jnp.where(oh1, base, 0.0), axis=-1, keepdims=True)
    p2 = jnp.sum(jnp.where(oh2, base, 0.0), axis=-1, keepdims=True)
    pos_ref[...] = jnp.where(lane == 0, p1, jnp.where(lane == 1, p2, 0.0)).astype(jnp.int32)
    run_scr[...] += jnp.sum(hits, axis=0, keepdims=True)


def _route_positions(route, cnt):
    t = route.shape[0]
    tp = min(MOE_ROUTE_TILE, t)
    n_rows = 2 * t
    assert n_rows % MOE_BLK == 0 and n_rows // MOE_BLK + MOE_EXPERTS <= MOE_META_LANES and n_rows < 2 ** 24
    return pl.pallas_call(
        functools.partial(_route_kernel, n_rows=n_rows),
        out_shape=(jax.ShapeDtypeStruct((t, LANES), jnp.int32),
                   jax.ShapeDtypeStruct((SUBLANES, MOE_META_LANES), jnp.int32)),
        grid=(t // tp,),
        in_specs=[pl.BlockSpec((tp, LANES), lambda i: (i, 0)),
                  pl.BlockSpec((SUBLANES, LANES), lambda i: (0, 0))],
        out_specs=(pl.BlockSpec((tp, LANES), lambda i: (i, 0)),
                   pl.BlockSpec((SUBLANES, MOE_META_LANES), lambda i: (0, 0))),
        scratch_shapes=[pltpu.VMEM((1, LANES), F32)],
        compiler_params=_params("arbitrary"),
        name="route_positions",
    )(route, cnt)


def _scatter_kernel(pos_ref, h2_ref, xs_ref, sem):
    ts = h2_ref.shape[0]

    def body(tok, carry):
        src = h2_ref.at[pl.ds(tok, 1), :]
        for k in range(2):
            pltpu.make_async_copy(src, xs_ref.at[pl.ds(pos_ref[0, 0, 2 * tok + k], 1), :], sem).start()
        return carry

    lax.fori_loop(0, ts, body, 0, unroll=8)
    for _ in range(2):
        pltpu.make_async_copy(h2_ref, xs_ref.at[pl.ds(0, ts), :], sem).wait()


def _scatter_rows(pos3, h2):
    t, d = h2.shape
    ts = pos3.shape[2] // 2
    return pl.pallas_call(
        _scatter_kernel,
        out_shape=jax.ShapeDtypeStruct((2 * t, d), F32),
        grid=(t // ts,),
        in_specs=[pl.BlockSpec((1, 1, 2 * ts), lambda i: (i, 0, 0), memory_space=pltpu.SMEM),
                  pl.BlockSpec((ts, d), lambda i: (i, 0))],
        out_specs=pl.BlockSpec(memory_space=pl.ANY),
        scratch_shapes=[pltpu.SemaphoreType.DMA],
        compiler_params=pltpu.CompilerParams(dimension_semantics=("arbitrary",), vmem_limit_bytes=VMEM_LIMIT,
                                             disable_bounds_checks=True),
        name="scatter_rows",
    )(pos3, h2)


def _expert_kernel(ie_ref, it_ref, lo_ref, hi_ref, xs_ref, wg_ref, wu_ref, wd_ref, y_ref, wg_scr, wu_scr, wd_scr):
    w = pl.program_id(0)
    prev = jnp.maximum(w - 1, 0)

    @pl.when((w == 0) | (ie_ref[w] != ie_ref[prev]))
    def _():
        wg_scr[...] = wg_ref[0].astype(BF16)
        wu_scr[...] = wu_ref[0].astype(BF16)
        wd_scr[...] = wd_ref[0].astype(BF16)

    @pl.when((w == 0) | (it_ref[w] != it_ref[prev]))
    def _():
        y_ref[...] = jnp.zeros(y_ref.shape, F32)

    lo = lo_ref[w]
    hi = hi_ref[w]

    @pl.when(hi > lo)
    def _():
        subs = [slice(r, r + MOE_SUB) for r in range(0, xs_ref.shape[0], MOE_SUB)]
        xs = [xs_ref[rs, :].astype(BF16) for rs in subs]
        hg = [jnp.dot(x, wg_scr[...], preferred_element_type=F32) for x in xs]
        hu = [jnp.dot(x, wu_scr[...], preferred_element_type=F32) for x in xs]
        act = [(_silu(g) * u).astype(BF16) for g, u in zip(hg, hu)]
        yp = [jnp.dot(a, wd_scr[...], preferred_element_type=F32) for a in act]
        for rs, v in zip(subs, yp):
            row = rs.start + lax.broadcasted_iota(jnp.int32, v.shape, 0)
            y_ref[rs, :] += jnp.where((row >= lo) & (row < hi), v, 0.0)


def _experts(meta, xs, wg, wu, wd):
    n_rows, d = xs.shape
    n_items = n_rows // MOE_BLK + MOE_EXPERTS
    ie, it, lo, hi = (meta[k, :n_items] for k in range(4))
    return pl.pallas_call(
        _expert_kernel,
        out_shape=jax.ShapeDtypeStruct((n_rows, d), F32),
        grid_spec=pltpu.PrefetchScalarGridSpec(
            num_scalar_prefetch=4,
            grid=(n_items,),
            in_specs=[
                pl.BlockSpec((MOE_BLK, d), lambda w, ie, it, lo, hi: (it[w], 0)),
                pl.BlockSpec((1, d, MOE_HIDDEN), lambda w, ie, it, lo, hi: (ie[w], 0, 0)),
                pl.BlockSpec((1, d, MOE_HIDDEN), lambda w, ie, it, lo, hi: (ie[w], 0, 0)),
                pl.BlockSpec((1, MOE_HIDDEN, d), lambda w, ie, it, lo, hi: (ie[w], 0, 0)),
            ],
            out_specs=pl.BlockSpec((MOE_BLK, d), lambda w, ie, it, lo, hi: (it[w], 0)),
            scratch_shapes=[pltpu.VMEM((d, MOE_HIDDEN), BF16), pltpu.VMEM((d, MOE_HIDDEN), BF16),
                            pltpu.VMEM((MOE_HIDDEN, d), BF16)],
        ),
        compiler_params=_params("arbitrary"),
        name="experts",
    )(ie, it, lo, hi, xs, wg, wu, wd)


def _combine_kernel(pos_ref, posn_ref, x1_ref, route_ref, gf_ref, y_ref, o_ref, ybuf, sem):
    i = pl.program_id(0)
    n = pl.num_programs(0)
    tc = x1_ref.shape[0]
    slot = i % 2

    def issue(p_ref, s):
        def body(tok, carry):
            for k in range(2):
                pltpu.make_async_copy(y_ref.at[pl.ds(p_ref[0, 0, 2 * tok + k], 1), :],
                                      ybuf.at[s, k, pl.ds(tok, 1), :], sem.at[s]).start()
            return carry

        lax.fori_loop(0, tc, body, 0, unroll=8)

    @pl.when(i == 0)
    def _():
        issue(pos_ref, 0)

    @pl.when(i + 1 < n)
    def _():
        issue(posn_ref, 1 - slot)

    for k in range(2):
        pltpu.make_async_copy(y_ref.at[pl.ds(0, tc), :], ybuf.at[slot, k], sem.at[slot]).wait()
    r = route_ref[...]
    x2 = x1_ref[...] + r[:, 2:3] * ybuf[slot, 0] + r[:, 3:4] * ybuf[slot, 1]
    ms = jnp.mean(x2 * x2, axis=-1, keepdims=True)
    o_ref[...] = x2 * lax.rsqrt(ms + EPS) * gf_ref[...]


def _combine(pos3, x1, route, gf, y):
    t, d = x1.shape
    tc = pos3.shape[2] // 2
    n = t // tc
    return pl.pallas_call(
        _combine_kernel,
        out_shape=jax.ShapeDtypeStruct((t, d), F32),
        grid=(n,),
        in_specs=[
            pl.BlockSpec((1, 1, 2 * tc), lambda i: (i, 0, 0), memory_space=pltpu.SMEM),
            pl.BlockSpec((1, 1, 2 * tc), lambda i: (jnp.minimum(i + 1, n - 1), 0, 0), memory_space=pltpu.SMEM),
            pl.BlockSpec((tc, d), lambda i: (i, 0)),
            pl.BlockSpec((tc, LANES), lambda i: (i, 0)),
            pl.BlockSpec((1, d), lambda i: (0, 0)),
            pl.BlockSpec(memory_space=pl.ANY),
        ],
        out_specs=pl.BlockSpec((tc, d), lambda i: (i, 0)),
        scratch_shapes=[pltpu.VMEM((2, 2, tc, d), F32), pltpu.SemaphoreType.DMA((2,))],
        compiler_params=pltpu.CompilerParams(dimension_semantics=("arbitrary",), vmem_limit_bytes=VMEM_LIMIT,
                                             disable_bounds_checks=True),
        name="combine_norm",
    )(pos3, pos3, x1, route, gf, y)


def _pad_lanes(v, offset):
    return jnp.zeros((1, LANES), F32).at[0, offset:offset + v.shape[0]].set(v.astype(F32))


def kernel(x, norm_mix_gain, w_in, diff_lambda_q1, diff_lambda_k1, diff_lambda_q2, diff_lambda_k2, diff_subln_gain, gdn_conv_w, gdn_a_log, gdn_dt_bias, gdn_norm_gain, w_branch_attn, w_branch_gdn, w_out, norm_ffn_gain, moe_w_group, moe_b_group, moe_w_expert, moe_b_expert, moe_w_gate, moe_w_up, moe_w_down, norm_final_gain):
    b, s, d = x.shape
    t = b * s
    x2 = x.reshape(t, d)

    w = w_in[0]
    small_lo = PROJ_GDN + 3 * GDN_WIDTH
    small_hi = small_lo + 2 * GDN_HEADS
    gate_hi = small_hi + GDN_WIDTH
    w_main = jnp.concatenate([w[:, :small_lo], w[:, gate_hi:], w[:, small_hi:gate_hi]], axis=1).astype(BF16)
    w_small = jnp.concatenate([w[:, small_lo:small_hi], jnp.zeros((d, LANES - 2 * GDN_HEADS), w.dtype)],
                              axis=1).astype(BF16)
    gain1 = norm_mix_gain[0].reshape(1, d)

    proj2, small2 = _inproj(x2, gain1, w_main, w_small, min(1024, t), PROJ_WIDTH // 2)
    proj = proj2.reshape(b, s, PROJ_WIDTH)

    lam_params = jnp.stack([diff_lambda_q1[0], diff_lambda_k1[0], diff_lambda_q2[0], diff_lambda_k2[0]]).astype(F32)
    ya = _diff_attention(lam_params, proj, diff_subln_gain[0].reshape(1, DIFF_V_DIM))

    yb = _gdn(proj, small2.reshape(b, s, LANES), gdn_conv_w[0], _pad_lanes(gdn_a_log[0], GDN_HEADS),
              _pad_lanes(gdn_dt_bias[0], GDN_HEADS), gdn_norm_gain[0].reshape(1, GDN_DIM), tb=min(512, s))

    wr = jnp.concatenate([moe_w_group[0], moe_w_expert[0],
                          jnp.zeros((d, LANES - MOE_GROUPS - MOE_EXPERTS), F32)], axis=1)
    br = _pad_lanes(jnp.concatenate([moe_b_group[0], moe_b_expert[0]]), 0)
    x1, h2, route, cnt = _merge(x2, ya.reshape(t, DIFF_WIDTH), yb.reshape(t, GDN_WIDTH), proj2,
                                w_branch_attn[0].astype(BF16), w_branch_gdn[0].astype(BF16),
                                w_out[0].astype(BF16), norm_ffn_gain[0].reshape(1, d), wr, br, tm=min(512, t))

    pos, meta = _route_positions(route, cnt)
    ts = min(MOE_DMA_TILE, t)
    pos3 = pos[:, :2].reshape(t // ts, 1, 2 * ts)
    xs = _scatter_rows(pos3, h2)
    y = _experts(meta, xs, moe_w_gate[0], moe_w_up[0], moe_w_down[0])
    out = _combine(pos3, x1, route, norm_final_gain.reshape(1, d), y)
    return out.reshape(b, s, d)
```

```python
import functools
import math

import jax
import jax.numpy as jnp
from jax import lax
from jax.experimental import pallas as pl
from jax.experimental.pallas import tpu as pltpu

F32 = jnp.float32
BF16 = jnp.bfloat16

D_MODEL = 1024
CHUNK = 64
EPS = 1e-6

DIFF_HEADS = 4
DIFF_HEAD_DIM = 64
DIFF_V_DIM = 2 * DIFF_HEAD_DIM
DIFF_WIDTH = DIFF_HEADS * DIFF_V_DIM

GDN_HEADS = 4
GDN_DIM = 128
GDN_WIDTH = GDN_HEADS * GDN_DIM
CONV_K = 4

MOE_GROUPS = 4
MOE_EXPERTS_PER_GROUP = 8
MOE_EXPERTS = MOE_GROUPS * MOE_EXPERTS_PER_GROUP
MOE_HIDDEN = 256

LANES = 128
SUBLANES = 8
BF16_SUBLANES = 16
VMEM_LIMIT = 56 * 1024 * 1024

LAMBDA_INIT = 0.8 - 0.6 * math.exp(-0.3 * 0)

PROJ_ATTN = 0
PROJ_GDN = 3 * DIFF_WIDTH
PROJ_MA = PROJ_GDN + 3 * GDN_WIDTH
PROJ_MB = PROJ_MA + D_MODEL
PROJ_GATE = PROJ_MB + D_MODEL
PROJ_WIDTH = PROJ_GATE + GDN_WIDTH

NEG_INF = float("-inf")
LOG2_E = math.log2(math.e)


def _params(*sem):
    return pltpu.CompilerParams(dimension_semantics=sem, vmem_limit_bytes=VMEM_LIMIT)


def _inproj_kernel(x_ref, g_ref, w_ref, ws_ref, o_ref, os_ref, h_scr):
    @pl.when(pl.program_id(1) == 0)
    def _():
        x = x_ref[...]
        ms = jnp.mean(x * x, axis=-1, keepdims=True)
        h_scr[...] = (x * lax.rsqrt(ms + EPS) * g_ref[...]).astype(BF16)
        os_ref[...] = jnp.dot(h_scr[...], ws_ref[...], preferred_element_type=F32)

    o_ref[...] = jnp.dot(h_scr[...], w_ref[...], preferred_element_type=F32).astype(o_ref.dtype)


def _inproj(x2, gain, w, w_small, tm, tn):
    t, d = x2.shape
    n = w.shape[1]
    return pl.pallas_call(
        _inproj_kernel,
        out_shape=(jax.ShapeDtypeStruct((t, n), BF16), jax.ShapeDtypeStruct((t, LANES), F32)),
        grid=(t // tm, n // tn),
        in_specs=[
            pl.BlockSpec((tm, d), lambda i, j: (i, 0)),
            pl.BlockSpec((1, d), lambda i, j: (0, 0)),
            pl.BlockSpec((d, tn), lambda i, j: (0, j)),
            pl.BlockSpec((d, LANES), lambda i, j: (0, 0)),
        ],
        out_specs=(pl.BlockSpec((tm, tn), lambda i, j: (i, j)), pl.BlockSpec((tm, LANES), lambda i, j: (i, 0))),
        scratch_shapes=[pltpu.VMEM((tm, d), BF16)],
        compiler_params=_params("parallel", "arbitrary"),
        name="inproj",
    )(x2, gain, w, w_small)


ATTN_TQ = 1024
ATTN_TK = 512
MASK_BIAS = -1e30
ATTN_V_ROWS = DIFF_V_DIM + BF16_SUBLANES
ATTN_COLS = 128


def _attn_kernel(lam_ref, q_ref, k_ref, v_ref, gain_ref, o_ref,
                 qt_scr, oh_scr, vt_scr, s0_scr, s1_scr, p0_scr, p1_scr, a0_scr, a1_scr, m_scr, acc_scr, *, tq, tk):
    i = pl.program_id(2)
    n_q = 2 * tq
    span = tq // tk
    t_last = span * (i + 1) - 1
    n_pairs = (t_last + 2) // 2
    n_slots = tk // CHUNK

    n_blk = n_q // ATTN_COLS
    half = tq // ATTN_COLS

    @pl.when(i == 0)
    def _():
        for c in range(v_ref.shape[1] // tk):
            v_t = jnp.transpose(v_ref[0, c * tk:(c + 1) * tk, :].astype(F32)).astype(BF16)
            for j in range(tk // LANES):
                vt_scr[c, j] = jnp.concatenate(
                    [v_t[:, j * LANES:(j + 1) * LANES], jnp.ones((BF16_SUBLANES, LANES), BF16)], axis=0)
        slot = lax.broadcasted_iota(jnp.int32, (LANES, ATTN_COLS), 0)
        lane = lax.broadcasted_iota(jnp.int32, (LANES, ATTN_COLS), 1)
        for ver in range(span + 2):
            for c in range(n_blk):
                rel = (lane + (c % half) * ATTN_COLS) // CHUNK - (ver - 1) * n_slots
                if ver == 0:
                    rel = rel * 0 + 2 * n_slots
                elif ver == span + 1:
                    rel = rel * 0 - 1
                want = jnp.where(rel < 0, n_slots, jnp.where(rel < n_slots, rel, -1))
                oh_scr[ver, c] = jnp.where(slot == want, 1.0, 0.0).astype(BF16)

    q_t = jnp.transpose(q_ref[0].astype(F32) * (DIFF_HEAD_DIM ** -0.5 * LOG2_E))
    dim = lax.broadcasted_iota(jnp.int32, (LANES, tq), 0)
    q_comp = (jnp.where(dim < DIFF_HEAD_DIM, q_t, 0.0).astype(BF16),
              jnp.where(dim >= DIFF_HEAD_DIM, q_t, 0.0).astype(BF16))
    for c in range(n_blk):
        qt_scr[c] = q_comp[c // half][:, (c % half) * ATTN_COLS:(c % half + 1) * ATTN_COLS]

    k_lane = lax.broadcasted_iota(jnp.int32, (tk, LANES), 1)
    k_chunk = lax.broadcasted_iota(jnp.int32, (tk, LANES), 0) // CHUNK
    bias = jnp.where((k_lane <= n_slots) & ((k_chunk > k_lane) | (k_lane == n_slots)), MASK_BIAS, 0.0).astype(BF16)

    m_scr[...] = jnp.full(m_scr.shape, NEG_INF, F32)
    acc_scr[...] = jnp.zeros(acc_scr.shape, F32)
    p1_scr[...] = jnp.zeros(p1_scr.shape, BF16)
    a1_scr[...] = jnp.ones(a1_scr.shape, F32)

    def scores(t, s_scr):
        ver = jnp.clip(t - span * i + 1, 0, span + 1)
        rows = pl.ds(pl.multiple_of(jnp.minimum(t, t_last) * tk, tk), tk)
        k_aug = jnp.concatenate([k_ref[0, rows, :], bias], axis=1)
        q_all = jnp.concatenate(
            [jnp.concatenate([qt_scr[c], oh_scr[ver, c]], axis=0) for c in range(n_blk)], axis=1)
        s = jnp.dot(k_aug, q_all, preferred_element_type=F32)
        for c in range(n_blk):
            s_scr[c] = s[:, c * ATTN_COLS:(c + 1) * ATTN_COLS]

    def softmax(s_scr, p_scr, a_scr):
        for c in range(n_blk):
            cs = slice(c * ATTN_COLS, (c + 1) * ATTN_COLS)
            s = s_scr[c]
            m_prev = m_scr[:, cs]
            m_new = jnp.maximum(m_prev, jnp.max(s, axis=0, keepdims=True))
            a_scr[:, cs] = jnp.exp2(m_prev - m_new)
            m_scr[:, cs] = m_new
            p_scr[c] = jnp.exp2(s - m_new).astype(BF16)

    def values(t, p_scr, a_scr):
        tile = jnp.minimum(t, t_last)
        v_all = jnp.concatenate([vt_scr[tile, j] for j in range(tk // LANES)], axis=1)
        p = jnp.concatenate([p_scr[c] for c in range(n_blk)], axis=1)
        pv = jnp.dot(v_all, p, preferred_element_type=F32)
        for c in range(n_blk):
            cs = slice(c * ATTN_COLS, (c + 1) * ATTN_COLS)
            acc_scr[c] = a_scr[:, cs] * acc_scr[c] + pv[:, cs]

    scores(0, s0_scr)

    def pair_step(g, carry):
        t0 = 2 * g
        scores(t0 + 1, s1_scr)
        softmax(s0_scr, p0_scr, a0_scr)
        values(jnp.maximum(t0 - 1, 0), p1_scr, a1_scr)
        scores(t0 + 2, s0_scr)
        softmax(s1_scr, p1_scr, a1_scr)
        values(t0, p0_scr, a0_scr)
        return carry

    lax.fori_loop(0, n_pairs, pair_step, 0)
    values(2 * n_pairs - 1, p1_scr, a1_scr)

    lp = lam_ref[...]
    lam = (jnp.exp(jnp.sum(lp[0:1] * lp[1:2], axis=-1, keepdims=True))
           - jnp.exp(jnp.sum(lp[2:3] * lp[3:4], axis=-1, keepdims=True)) + LAMBDA_INIT)
    for c in range(half):
        o0_t = acc_scr[c, 0:DIFF_V_DIM, :] / acc_scr[c, DIFF_V_DIM:DIFF_V_DIM + 1, :]
        o1_t = acc_scr[half + c, 0:DIFF_V_DIM, :] / acc_scr[half + c, DIFF_V_DIM:DIFF_V_DIM + 1, :]
        o = jnp.transpose(o0_t - lam * o1_t)
        ms = jnp.mean(o * o, axis=-1, keepdims=True)
        y = (o * lax.rsqrt(ms + EPS) * gain_ref[...]) * (1.0 - LAMBDA_INIT)
        o_ref[0, c * ATTN_COLS:(c + 1) * ATTN_COLS, :] = y.astype(o_ref.dtype)


def _diff_attention(lam_params, qkv, subln_gain):
    b, s, _ = qkv.shape
    tq = min(ATTN_TQ, s)
    tk = min(ATTN_TK, s)
    assert s % tq == 0 and tq % tk == 0 and tk % LANES == 0 and tk // CHUNK < LANES
    n_q = 2 * tq
    return pl.pallas_call(
        functools.partial(_attn_kernel, tq=tq, tk=tk),
        out_shape=jax.ShapeDtypeStruct((b, s, DIFF_WIDTH), BF16),
        grid=(b, DIFF_HEADS, s // tq),
        in_specs=[
            pl.BlockSpec((4, DIFF_HEAD_DIM), lambda bi, h, i: (0, 0)),
            pl.BlockSpec((1, tq, DIFF_V_DIM), lambda bi, h, i: (bi, i, h)),
            pl.BlockSpec((1, s, DIFF_V_DIM), lambda bi, h, i: (bi, 0, DIFF_HEADS + h)),
            pl.BlockSpec((1, s, DIFF_V_DIM), lambda bi, h, i: (bi, 0, 2 * DIFF_HEADS + h)),
            pl.BlockSpec((1, DIFF_V_DIM), lambda bi, h, i: (0, 0)),
        ],
        out_specs=pl.BlockSpec((1, tq, DIFF_V_DIM), lambda bi, h, i: (bi, i, h)),
        scratch_shapes=[
            pltpu.VMEM((n_q // ATTN_COLS, LANES, ATTN_COLS), BF16),
            pltpu.VMEM((tq // tk + 2, n_q // ATTN_COLS, LANES, ATTN_COLS), BF16),
            pltpu.VMEM((s // tk, tk // LANES, ATTN_V_ROWS, LANES), BF16),
            pltpu.VMEM((n_q // ATTN_COLS, tk, ATTN_COLS), F32),
            pltpu.VMEM((n_q // ATTN_COLS, tk, ATTN_COLS), F32),
            pltpu.VMEM((n_q // ATTN_COLS, tk, ATTN_COLS), BF16),
            pltpu.VMEM((n_q // ATTN_COLS, tk, ATTN_COLS), BF16),
            pltpu.VMEM((1, n_q), F32),
            pltpu.VMEM((1, n_q), F32),
            pltpu.VMEM((1, n_q), F32),
            pltpu.VMEM((n_q // ATTN_COLS, ATTN_V_ROWS, ATTN_COLS), F32),
        ],
        compiler_params=_params("parallel", "parallel", "arbitrary"),
        name="diff_attn",
    )(lam_params, qkv, qkv, qkv, subln_gain)


def _silu(x):
    return x * (1.0 / (1.0 + jnp.exp(-x)))


def _sigmoid(x):
    return 1.0 / (1.0 + jnp.exp(-x))


def _softplus(x):
    return jnp.maximum(x, 0.0) + jnp.log(1.0 + jnp.exp(-jnp.abs(x)))


def _split_bf16(x):
    hi = x.astype(BF16)
    return hi, (x - hi.astype(F32)).astype(BF16)


def _dot_bf16x3(a_parts, b_parts):
    (a_hi, a_lo), (b_hi, b_lo) = a_parts, b_parts
    d = lambda x, y: jnp.dot(x, y, preferred_element_type=F32)
    return d(a_hi, b_hi) + (d(a_lo, b_hi) + d(a_hi, b_lo))


def _dot_exact_lhs(lhs, x):
    x1 = x.astype(BF16)
    r1 = x - x1.astype(F32)
    x2 = r1.astype(BF16)
    x3 = (r1 - x2.astype(F32)).astype(BF16)
    d = lambda y: jnp.dot(lhs, y, preferred_element_type=F32)
    return d(x1) + (d(x2) + d(x3))


def _dot(a, b):
    return jnp.dot(a.astype(BF16), b.astype(BF16), preferred_element_type=F32)


def _dot_nt(a, b):
    return lax.dot_general(a.astype(BF16), b.astype(BF16), (((1,), (1,)), ((), ())), preferred_element_type=F32)


GDN_UNIT = 2 * CHUNK


def _gdn_kernel(prev_ref, qkv_ref, gate_ref, small_ref, convw_ref, alog_ref, dtb_ref, ngain_ref, o_ref,
                xp_scr, q_scr, k_scr, v_scr, pq_scr, n_scr, oc_scr, gl_scr, oraw_scr, state_scr, *, tb):
    i = pl.program_id(1)

    @pl.when(i == 0)
    def _():
        state_scr[...] = jnp.zeros(state_scr.shape, F32)

    n_sec = 3 * GDN_HEADS
    prev = jnp.where(i == 0, 0.0, prev_ref[0, BF16_SUBLANES - SUBLANES:BF16_SUBLANES, :].astype(F32))
    for sec in range(n_sec):
        xp_scr[sec, 0:SUBLANES, :] = prev[:, sec * LANES:(sec + 1) * LANES]
    for r0 in range(0, tb, GDN_UNIT):
        blk = qkv_ref[0, r0:r0 + GDN_UNIT, :].astype(F32)
        for sec in range(n_sec):
            xp_scr[sec, SUBLANES + r0:SUBLANES + r0 + GDN_UNIT, :] = blk[:, sec * LANES:(sec + 1) * LANES]
    for sec in range(n_sec):
        cols = slice(sec * LANES, (sec + 1) * LANES)
        acc = None
        for jj in range(CONV_K):
            start = SUBLANES - (CONV_K - 1) + jj
            term = xp_scr[sec, start:start + tb, :] * convw_ref[jj:jj + 1, cols]
            acc = term if acc is None else acc + term
        y = _silu(acc)
        which, head = divmod(sec, GDN_HEADS)
        if which == 0:
            q_scr[head] = y * lax.rsqrt(jnp.sum(y * y, axis=-1, keepdims=True) + EPS) * (GDN_DIM ** -0.5)
        elif which == 1:
            k_scr[head] = y * lax.rsqrt(jnp.sum(y * y, axis=-1, keepdims=True) + EPS)
        else:
            v_scr[head] = y

    sm = small_ref[0]
    beta_all = _sigmoid(sm)
    g_all = -jnp.exp(alog_ref[...]) * _softplus(sm + dtb_ref[...])

    br = lax.broadcasted_iota(jnp.int32, (tb, tb), 0)
    bc = lax.broadcasted_iota(jnp.int32, (tb, tb), 1)
    block_tril = jnp.where((br // CHUNK == bc // CHUNK) & (bc <= br), 1.0, 0.0).astype(BF16)
    gc_all = _dot_exact_lhs(block_tril, g_all)
    gct_all = jnp.transpose(gc_all)

    ur = lax.broadcasted_iota(jnp.int32, (GDN_UNIT, GDN_UNIT), 0)
    uc = lax.broadcasted_iota(jnp.int32, (GDN_UNIT, GDN_UNIT), 1)
    same = (ur // CHUNK) == (uc // CHUNK)
    causal = same & (uc <= ur)
    strict = same & (uc < ur)
    eye = jnp.where(ur == uc, 1.0, 0.0)
    first_rows = ur < CHUNK
    first_cols = uc < CHUNK
    n_units = tb // GDN_UNIT
    units = [(h, u) for h in range(GDN_HEADS) for u in range(n_units)]

    def lane_bcast(x, lane):
        return jnp.broadcast_to(x[:, lane:lane + 1], (GDN_UNIT, GDN_UNIT))

    def row_bcast(x, r):
        return jnp.broadcast_to(x[r:r + 1, :], (GDN_UNIT, GDN_UNIT))

    def rows_of(u):
        return slice(u * GDN_UNIT, (u + 1) * GDN_UNIT)

    def cols_of(h):
        return slice(h * LANES, (h + 1) * LANES)

    qs = [q_scr[h, rows_of(u), :] for h, u in units]
    ks = [k_scr[h, rows_of(u), :] for h, u in units]
    vs = [v_scr[h, rows_of(u), :] for h, u in units]
    betas = [lane_bcast(beta_all[rows_of(u)], h) for h, u in units]
    gcb = [lane_bcast(gc_all[rows_of(u)], GDN_HEADS + h) for h, u in units]
    gcr = [row_bcast(gct_all[:, rows_of(u)], GDN_HEADS + h) for h, u in units]
    decay = [jnp.exp(jnp.where(causal, a - b, NEG_INF)) for a, b in zip(gcb, gcr)]
    eg = [jnp.exp(a) for a in gcb]
    g_last = [jnp.where(first_rows, row_bcast(a, CHUNK - 1), row_bcast(a, GDN_UNIT - 1)) for a in gcb]
    kb = [k * b for k, b in zip(ks, betas)]
    vb = [v * b for v, b in zip(vs, betas)]
    a_low = [jnp.where(strict, _dot_nt(x, k) * d, 0.0) for x, k, d in zip(kb, ks, decay)]
    tinv = [eye - a for a in a_low]
    pw = a_low
    for _ in range(5):
        pw = [_dot(x, x) for x in pw]
        tinv = [t + _dot(t, x) for t, x in zip(tinv, pw)]
    uw = [_dot(t, jnp.concatenate([v, x * e], axis=1)) for t, v, x, e in zip(tinv, vb, kb, eg)]
    intra = [_dot_nt(q, k) * d for q, k, d in zip(qs, ks, decay)]
    iuw = [_dot(a, x) for a, x in zip(intra, uw)]
    k_dec_t = [jnp.transpose(k * jnp.exp(gl - a)) for k, gl, a in zip(ks, g_last, gcb)]
    wu = [jnp.concatenate([x[:, GDN_DIM:], x[:, :GDN_DIM]], axis=1) for x in uw]
    pn0 = [_dot(jnp.where(first_cols, kt, 0.0), x) for kt, x in zip(k_dec_t, wu)]
    pn1 = [_dot(jnp.where(first_cols, 0.0, kt), x) for kt, x in zip(k_dec_t, wu)]
    for n in range(len(units)):
        q_eff = (qs[n] * eg[n] - iuw[n][:, GDN_DIM:]).astype(BF16)
        for half, pn in enumerate((pn0[n], pn1[n])):
            hr = slice(half * CHUNK, (half + 1) * CHUNK)
            pq_scr[n, half, 0:GDN_DIM, :] = pn[:, :GDN_DIM].astype(BF16)
            pq_scr[n, half, GDN_DIM:GDN_DIM + CHUNK, :] = q_eff[hr]
            n_scr[n, half] = pn[:, GDN_DIM:]
            oc_scr[n, half] = iuw[n][hr, :GDN_DIM]
            gl_scr[n, half] = jnp.exp(gcb[n][(half + 1) * CHUNK - 1:(half + 1) * CHUNK, :])

    def chunk_body(c, carry):
        u = c // 2
        half = c % 2
        rows = pl.ds(pl.multiple_of(c * CHUNK, CHUNK), CHUNK)
        states = [state_scr[h] for h in range(GDN_HEADS)]
        res = [jnp.dot(pq_scr[h * n_units + u, half], states[h].astype(BF16), preferred_element_type=F32)
               for h in range(GDN_HEADS)]
        for h in range(GDN_HEADS):
            n = h * n_units + u
            state_scr[h] = gl_scr[n, half] * states[h] - res[h][0:GDN_DIM] + n_scr[n, half]
            oraw_scr[h, rows, :] = res[h][GDN_DIM:GDN_DIM + CHUNK] + oc_scr[n, half]
        return carry

    lax.fori_loop(0, tb // CHUNK, chunk_body, 0)

    for h in range(GDN_HEADS):
        o = oraw_scr[h]
        ms = jnp.mean(o * o, axis=-1, keepdims=True)
        y = o * lax.rsqrt(ms + EPS) * ngain_ref[...] * _silu(gate_ref[0, :, cols_of(h)].astype(F32))
        o_ref[0, :, cols_of(h)] = y.astype(o_ref.dtype)


def _gdn(proj, small, conv_w, alog_pad, dtb_pad, norm_gain, tb):
    b, s, _ = proj.shape
    nb = s // tb
    w3 = 3 * GDN_WIDTH
    assert tb % GDN_UNIT == 0
    n_hu = GDN_HEADS * (tb // GDN_UNIT)
    return pl.pallas_call(
        functools.partial(_gdn_kernel, tb=tb),
        out_shape=jax.ShapeDtypeStruct((b, s, GDN_WIDTH), BF16),
        grid=(b, nb),
        in_specs=[
            pl.BlockSpec((1, BF16_SUBLANES, w3),
                         lambda bi, i: (bi, jnp.maximum(i * (tb // BF16_SUBLANES) - 1, 0), PROJ_GDN // w3)),
            pl.BlockSpec((1, tb, w3), lambda bi, i: (bi, i, PROJ_GDN // w3)),
            pl.BlockSpec((1, tb, GDN_WIDTH), lambda bi, i: (bi, i, PROJ_GATE // GDN_WIDTH)),
            pl.BlockSpec((1, tb, LANES), lambda bi, i: (bi, i, 0)),
            pl.BlockSpec((CONV_K, w3), lambda bi, i: (0, 0)),
            pl.BlockSpec((1, LANES), lambda bi, i: (0, 0)),
            pl.BlockSpec((1, LANES), lambda bi, i: (0, 0)),
            pl.BlockSpec((1, GDN_DIM), lambda bi, i: (0, 0)),
        ],
        out_specs=pl.BlockSpec((1, tb, GDN_WIDTH), lambda bi, i: (bi, i, 0)),
        scratch_shapes=[
            pltpu.VMEM((w3 // LANES, tb + SUBLANES, LANES), F32),
            pltpu.VMEM((GDN_HEADS, tb, GDN_DIM), F32),
            pltpu.VMEM((GDN_HEADS, tb, GDN_DIM), F32),
            pltpu.VMEM((GDN_HEADS, tb, GDN_DIM), F32),
            pltpu.VMEM((n_hu, 2, GDN_DIM + CHUNK, GDN_DIM), BF16),
            pltpu.VMEM((n_hu, 2, GDN_DIM, GDN_DIM), F32),
            pltpu.VMEM((n_hu, 2, CHUNK, GDN_DIM), F32),
            pltpu.VMEM((n_hu, 2, 1, GDN_DIM), F32),
            pltpu.VMEM((GDN_HEADS, tb, GDN_DIM), F32),
            pltpu.VMEM((GDN_HEADS, GDN_DIM, GDN_DIM), F32),
        ],
        compiler_params=_params("parallel", "arbitrary"),
        name="gdn",
    )(proj, proj, proj, small, conv_w, alog_pad, dtb_pad, norm_gain)


MERGE_SUB = 256

ROW_PIECES = D_MODEL // LANES


def _store_tile_rows(ref, r0, value):
    n = value.shape[0]
    for j in range(ROW_PIECES):
        ref[pl.ds(r0 * ROW_PIECES + j, n, stride=ROW_PIECES), :] = value[:, j * LANES:(j + 1) * LANES]


def _load_tile_rows(ref, r0, n):
    return jnp.concatenate(
        [ref[pl.ds(r0 * ROW_PIECES + j, n, stride=ROW_PIECES), :] for j in range(ROW_PIECES)], axis=1)


def _merge_kernel(x_ref, ya_ref, yb_ref, ma_ref, mb_ref, wa_ref, wb_ref, wo_ref, g2_ref, wr_ref, br_ref,
                  x1_ref, h2_ref, route_ref, cnt_ref):
    tm = x_ref.shape[0]
    subs = [slice(r, r + MERGE_SUB) for r in range(0, tm, MERGE_SUB)]
    pa = [jnp.dot(ya_ref[rs, :], wa_ref[...], preferred_element_type=F32) for rs in subs]
    pb = [jnp.dot(yb_ref[rs, :], wb_ref[...], preferred_element_type=F32) for rs in subs]
    merged = [(_sigmoid(ma_ref[rs, :].astype(F32)) * a + _sigmoid(mb_ref[rs, :].astype(F32)) * b).astype(BF16)
              for rs, a, b in zip(subs, pa, pb)]
    x1s = [x_ref[rs, :] + jnp.dot(m, wo_ref[...], preferred_element_type=F32) for rs, m in zip(subs, merged)]
    h2s = [v * lax.rsqrt(jnp.mean(v * v, axis=-1, keepdims=True) + EPS) * g2_ref[...] for v in x1s]
    w_parts = _split_bf16(wr_ref[...])
    logit_s = [_dot_bf16x3(_split_bf16(v), w_parts) for v in h2s]
    for rs, v, hh in zip(subs, x1s, h2s):
        x1_ref[rs, :] = v
        _store_tile_rows(h2_ref, rs.start, hh)
    logits = jnp.concatenate(logit_s, axis=0) + br_ref[...]
    lane = lax.broadcasted_iota(jnp.int32, logits.shape, 1)
    big = jnp.int32(4 * LANES)
    gl = jnp.where(lane < MOE_GROUPS, logits, NEG_INF)
    gmax = jnp.max(gl, axis=-1, keepdims=True)
    gidx = jnp.min(jnp.where(gl == gmax, lane, big), axis=-1, keepdims=True)
    grp_w = 1.0 / jnp.sum(jnp.exp(gl - gmax), axis=-1, keepdims=True)
    lo = MOE_GROUPS + gidx * MOE_EXPERTS_PER_GROUP
    el = jnp.where((lane >= lo) & (lane < lo + MOE_EXPERTS_PER_GROUP), logits, NEG_INF)
    v1 = jnp.max(el, axis=-1, keepdims=True)
    i1 = jnp.min(jnp.where(el == v1, lane, big), axis=-1, keepdims=True)
    el2 = jnp.where(lane == i1, NEG_INF, el)
    v2 = jnp.max(el2, axis=-1, keepdims=True)
    i2 = jnp.min(jnp.where(el2 == v2, lane, big), axis=-1, keepdims=True)
    e2 = jnp.exp(v2 - v1)
    w1 = grp_w / (1.0 + e2)
    w2 = w1 * e2
    id1 = i1 - MOE_GROUPS
    id2 = i2 - MOE_GROUPS
    route_ref[...] = jnp.where(lane == 0, id1.astype(F32), jnp.where(lane == 1, id2.astype(F32),
                               jnp.where(lane == 2, w1, jnp.where(lane == 3, w2, 0.0))))

    @pl.when(pl.program_id(0) == 0)
    def _():
        cnt_ref[...] = jnp.zeros(cnt_ref.shape, F32)

    hits = jnp.where((lane == id1) | (lane == id2), 1.0, 0.0)
    cnt_ref[...] += jnp.broadcast_to(jnp.sum(hits, axis=0, keepdims=True), cnt_ref.shape)


def _merge(x2, ya, yb, proj, wa, wb, wo, g2, wr, br, tm):
    t, d = x2.shape
    row = lambda i: (i, 0)
    const = lambda i: (0, 0)
    return pl.pallas_call(
        _merge_kernel,
        out_shape=(
            jax.ShapeDtypeStruct((t, d), F32),
            jax.ShapeDtypeStruct((t * ROW_PIECES, LANES), F32),
            jax.ShapeDtypeStruct((t, LANES), F32),
            jax.ShapeDtypeStruct((SUBLANES, LANES), F32),
        ),
        grid=(t // tm,),
        in_specs=[
            pl.BlockSpec((tm, d), row),
            pl.BlockSpec((tm, DIFF_WIDTH), row),
            pl.BlockSpec((tm, GDN_WIDTH), row),
            pl.BlockSpec((tm, d), lambda i: (i, PROJ_MA // D_MODEL)),
            pl.BlockSpec((tm, d), lambda i: (i, PROJ_MB // D_MODEL)),
            pl.BlockSpec((DIFF_WIDTH, d), const),
            pl.BlockSpec((GDN_WIDTH, d), const),
            pl.BlockSpec((d, d), const),
            pl.BlockSpec((1, d), const),
            pl.BlockSpec((d, LANES), const),
            pl.BlockSpec((1, LANES), const),
        ],
        out_specs=(pl.BlockSpec((tm, d), row), pl.BlockSpec((tm * ROW_PIECES, LANES), row),
                   pl.BlockSpec((tm, LANES), row), pl.BlockSpec((SUBLANES, LANES), const)),
        compiler_params=_params("arbitrary"),
        name="merge_router",
    )(x2, ya, yb, proj, proj, wa, wb, wo, g2, wr, br)


MOE_BLK = 512
MOE_SUB = 256
MOE_META_LANES = 256
MOE_ROUTE_TILE = 512
MOE_DMA_TILE = 256


def _route_kernel(route_ref, cnt_ref, pos_ref, meta_ref, run_scr, *, n_rows):
    i = pl.program_id(0)
    tp = route_ref.shape[0]

    @pl.when(i == 0)
    def _():
        cnt = cnt_ref[...]
        jr = lax.broadcasted_iota(jnp.int32, (LANES, LANES), 0)
        jc = lax.broadcasted_iota(jnp.int32, (LANES, LANES), 1)
        upper = jnp.where(jr < jc, 1.0, 0.0)
        hi_prec = dict(preferred_element_type=F32, precision=lax.Precision.HIGHEST)
        off = jnp.dot(cnt, upper, **hi_prec)
        run_scr[...] = off[0:1]
        blk = float(MOE_BLK)
        first_tile = jnp.floor(off / blk)
        last_tile = jnp.floor((off + cnt - 1.0) / blk)
        n_it = jnp.where(cnt > 0.0, last_tile - first_tile + 1.0, 0.0)
        it_start = jnp.dot(n_it, upper, **hi_prec)
        it_end = it_start + n_it
        lane8 = lax.broadcasted_iota(jnp.int32, cnt.shape, 1)
        e_max = jnp.max(jnp.where(cnt > 0.0, lane8, 0), axis=-1, keepdims=True).astype(F32)[0:1]
        sub8 = lax.broadcasted_iota(jnp.int32, cnt.shape, 0)
        table = jnp.where(sub8 == 0, first_tile, jnp.where(sub8 == 1, it_start, jnp.where(
            sub8 == 2, off, jnp.where(sub8 == 3, cnt, it_end))))
        cols = jnp.transpose(table)
        shape = (LANES, MOE_META_LANES)
        e_sub = lax.broadcasted_iota(jnp.int32, shape, 0)
        w_lane = lax.broadcasted_iota(jnp.int32, shape, 1).astype(F32)
        col = lambda k: jnp.broadcast_to(cols[:, k:k + 1], shape)
        e_w = jnp.sum(jnp.where((e_sub < MOE_EXPERTS) & (col(4) <= w_lane), 1.0, 0.0), axis=0, keepdims=True)
        valid = e_w < float(MOE_EXPERTS)
        e_w = jnp.minimum(e_w, e_max)
        sel = e_sub.astype(F32) == e_w
        pick = lambda k: jnp.sum(jnp.where(sel, col(k), 0.0), axis=0, keepdims=True)
        w_row = w_lane[0:1]
        tile_w = jnp.where(valid, pick(0) + (w_row - pick(1)), float(n_rows // MOE_BLK - 1))
        lo_w = jnp.maximum(pick(2) - tile_w * blk, 0.0)
        hi_w = jnp.minimum(pick(2) + pick(3) - tile_w * blk, blk)
        lo_w = jnp.where(valid, lo_w, 0.0)
        hi_w = jnp.where(valid, hi_w, 0.0)
        sub_m = lax.broadcasted_iota(jnp.int32, meta_ref.shape, 0)
        bc = lambda v: jnp.broadcast_to(v, meta_ref.shape)
        meta_ref[...] = jnp.where(sub_m == 0, bc(e_w), jnp.where(sub_m == 1, bc(tile_w), jnp.where(
            sub_m == 2, bc(lo_w), bc(hi_w)))).astype(jnp.int32)

    r = route_ref[...]
    lane = lax.broadcasted_iota(jnp.int32, r.shape, 1)
    lane_f = lane.astype(F32)
    oh1 = lane_f == r[:, 0:1]
    oh2 = lane_f == r[:, 1:2]
    hits = jnp.where(oh1 | oh2, 1.0, 0.0)
    tr = lax.broadcasted_iota(jnp.int32, (tp, tp), 0)
    tc = lax.broadcasted_iota(jnp.int32, (tp, tp), 1)
    earlier = jnp.where(tc < tr, 1.0, 0.0).astype(BF16)
    rank = jnp.dot(earlier, hits.astype(BF16), preferred_element_type=F32)
    base = run_scr[...] + rank
    p1 = jnp.sum(jnp.where(oh1, base, 0.0), axis=-1, keepdims=True)
    p2 = jnp.sum(jnp.where(oh2, base, 0.0), axis=-1, keepdims=True)
    pos_ref[...] = jnp.where(lane == 0, p1, jnp.where(lane == 1, p2, 0.0)).astype(jnp.int32)
    run_scr[...] += jnp.sum(hits, axis=0, keepdims=True)


def _route_positions(route, cnt):
    t = route.shape[0]
    tp = min(MOE_ROUTE_TILE, t)
    n_rows = 2 * t
    assert n_rows % MOE_BLK == 0 and n_rows // MOE_BLK + MOE_EXPERTS <= MOE_META_LANES and n_rows < 2 ** 24
    return pl.pallas_call(
        functools.partial(_route_kernel, n_rows=n_rows),
        out_shape=(jax.ShapeDtypeStruct((t, LANES), jnp.int32),
                   jax.ShapeDtypeStruct((SUBLANES, MOE_META_LANES), jnp.int32)),
        grid=(t // tp,),
        in_specs=[pl.BlockSpec((tp, LANES), lambda i: (i, 0)),
                  pl.BlockSpec((SUBLANES, LANES), lambda i: (0, 0))],
        out_specs=(pl.BlockSpec((tp, LANES), lambda i: (i, 0)),
                   pl.BlockSpec((SUBLANES, MOE_META_LANES), lambda i: (0, 0))),
        scratch_shapes=[pltpu.VMEM((1, LANES), F32)],
        compiler_params=_params("arbitrary"),
        name="route_positions",
    )(route, cnt)


def _scatter_kernel(pos_ref, h2_ref, xs_ref, sem):
    ts = h2_ref.shape[0] // ROW_PIECES

    def body(tok, carry):
        src = h2_ref.at[pl.ds(pl.multiple_of(tok * ROW_PIECES, ROW_PIECES), ROW_PIECES), :]
        for k in range(2):
            dst = xs_ref.at[pl.ds(pl.multiple_of(pos_ref[0, 0, 2 * tok + k], ROW_PIECES), ROW_PIECES), :]
            pltpu.make_async_copy(src, dst, sem).start()
        return carry

    lax.fori_loop(0, ts, body, 0, unroll=8)
    for _ in range(2):
        pltpu.make_async_copy(h2_ref, xs_ref.at[pl.ds(0, ts * ROW_PIECES), :], sem).wait()


def _scatter_rows(pos3, h2t):
    t = h2t.shape[0] // ROW_PIECES
    ts = pos3.shape[2] // 2
    return pl.pallas_call(
        _scatter_kernel,
        out_shape=jax.ShapeDtypeStruct((2 * t * ROW_PIECES, LANES), F32),
        grid=(t // ts,),
        in_specs=[pl.BlockSpec((1, 1, 2 * ts), lambda i: (i, 0, 0), memory_space=pltpu.SMEM),
                  pl.BlockSpec((ts * ROW_PIECES, LANES), lambda i: (i, 0))],
        out_specs=pl.BlockSpec(memory_space=pl.ANY),
        scratch_shapes=[pltpu.SemaphoreType.DMA],
        compiler_params=pltpu.CompilerParams(dimension_semantics=("arbitrary",), vmem_limit_bytes=VMEM_LIMIT,
                                             disable_bounds_checks=True),
        name="scatter_rows",
    )(pos3, h2t)


def _expert_kernel(ie_ref, it_ref, lo_ref, hi_ref, xs_ref, wg_ref, wu_ref, wd_ref, y_ref,
                   wg_scr, wu_scr, wd_scr, acc_scr):
    w = pl.program_id(0)
    n_items = pl.num_programs(0)
    prev = jnp.maximum(w - 1, 0)
    nxt = jnp.minimum(w + 1, n_items - 1)

    @pl.when((w == 0) | (ie_ref[w] != ie_ref[prev]))
    def _():
        wg_scr[...] = wg_ref[0].astype(BF16)
        wu_scr[...] = wu_ref[0].astype(BF16)
        wd_scr[...] = wd_ref[0].astype(BF16)

    @pl.when((w == 0) | (it_ref[w] != it_ref[prev]))
    def _():
        acc_scr[...] = jnp.zeros(acc_scr.shape, F32)

    lo = lo_ref[w]
    hi = hi_ref[w]
    subs = [slice(r, r + MOE_SUB) for r in range(0, acc_scr.shape[0], MOE_SUB)]

    @pl.when(hi > lo)
    def _():
        xs = [_load_tile_rows(xs_ref, rs.start, MOE_SUB).astype(BF16) for rs in subs]
        hg = [jnp.dot(x, wg_scr[...], preferred_element_type=F32) for x in xs]
        hu = [jnp.dot(x, wu_scr[...], preferred_element_type=F32) for x in xs]
        act = [(_silu(g) * u).astype(BF16) for g, u in zip(hg, hu)]
        yp = [jnp.dot(a, wd_scr[...], preferred_element_type=F32) for a in act]
        for rs, v in zip(subs, yp):
            row = rs.start + lax.broadcasted_iota(jnp.int32, v.shape, 0)
            acc_scr[rs, :] += jnp.where((row >= lo) & (row < hi), v, 0.0)

    @pl.when((w == n_items - 1) | (it_ref[nxt] != it_ref[w]))
    def _():
        for rs in subs:
            _store_tile_rows(y_ref, rs.start, acc_scr[rs, :])


def _experts(meta, xs, wg, wu, wd):
    n_rows = xs.shape[0] // ROW_PIECES
    d = wg.shape[1]
    n_items = n_rows // MOE_BLK + MOE_EXPERTS
    ie, it, lo, hi = (meta[k, :n_items] for k in range(4))
    return pl.pallas_call(
        _expert_kernel,
        out_shape=jax.ShapeDtypeStruct((n_rows * ROW_PIECES, LANES), F32),
        grid_spec=pltpu.PrefetchScalarGridSpec(
            num_scalar_prefetch=4,
            grid=(n_items,),
            in_specs=[
                pl.BlockSpec((MOE_BLK * ROW_PIECES, LANES), lambda w, ie, it, lo, hi: (it[w], 0)),
                pl.BlockSpec((1, d, MOE_HIDDEN), lambda w, ie, it, lo, hi: (ie[w], 0, 0)),
                pl.BlockSpec((1, d, MOE_HIDDEN), lambda w, ie, it, lo, hi: (ie[w], 0, 0)),
                pl.BlockSpec((1, MOE_HIDDEN, d), lambda w, ie, it, lo, hi: (ie[w], 0, 0)),
            ],
            out_specs=pl.BlockSpec((MOE_BLK * ROW_PIECES, LANES), lambda w, ie, it, lo, hi: (it[w], 0)),
            scratch_shapes=[pltpu.VMEM((d, MOE_HIDDEN), BF16), pltpu.VMEM((d, MOE_HIDDEN), BF16),
                            pltpu.VMEM((MOE_HIDDEN, d), BF16), pltpu.VMEM((MOE_BLK, d), F32)],
        ),
        compiler_params=_params("arbitrary"),
        name="experts",
    )(ie, it, lo, hi, xs, wg, wu, wd)


def _combine_kernel(pos_ref, posn_ref, x1_ref, route_ref, gf_ref, y_ref, o_ref, ybuf, sem):
    i = pl.program_id(0)
    n = pl.num_programs(0)
    tc = x1_ref.shape[0]
    slot = i % 2

    def issue(p_ref, s):
        def body(tok, carry):
            for k in range(2):
                src = y_ref.at[pl.ds(pl.multiple_of(p_ref[0, 0, 2 * tok + k], ROW_PIECES), ROW_PIECES), :]
                dst = ybuf.at[s, k, pl.ds(pl.multiple_of(tok * ROW_PIECES, ROW_PIECES), ROW_PIECES), :]
                pltpu.make_async_copy(src, dst, sem.at[s]).start()
            return carry

        lax.fori_loop(0, tc, body, 0, unroll=8)

    @pl.when(i == 0)
    def _():
        issue(pos_ref, 0)

    @pl.when(i + 1 < n)
    def _():
        issue(posn_ref, 1 - slot)

    for k in range(2):
        pltpu.make_async_copy(y_ref.at[pl.ds(0, tc * ROW_PIECES), :], ybuf.at[slot, k], sem.at[slot]).wait()
    r = route_ref[...]
    x2 = (x1_ref[...] + r[:, 2:3] * _load_tile_rows(ybuf.at[slot, 0], 0, tc)
          + r[:, 3:4] * _load_tile_rows(ybuf.at[slot, 1], 0, tc))
    ms = jnp.mean(x2 * x2, axis=-1, keepdims=True)
    o_ref[...] = x2 * lax.rsqrt(ms + EPS) * gf_ref[...]


def _combine(pos3, x1, route, gf, y):
    t, d = x1.shape
    tc = pos3.shape[2] // 2
    n = t // tc
    return pl.pallas_call(
        _combine_kernel,
        out_shape=jax.ShapeDtypeStruct((t, d), F32),
        grid=(n,),
        in_specs=[
            pl.BlockSpec((1, 1, 2 * tc), lambda i: (i, 0, 0), memory_space=pltpu.SMEM),
            pl.BlockSpec((1, 1, 2 * tc), lambda i: (jnp.minimum(i + 1, n - 1), 0, 0), memory_space=pltpu.SMEM),
            pl.BlockSpec((tc, d), lambda i: (i, 0)),
            pl.BlockSpec((tc, LANES), lambda i: (i, 0)),
            pl.BlockSpec((1, d), lambda i: (0, 0)),
            pl.BlockSpec(memory_space=pl.ANY),
        ],
        out_specs=pl.BlockSpec((tc, d), lambda i: (i, 0)),
        scratch_shapes=[pltpu.VMEM((2, 2, tc * ROW_PIECES, LANES), F32), pltpu.SemaphoreType.DMA((2,))],
        compiler_params=pltpu.CompilerParams(dimension_semantics=("arbitrary",), vmem_limit_bytes=VMEM_LIMIT,
                                             disable_bounds_checks=True),
        name="combine_norm",
    )(pos3, pos3, x1, route, gf, y)


def _pad_lanes(v, offset):
    return jnp.zeros((1, LANES), F32).at[0, offset:offset + v.shape[0]].set(v.astype(F32))


def kernel(x, norm_mix_gain, w_in, diff_lambda_q1, diff_lambda_k1, diff_lambda_q2, diff_lambda_k2, diff_subln_gain, gdn_conv_w, gdn_a_log, gdn_dt_bias, gdn_norm_gain, w_branch_attn, w_branch_gdn, w_out, norm_ffn_gain, moe_w_group, moe_b_group, moe_w_expert, moe_b_expert, moe_w_gate, moe_w_up, moe_w_down, norm_final_gain):
    b, s, d = x.shape
    t = b * s
    x2 = x.reshape(t, d)

    w = w_in[0]
    small_lo = PROJ_GDN + 3 * GDN_WIDTH
    small_hi = small_lo + 2 * GDN_HEADS
    gate_hi = small_hi + GDN_WIDTH
    w_main = jnp.concatenate([w[:, :small_lo], w[:, gate_hi:], w[:, small_hi:gate_hi]], axis=1).astype(BF16)
    w_small = jnp.concatenate([w[:, small_lo:small_hi], jnp.zeros((d, LANES - 2 * GDN_HEADS), w.dtype)],
                              axis=1).astype(BF16)
    gain1 = norm_mix_gain[0].reshape(1, d)

    proj2, small2 = _inproj(x2, gain1, w_main, w_small, min(1024, t), PROJ_WIDTH // 2)
    proj = proj2.reshape(b, s, PROJ_WIDTH)

    lam_params = jnp.stack([diff_lambda_q1[0], diff_lambda_k1[0], diff_lambda_q2[0], diff_lambda_k2[0]]).astype(F32)
    ya = _diff_attention(lam_params, proj, diff_subln_gain[0].reshape(1, DIFF_V_DIM))

    yb = _gdn(proj, small2.reshape(b, s, LANES), gdn_conv_w[0], _pad_lanes(gdn_a_log[0], GDN_HEADS),
              _pad_lanes(gdn_dt_bias[0], GDN_HEADS), gdn_norm_gain[0].reshape(1, GDN_DIM), tb=min(512, s))

    wr = jnp.concatenate([moe_w_group[0], moe_w_expert[0],
                          jnp.zeros((d, LANES - MOE_GROUPS - MOE_EXPERTS), F32)], axis=1)
    br = _pad_lanes(jnp.concatenate([moe_b_group[0], moe_b_expert[0]]), 0)
    x1, h2, route, cnt = _merge(x2, ya.reshape(t, DIFF_WIDTH), yb.reshape(t, GDN_WIDTH), proj2,
                                w_branch_attn[0].astype(BF16), w_branch_gdn[0].astype(BF16),
                                w_out[0].astype(BF16), norm_ffn_gain[0].reshape(1, d), wr, br, tm=min(512, t))

    pos, meta = _route_positions(route, cnt)
    ts = min(MOE_DMA_TILE, t)
    pos3 = (pos[:, :2] * ROW_PIECES).reshape(t // ts, 1, 2 * ts)
    xs = _scatter_rows(pos3, h2)
    y = _experts(meta, xs, moe_w_gate[0], moe_w_up[0], moe_w_down[0])
    out = _combine(pos3, x1, route, norm_final_gain.reshape(1, d), y)
    return out.reshape(b, s, d)
```

```python
import functools
import math

import jax
import jax.numpy as jnp
from jax import lax
from jax.experimental import pallas as pl
from jax.experimental.pallas import tpu as pltpu

F32 = jnp.float32
BF16 = jnp.bfloat16

D_MODEL = 1024
CHUNK = 64
EPS = 1e-6

DIFF_HEADS = 4
DIFF_HEAD_DIM = 64
DIFF_V_DIM = 2 * DIFF_HEAD_DIM
DIFF_WIDTH = DIFF_HEADS * DIFF_V_DIM

GDN_HEADS = 4
GDN_DIM = 128
GDN_WIDTH = GDN_HEADS * GDN_DIM
CONV_K = 4

MOE_GROUPS = 4
MOE_EXPERTS_PER_GROUP = 8
MOE_EXPERTS = MOE_GROUPS * MOE_EXPERTS_PER_GROUP
MOE_HIDDEN = 256

LANES = 128
SUBLANES = 8
BF16_SUBLANES = 16
VMEM_LIMIT = 56 * 1024 * 1024

LAMBDA_INIT = 0.8 - 0.6 * math.exp(-0.3 * 0)

PROJ_ATTN = 0
PROJ_GDN = 3 * DIFF_WIDTH
PROJ_MA = PROJ_GDN + 3 * GDN_WIDTH
PROJ_MB = PROJ_MA + D_MODEL
PROJ_GATE = PROJ_MB + D_MODEL
PROJ_WIDTH = PROJ_GATE + GDN_WIDTH

NEG_INF = float("-inf")
LOG2_E = math.log2(math.e)


def _params(*sem):
    return pltpu.CompilerParams(dimension_semantics=sem, vmem_limit_bytes=VMEM_LIMIT)


def _inproj_kernel(x_ref, g_ref, w_ref, ws_ref, o_ref, os_ref, h_scr):
    @pl.when(pl.program_id(1) == 0)
    def _():
        x = x_ref[...]
        ms = jnp.mean(x * x, axis=-1, keepdims=True)
        h_scr[...] = (x * lax.rsqrt(ms + EPS) * g_ref[...]).astype(BF16)
        os_ref[...] = jnp.dot(h_scr[...], ws_ref[...], preferred_element_type=F32)

    o_ref[...] = jnp.dot(h_scr[...], w_ref[...], preferred_element_type=F32).astype(o_ref.dtype)


def _inproj(x2, gain, w, w_small, tm, tn):
    t, d = x2.shape
    n = w.shape[1]
    return pl.pallas_call(
        _inproj_kernel,
        out_shape=(jax.ShapeDtypeStruct((t, n), BF16), jax.ShapeDtypeStruct((t, LANES), F32)),
        grid=(t // tm, n // tn),
        in_specs=[
            pl.BlockSpec((tm, d), lambda i, j: (i, 0)),
            pl.BlockSpec((1, d), lambda i, j: (0, 0)),
            pl.BlockSpec((d, tn), lambda i, j: (0, j)),
            pl.BlockSpec((d, LANES), lambda i, j: (0, 0)),
        ],
        out_specs=(pl.BlockSpec((tm, tn), lambda i, j: (i, j)), pl.BlockSpec((tm, LANES), lambda i, j: (i, 0))),
        scratch_shapes=[pltpu.VMEM((tm, d), BF16)],
        compiler_params=_params("parallel", "arbitrary"),
        name="inproj",
    )(x2, gain, w, w_small)


ATTN_TQ = 1024
ATTN_TK = 512
MASK_BIAS = -1e30
ATTN_V_ROWS = DIFF_V_DIM + BF16_SUBLANES
ATTN_COLS = 128


def _attn_kernel(lam_ref, q_ref, k_ref, v_ref, gain_ref, o_ref,
                 qt_scr, oh_scr, vt_scr, s0_scr, s1_scr, p0_scr, p1_scr, a0_scr, a1_scr, m_scr, acc_scr, *, tq, tk):
    i = pl.program_id(2)
    n_q = 2 * tq
    span = tq // tk
    t_last = span * (i + 1) - 1
    n_pairs = (t_last + 2) // 2
    n_slots = tk // CHUNK

    n_blk = n_q // ATTN_COLS
    half = tq // ATTN_COLS

    @pl.when(i == 0)
    def _():
        for c in range(v_ref.shape[1] // tk):
            v_t = jnp.transpose(v_ref[0, c * tk:(c + 1) * tk, :].astype(F32)).astype(BF16)
            for j in range(tk // LANES):
                vt_scr[c, j] = jnp.concatenate(
                    [v_t[:, j * LANES:(j + 1) * LANES], jnp.ones((BF16_SUBLANES, LANES), BF16)], axis=0)
        slot = lax.broadcasted_iota(jnp.int32, (LANES, ATTN_COLS), 0)
        lane = lax.broadcasted_iota(jnp.int32, (LANES, ATTN_COLS), 1)
        for ver in range(span + 2):
            for c in range(n_blk):
                rel = (lane + (c % half) * ATTN_COLS) // CHUNK - (ver - 1) * n_slots
                if ver == 0:
                    rel = rel * 0 + 2 * n_slots
                elif ver == span + 1:
                    rel = rel * 0 - 1
                want = jnp.where(rel < 0, n_slots, jnp.where(rel < n_slots, rel, -1))
                oh_scr[ver, c] = jnp.where(slot == want, 1.0, 0.0).astype(BF16)

    q_t = jnp.transpose(q_ref[0].astype(F32) * (DIFF_HEAD_DIM ** -0.5 * LOG2_E))
    dim = lax.broadcasted_iota(jnp.int32, (LANES, tq), 0)
    q_comp = (jnp.where(dim < DIFF_HEAD_DIM, q_t, 0.0).astype(BF16),
              jnp.where(dim >= DIFF_HEAD_DIM, q_t, 0.0).astype(BF16))
    for c in range(n_blk):
        qt_scr[c] = q_comp[c // half][:, (c % half) * ATTN_COLS:(c % half + 1) * ATTN_COLS]

    k_lane = lax.broadcasted_iota(jnp.int32, (tk, LANES), 1)
    k_chunk = lax.broadcasted_iota(jnp.int32, (tk, LANES), 0) // CHUNK
    bias = jnp.where((k_lane <= n_slots) & ((k_chunk > k_lane) | (k_lane == n_slots)), MASK_BIAS, 0.0).astype(BF16)

    m_scr[...] = jnp.full(m_scr.shape, NEG_INF, F32)
    acc_scr[...] = jnp.zeros(acc_scr.shape, F32)
    p1_scr[...] = jnp.zeros(p1_scr.shape, BF16)
    a1_scr[...] = jnp.ones(a1_scr.shape, F32)

    def scores(t, s_scr):
        ver = jnp.clip(t - span * i + 1, 0, span + 1)
        rows = pl.ds(pl.multiple_of(jnp.minimum(t, t_last) * tk, tk), tk)
        k_aug = jnp.concatenate([k_ref[0, rows, :], bias], axis=1)
        q_all = jnp.concatenate(
            [jnp.concatenate([qt_scr[c], oh_scr[ver, c]], axis=0) for c in range(n_blk)], axis=1)
        s = jnp.dot(k_aug, q_all, preferred_element_type=F32)
        for c in range(n_blk):
            s_scr[c] = s[:, c * ATTN_COLS:(c + 1) * ATTN_COLS]

    def softmax(s_scr, p_scr, a_scr):
        for c in range(n_blk):
            cs = slice(c * ATTN_COLS, (c + 1) * ATTN_COLS)
            s = s_scr[c]
            m_prev = m_scr[:, cs]
            m_new = jnp.maximum(m_prev, jnp.max(s, axis=0, keepdims=True))
            a_scr[:, cs] = jnp.exp2(m_prev - m_new)
            m_scr[:, cs] = m_new
            p_scr[c] = jnp.exp2(s - m_new).astype(BF16)

    def values(t, p_scr, a_scr):
        tile = jnp.minimum(t, t_last)
        v_all = jnp.concatenate([vt_scr[tile, j] for j in range(tk // LANES)], axis=1)
        p = jnp.concatenate([p_scr[c] for c in range(n_blk)], axis=1)
        pv = jnp.dot(v_all, p, preferred_element_type=F32)
        for c in range(n_blk):
            cs = slice(c * ATTN_COLS, (c + 1) * ATTN_COLS)
            acc_scr[c] = a_scr[:, cs] * acc_scr[c] + pv[:, cs]

    scores(0, s0_scr)

    def pair_step(g, carry):
        t0 = 2 * g
        scores(t0 + 1, s1_scr)
        softmax(s0_scr, p0_scr, a0_scr)
        values(jnp.maximum(t0 - 1, 0), p1_scr, a1_scr)
        scores(t0 + 2, s0_scr)
        softmax(s1_scr, p1_scr, a1_scr)
        values(t0, p0_scr, a0_scr)
        return carry

    lax.fori_loop(0, n_pairs, pair_step, 0)
    values(2 * n_pairs - 1, p1_scr, a1_scr)

    lp = lam_ref[...]
    lam = (jnp.exp(jnp.sum(lp[0:1] * lp[1:2], axis=-1, keepdims=True))
           - jnp.exp(jnp.sum(lp[2:3] * lp[3:4], axis=-1, keepdims=True)) + LAMBDA_INIT)
    for c in range(half):
        o0_t = acc_scr[c, 0:DIFF_V_DIM, :] / acc_scr[c, DIFF_V_DIM:DIFF_V_DIM + 1, :]
        o1_t = acc_scr[half + c, 0:DIFF_V_DIM, :] / acc_scr[half + c, DIFF_V_DIM:DIFF_V_DIM + 1, :]
        o = jnp.transpose(o0_t - lam * o1_t)
        ms = jnp.mean(o * o, axis=-1, keepdims=True)
        y = (o * lax.rsqrt(ms + EPS) * gain_ref[...]) * (1.0 - LAMBDA_INIT)
        o_ref[0, c * ATTN_COLS:(c + 1) * ATTN_COLS, :] = y.astype(o_ref.dtype)


def _diff_attention(lam_params, qkv, subln_gain):
    b, s, _ = qkv.shape
    tq = min(ATTN_TQ, s)
    tk = min(ATTN_TK, s)
    assert s % tq == 0 and tq % tk == 0 and tk % LANES == 0 and tk // CHUNK < LANES
    n_q = 2 * tq
    return pl.pallas_call(
        functools.partial(_attn_kernel, tq=tq, tk=tk),
        out_shape=jax.ShapeDtypeStruct((b, s, DIFF_WIDTH), BF16),
        grid=(b, DIFF_HEADS, s // tq),
        in_specs=[
            pl.BlockSpec((4, DIFF_HEAD_DIM), lambda bi, h, i: (0, 0)),
            pl.BlockSpec((1, tq, DIFF_V_DIM), lambda bi, h, i: (bi, i, h)),
            pl.BlockSpec((1, s, DIFF_V_DIM), lambda bi, h, i: (bi, 0, DIFF_HEADS + h)),
            pl.BlockSpec((1, s, DIFF_V_DIM), lambda bi, h, i: (bi, 0, 2 * DIFF_HEADS + h)),
            pl.BlockSpec((1, DIFF_V_DIM), lambda bi, h, i: (0, 0)),
        ],
        out_specs=pl.BlockSpec((1, tq, DIFF_V_DIM), lambda bi, h, i: (bi, i, h)),
        scratch_shapes=[
            pltpu.VMEM((n_q // ATTN_COLS, LANES, ATTN_COLS), BF16),
            pltpu.VMEM((tq // tk + 2, n_q // ATTN_COLS, LANES, ATTN_COLS), BF16),
            pltpu.VMEM((s // tk, tk // LANES, ATTN_V_ROWS, LANES), BF16),
            pltpu.VMEM((n_q // ATTN_COLS, tk, ATTN_COLS), F32),
            pltpu.VMEM((n_q // ATTN_COLS, tk, ATTN_COLS), F32),
            pltpu.VMEM((n_q // ATTN_COLS, tk, ATTN_COLS), BF16),
            pltpu.VMEM((n_q // ATTN_COLS, tk, ATTN_COLS), BF16),
            pltpu.VMEM((1, n_q), F32),
            pltpu.VMEM((1, n_q), F32),
            pltpu.VMEM((1, n_q), F32),
            pltpu.VMEM((n_q // ATTN_COLS, ATTN_V_ROWS, ATTN_COLS), F32),
        ],
        compiler_params=_params("parallel", "parallel", "arbitrary"),
        name="diff_attn",
    )(lam_params, qkv, qkv, qkv, subln_gain)


def _silu(x):
    return x * (1.0 / (1.0 + jnp.exp(-x)))


def _sigmoid(x):
    return 1.0 / (1.0 + jnp.exp(-x))


def _softplus(x):
    return jnp.maximum(x, 0.0) + jnp.log(1.0 + jnp.exp(-jnp.abs(x)))


def _split_bf16(x):
    hi = x.astype(BF16)
    return hi, (x - hi.astype(F32)).astype(BF16)


def _dot_bf16x3(a_parts, b_parts):
    (a_hi, a_lo), (b_hi, b_lo) = a_parts, b_parts
    d = lambda x, y: jnp.dot(x, y, preferred_element_type=F32)
    return d(a_hi, b_hi) + (d(a_lo, b_hi) + d(a_hi, b_lo))


def _dot_exact_lhs(lhs, x):
    x1 = x.astype(BF16)
    r1 = x - x1.astype(F32)
    x2 = r1.astype(BF16)
    x3 = (r1 - x2.astype(F32)).astype(BF16)
    d = lambda y: jnp.dot(lhs, y, preferred_element_type=F32)
    return d(x1) + (d(x2) + d(x3))


def _dot(a, b):
    return jnp.dot(a.astype(BF16), b.astype(BF16), preferred_element_type=F32)


def _dot_nt(a, b):
    return lax.dot_general(a.astype(BF16), b.astype(BF16), (((1,), (1,)), ((), ())), preferred_element_type=F32)


GDN_UNIT = 2 * CHUNK


def _gdn_kernel(prev_ref, qkv_ref, gate_ref, small_ref, convw_ref, alog_ref, dtb_ref, ngain_ref, o_ref,
                xp_scr, q_scr, k_scr, v_scr, pq_scr, n_scr, oc_scr, gl_scr, oraw_scr, state_scr, *, tb):
    i = pl.program_id(1)

    @pl.when(i == 0)
    def _():
        state_scr[...] = jnp.zeros(state_scr.shape, F32)

    n_sec = 3 * GDN_HEADS
    prev = jnp.where(i == 0, 0.0, prev_ref[0, BF16_SUBLANES - SUBLANES:BF16_SUBLANES, :].astype(F32))
    for sec in range(n_sec):
        xp_scr[sec, 0:SUBLANES, :] = prev[:, sec * LANES:(sec + 1) * LANES]
    for r0 in range(0, tb, GDN_UNIT):
        blk = qkv_ref[0, r0:r0 + GDN_UNIT, :].astype(F32)
        for sec in range(n_sec):
            xp_scr[sec, SUBLANES + r0:SUBLANES + r0 + GDN_UNIT, :] = blk[:, sec * LANES:(sec + 1) * LANES]
    for sec in range(n_sec):
        cols = slice(sec * LANES, (sec + 1) * LANES)
        acc = None
        for jj in range(CONV_K):
            start = SUBLANES - (CONV_K - 1) + jj
            term = xp_scr[sec, start:start + tb, :] * convw_ref[jj:jj + 1, cols]
            acc = term if acc is None else acc + term
        y = _silu(acc)
        which, head = divmod(sec, GDN_HEADS)
        if which == 0:
            q_scr[head] = y * lax.rsqrt(jnp.sum(y * y, axis=-1, keepdims=True) + EPS) * (GDN_DIM ** -0.5)
        elif which == 1:
            k_scr[head] = y * lax.rsqrt(jnp.sum(y * y, axis=-1, keepdims=True) + EPS)
        else:
            v_scr[head] = y

    sm = small_ref[0]
    beta_all = _sigmoid(sm)
    g_all = -jnp.exp(alog_ref[...]) * _softplus(sm + dtb_ref[...])

    br = lax.broadcasted_iota(jnp.int32, (tb, tb), 0)
    bc = lax.broadcasted_iota(jnp.int32, (tb, tb), 1)
    block_tril = jnp.where((br // CHUNK == bc // CHUNK) & (bc <= br), 1.0, 0.0).astype(BF16)
    gc_all = _dot_exact_lhs(block_tril, g_all)
    gct_all = jnp.transpose(gc_all)

    ur = lax.broadcasted_iota(jnp.int32, (GDN_UNIT, GDN_UNIT), 0)
    uc = lax.broadcasted_iota(jnp.int32, (GDN_UNIT, GDN_UNIT), 1)
    same = (ur // CHUNK) == (uc // CHUNK)
    causal = same & (uc <= ur)
    strict = same & (uc < ur)
    eye = jnp.where(ur == uc, 1.0, 0.0)
    first_rows = ur < CHUNK
    first_cols = uc < CHUNK
    n_units = tb // GDN_UNIT
    units = [(h, u) for h in range(GDN_HEADS) for u in range(n_units)]

    def lane_bcast(x, lane):
        return jnp.broadcast_to(x[:, lane:lane + 1], (GDN_UNIT, GDN_UNIT))

    def row_bcast(x, r):
        return jnp.broadcast_to(x[r:r + 1, :], (GDN_UNIT, GDN_UNIT))

    def rows_of(u):
        return slice(u * GDN_UNIT, (u + 1) * GDN_UNIT)

    def cols_of(h):
        return slice(h * LANES, (h + 1) * LANES)

    qs = [q_scr[h, rows_of(u), :] for h, u in units]
    ks = [k_scr[h, rows_of(u), :] for h, u in units]
    vs = [v_scr[h, rows_of(u), :] for h, u in units]
    betas = [lane_bcast(beta_all[rows_of(u)], h) for h, u in units]
    gcb = [lane_bcast(gc_all[rows_of(u)], GDN_HEADS + h) for h, u in units]
    gcr = [row_bcast(gct_all[:, rows_of(u)], GDN_HEADS + h) for h, u in units]
    decay = [jnp.exp(jnp.where(causal, a - b, NEG_INF)) for a, b in zip(gcb, gcr)]
    eg = [jnp.exp(a) for a in gcb]
    g_last = [jnp.where(first_rows, row_bcast(a, CHUNK - 1), row_bcast(a, GDN_UNIT - 1)) for a in gcb]
    kb = [k * b for k, b in zip(ks, betas)]
    vb = [v * b for v, b in zip(vs, betas)]
    a_low = [jnp.where(strict, _dot_nt(x, k) * d, 0.0) for x, k, d in zip(kb, ks, decay)]
    tinv = [eye - a for a in a_low]
    pw = a_low
    for _ in range(5):
        pw = [_dot(x, x) for x in pw]
        tinv = [t + _dot(t, x) for t, x in zip(tinv, pw)]
    uw = [_dot(t, jnp.concatenate([v, x * e], axis=1)) for t, v, x, e in zip(tinv, vb, kb, eg)]
    intra = [_dot_nt(q, k) * d for q, k, d in zip(qs, ks, decay)]
    iuw = [_dot(a, x) for a, x in zip(intra, uw)]
    k_dec_t = [jnp.transpose(k * jnp.exp(gl - a)) for k, gl, a in zip(ks, g_last, gcb)]
    wu = [jnp.concatenate([x[:, GDN_DIM:], x[:, :GDN_DIM]], axis=1) for x in uw]
    pn0 = [_dot(jnp.where(first_cols, kt, 0.0), x) for kt, x in zip(k_dec_t, wu)]
    pn1 = [_dot(jnp.where(first_cols, 0.0, kt), x) for kt, x in zip(k_dec_t, wu)]
    for n in range(len(units)):
        q_eff = (qs[n] * eg[n] - iuw[n][:, GDN_DIM:]).astype(BF16)
        for half, pn in enumerate((pn0[n], pn1[n])):
            hr = slice(half * CHUNK, (half + 1) * CHUNK)
            pq_scr[n, half, 0:GDN_DIM, :] = pn[:, :GDN_DIM].astype(BF16)
            pq_scr[n, half, GDN_DIM:GDN_DIM + CHUNK, :] = q_eff[hr]
            n_scr[n, half] = pn[:, GDN_DIM:]
            oc_scr[n, half] = iuw[n][hr, :GDN_DIM]
            gl_scr[n, half] = jnp.exp(gcb[n][(half + 1) * CHUNK - 1:(half + 1) * CHUNK, :])

    def chunk_body(c, carry):
        u = c // 2
        half = c % 2
        rows = pl.ds(pl.multiple_of(c * CHUNK, CHUNK), CHUNK)
        states = [state_scr[h] for h in range(GDN_HEADS)]
        res = [jnp.dot(pq_scr[h * n_units + u, half], states[h].astype(BF16), preferred_element_type=F32)
               for h in range(GDN_HEADS)]
        for h in range(GDN_HEADS):
            n = h * n_units + u
            state_scr[h] = gl_scr[n, half] * states[h] - res[h][0:GDN_DIM] + n_scr[n, half]
            oraw_scr[h, rows, :] = res[h][GDN_DIM:GDN_DIM + CHUNK] + oc_scr[n, half]
        return carry

    lax.fori_loop(0, tb // CHUNK, chunk_body, 0)

    for h in range(GDN_HEADS):
        o = oraw_scr[h]
        ms = jnp.mean(o * o, axis=-1, keepdims=True)
        y = o * lax.rsqrt(ms + EPS) * ngain_ref[...] * _silu(gate_ref[0, :, cols_of(h)].astype(F32))
        o_ref[0, :, cols_of(h)] = y.astype(o_ref.dtype)


def _gdn(proj, small, conv_w, alog_pad, dtb_pad, norm_gain, tb):
    b, s, _ = proj.shape
    nb = s // tb
    w3 = 3 * GDN_WIDTH
    assert tb % GDN_UNIT == 0
    n_hu = GDN_HEADS * (tb // GDN_UNIT)
    return pl.pallas_call(
        functools.partial(_gdn_kernel, tb=tb),
        out_shape=jax.ShapeDtypeStruct((b, s, GDN_WIDTH), BF16),
        grid=(b, nb),
        in_specs=[
            pl.BlockSpec((1, BF16_SUBLANES, w3),
                         lambda bi, i: (bi, jnp.maximum(i * (tb // BF16_SUBLANES) - 1, 0), PROJ_GDN // w3)),
            pl.BlockSpec((1, tb, w3), lambda bi, i: (bi, i, PROJ_GDN // w3)),
            pl.BlockSpec((1, tb, GDN_WIDTH), lambda bi, i: (bi, i, PROJ_GATE // GDN_WIDTH)),
            pl.BlockSpec((1, tb, LANES), lambda bi, i: (bi, i, 0)),
            pl.BlockSpec((CONV_K, w3), lambda bi, i: (0, 0)),
            pl.BlockSpec((1, LANES), lambda bi, i: (0, 0)),
            pl.BlockSpec((1, LANES), lambda bi, i: (0, 0)),
            pl.BlockSpec((1, GDN_DIM), lambda bi, i: (0, 0)),
        ],
        out_specs=pl.BlockSpec((1, tb, GDN_WIDTH), lambda bi, i: (bi, i, 0)),
        scratch_shapes=[
            pltpu.VMEM((w3 // LANES, tb + SUBLANES, LANES), F32),
            pltpu.VMEM((GDN_HEADS, tb, GDN_DIM), F32),
            pltpu.VMEM((GDN_HEADS, tb, GDN_DIM), F32),
            pltpu.VMEM((GDN_HEADS, tb, GDN_DIM), F32),
            pltpu.VMEM((n_hu, 2, GDN_DIM + CHUNK, GDN_DIM), BF16),
            pltpu.VMEM((n_hu, 2, GDN_DIM, GDN_DIM), F32),
            pltpu.VMEM((n_hu, 2, CHUNK, GDN_DIM), F32),
            pltpu.VMEM((n_hu, 2, 1, GDN_DIM), F32),
            pltpu.VMEM((GDN_HEADS, tb, GDN_DIM), F32),
            pltpu.VMEM((GDN_HEADS, GDN_DIM, GDN_DIM), F32),
        ],
        compiler_params=_params("parallel", "arbitrary"),
        name="gdn",
    )(proj, proj, proj, small, conv_w, alog_pad, dtb_pad, norm_gain)


MERGE_SUB = 256

ROW_PIECES = D_MODEL // LANES


def _store_tile_rows(ref, r0, value):
    n = value.shape[0]
    for j in range(ROW_PIECES):
        ref[pl.ds(r0 * ROW_PIECES + j, n, stride=ROW_PIECES), :] = value[:, j * LANES:(j + 1) * LANES]


def _load_tile_rows(ref, r0, n):
    return jnp.concatenate(
        [ref[pl.ds(r0 * ROW_PIECES + j, n, stride=ROW_PIECES), :] for j in range(ROW_PIECES)], axis=1)


def _merge_kernel(x_ref, ya_ref, yb_ref, ma_ref, mb_ref, wa_ref, wb_ref, wo_ref, g2_ref, wr_ref, br_ref,
                  x1_ref, h2_ref, route_ref, cnt_ref):
    tm = x_ref.shape[0]
    subs = [slice(r, r + MERGE_SUB) for r in range(0, tm, MERGE_SUB)]
    pa = [jnp.dot(ya_ref[rs, :], wa_ref[...], preferred_element_type=F32) for rs in subs]
    pb = [jnp.dot(yb_ref[rs, :], wb_ref[...], preferred_element_type=F32) for rs in subs]
    merged = [(_sigmoid(ma_ref[rs, :].astype(F32)) * a + _sigmoid(mb_ref[rs, :].astype(F32)) * b).astype(BF16)
              for rs, a, b in zip(subs, pa, pb)]
    x1s = [x_ref[rs, :] + jnp.dot(m, wo_ref[...], preferred_element_type=F32) for rs, m in zip(subs, merged)]
    h2s = [v * lax.rsqrt(jnp.mean(v * v, axis=-1, keepdims=True) + EPS) * g2_ref[...] for v in x1s]
    w_parts = _split_bf16(wr_ref[...])
    logit_s = [_dot_bf16x3(_split_bf16(v), w_parts) for v in h2s]
    for rs, v, hh in zip(subs, x1s, h2s):
        x1_ref[rs, :] = v
        h2_ref[rs, :] = hh
    logits = jnp.concatenate(logit_s, axis=0) + br_ref[...]
    lane = lax.broadcasted_iota(jnp.int32, logits.shape, 1)
    big = jnp.int32(4 * LANES)
    gl = jnp.where(lane < MOE_GROUPS, logits, NEG_INF)
    gmax = jnp.max(gl, axis=-1, keepdims=True)
    gidx = jnp.min(jnp.where(gl == gmax, lane, big), axis=-1, keepdims=True)
    grp_w = 1.0 / jnp.sum(jnp.exp(gl - gmax), axis=-1, keepdims=True)
    lo = MOE_GROUPS + gidx * MOE_EXPERTS_PER_GROUP
    el = jnp.where((lane >= lo) & (lane < lo + MOE_EXPERTS_PER_GROUP), logits, NEG_INF)
    v1 = jnp.max(el, axis=-1, keepdims=True)
    i1 = jnp.min(jnp.where(el == v1, lane, big), axis=-1, keepdims=True)
    el2 = jnp.where(lane == i1, NEG_INF, el)
    v2 = jnp.max(el2, axis=-1, keepdims=True)
    i2 = jnp.min(jnp.where(el2 == v2, lane, big), axis=-1, keepdims=True)
    e2 = jnp.exp(v2 - v1)
    w1 = grp_w / (1.0 + e2)
    w2 = w1 * e2
    id1 = i1 - MOE_GROUPS
    id2 = i2 - MOE_GROUPS
    route_ref[...] = jnp.where(lane == 0, id1.astype(F32), jnp.where(lane == 1, id2.astype(F32),
                               jnp.where(lane == 2, w1, jnp.where(lane == 3, w2, 0.0))))

    @pl.when(pl.program_id(0) == 0)
    def _():
        cnt_ref[...] = jnp.zeros(cnt_ref.shape, F32)

    hits = jnp.where((lane == id1) | (lane == id2), 1.0, 0.0)
    cnt_ref[...] += jnp.broadcast_to(jnp.sum(hits, axis=0, keepdims=True), cnt_ref.shape)


def _merge(x2, ya, yb, proj, wa, wb, wo, g2, wr, br, tm):
    t, d = x2.shape
    row = lambda i: (i, 0)
    const = lambda i: (0, 0)
    return pl.pallas_call(
        _merge_kernel,
        out_shape=(
            jax.ShapeDtypeStruct((t, d), F32),
            jax.ShapeDtypeStruct((t, d), F32),
            jax.ShapeDtypeStruct((t, LANES), F32),
            jax.ShapeDtypeStruct((SUBLANES, LANES), F32),
        ),
        grid=(t // tm,),
        in_specs=[
            pl.BlockSpec((tm, d), row),
            pl.BlockSpec((tm, DIFF_WIDTH), row),
            pl.BlockSpec((tm, GDN_WIDTH), row),
            pl.BlockSpec((tm, d), lambda i: (i, PROJ_MA // D_MODEL)),
            pl.BlockSpec((tm, d), lambda i: (i, PROJ_MB // D_MODEL)),
            pl.BlockSpec((DIFF_WIDTH, d), const),
            pl.BlockSpec((GDN_WIDTH, d), const),
            pl.BlockSpec((d, d), const),
            pl.BlockSpec((1, d), const),
            pl.BlockSpec((d, LANES), const),
            pl.BlockSpec((1, LANES), const),
        ],
        out_specs=(pl.BlockSpec((tm, d), row), pl.BlockSpec((tm, d), row), pl.BlockSpec((tm, LANES), row),
                   pl.BlockSpec((SUBLANES, LANES), const)),
        compiler_params=_params("arbitrary"),
        name="merge_router",
    )(x2, ya, yb, proj, proj, wa, wb, wo, g2, wr, br)


MOE_BLK = 512
MOE_SUB = 256
MOE_META_LANES = 256
MOE_ROUTE_TILE = 512
MOE_DMA_TILE = 256


def _route_kernel(route_ref, cnt_ref, pos_ref, meta_ref, run_scr, *, n_rows):
    i = pl.program_id(0)
    tp = route_ref.shape[0]

    @pl.when(i == 0)
    def _():
        cnt = cnt_ref[...]
        jr = lax.broadcasted_iota(jnp.int32, (LANES, LANES), 0)
        jc = lax.broadcasted_iota(jnp.int32, (LANES, LANES), 1)
        upper = jnp.where(jr < jc, 1.0, 0.0)
        hi_prec = dict(preferred_element_type=F32, precision=lax.Precision.HIGHEST)
        off = jnp.dot(cnt, upper, **hi_prec)
        run_scr[...] = off[0:1]
        blk = float(MOE_BLK)
        first_tile = jnp.floor(off / blk)
        last_tile = jnp.floor((off + cnt - 1.0) / blk)
        n_it = jnp.where(cnt > 0.0, last_tile - first_tile + 1.0, 0.0)
        it_start = jnp.dot(n_it, upper, **hi_prec)
        it_end = it_start + n_it
        lane8 = lax.broadcasted_iota(jnp.int32, cnt.shape, 1)
        e_max = jnp.max(jnp.where(cnt > 0.0, lane8, 0), axis=-1, keepdims=True).astype(F32)[0:1]
        sub8 = lax.broadcasted_iota(jnp.int32, cnt.shape, 0)
        table = jnp.where(sub8 == 0, first_tile, jnp.where(sub8 == 1, it_start, jnp.where(
            sub8 == 2, off, jnp.where(sub8 == 3, cnt, it_end))))
        cols = jnp.transpose(table)
        shape = (LANES, MOE_META_LANES)
        e_sub = lax.broadcasted_iota(jnp.int32, shape, 0)
        w_lane = lax.broadcasted_iota(jnp.int32, shape, 1).astype(F32)
        col = lambda k: jnp.broadcast_to(cols[:, k:k + 1], shape)
        e_w = jnp.sum(jnp.where((e_sub < MOE_EXPERTS) & (col(4) <= w_lane), 1.0, 0.0), axis=0, keepdims=True)
        valid = e_w < float(MOE_EXPERTS)
        e_w = jnp.minimum(e_w, e_max)
        sel = e_sub.astype(F32) == e_w
        pick = lambda k: jnp.sum(jnp.where(sel, col(k), 0.0), axis=0, keepdims=True)
        w_row = w_lane[0:1]
        tile_w = jnp.where(valid, pick(0) + (w_row - pick(1)), float(n_rows // MOE_BLK - 1))
        lo_w = jnp.maximum(pick(2) - tile_w * blk, 0.0)
        hi_w = jnp.minimum(pick(2) + pick(3) - tile_w * blk, blk)
        lo_w = jnp.where(valid, lo_w, 0.0)
        hi_w = jnp.where(valid, hi_w, 0.0)
        sub_m = lax.broadcasted_iota(jnp.int32, meta_ref.shape, 0)
        bc = lambda v: jnp.broadcast_to(v, meta_ref.shape)
        meta_ref[...] = jnp.where(sub_m == 0, bc(e_w), jnp.where(sub_m == 1, bc(tile_w), jnp.where(
            sub_m == 2, bc(lo_w), bc(hi_w)))).astype(jnp.int32)

    r = route_ref[...]
    lane = lax.broadcasted_iota(jnp.int32, r.shape, 1)
    lane_f = lane.astype(F32)
    oh1 = lane_f == r[:, 0:1]
    oh2 = lane_f == r[:, 1:2]
    hits = jnp.where(oh1 | oh2, 1.0, 0.0)
    tr = lax.broadcasted_iota(jnp.int32, (tp, tp), 0)
    tc = lax.broadcasted_iota(jnp.int32, (tp, tp), 1)
    earlier = jnp.where(tc < tr, 1.0, 0.0).astype(BF16)
    rank = jnp.dot(earlier, hits.astype(BF16), preferred_element_type=F32)
    base = run_scr[...] + rank
    p1 = jnp.sum(jnp.where(oh1, base, 0.0), axis=-1, keepdims=True)
    p2 = jnp.sum(jnp.where(oh2, base, 0.0), axis=-1, keepdims=True)
    pos_ref[...] = jnp.where(lane == 0, p1, jnp.where(lane == 1, p2, 0.0)).astype(jnp.int32)
    run_scr[...] += jnp.sum(hits, axis=0, keepdims=True)


def _route_positions(route, cnt):
    t = route.shape[0]
    tp = min(MOE_ROUTE_TILE, t)
    n_rows = 2 * t
    assert n_rows % MOE_BLK == 0 and n_rows // MOE_BLK + MOE_EXPERTS <= MOE_META_LANES and n_rows < 2 ** 24
    return pl.pallas_call(
        functools.partial(_route_kernel, n_rows=n_rows),
        out_shape=(jax.ShapeDtypeStruct((t, LANES), jnp.int32),
                   jax.ShapeDtypeStruct((SUBLANES, MOE_META_LANES), jnp.int32)),
        grid=(t // tp,),
        in_specs=[pl.BlockSpec((tp, LANES), lambda i: (i, 0)),
                  pl.BlockSpec((SUBLANES, LANES), lambda i: (0, 0))],
        out_specs=(pl.BlockSpec((tp, LANES), lambda i: (i, 0)),
                   pl.BlockSpec((SUBLANES, MOE_META_LANES), lambda i: (0, 0))),
        scratch_shapes=[pltpu.VMEM((1, LANES), F32)],
        compiler_params=_params("arbitrary"),
        name="route_positions",
    )(route, cnt)


def _scatter_kernel(pos_ref, h2_ref, xs_ref, sem):
    ts = h2_ref.shape[0]

    def body(tok, carry):
        src = h2_ref.at[pl.ds(tok, 1), :]
        for k in range(2):
            pltpu.make_async_copy(src, xs_ref.at[pl.ds(pos_ref[0, 0, 2 * tok + k], 1), :], sem).start(priority=k)
        return carry

    lax.fori_loop(0, ts, body, 0, unroll=8)
    for _ in range(2):
        pltpu.make_async_copy(h2_ref, xs_ref.at[pl.ds(0, ts), :], sem).wait()


def _scatter_rows(pos3, h2):
    t, d = h2.shape
    ts = pos3.shape[2] // 2
    return pl.pallas_call(
        _scatter_kernel,
        out_shape=jax.ShapeDtypeStruct((2 * t, d), F32),
        grid=(t // ts,),
        in_specs=[pl.BlockSpec((1, 1, 2 * ts), lambda i: (i, 0, 0), memory_space=pltpu.SMEM),
                  pl.BlockSpec((ts, d), lambda i: (i, 0))],
        out_specs=pl.BlockSpec(memory_space=pl.ANY),
        scratch_shapes=[pltpu.SemaphoreType.DMA],
        compiler_params=pltpu.CompilerParams(dimension_semantics=("arbitrary",), vmem_limit_bytes=VMEM_LIMIT,
                                             disable_bounds_checks=True),
        name="scatter_rows",
    )(pos3, h2)


def _expert_kernel(ie_ref, it_ref, lo_ref, hi_ref, xs_ref, wg_ref, wu_ref, wd_ref, y_ref,
                   wg_scr, wu_scr, wd_scr, acc_scr):
    w = pl.program_id(0)
    n_items = pl.num_programs(0)
    prev = jnp.maximum(w - 1, 0)
    nxt = jnp.minimum(w + 1, n_items - 1)

    @pl.when((w == 0) | (ie_ref[w] != ie_ref[prev]))
    def _():
        wg_scr[...] = wg_ref[0].astype(BF16)
        wu_scr[...] = wu_ref[0].astype(BF16)
        wd_scr[...] = wd_ref[0].astype(BF16)

    @pl.when((w == 0) | (it_ref[w] != it_ref[prev]))
    def _():
        acc_scr[...] = jnp.zeros(acc_scr.shape, F32)

    lo = lo_ref[w]
    hi = hi_ref[w]
    subs = [slice(r, r + MOE_SUB) for r in range(0, acc_scr.shape[0], MOE_SUB)]

    @pl.when(hi > lo)
    def _():
        xs = [xs_ref[rs, :].astype(BF16) for rs in subs]
        hg = [jnp.dot(x, wg_scr[...], preferred_element_type=F32) for x in xs]
        hu = [jnp.dot(x, wu_scr[...], preferred_element_type=F32) for x in xs]
        act = [(_silu(g) * u).astype(BF16) for g, u in zip(hg, hu)]
        yp = [jnp.dot(a, wd_scr[...], preferred_element_type=F32) for a in act]
        for rs, v in zip(subs, yp):
            row = rs.start + lax.broadcasted_iota(jnp.int32, v.shape, 0)
            acc_scr[rs, :] += jnp.where((row >= lo) & (row < hi), v, 0.0)

    @pl.when((w == n_items - 1) | (it_ref[nxt] != it_ref[w]))
    def _():
        for rs in subs:
            _store_tile_rows(y_ref, rs.start, acc_scr[rs, :])


def _experts(meta, xs, wg, wu, wd):
    n_rows, d = xs.shape
    n_items = n_rows // MOE_BLK + MOE_EXPERTS
    ie, it, lo, hi = (meta[k, :n_items] for k in range(4))
    return pl.pallas_call(
        _expert_kernel,
        out_shape=jax.ShapeDtypeStruct((n_rows * ROW_PIECES, LANES), F32),
        grid_spec=pltpu.PrefetchScalarGridSpec(
            num_scalar_prefetch=4,
            grid=(n_items,),
            in_specs=[
                pl.BlockSpec((MOE_BLK, d), lambda w, ie, it, lo, hi: (it[w], 0)),
                pl.BlockSpec((1, d, MOE_HIDDEN), lambda w, ie, it, lo, hi: (ie[w], 0, 0)),
                pl.BlockSpec((1, d, MOE_HIDDEN), lambda w, ie, it, lo, hi: (ie[w], 0, 0)),
                pl.BlockSpec((1, MOE_HIDDEN, d), lambda w, ie, it, lo, hi: (ie[w], 0, 0)),
            ],
            out_specs=pl.BlockSpec((MOE_BLK * ROW_PIECES, LANES), lambda w, ie, it, lo, hi: (it[w], 0)),
            scratch_shapes=[pltpu.VMEM((d, MOE_HIDDEN), BF16), pltpu.VMEM((d, MOE_HIDDEN), BF16),
                            pltpu.VMEM((MOE_HIDDEN, d), BF16), pltpu.VMEM((MOE_BLK, d), F32)],
        ),
        compiler_params=_params("arbitrary"),
        name="experts",
    )(ie, it, lo, hi, xs, wg, wu, wd)


def _combine_kernel(pos_ref, posn_ref, x1_ref, route_ref, gf_ref, y_ref, o_ref, ybuf, sem):
    i = pl.program_id(0)
    n = pl.num_programs(0)
    tc = x1_ref.shape[0]
    slot = i % 2

    def issue(p_ref, s):
        def body(tok, carry):
            for k in range(2):
                src = y_ref.at[pl.ds(pl.multiple_of(p_ref[0, 0, 2 * tok + k], ROW_PIECES), ROW_PIECES), :]
                dst = ybuf.at[s, k, pl.ds(pl.multiple_of(tok * ROW_PIECES, ROW_PIECES), ROW_PIECES), :]
                pltpu.make_async_copy(src, dst, sem.at[s]).start(priority=k)
            return carry

        lax.fori_loop(0, tc, body, 0, unroll=8)

    @pl.when(i == 0)
    def _():
        issue(pos_ref, 0)

    @pl.when(i + 1 < n)
    def _():
        issue(posn_ref, 1 - slot)

    for k in range(2):
        pltpu.make_async_copy(y_ref.at[pl.ds(0, tc * ROW_PIECES), :], ybuf.at[slot, k], sem.at[slot]).wait()
    r = route_ref[...]
    x2 = (x1_ref[...] + r[:, 2:3] * _load_tile_rows(ybuf.at[slot, 0], 0, tc)
          + r[:, 3:4] * _load_tile_rows(ybuf.at[slot, 1], 0, tc))
    ms = jnp.mean(x2 * x2, axis=-1, keepdims=True)
    o_ref[...] = x2 * lax.rsqrt(ms + EPS) * gf_ref[...]


def _combine(pos3, x1, route, gf, y):
    t, d = x1.shape
    tc = pos3.shape[2] // 2
    n = t // tc
    return pl.pallas_call(
        _combine_kernel,
        out_shape=jax.ShapeDtypeStruct((t, d), F32),
        grid=(n,),
        in_specs=[
            pl.BlockSpec((1, 1, 2 * tc), lambda i: (i, 0, 0), memory_space=pltpu.SMEM),
            pl.BlockSpec((1, 1, 2 * tc), lambda i: (jnp.minimum(i + 1, n - 1), 0, 0), memory_space=pltpu.SMEM),
            pl.BlockSpec((tc, d), lambda i: (i, 0)),
            pl.BlockSpec((tc, LANES), lambda i: (i, 0)),
            pl.BlockSpec((1, d), lambda i: (0, 0)),
            pl.BlockSpec(memory_space=pl.ANY),
        ],
        out_specs=pl.BlockSpec((tc, d), lambda i: (i, 0)),
        scratch_shapes=[pltpu.VMEM((2, 2, tc * ROW_PIECES, LANES), F32), pltpu.SemaphoreType.DMA((2,))],
        compiler_params=pltpu.CompilerParams(dimension_semantics=("arbitrary",), vmem_limit_bytes=VMEM_LIMIT,
                                             disable_bounds_checks=True),
        name="combine_norm",
    )(pos3, pos3, x1, route, gf, y)


def _pad_lanes(v, offset):
    return jnp.zeros((1, LANES), F32).at[0, offset:offset + v.shape[0]].set(v.astype(F32))


def kernel(x, norm_mix_gain, w_in, diff_lambda_q1, diff_lambda_k1, diff_lambda_q2, diff_lambda_k2, diff_subln_gain, gdn_conv_w, gdn_a_log, gdn_dt_bias, gdn_norm_gain, w_branch_attn, w_branch_gdn, w_out, norm_ffn_gain, moe_w_group, moe_b_group, moe_w_expert, moe_b_expert, moe_w_gate, moe_w_up, moe_w_down, norm_final_gain):
    b, s, d = x.shape
    t = b * s
    x2 = x.reshape(t, d)

    w = w_in[0]
    small_lo = PROJ_GDN + 3 * GDN_WIDTH
    small_hi = small_lo + 2 * GDN_HEADS
    gate_hi = small_hi + GDN_WIDTH
    w_main = jnp.concatenate([w[:, :small_lo], w[:, gate_hi:], w[:, small_hi:gate_hi]], axis=1).astype(BF16)
    w_small = jnp.concatenate([w[:, small_lo:small_hi], jnp.zeros((d, LANES - 2 * GDN_HEADS), w.dtype)],
                              axis=1).astype(BF16)
    gain1 = norm_mix_gain[0].reshape(1, d)

    proj2, small2 = _inproj(x2, gain1, w_main, w_small, min(1024, t), PROJ_WIDTH // 2)
    proj = proj2.reshape(b, s, PROJ_WIDTH)

    lam_params = jnp.stack([diff_lambda_q1[0], diff_lambda_k1[0], diff_lambda_q2[0], diff_lambda_k2[0]]).astype(F32)
    ya = _diff_attention(lam_params, proj, diff_subln_gain[0].reshape(1, DIFF_V_DIM))

    yb = _gdn(proj, small2.reshape(b, s, LANES), gdn_conv_w[0], _pad_lanes(gdn_a_log[0], GDN_HEADS),
              _pad_lanes(gdn_dt_bias[0], GDN_HEADS), gdn_norm_gain[0].reshape(1, GDN_DIM), tb=min(512, s))

    wr = jnp.concatenate([moe_w_group[0], moe_w_expert[0],
                          jnp.zeros((d, LANES - MOE_GROUPS - MOE_EXPERTS), F32)], axis=1)
    br = _pad_lanes(jnp.concatenate([moe_b_group[0], moe_b_expert[0]]), 0)
    x1, h2, route, cnt = _merge(x2, ya.reshape(t, DIFF_WIDTH), yb.reshape(t, GDN_WIDTH), proj2,
                                w_branch_attn[0].astype(BF16), w_branch_gdn[0].astype(BF16),
                                w_out[0].astype(BF16), norm_ffn_gain[0].reshape(1, d), wr, br, tm=min(512, t))

    pos, meta = _route_positions(route, cnt)
    ts = min(MOE_DMA_TILE, t)
    pos3 = pos[:, :2].reshape(t // ts, 1, 2 * ts)
    xs = _scatter_rows(pos3, h2)
    y = _experts(meta, xs, moe_w_gate[0], moe_w_up[0], moe_w_down[0])
    out = _combine(pos3 * ROW_PIECES, x1, route, norm_final_gain.reshape(1, d), y)
    return out.reshape(b, s, d)
```

```python
import functools
import math

import jax
import jax.numpy as jnp
from jax import lax
from jax.experimental import pallas as pl
from jax.experimental.pallas import tpu as pltpu

F32 = jnp.float32
BF16 = jnp.bfloat16

D_MODEL = 1024
CHUNK = 64
EPS = 1e-6

DIFF_HEADS = 4
DIFF_HEAD_DIM = 64
DIFF_V_DIM = 2 * DIFF_HEAD_DIM
DIFF_WIDTH = DIFF_HEADS * DIFF_V_DIM

GDN_HEADS = 4
GDN_DIM = 128
GDN_WIDTH = GDN_HEADS * GDN_DIM
CONV_K = 4

MOE_GROUPS = 4
MOE_EXPERTS_PER_GROUP = 8
MOE_EXPERTS = MOE_GROUPS * MOE_EXPERTS_PER_GROUP
MOE_HIDDEN = 256

LANES = 128
SUBLANES = 8
BF16_SUBLANES = 16
VMEM_LIMIT = 56 * 1024 * 1024

LAMBDA_INIT = 0.8 - 0.6 * math.exp(-0.3 * 0)

PROJ_ATTN = 0
PROJ_GDN = 3 * DIFF_WIDTH
PROJ_MA = PROJ_GDN + 3 * GDN_WIDTH
PROJ_MB = PROJ_MA + D_MODEL
PROJ_GATE = PROJ_MB + D_MODEL
PROJ_WIDTH = PROJ_GATE + GDN_WIDTH

NEG_INF = float("-inf")
LOG2_E = math.log2(math.e)


def _params(*sem):
    return pltpu.CompilerParams(dimension_semantics=sem, vmem_limit_bytes=VMEM_LIMIT)


def _inproj_kernel(x_ref, g_ref, w_ref, ws_ref, o_ref, os_ref, h_scr):
    @pl.when(pl.program_id(1) == 0)
    def _():
        x = x_ref[...]
        ms = jnp.mean(x * x, axis=-1, keepdims=True)
        h_scr[...] = (x * lax.rsqrt(ms + EPS) * g_ref[...]).astype(BF16)
        os_ref[...] = jnp.dot(h_scr[...], ws_ref[...], preferred_element_type=F32)

    o_ref[...] = jnp.dot(h_scr[...], w_ref[...], preferred_element_type=F32).astype(o_ref.dtype)


def _inproj(x2, gain, w, w_small, tm, tn):
    t, d = x2.shape
    n = w.shape[1]
    return pl.pallas_call(
        _inproj_kernel,
        out_shape=(jax.ShapeDtypeStruct((t, n), BF16), jax.ShapeDtypeStruct((t, LANES), F32)),
        grid=(t // tm, n // tn),
        in_specs=[
            pl.BlockSpec((tm, d), lambda i, j: (i, 0)),
            pl.BlockSpec((1, d), lambda i, j: (0, 0)),
            pl.BlockSpec((d, tn), lambda i, j: (0, j)),
            pl.BlockSpec((d, LANES), lambda i, j: (0, 0)),
        ],
        out_specs=(pl.BlockSpec((tm, tn), lambda i, j: (i, j)), pl.BlockSpec((tm, LANES), lambda i, j: (i, 0))),
        scratch_shapes=[pltpu.VMEM((tm, d), BF16)],
        compiler_params=_params("parallel", "arbitrary"),
        name="inproj",
    )(x2, gain, w, w_small)


ATTN_TQ = 1024
ATTN_SPAN = 2
MASK_BIAS = -1e30
ATTN_V_ROWS = DIFF_V_DIM + BF16_SUBLANES
ATTN_COLS = 128


def _attn_kernel(lam_ref, q_ref, k_ref, v_ref, gain_ref, o_ref,
                 qt_scr, oh_scr, vt_scr, s0_scr, s1_scr, p0_scr, p1_scr, a0_scr, a1_scr, m_scr, acc_scr, *, tq, tk):
    i = pl.program_id(2)
    n_q = 2 * tq
    span = ATTN_SPAN
    n_slots = tk // CHUNK

    n_blk = n_q // ATTN_COLS
    half = tq // ATTN_COLS

    @pl.when(i == 0)
    def _():
        for c in range(v_ref.shape[1] // tk):
            v_t = jnp.transpose(v_ref[0, c * tk:(c + 1) * tk, :].astype(F32)).astype(BF16)
            for j in range(tk // LANES):
                vt_scr[c, j] = jnp.concatenate(
                    [v_t[:, j * LANES:(j + 1) * LANES], jnp.ones((BF16_SUBLANES, LANES), BF16)], axis=0)
        slot = lax.broadcasted_iota(jnp.int32, (LANES, ATTN_COLS), 0)
        lane = lax.broadcasted_iota(jnp.int32, (LANES, ATTN_COLS), 1)
        for ver in range(span + 1):
            for c in range(n_blk):
                rel = (lane + (c % half) * ATTN_COLS) // CHUNK - (ver - 1) * n_slots
                if ver == 0:
                    rel = rel * 0 + 2 * n_slots
                want = jnp.where(rel < 0, n_slots, jnp.where(rel < n_slots, rel, -1))
                oh_scr[ver, c] = jnp.where(slot == want, 1.0, 0.0).astype(BF16)

    q_t = jnp.transpose(q_ref[0].astype(F32) * (DIFF_HEAD_DIM ** -0.5 * LOG2_E))
    dim = lax.broadcasted_iota(jnp.int32, (LANES, tq), 0)
    q_comp = (jnp.where(dim < DIFF_HEAD_DIM, q_t, 0.0).astype(BF16),
              jnp.where(dim >= DIFF_HEAD_DIM, q_t, 0.0).astype(BF16))
    for c in range(n_blk):
        qt_scr[c] = q_comp[c // half][:, (c % half) * ATTN_COLS:(c % half + 1) * ATTN_COLS]

    k_lane = lax.broadcasted_iota(jnp.int32, (tk, LANES), 1)
    k_chunk = lax.broadcasted_iota(jnp.int32, (tk, LANES), 0) // CHUNK
    bias = jnp.where((k_lane <= n_slots) & ((k_chunk > k_lane) | (k_lane == n_slots)), MASK_BIAS, 0.0).astype(BF16)

    m_scr[...] = jnp.full(m_scr.shape, NEG_INF, F32)
    acc_scr[...] = jnp.zeros(acc_scr.shape, F32)
    p1_scr[...] = jnp.zeros(p1_scr.shape, BF16)
    a1_scr[...] = jnp.ones(a1_scr.shape, F32)

    all_blocks = tuple(range(n_blk))
    late_blocks = tuple(c for c in all_blocks if (c % half) * ATTN_COLS >= tk)

    def scores(t, s_scr, blocks=all_blocks):
        ver = jnp.clip(t - span * i + 1, 0, span)
        rows = pl.ds(pl.multiple_of(t * tk, tk), tk)
        k_aug = jnp.concatenate([k_ref[0, rows, :], bias], axis=1)
        q_all = jnp.concatenate(
            [jnp.concatenate([qt_scr[c], oh_scr[ver, c]], axis=0) for c in blocks], axis=1)
        s = jnp.dot(k_aug, q_all, preferred_element_type=F32)
        for n, c in enumerate(blocks):
            s_scr[c] = s[:, n * ATTN_COLS:(n + 1) * ATTN_COLS]

    def softmax(s_scr, p_scr, a_scr, blocks=all_blocks):
        for c in blocks:
            cs = slice(c * ATTN_COLS, (c + 1) * ATTN_COLS)
            s = s_scr[c]
            m_prev = m_scr[:, cs]
            m_new = jnp.maximum(m_prev, jnp.max(s, axis=0, keepdims=True))
            a_scr[:, cs] = jnp.exp2(m_prev - m_new)
            m_scr[:, cs] = m_new
            p_scr[c] = jnp.exp2(s - m_new).astype(BF16)

    def values(t, p_scr, a_scr, blocks=all_blocks):
        v_all = jnp.concatenate([vt_scr[t, j] for j in range(tk // LANES)], axis=1)
        p = jnp.concatenate([p_scr[c] for c in blocks], axis=1)
        pv = jnp.dot(v_all, p, preferred_element_type=F32)
        for n, c in enumerate(blocks):
            cs = slice(c * ATTN_COLS, (c + 1) * ATTN_COLS)
            acc_scr[c] = a_scr[:, cs] * acc_scr[c] + pv[:, n * ATTN_COLS:(n + 1) * ATTN_COLS]

    scores(0, s0_scr)

    def pair_step(g, carry):
        t0 = 2 * g
        scores(t0 + 1, s1_scr)
        softmax(s0_scr, p0_scr, a0_scr)
        values(jnp.maximum(t0 - 1, 0), p1_scr, a1_scr)
        scores(t0 + 2, s0_scr)
        softmax(s1_scr, p1_scr, a1_scr)
        values(t0, p0_scr, a0_scr)
        return carry

    lax.fori_loop(0, i, pair_step, 0)
    t0 = span * i
    scores(t0 + 1, s1_scr, late_blocks)
    softmax(s0_scr, p0_scr, a0_scr)
    values(jnp.maximum(t0 - 1, 0), p1_scr, a1_scr)
    softmax(s1_scr, p1_scr, a1_scr, late_blocks)
    values(t0, p0_scr, a0_scr)
    values(t0 + 1, p1_scr, a1_scr, late_blocks)

    lp = lam_ref[...]
    lam = (jnp.exp(jnp.sum(lp[0:1] * lp[1:2], axis=-1, keepdims=True))
           - jnp.exp(jnp.sum(lp[2:3] * lp[3:4], axis=-1, keepdims=True)) + LAMBDA_INIT)
    for c in range(half):
        o0_t = acc_scr[c, 0:DIFF_V_DIM, :] / acc_scr[c, DIFF_V_DIM:DIFF_V_DIM + 1, :]
        o1_t = acc_scr[half + c, 0:DIFF_V_DIM, :] / acc_scr[half + c, DIFF_V_DIM:DIFF_V_DIM + 1, :]
        o = jnp.transpose(o0_t - lam * o1_t)
        ms = jnp.mean(o * o, axis=-1, keepdims=True)
        y = (o * lax.rsqrt(ms + EPS) * gain_ref[...]) * (1.0 - LAMBDA_INIT)
        o_ref[0, c * ATTN_COLS:(c + 1) * ATTN_COLS, :] = y.astype(o_ref.dtype)


def _diff_attention(lam_params, qkv, subln_gain):
    b, s, _ = qkv.shape
    tq = min(ATTN_TQ, s)
    tk = tq // ATTN_SPAN
    assert s % tq == 0 and ATTN_SPAN == 2 and tk % LANES == 0 and tk // CHUNK < LANES
    n_q = 2 * tq
    return pl.pallas_call(
        functools.partial(_attn_kernel, tq=tq, tk=tk),
        out_shape=jax.ShapeDtypeStruct((b, s, DIFF_WIDTH), BF16),
        grid=(b, DIFF_HEADS, s // tq),
        in_specs=[
            pl.BlockSpec((4, DIFF_HEAD_DIM), lambda bi, h, i: (0, 0)),
            pl.BlockSpec((1, tq, DIFF_V_DIM), lambda bi, h, i: (bi, i, h)),
            pl.BlockSpec((1, s, DIFF_V_DIM), lambda bi, h, i: (bi, 0, DIFF_HEADS + h)),
            pl.BlockSpec((1, s, DIFF_V_DIM), lambda bi, h, i: (bi, 0, 2 * DIFF_HEADS + h)),
            pl.BlockSpec((1, DIFF_V_DIM), lambda bi, h, i: (0, 0)),
        ],
        out_specs=pl.BlockSpec((1, tq, DIFF_V_DIM), lambda bi, h, i: (bi, i, h)),
        scratch_shapes=[
            pltpu.VMEM((n_q // ATTN_COLS, LANES, ATTN_COLS), BF16),
            pltpu.VMEM((ATTN_SPAN + 1, n_q // ATTN_COLS, LANES, ATTN_COLS), BF16),
            pltpu.VMEM((s // tk, tk // LANES, ATTN_V_ROWS, LANES), BF16),
            pltpu.VMEM((n_q // ATTN_COLS, tk, ATTN_COLS), F32),
            pltpu.VMEM((n_q // ATTN_COLS, tk, ATTN_COLS), F32),
            pltpu.VMEM((n_q // ATTN_COLS, tk, ATTN_COLS), BF16),
            pltpu.VMEM((n_q // ATTN_COLS, tk, ATTN_COLS), BF16),
            pltpu.VMEM((1, n_q), F32),
            pltpu.VMEM((1, n_q), F32),
            pltpu.VMEM((1, n_q), F32),
            pltpu.VMEM((n_q // ATTN_COLS, ATTN_V_ROWS, ATTN_COLS), F32),
        ],
        compiler_params=_params("parallel", "parallel", "arbitrary"),
        name="diff_attn",
    )(lam_params, qkv, qkv, qkv, subln_gain)


def _silu(x):
    return x * (1.0 / (1.0 + jnp.exp(-x)))


def _sigmoid(x):
    return 1.0 / (1.0 + jnp.exp(-x))


def _softplus(x):
    return jnp.maximum(x, 0.0) + jnp.log(1.0 + jnp.exp(-jnp.abs(x)))


def _split_bf16(x):
    hi = x.astype(BF16)
    return hi, (x - hi.astype(F32)).astype(BF16)


def _dot_bf16x3(a_parts, b_parts):
    (a_hi, a_lo), (b_hi, b_lo) = a_parts, b_parts
    d = lambda x, y: jnp.dot(x, y, preferred_element_type=F32)
    return d(a_hi, b_hi) + (d(a_lo, b_hi) + d(a_hi, b_lo))


def _dot_exact_lhs(lhs, x):
    x1 = x.astype(BF16)
    r1 = x - x1.astype(F32)
    x2 = r1.astype(BF16)
    x3 = (r1 - x2.astype(F32)).astype(BF16)
    d = lambda y: jnp.dot(lhs, y, preferred_element_type=F32)
    return d(x1) + (d(x2) + d(x3))


def _dot(a, b):
    return jnp.dot(a.astype(BF16), b.astype(BF16), preferred_element_type=F32)


def _dot_nt(a, b):
    return lax.dot_general(a.astype(BF16), b.astype(BF16), (((1,), (1,)), ((), ())), preferred_element_type=F32)


GDN_UNIT = 2 * CHUNK


def _gdn_kernel(prev_ref, qkv_ref, gate_ref, small_ref, convw_ref, alog_ref, dtb_ref, ngain_ref, o_ref,
                xp_scr, q_scr, k_scr, v_scr, pq_scr, n_scr, oc_scr, gl_scr, oraw_scr, state_scr, *, tb):
    i = pl.program_id(1)

    @pl.when(i == 0)
    def _():
        state_scr[...] = jnp.zeros(state_scr.shape, F32)

    n_sec = 3 * GDN_HEADS
    prev = jnp.where(i == 0, 0.0, prev_ref[0, BF16_SUBLANES - SUBLANES:BF16_SUBLANES, :].astype(F32))
    for sec in range(n_sec):
        xp_scr[sec, 0:SUBLANES, :] = prev[:, sec * LANES:(sec + 1) * LANES]
    for r0 in range(0, tb, GDN_UNIT):
        blk = qkv_ref[0, r0:r0 + GDN_UNIT, :].astype(F32)
        for sec in range(n_sec):
            xp_scr[sec, SUBLANES + r0:SUBLANES + r0 + GDN_UNIT, :] = blk[:, sec * LANES:(sec + 1) * LANES]
    for sec in range(n_sec):
        cols = slice(sec * LANES, (sec + 1) * LANES)
        acc = None
        for jj in range(CONV_K):
            start = SUBLANES - (CONV_K - 1) + jj
            term = xp_scr[sec, start:start + tb, :] * convw_ref[jj:jj + 1, cols]
            acc = term if acc is None else acc + term
        y = _silu(acc)
        which, head = divmod(sec, GDN_HEADS)
        if which == 0:
            q_scr[head] = y * lax.rsqrt(jnp.sum(y * y, axis=-1, keepdims=True) + EPS) * (GDN_DIM ** -0.5)
        elif which == 1:
            k_scr[head] = y * lax.rsqrt(jnp.sum(y * y, axis=-1, keepdims=True) + EPS)
        else:
            v_scr[head] = y

    sm = small_ref[0]
    beta_all = _sigmoid(sm)
    g_all = -jnp.exp(alog_ref[...]) * _softplus(sm + dtb_ref[...])

    br = lax.broadcasted_iota(jnp.int32, (tb, tb), 0)
    bc = lax.broadcasted_iota(jnp.int32, (tb, tb), 1)
    block_tril = jnp.where((br // CHUNK == bc // CHUNK) & (bc <= br), 1.0, 0.0).astype(BF16)
    gc_all = _dot_exact_lhs(block_tril, g_all)
    gct_all = jnp.transpose(gc_all)

    ur = lax.broadcasted_iota(jnp.int32, (GDN_UNIT, GDN_UNIT), 0)
    uc = lax.broadcasted_iota(jnp.int32, (GDN_UNIT, GDN_UNIT), 1)
    same = (ur // CHUNK) == (uc // CHUNK)
    causal = same & (uc <= ur)
    strict = same & (uc < ur)
    eye = jnp.where(ur == uc, 1.0, 0.0)
    first_rows = ur < CHUNK
    first_cols = uc < CHUNK
    n_units = tb // GDN_UNIT
    units = [(h, u) for h in range(GDN_HEADS) for u in range(n_units)]

    def lane_bcast(x, lane):
        return jnp.broadcast_to(x[:, lane:lane + 1], (GDN_UNIT, GDN_UNIT))

    def row_bcast(x, r):
        return jnp.broadcast_to(x[r:r + 1, :], (GDN_UNIT, GDN_UNIT))

    def rows_of(u):
        return slice(u * GDN_UNIT, (u + 1) * GDN_UNIT)

    def cols_of(h):
        return slice(h * LANES, (h + 1) * LANES)

    qs = [q_scr[h, rows_of(u), :] for h, u in units]
    ks = [k_scr[h, rows_of(u), :] for h, u in units]
    vs = [v_scr[h, rows_of(u), :] for h, u in units]
    betas = [lane_bcast(beta_all[rows_of(u)], h) for h, u in units]
    gcb = [lane_bcast(gc_all[rows_of(u)], GDN_HEADS + h) for h, u in units]
    gcr = [row_bcast(gct_all[:, rows_of(u)], GDN_HEADS + h) for h, u in units]
    decay = [jnp.exp(jnp.where(causal, a - b, NEG_INF)) for a, b in zip(gcb, gcr)]
    eg = [jnp.exp(a) for a in gcb]
    g_last = [jnp.where(first_rows, row_bcast(a, CHUNK - 1), row_bcast(a, GDN_UNIT - 1)) for a in gcb]
    kb = [k * b for k, b in zip(ks, betas)]
    vb = [v * b for v, b in zip(vs, betas)]
    a_low = [jnp.where(strict, _dot_nt(x, k) * d, 0.0) for x, k, d in zip(kb, ks, decay)]
    tinv = [eye - a for a in a_low]
    pw = a_low
    for _ in range(5):
        pw = [_dot(x, x) for x in pw]
        tinv = [t + _dot(t, x) for t, x in zip(tinv, pw)]
    uw = [_dot(t, jnp.concatenate([v, x * e], axis=1)) for t, v, x, e in zip(tinv, vb, kb, eg)]
    intra = [_dot_nt(q, k) * d for q, k, d in zip(qs, ks, decay)]
    iuw = [_dot(a, x) for a, x in zip(intra, uw)]
    k_dec_t = [jnp.transpose(k * jnp.exp(gl - a)) for k, gl, a in zip(ks, g_last, gcb)]
    wu = [jnp.concatenate([x[:, GDN_DIM:], x[:, :GDN_DIM]], axis=1) for x in uw]
    pn0 = [_dot(jnp.where(first_cols, kt, 0.0), x) for kt, x in zip(k_dec_t, wu)]
    pn1 = [_dot(jnp.where(first_cols, 0.0, kt), x) for kt, x in zip(k_dec_t, wu)]
    for n in range(len(units)):
        q_eff = (qs[n] * eg[n] - iuw[n][:, GDN_DIM:]).astype(BF16)
        for half, pn in enumerate((pn0[n], pn1[n])):
            hr = slice(half * CHUNK, (half + 1) * CHUNK)
            pq_scr[n, half, 0:GDN_DIM, :] = pn[:, :GDN_DIM].astype(BF16)
            pq_scr[n, half, GDN_DIM:GDN_DIM + CHUNK, :] = q_eff[hr]
            n_scr[n, half] = pn[:, GDN_DIM:]
            oc_scr[n, half] = iuw[n][hr, :GDN_DIM]
            gl_scr[n, half] = jnp.exp(gcb[n][(half + 1) * CHUNK - 1:(half + 1) * CHUNK, :])

    def chunk_body(c, carry):
        u = c // 2
        half = c % 2
        rows = pl.ds(pl.multiple_of(c * CHUNK, CHUNK), CHUNK)
        states = [state_scr[h] for h in range(GDN_HEADS)]
        res = [jnp.dot(pq_scr[h * n_units + u, half], states[h].astype(BF16), preferred_element_type=F32)
               for h in range(GDN_HEADS)]
        for h in range(GDN_HEADS):
            n = h * n_units + u
            state_scr[h] = gl_scr[n, half] * states[h] - res[h][0:GDN_DIM] + n_scr[n, half]
            oraw_scr[h, rows, :] = res[h][GDN_DIM:GDN_DIM + CHUNK] + oc_scr[n, half]
        return carry

    lax.fori_loop(0, tb // CHUNK, chunk_body, 0)

    for h in range(GDN_HEADS):
        o = oraw_scr[h]
        ms = jnp.mean(o * o, axis=-1, keepdims=True)
        y = o * lax.rsqrt(ms + EPS) * ngain_ref[...] * _silu(gate_ref[0, :, cols_of(h)].astype(F32))
        o_ref[0, :, cols_of(h)] = y.astype(o_ref.dtype)


def _gdn(proj, small, conv_w, alog_pad, dtb_pad, norm_gain, tb):
    b, s, _ = proj.shape
    nb = s // tb
    w3 = 3 * GDN_WIDTH
    assert tb % GDN_UNIT == 0
    n_hu = GDN_HEADS * (tb // GDN_UNIT)
    return pl.pallas_call(
        functools.partial(_gdn_kernel, tb=tb),
        out_shape=jax.ShapeDtypeStruct((b, s, GDN_WIDTH), BF16),
        grid=(b, nb),
        in_specs=[
            pl.BlockSpec((1, BF16_SUBLANES, w3),
                         lambda bi, i: (bi, jnp.maximum(i * (tb // BF16_SUBLANES) - 1, 0), PROJ_GDN // w3)),
            pl.BlockSpec((1, tb, w3), lambda bi, i: (bi, i, PROJ_GDN // w3)),
            pl.BlockSpec((1, tb, GDN_WIDTH), lambda bi, i: (bi, i, PROJ_GATE // GDN_WIDTH)),
            pl.BlockSpec((1, tb, LANES), lambda bi, i: (bi, i, 0)),
            pl.BlockSpec((CONV_K, w3), lambda bi, i: (0, 0)),
            pl.BlockSpec((1, LANES), lambda bi, i: (0, 0)),
            pl.BlockSpec((1, LANES), lambda bi, i: (0, 0)),
            pl.BlockSpec((1, GDN_DIM), lambda bi, i: (0, 0)),
        ],
        out_specs=pl.BlockSpec((1, tb, GDN_WIDTH), lambda bi, i: (bi, i, 0)),
        scratch_shapes=[
            pltpu.VMEM((w3 // LANES, tb + SUBLANES, LANES), F32),
            pltpu.VMEM((GDN_HEADS, tb, GDN_DIM), F32),
            pltpu.VMEM((GDN_HEADS, tb, GDN_DIM), F32),
            pltpu.VMEM((GDN_HEADS, tb, GDN_DIM), F32),
            pltpu.VMEM((n_hu, 2, GDN_DIM + CHUNK, GDN_DIM), BF16),
            pltpu.VMEM((n_hu, 2, GDN_DIM, GDN_DIM), F32),
            pltpu.VMEM((n_hu, 2, CHUNK, GDN_DIM), F32),
            pltpu.VMEM((n_hu, 2, 1, GDN_DIM), F32),
            pltpu.VMEM((GDN_HEADS, tb, GDN_DIM), F32),
            pltpu.VMEM((GDN_HEADS, GDN_DIM, GDN_DIM), F32),
        ],
        compiler_params=_params("parallel", "arbitrary"),
        name="gdn",
    )(proj, proj, proj, small, conv_w, alog_pad, dtb_pad, norm_gain)


MERGE_SUB = 256

ROW_PIECES = D_MODEL // LANES


def _store_tile_rows(ref, r0, value):
    n = value.shape[0]
    for j in range(ROW_PIECES):
        ref[pl.ds(r0 * ROW_PIECES + j, n, stride=ROW_PIECES), :] = value[:, j * LANES:(j + 1) * LANES]


def _load_tile_rows(ref, r0, n):
    return jnp.concatenate(
        [ref[pl.ds(r0 * ROW_PIECES + j, n, stride=ROW_PIECES), :] for j in range(ROW_PIECES)], axis=1)


def _merge_kernel(x_ref, ya_ref, yb_ref, ma_ref, mb_ref, wa_ref, wb_ref, wo_ref, g2_ref, wr_ref, br_ref,
                  x1_ref, h2_ref, route_ref, cnt_ref):
    tm = x_ref.shape[0]
    subs = [slice(r, r + MERGE_SUB) for r in range(0, tm, MERGE_SUB)]
    pa = [jnp.dot(ya_ref[rs, :], wa_ref[...], preferred_element_type=F32) for rs in subs]
    pb = [jnp.dot(yb_ref[rs, :], wb_ref[...], preferred_element_type=F32) for rs in subs]
    merged = [(_sigmoid(ma_ref[rs, :].astype(F32)) * a + _sigmoid(mb_ref[rs, :].astype(F32)) * b).astype(BF16)
              for rs, a, b in zip(subs, pa, pb)]
    x1s = [x_ref[rs, :] + jnp.dot(m, wo_ref[...], preferred_element_type=F32) for rs, m in zip(subs, merged)]
    h2s = [v * lax.rsqrt(jnp.mean(v * v, axis=-1, keepdims=True) + EPS) * g2_ref[...] for v in x1s]
    w_parts = _split_bf16(wr_ref[...])
    logit_s = [_dot_bf16x3(_split_bf16(v), w_parts) for v in h2s]
    for rs, v, hh in zip(subs, x1s, h2s):
        x1_ref[rs, :] = v
        h2_ref[rs, :] = hh
    logits = jnp.concatenate(logit_s, axis=0) + br_ref[...]
    lane = lax.broadcasted_iota(jnp.int32, logits.shape, 1)
    big = jnp.int32(4 * LANES)
    gl = jnp.where(lane < MOE_GROUPS, logits, NEG_INF)
    gmax = jnp.max(gl, axis=-1, keepdims=True)
    gidx = jnp.min(jnp.where(gl == gmax, lane, big), axis=-1, keepdims=True)
    grp_w = 1.0 / jnp.sum(jnp.exp(gl - gmax), axis=-1, keepdims=True)
    lo = MOE_GROUPS + gidx * MOE_EXPERTS_PER_GROUP
    el = jnp.where((lane >= lo) & (lane < lo + MOE_EXPERTS_PER_GROUP), logits, NEG_INF)
    v1 = jnp.max(el, axis=-1, keepdims=True)
    i1 = jnp.min(jnp.where(el == v1, lane, big), axis=-1, keepdims=True)
    el2 = jnp.where(lane == i1, NEG_INF, el)
    v2 = jnp.max(el2, axis=-1, keepdims=True)
    i2 = jnp.min(jnp.where(el2 == v2, lane, big), axis=-1, keepdims=True)
    e2 = jnp.exp(v2 - v1)
    w1 = grp_w / (1.0 + e2)
    w2 = w1 * e2
    id1 = i1 - MOE_GROUPS
    id2 = i2 - MOE_GROUPS
    route_ref[...] = jnp.where(lane == 0, id1.astype(F32), jnp.where(lane == 1, id2.astype(F32),
                               jnp.where(lane == 2, w1, jnp.where(lane == 3, w2, 0.0))))

    @pl.when(pl.program_id(0) == 0)
    def _():
        cnt_ref[...] = jnp.zeros(cnt_ref.shape, F32)

    hits = jnp.where((lane == id1) | (lane == id2), 1.0, 0.0)
    cnt_ref[...] += jnp.broadcast_to(jnp.sum(hits, axis=0, keepdims=True), cnt_ref.shape)


def _merge(x2, ya, yb, proj, wa, wb, wo, g2, wr, br, tm):
    t, d = x2.shape
    row = lambda i: (i, 0)
    const = lambda i: (0, 0)
    return pl.pallas_call(
        _merge_kernel,
        out_shape=(
            jax.ShapeDtypeStruct((t, d), F32),
            jax.ShapeDtypeStruct((t, d), F32),
            jax.ShapeDtypeStruct((t, LANES), F32),
            jax.ShapeDtypeStruct((SUBLANES, LANES), F32),
        ),
        grid=(t // tm,),
        in_specs=[
            pl.BlockSpec((tm, d), row),
            pl.BlockSpec((tm, DIFF_WIDTH), row),
            pl.BlockSpec((tm, GDN_WIDTH), row),
            pl.BlockSpec((tm, d), lambda i: (i, PROJ_MA // D_MODEL)),
            pl.BlockSpec((tm, d), lambda i: (i, PROJ_MB // D_MODEL)),
            pl.BlockSpec((DIFF_WIDTH, d), const),
            pl.BlockSpec((GDN_WIDTH, d), const),
            pl.BlockSpec((d, d), const),
            pl.BlockSpec((1, d), const),
            pl.BlockSpec((d, LANES), const),
            pl.BlockSpec((1, LANES), const),
        ],
        out_specs=(pl.BlockSpec((tm, d), row), pl.BlockSpec((tm, d), row), pl.BlockSpec((tm, LANES), row),
                   pl.BlockSpec((SUBLANES, LANES), const)),
        compiler_params=_params("arbitrary"),
        name="merge_router",
    )(x2, ya, yb, proj, proj, wa, wb, wo, g2, wr, br)


MOE_BLK = 512
MOE_SUB = 256
MOE_META_LANES = 256
MOE_ROUTE_TILE = 512
MOE_DMA_TILE = 256


def _route_kernel(route_ref, cnt_ref, pos_ref, meta_ref, run_scr, *, n_rows):
    i = pl.program_id(0)
    tp = route_ref.shape[0]

    @pl.when(i == 0)
    def _():
        cnt = cnt_ref[...]
        jr = lax.broadcasted_iota(jnp.int32, (LANES, LANES), 0)
        jc = lax.broadcasted_iota(jnp.int32, (LANES, LANES), 1)
        upper = jnp.where(jr < jc, 1.0, 0.0)
        hi_prec = dict(preferred_element_type=F32, precision=lax.Precision.HIGHEST)
        off = jnp.dot(cnt, upper, **hi_prec)
        run_scr[...] = off[0:1]
        blk = float(MOE_BLK)
        first_tile = jnp.floor(off / blk)
        last_tile = jnp.floor((off + cnt - 1.0) / blk)
        n_it = jnp.where(cnt > 0.0, last_tile - first_tile + 1.0, 0.0)
        it_start = jnp.dot(n_it, upper, **hi_prec)
        it_end = it_start + n_it
        lane8 = lax.broadcasted_iota(jnp.int32, cnt.shape, 1)
        e_max = jnp.max(jnp.where(cnt > 0.0, lane8, 0), axis=-1, keepdims=True).astype(F32)[0:1]
        sub8 = lax.broadcasted_iota(jnp.int32, cnt.shape, 0)
        table = jnp.where(sub8 == 0, first_tile, jnp.where(sub8 == 1, it_start, jnp.where(
            sub8 == 2, off, jnp.where(sub8 == 3, cnt, it_end))))
        cols = jnp.transpose(table)
        shape = (LANES, MOE_META_LANES)
        e_sub = lax.broadcasted_iota(jnp.int32, shape, 0)
        w_lane = lax.broadcasted_iota(jnp.int32, shape, 1).astype(F32)
        col = lambda k: jnp.broadcast_to(cols[:, k:k + 1], shape)
        e_w = jnp.sum(jnp.where((e_sub < MOE_EXPERTS) & (col(4) <= w_lane), 1.0, 0.0), axis=0, keepdims=True)
        valid = e_w < float(MOE_EXPERTS)
        e_w = jnp.minimum(e_w, e_max)
        sel = e_sub.astype(F32) == e_w
        pick = lambda k: jnp.sum(jnp.where(sel, col(k), 0.0), axis=0, keepdims=True)
        w_row = w_lane[0:1]
        tile_w = jnp.where(valid, pick(0) + (w_row - pick(1)), float(n_rows // MOE_BLK - 1))
        lo_w = jnp.maximum(pick(2) - tile_w * blk, 0.0)
        hi_w = jnp.minimum(pick(2) + pick(3) - tile_w * blk, blk)
        lo_w = jnp.where(valid, lo_w, 0.0)
        hi_w = jnp.where(valid, hi_w, 0.0)
        sub_m = lax.broadcasted_iota(jnp.int32, meta_ref.shape, 0)
        bc = lambda v: jnp.broadcast_to(v, meta_ref.shape)
        meta_ref[...] = jnp.where(sub_m == 0, bc(e_w), jnp.where(sub_m == 1, bc(tile_w), jnp.where(
            sub_m == 2, bc(lo_w), bc(hi_w)))).astype(jnp.int32)

    r = route_ref[...]
    lane = lax.broadcasted_iota(jnp.int32, r.shape, 1)
    lane_f = lane.astype(F32)
    oh1 = lane_f == r[:, 0:1]
    oh2 = lane_f == r[:, 1:2]
    hits = jnp.where(oh1 | oh2, 1.0, 0.0)
    tr = lax.broadcasted_iota(jnp.int32, (tp, tp), 0)
    tc = lax.broadcasted_iota(jnp.int32, (tp, tp), 1)
    earlier = jnp.where(tc < tr, 1.0, 0.0).astype(BF16)
    rank = jnp.dot(earlier, hits.astype(BF16), preferred_element_type=F32)
    base = run_scr[...] + rank
    p1 = jnp.sum(jnp.where(oh1, base, 0.0), axis=-1, keepdims=True)
    p2 = jnp.sum(jnp.where(oh2, base, 0.0), axis=-1, keepdims=True)
    pos_ref[...] = jnp.where(lane == 0, p1, jnp.where(lane == 1, p2, 0.0)).astype(jnp.int32)
    run_scr[...] += jnp.sum(hits, axis=0, keepdims=True)


def _route_positions(route, cnt):
    t = route.shape[0]
    tp = min(MOE_ROUTE_TILE, t)
    n_rows = 2 * t
    assert n_rows % MOE_BLK == 0 and n_rows // MOE_BLK + MOE_EXPERTS <= MOE_META_LANES and n_rows < 2 ** 24
    return pl.pallas_call(
        functools.partial(_route_kernel, n_rows=n_rows),
        out_shape=(jax.ShapeDtypeStruct((t, LANES), jnp.int32),
                   jax.ShapeDtypeStruct((SUBLANES, MOE_META_LANES), jnp.int32)),
        grid=(t // tp,),
        in_specs=[pl.BlockSpec((tp, LANES), lambda i: (i, 0)),
                  pl.BlockSpec((SUBLANES, LANES), lambda i: (0, 0))],
        out_specs=(pl.BlockSpec((tp, LANES), lambda i: (i, 0)),
                   pl.BlockSpec((SUBLANES, MOE_META_LANES), lambda i: (0, 0))),
        scratch_shapes=[pltpu.VMEM((1, LANES), F32)],
        compiler_params=_params("arbitrary"),
        name="route_positions",
    )(route, cnt)


def _scatter_kernel(pos_ref, h2_ref, xs_ref, sem):
    ts = h2_ref.shape[0]

    def body(tok, carry):
        src = h2_ref.at[pl.ds(tok, 1), :]
        for k in range(2):
            pltpu.make_async_copy(src, xs_ref.at[pl.ds(pos_ref[0, 0, 2 * tok + k], 1), :], sem).start(priority=k)
        return carry

    lax.fori_loop(0, ts, body, 0, unroll=8)
    for _ in range(2):
        pltpu.make_async_copy(h2_ref, xs_ref.at[pl.ds(0, ts), :], sem).wait()


def _scatter_rows(pos3, h2):
    t, d = h2.shape
    ts = pos3.shape[2] // 2
    return pl.pallas_call(
        _scatter_kernel,
        out_shape=jax.ShapeDtypeStruct((2 * t, d), F32),
        grid=(t // ts,),
        in_specs=[pl.BlockSpec((1, 1, 2 * ts), lambda i: (i, 0, 0), memory_space=pltpu.SMEM),
                  pl.BlockSpec((ts, d), lambda i: (i, 0))],
        out_specs=pl.BlockSpec(memory_space=pl.ANY),
        scratch_shapes=[pltpu.SemaphoreType.DMA],
        compiler_params=pltpu.CompilerParams(dimension_semantics=("arbitrary",), vmem_limit_bytes=VMEM_LIMIT,
                                             disable_bounds_checks=True),
        name="scatter_rows",
    )(pos3, h2)


def _expert_kernel(ie_ref, it_ref, lo_ref, hi_ref, xs_ref, wg_ref, wu_ref, wd_ref, y_ref,
                   wg_scr, wu_scr, wd_scr, acc_scr):
    w = pl.program_id(0)
    n_items = pl.num_programs(0)
    prev = jnp.maximum(w - 1, 0)
    nxt = jnp.minimum(w + 1, n_items - 1)

    @pl.when((w == 0) | (ie_ref[w] != ie_ref[prev]))
    def _():
        wg_scr[...] = wg_ref[0].astype(BF16)
        wu_scr[...] = wu_ref[0].astype(BF16)
        wd_scr[...] = wd_ref[0].astype(BF16)

    @pl.when((w == 0) | (it_ref[w] != it_ref[prev]))
    def _():
        acc_scr[...] = jnp.zeros(acc_scr.shape, F32)

    lo = lo_ref[w]
    hi = hi_ref[w]
    subs = [slice(r, r + MOE_SUB) for r in range(0, acc_scr.shape[0], MOE_SUB)]

    @pl.when(hi > lo)
    def _():
        xs = [xs_ref[rs, :].astype(BF16) for rs in subs]
        hg = [jnp.dot(x, wg_scr[...], preferred_element_type=F32) for x in xs]
        hu = [jnp.dot(x, wu_scr[...], preferred_element_type=F32) for x in xs]
        act = [(_silu(g) * u).astype(BF16) for g, u in zip(hg, hu)]
        yp = [jnp.dot(a, wd_scr[...], preferred_element_type=F32) for a in act]
        for rs, v in zip(subs, yp):
            row = rs.start + lax.broadcasted_iota(jnp.int32, v.shape, 0)
            acc_scr[rs, :] += jnp.where((row >= lo) & (row < hi), v, 0.0)

    @pl.when((w == n_items - 1) | (it_ref[nxt] != it_ref[w]))
    def _():
        for rs in subs:
            _store_tile_rows(y_ref, rs.start, acc_scr[rs, :])


def _experts(meta, xs, wg, wu, wd):
    n_rows, d = xs.shape
    n_items = n_rows // MOE_BLK + MOE_EXPERTS
    ie, it, lo, hi = (meta[k, :n_items] for k in range(4))
    return pl.pallas_call(
        _expert_kernel,
        out_shape=jax.ShapeDtypeStruct((n_rows * ROW_PIECES, LANES), F32),
        grid_spec=pltpu.PrefetchScalarGridSpec(
            num_scalar_prefetch=4,
            grid=(n_items,),
            in_specs=[
                pl.BlockSpec((MOE_BLK, d), lambda w, ie, it, lo, hi: (it[w], 0)),
                pl.BlockSpec((1, d, MOE_HIDDEN), lambda w, ie, it, lo, hi: (ie[w], 0, 0)),
                pl.BlockSpec((1, d, MOE_HIDDEN), lambda w, ie, it, lo, hi: (ie[w], 0, 0)),
                pl.BlockSpec((1, MOE_HIDDEN, d), lambda w, ie, it, lo, hi: (ie[w], 0, 0)),
            ],
            out_specs=pl.BlockSpec((MOE_BLK * ROW_PIECES, LANES), lambda w, ie, it, lo, hi: (it[w], 0)),
            scratch_shapes=[pltpu.VMEM((d, MOE_HIDDEN), BF16), pltpu.VMEM((d, MOE_HIDDEN), BF16),
                            pltpu.VMEM((MOE_HIDDEN, d), BF16), pltpu.VMEM((MOE_BLK, d), F32)],
        ),
        compiler_params=_params("arbitrary"),
        name="experts",
    )(ie, it, lo, hi, xs, wg, wu, wd)


def _combine_kernel(pos_ref, posn_ref, x1_ref, route_ref, gf_ref, y_ref, o_ref, ybuf, sem):
    i = pl.program_id(0)
    n = pl.num_programs(0)
    tc = x1_ref.shape[0]
    slot = i % 2

    def issue(p_ref, s):
        def body(tok, carry):
            for k in range(2):
                src = y_ref.at[pl.ds(pl.multiple_of(p_ref[0, 0, 2 * tok + k], ROW_PIECES), ROW_PIECES), :]
                dst = ybuf.at[s, k, pl.ds(pl.multiple_of(tok * ROW_PIECES, ROW_PIECES), ROW_PIECES), :]
                pltpu.make_async_copy(src, dst, sem.at[s]).start(priority=k)
            return carry

        lax.fori_loop(0, tc, body, 0, unroll=8)

    @pl.when(i == 0)
    def _():
        issue(pos_ref, 0)

    @pl.when(i + 1 < n)
    def _():
        issue(posn_ref, 1 - slot)

    for k in range(2):
        pltpu.make_async_copy(y_ref.at[pl.ds(0, tc * ROW_PIECES), :], ybuf.at[slot, k], sem.at[slot]).wait()
    r = route_ref[...]
    x2 = (x1_ref[...] + r[:, 2:3] * _load_tile_rows(ybuf.at[slot, 0], 0, tc)
          + r[:, 3:4] * _load_tile_rows(ybuf.at[slot, 1], 0, tc))
    ms = jnp.mean(x2 * x2, axis=-1, keepdims=True)
    o_ref[...] = x2 * lax.rsqrt(ms + EPS) * gf_ref[...]


def _combine(pos3, x1, route, gf, y):
    t, d = x1.shape
    tc = pos3.shape[2] // 2
    n = t // tc
    return pl.pallas_call(
        _combine_kernel,
        out_shape=jax.ShapeDtypeStruct((t, d), F32),
        grid=(n,),
        in_specs=[
            pl.BlockSpec((1, 1, 2 * tc), lambda i: (i, 0, 0), memory_space=pltpu.SMEM),
            pl.BlockSpec((1, 1, 2 * tc), lambda i: (jnp.minimum(i + 1, n - 1), 0, 0), memory_space=pltpu.SMEM),
            pl.BlockSpec((tc, d), lambda i: (i, 0)),
            pl.BlockSpec((tc, LANES), lambda i: (i, 0)),
            pl.BlockSpec((1, d), lambda i: (0, 0)),
            pl.BlockSpec(memory_space=pl.ANY),
        ],
        out_specs=pl.BlockSpec((tc, d), lambda i: (i, 0)),
        scratch_shapes=[pltpu.VMEM((2, 2, tc * ROW_PIECES, LANES), F32), pltpu.SemaphoreType.DMA((2,))],
        compiler_params=pltpu.CompilerParams(dimension_semantics=("arbitrary",), vmem_limit_bytes=VMEM_LIMIT,
                                             disable_bounds_checks=True),
        name="combine_norm",
    )(pos3, pos3, x1, route, gf, y)


def _pad_lanes(v, offset):
    return jnp.zeros((1, LANES), F32).at[0, offset:offset + v.shape[0]].set(v.astype(F32))


def kernel(x, norm_mix_gain, w_in, diff_lambda_q1, diff_lambda_k1, diff_lambda_q2, diff_lambda_k2, diff_subln_gain, gdn_conv_w, gdn_a_log, gdn_dt_bias, gdn_norm_gain, w_branch_attn, w_branch_gdn, w_out, norm_ffn_gain, moe_w_group, moe_b_group, moe_w_expert, moe_b_expert, moe_w_gate, moe_w_up, moe_w_down, norm_final_gain):
    b, s, d = x.shape
    t = b * s
    x2 = x.reshape(t, d)

    w = w_in[0]
    small_lo = PROJ_GDN + 3 * GDN_WIDTH
    small_hi = small_lo + 2 * GDN_HEADS
    gate_hi = small_hi + GDN_WIDTH
    w_main = jnp.concatenate([w[:, :small_lo], w[:, gate_hi:], w[:, small_hi:gate_hi]], axis=1).astype(BF16)
    w_small = jnp.concatenate([w[:, small_lo:small_hi], jnp.zeros((d, LANES - 2 * GDN_HEADS), w.dtype)],
                              axis=1).astype(BF16)
    gain1 = norm_mix_gain[0].reshape(1, d)

    proj2, small2 = _inproj(x2, gain1, w_main, w_small, min(1024, t), PROJ_WIDTH // 2)
    proj = proj2.reshape(b, s, PROJ_WIDTH)

    lam_params = jnp.stack([diff_lambda_q1[0], diff_lambda_k1[0], diff_lambda_q2[0], diff_lambda_k2[0]]).astype(F32)
    ya = _diff_attention(lam_params, proj, diff_subln_gain[0].reshape(1, DIFF_V_DIM))

    yb = _gdn(proj, small2.reshape(b, s, LANES), gdn_conv_w[0], _pad_lanes(gdn_a_log[0], GDN_HEADS),
              _pad_lanes(gdn_dt_bias[0], GDN_HEADS), gdn_norm_gain[0].reshape(1, GDN_DIM), tb=min(512, s))

    wr = jnp.concatenate([moe_w_group[0], moe_w_expert[0],
                          jnp.zeros((d, LANES - MOE_GROUPS - MOE_EXPERTS), F32)], axis=1)
    br = _pad_lanes(jnp.concatenate([moe_b_group[0], moe_b_expert[0]]), 0)
    x1, h2, route, cnt = _merge(x2, ya.reshape(t, DIFF_WIDTH), yb.reshape(t, GDN_WIDTH), proj2,
                                w_branch_attn[0].astype(BF16), w_branch_gdn[0].astype(BF16),
                                w_out[0].astype(BF16), norm_ffn_gain[0].reshape(1, d), wr, br, tm=min(512, t))

    pos, meta = _route_positions(route, cnt)
    ts = min(MOE_DMA_TILE, t)
    pos3 = pos[:, :2].reshape(t // ts, 1, 2 * ts)
    xs = _scatter_rows(pos3, h2)
    y = _experts(meta, xs, moe_w_gate[0], moe_w_up[0], moe_w_down[0])
    out = _combine(pos3 * ROW_PIECES, x1, route, norm_final_gain.reshape(1, d), y)
    return out.reshape(b, s, d)
```

```python
import functools
import math

import jax
import jax.numpy as jnp
from jax import lax
from jax.experimental import pallas as pl
from jax.experimental.pallas import tpu as pltpu

F32 = jnp.float32
BF16 = jnp.bfloat16

D_MODEL = 1024
CHUNK = 64
EPS = 1e-6

DIFF_HEADS = 4
DIFF_HEAD_DIM = 64
DIFF_V_DIM = 2 * DIFF_HEAD_DIM
DIFF_WIDTH = DIFF_HEADS * DIFF_V_DIM

GDN_HEADS = 4
GDN_DIM = 128
GDN_WIDTH = GDN_HEADS * GDN_DIM
CONV_K = 4

MOE_GROUPS = 4
MOE_EXPERTS_PER_GROUP = 8
MOE_EXPERTS = MOE_GROUPS * MOE_EXPERTS_PER_GROUP
MOE_HIDDEN = 256

LANES = 128
SUBLANES = 8
BF16_SUBLANES = 16
VMEM_LIMIT = 56 * 1024 * 1024

LAMBDA_INIT = 0.8 - 0.6 * math.exp(-0.3 * 0)

PROJ_ATTN = 0
PROJ_GDN = 3 * DIFF_WIDTH
PROJ_MA = PROJ_GDN + 3 * GDN_WIDTH
PROJ_MB = PROJ_MA + D_MODEL
PROJ_GATE = PROJ_MB + D_MODEL
PROJ_WIDTH = PROJ_GATE + GDN_WIDTH

NEG_INF = float("-inf")
LOG2_E = math.log2(math.e)


def _params(*sem):
    return pltpu.CompilerParams(dimension_semantics=sem, vmem_limit_bytes=VMEM_LIMIT)


def _inproj_kernel(x_ref, g_ref, w_ref, ws_ref, o_ref, os_ref, h_scr):
    @pl.when(pl.program_id(1) == 0)
    def _():
        x = x_ref[...]
        ms = jnp.mean(x * x, axis=-1, keepdims=True)
        h_scr[...] = (x * lax.rsqrt(ms + EPS) * g_ref[...]).astype(BF16)
        os_ref[...] = jnp.dot(h_scr[...], ws_ref[...], preferred_element_type=F32)

    o_ref[...] = jnp.dot(h_scr[...], w_ref[...], preferred_element_type=F32).astype(o_ref.dtype)


def _inproj(x2, gain, w, w_small, tm, tn):
    t, d = x2.shape
    n = w.shape[1]
    return pl.pallas_call(
        _inproj_kernel,
        out_shape=(jax.ShapeDtypeStruct((t, n), BF16), jax.ShapeDtypeStruct((t, LANES), F32)),
        grid=(t // tm, n // tn),
        in_specs=[
            pl.BlockSpec((tm, d), lambda i, j: (i, 0)),
            pl.BlockSpec((1, d), lambda i, j: (0, 0)),
            pl.BlockSpec((d, tn), lambda i, j: (0, j)),
            pl.BlockSpec((d, LANES), lambda i, j: (0, 0)),
        ],
        out_specs=(pl.BlockSpec((tm, tn), lambda i, j: (i, j)), pl.BlockSpec((tm, LANES), lambda i, j: (i, 0))),
        scratch_shapes=[pltpu.VMEM((tm, d), BF16)],
        compiler_params=_params("parallel", "arbitrary"),
        name="inproj",
    )(x2, gain, w, w_small)


ATTN_TQ = 1024
ATTN_SPAN = 2
MASK_BIAS = -1e30
ATTN_V_ROWS = DIFF_V_DIM + BF16_SUBLANES
ATTN_COLS = 128


def _attn_kernel(lam_ref, q_ref, k_ref, v_ref, gain_ref, o_ref,
                 qt_scr, oh_scr, vt_scr, s0_scr, s1_scr, p0_scr, p1_scr, a0_scr, a1_scr, m_scr, acc_scr, *, tq, tk):
    i = pl.program_id(2)
    n_q = 2 * tq
    span = ATTN_SPAN
    n_slots = tk // CHUNK

    n_blk = n_q // ATTN_COLS
    half = tq // ATTN_COLS

    @pl.when(i == 0)
    def _():
        for c in range(v_ref.shape[1] // tk):
            v_t = jnp.transpose(v_ref[0, c * tk:(c + 1) * tk, :].astype(F32)).astype(BF16)
            for j in range(tk // LANES):
                vt_scr[c, j] = jnp.concatenate(
                    [v_t[:, j * LANES:(j + 1) * LANES], jnp.ones((BF16_SUBLANES, LANES), BF16)], axis=0)
        slot = lax.broadcasted_iota(jnp.int32, (LANES, ATTN_COLS), 0)
        lane = lax.broadcasted_iota(jnp.int32, (LANES, ATTN_COLS), 1)
        for ver in range(span + 1):
            for c in range(n_blk):
                rel = (lane + (c % half) * ATTN_COLS) // CHUNK - (ver - 1) * n_slots
                if ver == 0:
                    rel = rel * 0 + 2 * n_slots
                want = jnp.where(rel < 0, n_slots, jnp.where(rel < n_slots, rel, -1))
                oh_scr[ver, c] = jnp.where(slot == want, 1.0, 0.0).astype(BF16)

    q_t = jnp.transpose(q_ref[0].astype(F32) * (DIFF_HEAD_DIM ** -0.5 * LOG2_E))
    dim = lax.broadcasted_iota(jnp.int32, (LANES, tq), 0)
    q_comp = (jnp.where(dim < DIFF_HEAD_DIM, q_t, 0.0).astype(BF16),
              jnp.where(dim >= DIFF_HEAD_DIM, q_t, 0.0).astype(BF16))
    for c in range(n_blk):
        qt_scr[c] = q_comp[c // half][:, (c % half) * ATTN_COLS:(c % half + 1) * ATTN_COLS]

    k_lane = lax.broadcasted_iota(jnp.int32, (tk, LANES), 1)
    k_chunk = lax.broadcasted_iota(jnp.int32, (tk, LANES), 0) // CHUNK
    bias = jnp.where((k_lane <= n_slots) & ((k_chunk > k_lane) | (k_lane == n_slots)), MASK_BIAS, 0.0).astype(BF16)

    m_scr[...] = jnp.full(m_scr.shape, NEG_INF, F32)
    acc_scr[...] = jnp.zeros(acc_scr.shape, F32)
    p1_scr[...] = jnp.zeros(p1_scr.shape, BF16)
    a1_scr[...] = jnp.ones(a1_scr.shape, F32)

    all_blocks = tuple(range(n_blk))
    late_blocks = tuple(c for c in all_blocks if (c % half) * ATTN_COLS >= tk)

    def scores(t, s_scr, blocks=all_blocks):
        ver = jnp.clip(t - span * i + 1, 0, span)
        rows = pl.ds(pl.multiple_of(t * tk, tk), tk)
        k_aug = jnp.concatenate([k_ref[0, rows, :], bias], axis=1)
        q_all = jnp.concatenate(
            [jnp.concatenate([qt_scr[c], oh_scr[ver, c]], axis=0) for c in blocks], axis=1)
        s = jnp.dot(k_aug, q_all, preferred_element_type=F32)
        for n, c in enumerate(blocks):
            s_scr[c] = s[:, n * ATTN_COLS:(n + 1) * ATTN_COLS]

    def softmax(s_scr, p_scr, a_scr, blocks=all_blocks):
        for c in blocks:
            cs = slice(c * ATTN_COLS, (c + 1) * ATTN_COLS)
            s = s_scr[c]
            m_prev = m_scr[:, cs]
            m_new = jnp.maximum(m_prev, jnp.max(s, axis=0, keepdims=True))
            a_scr[:, cs] = jnp.exp2(m_prev - m_new)
            m_scr[:, cs] = m_new
            p_scr[c] = jnp.exp2(s - m_new).astype(BF16)

    def values(t, p_scr, a_scr, blocks=all_blocks):
        v_all = jnp.concatenate([vt_scr[t, j] for j in range(tk // LANES)], axis=1)
        p = jnp.concatenate([p_scr[c] for c in blocks], axis=1)
        pv = jnp.dot(v_all, p, preferred_element_type=F32)
        for n, c in enumerate(blocks):
            cs = slice(c * ATTN_COLS, (c + 1) * ATTN_COLS)
            acc_scr[c] = a_scr[:, cs] * acc_scr[c] + pv[:, n * ATTN_COLS:(n + 1) * ATTN_COLS]

    scores(0, s0_scr)

    def pair_step(g, carry):
        t0 = 2 * g
        scores(t0 + 1, s1_scr)
        softmax(s0_scr, p0_scr, a0_scr)
        values(jnp.maximum(t0 - 1, 0), p1_scr, a1_scr)
        scores(t0 + 2, s0_scr)
        softmax(s1_scr, p1_scr, a1_scr)
        values(t0, p0_scr, a0_scr)
        return carry

    lax.fori_loop(0, i, pair_step, 0)
    t0 = span * i
    scores(t0 + 1, s1_scr, late_blocks)
    softmax(s0_scr, p0_scr, a0_scr)
    values(jnp.maximum(t0 - 1, 0), p1_scr, a1_scr)
    softmax(s1_scr, p1_scr, a1_scr, late_blocks)
    values(t0, p0_scr, a0_scr)
    values(t0 + 1, p1_scr, a1_scr, late_blocks)

    lp = lam_ref[...]
    lam = (jnp.exp(jnp.sum(lp[0:1] * lp[1:2], axis=-1, keepdims=True))
           - jnp.exp(jnp.sum(lp[2:3] * lp[3:4], axis=-1, keepdims=True)) + LAMBDA_INIT)
    for c in range(half):
        o0_t = acc_scr[c, 0:DIFF_V_DIM, :] / acc_scr[c, DIFF_V_DIM:DIFF_V_DIM + 1, :]
        o1_t = acc_scr[half + c, 0:DIFF_V_DIM, :] / acc_scr[half + c, DIFF_V_DIM:DIFF_V_DIM + 1, :]
        o = jnp.transpose(o0_t - lam * o1_t)
        ms = jnp.mean(o * o, axis=-1, keepdims=True)
        y = (o * lax.rsqrt(ms + EPS) * gain_ref[...]) * (1.0 - LAMBDA_INIT)
        o_ref[0, c * ATTN_COLS:(c + 1) * ATTN_COLS, :] = y.astype(o_ref.dtype)


def _diff_attention(lam_params, qkv, subln_gain):
    b, s, _ = qkv.shape
    tq = min(ATTN_TQ, s)
    tk = tq // ATTN_SPAN
    assert s % tq == 0 and ATTN_SPAN == 2 and tk % LANES == 0 and tk // CHUNK < LANES
    n_q = 2 * tq
    return pl.pallas_call(
        functools.partial(_attn_kernel, tq=tq, tk=tk),
        out_shape=jax.ShapeDtypeStruct((b, s, DIFF_WIDTH), BF16),
        grid=(b, DIFF_HEADS, s // tq),
        in_specs=[
            pl.BlockSpec((4, DIFF_HEAD_DIM), lambda bi, h, i: (0, 0)),
            pl.BlockSpec((1, tq, DIFF_V_DIM), lambda bi, h, i: (bi, i, h)),
            pl.BlockSpec((1, s, DIFF_V_DIM), lambda bi, h, i: (bi, 0, DIFF_HEADS + h)),
            pl.BlockSpec((1, s, DIFF_V_DIM), lambda bi, h, i: (bi, 0, 2 * DIFF_HEADS + h)),
            pl.BlockSpec((1, DIFF_V_DIM), lambda bi, h, i: (0, 0)),
        ],
        out_specs=pl.BlockSpec((1, tq, DIFF_V_DIM), lambda bi, h, i: (bi, i, h)),
        scratch_shapes=[
            pltpu.VMEM((n_q // ATTN_COLS, LANES, ATTN_COLS), BF16),
            pltpu.VMEM((ATTN_SPAN + 1, n_q // ATTN_COLS, LANES, ATTN_COLS), BF16),
            pltpu.VMEM((s // tk, tk // LANES, ATTN_V_ROWS, LANES), BF16),
            pltpu.VMEM((n_q // ATTN_COLS, tk, ATTN_COLS), F32),
            pltpu.VMEM((n_q // ATTN_COLS, tk, ATTN_COLS), F32),
            pltpu.VMEM((n_q // ATTN_COLS, tk, ATTN_COLS), BF16),
            pltpu.VMEM((n_q // ATTN_COLS, tk, ATTN_COLS), BF16),
            pltpu.VMEM((1, n_q), F32),
            pltpu.VMEM((1, n_q), F32),
            pltpu.VMEM((1, n_q), F32),
            pltpu.VMEM((n_q // ATTN_COLS, ATTN_V_ROWS, ATTN_COLS), F32),
        ],
        compiler_params=_params("parallel", "parallel", "arbitrary"),
        name="diff_attn",
    )(lam_params, qkv, qkv, qkv, subln_gain)


def _silu(x):
    return x * (1.0 / (1.0 + jnp.exp(-x)))


def _sigmoid(x):
    return 1.0 / (1.0 + jnp.exp(-x))


def _softplus(x):
    return jnp.maximum(x, 0.0) + jnp.log(1.0 + jnp.exp(-jnp.abs(x)))


def _split_bf16(x):
    hi = x.astype(BF16)
    return hi, (x - hi.astype(F32)).astype(BF16)


def _dot_bf16x3(a_parts, b_parts):
    (a_hi, a_lo), (b_hi, b_lo) = a_parts, b_parts
    d = lambda x, y: jnp.dot(x, y, preferred_element_type=F32)
    return d(a_hi, b_hi) + (d(a_lo, b_hi) + d(a_hi, b_lo))


def _dot_exact_lhs(lhs, x):
    x1 = x.astype(BF16)
    r1 = x - x1.astype(F32)
    x2 = r1.astype(BF16)
    x3 = (r1 - x2.astype(F32)).astype(BF16)
    d = lambda y: jnp.dot(lhs, y, preferred_element_type=F32)
    return d(x1) + (d(x2) + d(x3))


def _dot(a, b):
    return jnp.dot(a.astype(BF16), b.astype(BF16), preferred_element_type=F32)


def _dot_nt(a, b):
    return lax.dot_general(a.astype(BF16), b.astype(BF16), (((1,), (1,)), ((), ())), preferred_element_type=F32)


GDN_UNIT = 2 * CHUNK


def _gdn_kernel(prev_ref, qkv_ref, gate_ref, small_ref, convw_ref, alog_ref, dtb_ref, ngain_ref, o_ref,
                xp_scr, q_scr, k_scr, v_scr, pq_scr, n_scr, oc_scr, gl_scr, oraw_scr, state_scr, *, tb):
    i = pl.program_id(1)

    @pl.when(i == 0)
    def _():
        state_scr[...] = jnp.zeros(state_scr.shape, F32)

    n_sec = 3 * GDN_HEADS
    prev = jnp.where(i == 0, 0.0, prev_ref[0, BF16_SUBLANES - SUBLANES:BF16_SUBLANES, :].astype(F32))
    for sec in range(n_sec):
        xp_scr[sec, 0:SUBLANES, :] = prev[:, sec * LANES:(sec + 1) * LANES]
    for r0 in range(0, tb, GDN_UNIT):
        blk = qkv_ref[0, r0:r0 + GDN_UNIT, :].astype(F32)
        for sec in range(n_sec):
            xp_scr[sec, SUBLANES + r0:SUBLANES + r0 + GDN_UNIT, :] = blk[:, sec * LANES:(sec + 1) * LANES]
    for sec in range(n_sec):
        cols = slice(sec * LANES, (sec + 1) * LANES)
        acc = None
        for jj in range(CONV_K):
            start = SUBLANES - (CONV_K - 1) + jj
            term = xp_scr[sec, start:start + tb, :] * convw_ref[jj:jj + 1, cols]
            acc = term if acc is None else acc + term
        y = _silu(acc)
        which, head = divmod(sec, GDN_HEADS)
        if which == 0:
            q_scr[head] = y * lax.rsqrt(jnp.sum(y * y, axis=-1, keepdims=True) + EPS) * (GDN_DIM ** -0.5)
        elif which == 1:
            k_scr[head] = y * lax.rsqrt(jnp.sum(y * y, axis=-1, keepdims=True) + EPS)
        else:
            v_scr[head] = y

    sm = small_ref[0]
    beta_all = _sigmoid(sm)
    g_all = -jnp.exp(alog_ref[...]) * _softplus(sm + dtb_ref[...])

    br = lax.broadcasted_iota(jnp.int32, (tb, tb), 0)
    bc = lax.broadcasted_iota(jnp.int32, (tb, tb), 1)
    block_tril = jnp.where((br // CHUNK == bc // CHUNK) & (bc <= br), 1.0, 0.0).astype(BF16)
    gc_all = _dot_exact_lhs(block_tril, g_all)
    gct_all = jnp.transpose(gc_all)

    ur = lax.broadcasted_iota(jnp.int32, (GDN_UNIT, GDN_UNIT), 0)
    uc = lax.broadcasted_iota(jnp.int32, (GDN_UNIT, GDN_UNIT), 1)
    same = (ur // CHUNK) == (uc // CHUNK)
    causal = same & (uc <= ur)
    strict = same & (uc < ur)
    eye = jnp.where(ur == uc, 1.0, 0.0)
    first_rows = ur < CHUNK
    first_cols = uc < CHUNK
    n_units = tb // GDN_UNIT
    units = [(h, u) for h in range(GDN_HEADS) for u in range(n_units)]

    def lane_bcast(x, lane):
        return jnp.broadcast_to(x[:, lane:lane + 1], (GDN_UNIT, GDN_UNIT))

    def row_bcast(x, r):
        return jnp.broadcast_to(x[r:r + 1, :], (GDN_UNIT, GDN_UNIT))

    def rows_of(u):
        return slice(u * GDN_UNIT, (u + 1) * GDN_UNIT)

    def cols_of(h):
        return slice(h * LANES, (h + 1) * LANES)

    qs = [q_scr[h, rows_of(u), :] for h, u in units]
    ks = [k_scr[h, rows_of(u), :] for h, u in units]
    vs = [v_scr[h, rows_of(u), :] for h, u in units]
    betas = [lane_bcast(beta_all[rows_of(u)], h) for h, u in units]
    gcb = [lane_bcast(gc_all[rows_of(u)], GDN_HEADS + h) for h, u in units]
    gcr = [row_bcast(gct_all[:, rows_of(u)], GDN_HEADS + h) for h, u in units]
    decay = [jnp.exp(jnp.where(causal, a - b, NEG_INF)) for a, b in zip(gcb, gcr)]
    eg = [jnp.exp(a) for a in gcb]
    g_last = [jnp.where(first_rows, row_bcast(a, CHUNK - 1), row_bcast(a, GDN_UNIT - 1)) for a in gcb]
    kb = [k * b for k, b in zip(ks, betas)]
    vb = [v * b for v, b in zip(vs, betas)]
    a_low = [jnp.where(strict, _dot_nt(x, k) * d, 0.0) for x, k, d in zip(kb, ks, decay)]
    tinv = [eye - a for a in a_low]
    pw = a_low
    for _ in range(5):
        pw = [_dot(x, x) for x in pw]
        tinv = [t + _dot(t, x) for t, x in zip(tinv, pw)]
    uw = [_dot(t, jnp.concatenate([v, x * e], axis=1)) for t, v, x, e in zip(tinv, vb, kb, eg)]
    intra = [_dot_nt(q, k) * d for q, k, d in zip(qs, ks, decay)]
    iuw = [_dot(a, x) for a, x in zip(intra, uw)]
    k_dec_t = [jnp.transpose(k * jnp.exp(gl - a)) for k, gl, a in zip(ks, g_last, gcb)]
    wu = [jnp.concatenate([x[:, GDN_DIM:], x[:, :GDN_DIM]], axis=1) for x in uw]
    pn0 = [_dot(jnp.where(first_cols, kt, 0.0), x) for kt, x in zip(k_dec_t, wu)]
    pn1 = [_dot(jnp.where(first_cols, 0.0, kt), x) for kt, x in zip(k_dec_t, wu)]
    for n in range(len(units)):
        q_eff = (qs[n] * eg[n] - iuw[n][:, GDN_DIM:]).astype(BF16)
        for half, pn in enumerate((pn0[n], pn1[n])):
            hr = slice(half * CHUNK, (half + 1) * CHUNK)
            pq_scr[n, half, 0:GDN_DIM, :] = pn[:, :GDN_DIM].astype(BF16)
            pq_scr[n, half, GDN_DIM:GDN_DIM + CHUNK, :] = q_eff[hr]
            n_scr[n, half] = pn[:, GDN_DIM:]
            oc_scr[n, half] = iuw[n][hr, :GDN_DIM]
            gl_scr[n, half] = jnp.exp(gcb[n][(half + 1) * CHUNK - 1:(half + 1) * CHUNK, :])

    def chunk_body(c, carry):
        u = c // 2
        half = c % 2
        rows = pl.ds(pl.multiple_of(c * CHUNK, CHUNK), CHUNK)
        states = [state_scr[h] for h in range(GDN_HEADS)]
        res = [jnp.dot(pq_scr[h * n_units + u, half], states[h].astype(BF16), preferred_element_type=F32)
               for h in range(GDN_HEADS)]
        for h in range(GDN_HEADS):
            n = h * n_units + u
            state_scr[h] = gl_scr[n, half] * states[h] - res[h][0:GDN_DIM] + n_scr[n, half]
            oraw_scr[h, rows, :] = res[h][GDN_DIM:GDN_DIM + CHUNK] + oc_scr[n, half]
        return carry

    lax.fori_loop(0, tb // CHUNK, chunk_body, 0)

    for h in range(GDN_HEADS):
        o = oraw_scr[h]
        ms = jnp.mean(o * o, axis=-1, keepdims=True)
        y = o * lax.rsqrt(ms + EPS) * ngain_ref[...] * _silu(gate_ref[0, :, cols_of(h)].astype(F32))
        o_ref[0, :, cols_of(h)] = y.astype(o_ref.dtype)


def _gdn(proj, small, conv_w, alog_pad, dtb_pad, norm_gain, tb):
    b, s, _ = proj.shape
    nb = s // tb
    w3 = 3 * GDN_WIDTH
    assert tb % GDN_UNIT == 0
    n_hu = GDN_HEADS * (tb // GDN_UNIT)
    return pl.pallas_call(
        functools.partial(_gdn_kernel, tb=tb),
        out_shape=jax.ShapeDtypeStruct((b, s, GDN_WIDTH), BF16),
        grid=(b, nb),
        in_specs=[
            pl.BlockSpec((1, BF16_SUBLANES, w3),
                         lambda bi, i: (bi, jnp.maximum(i * (tb // BF16_SUBLANES) - 1, 0), PROJ_GDN // w3)),
            pl.BlockSpec((1, tb, w3), lambda bi, i: (bi, i, PROJ_GDN // w3)),
            pl.BlockSpec((1, tb, GDN_WIDTH), lambda bi, i: (bi, i, PROJ_GATE // GDN_WIDTH)),
            pl.BlockSpec((1, tb, LANES), lambda bi, i: (bi, i, 0)),
            pl.BlockSpec((CONV_K, w3), lambda bi, i: (0, 0)),
            pl.BlockSpec((1, LANES), lambda bi, i: (0, 0)),
            pl.BlockSpec((1, LANES), lambda bi, i: (0, 0)),
            pl.BlockSpec((1, GDN_DIM), lambda bi, i: (0, 0)),
        ],
        out_specs=pl.BlockSpec((1, tb, GDN_WIDTH), lambda bi, i: (bi, i, 0)),
        scratch_shapes=[
            pltpu.VMEM((w3 // LANES, tb + SUBLANES, LANES), F32),
            pltpu.VMEM((GDN_HEADS, tb, GDN_DIM), F32),
            pltpu.VMEM((GDN_HEADS, tb, GDN_DIM), F32),
            pltpu.VMEM((GDN_HEADS, tb, GDN_DIM), F32),
            pltpu.VMEM((n_hu, 2, GDN_DIM + CHUNK, GDN_DIM), BF16),
            pltpu.VMEM((n_hu, 2, GDN_DIM, GDN_DIM), F32),
            pltpu.VMEM((n_hu, 2, CHUNK, GDN_DIM), F32),
            pltpu.VMEM((n_hu, 2, 1, GDN_DIM), F32),
            pltpu.VMEM((GDN_HEADS, tb, GDN_DIM), F32),
            pltpu.VMEM((GDN_HEADS, GDN_DIM, GDN_DIM), F32),
        ],
        compiler_params=_params("parallel", "arbitrary"),
        name="gdn",
    )(proj, proj, proj, small, conv_w, alog_pad, dtb_pad, norm_gain)


MERGE_SUB = 256

ROW_PIECES = D_MODEL // LANES


def _store_tile_rows(ref, r0, value):
    n = value.shape[0]
    for j in range(ROW_PIECES):
        ref[pl.ds(r0 * ROW_PIECES + j, n, stride=ROW_PIECES), :] = value[:, j * LANES:(j + 1) * LANES]


def _load_tile_rows(ref, r0, n):
    return jnp.concatenate(
        [ref[pl.ds(r0 * ROW_PIECES + j, n, stride=ROW_PIECES), :] for j in range(ROW_PIECES)], axis=1)


def _merge_kernel(x_ref, ya_ref, yb_ref, ma_ref, mb_ref, wa_ref, wb_ref, wo_ref, g2_ref, wr_ref, br_ref,
                  x1_ref, h2_ref, route_ref, cnt_ref):
    tm = x_ref.shape[0]
    subs = [slice(r, r + MERGE_SUB) for r in range(0, tm, MERGE_SUB)]
    pa = [jnp.dot(ya_ref[rs, :], wa_ref[...], preferred_element_type=F32) for rs in subs]
    pb = [jnp.dot(yb_ref[rs, :], wb_ref[...], preferred_element_type=F32) for rs in subs]
    merged = [(_sigmoid(ma_ref[rs, :].astype(F32)) * a + _sigmoid(mb_ref[rs, :].astype(F32)) * b).astype(BF16)
              for rs, a, b in zip(subs, pa, pb)]
    x1s = [x_ref[rs, :] + jnp.dot(m, wo_ref[...], preferred_element_type=F32) for rs, m in zip(subs, merged)]
    h2s = [v * lax.rsqrt(jnp.mean(v * v, axis=-1, keepdims=True) + EPS) * g2_ref[...] for v in x1s]
    w_parts = _split_bf16(wr_ref[...])
    logit_s = [_dot_bf16x3(_split_bf16(v), w_parts) for v in h2s]
    for rs, v, hh in zip(subs, x1s, h2s):
        x1_ref[rs, :] = v
        _store_tile_rows(h2_ref, rs.start, hh)
    logits = jnp.concatenate(logit_s, axis=0) + br_ref[...]
    lane = lax.broadcasted_iota(jnp.int32, logits.shape, 1)
    big = jnp.int32(4 * LANES)
    gl = jnp.where(lane < MOE_GROUPS, logits, NEG_INF)
    gmax = jnp.max(gl, axis=-1, keepdims=True)
    gidx = jnp.min(jnp.where(gl == gmax, lane, big), axis=-1, keepdims=True)
    grp_w = 1.0 / jnp.sum(jnp.exp(gl - gmax), axis=-1, keepdims=True)
    lo = MOE_GROUPS + gidx * MOE_EXPERTS_PER_GROUP
    el = jnp.where((lane >= lo) & (lane < lo + MOE_EXPERTS_PER_GROUP), logits, NEG_INF)
    v1 = jnp.max(el, axis=-1, keepdims=True)
    i1 = jnp.min(jnp.where(el == v1, lane, big), axis=-1, keepdims=True)
    el2 = jnp.where(lane == i1, NEG_INF, el)
    v2 = jnp.max(el2, axis=-1, keepdims=True)
    i2 = jnp.min(jnp.where(el2 == v2, lane, big), axis=-1, keepdims=True)
    e2 = jnp.exp(v2 - v1)
    w1 = grp_w / (1.0 + e2)
    w2 = w1 * e2
    id1 = i1 - MOE_GROUPS
    id2 = i2 - MOE_GROUPS
    route_ref[...] = jnp.where(lane == 0, id1.astype(F32), jnp.where(lane == 1, id2.astype(F32),
                               jnp.where(lane == 2, w1, jnp.where(lane == 3, w2, 0.0))))

    @pl.when(pl.program_id(0) == 0)
    def _():
        cnt_ref[...] = jnp.zeros(cnt_ref.shape, F32)

    hits = jnp.where((lane == id1) | (lane == id2), 1.0, 0.0)
    cnt_ref[...] += jnp.broadcast_to(jnp.sum(hits, axis=0, keepdims=True), cnt_ref.shape)


def _merge(x2, ya, yb, proj, wa, wb, wo, g2, wr, br, tm):
    t, d = x2.shape
    row = lambda i: (i, 0)
    const = lambda i: (0, 0)
    return pl.pallas_call(
        _merge_kernel,
        out_shape=(
            jax.ShapeDtypeStruct((t, d), F32),
            jax.ShapeDtypeStruct((t * ROW_PIECES, LANES), F32),
            jax.ShapeDtypeStruct((t, LANES), F32),
            jax.ShapeDtypeStruct((SUBLANES, LANES), F32),
        ),
        grid=(t // tm,),
        in_specs=[
            pl.BlockSpec((tm, d), row),
            pl.BlockSpec((tm, DIFF_WIDTH), row),
            pl.BlockSpec((tm, GDN_WIDTH), row),
            pl.BlockSpec((tm, d), lambda i: (i, PROJ_MA // D_MODEL)),
            pl.BlockSpec((tm, d), lambda i: (i, PROJ_MB // D_MODEL)),
            pl.BlockSpec((DIFF_WIDTH, d), const),
            pl.BlockSpec((GDN_WIDTH, d), const),
            pl.BlockSpec((d, d), const),
            pl.BlockSpec((1, d), const),
            pl.BlockSpec((d, LANES), const),
            pl.BlockSpec((1, LANES), const),
        ],
        out_specs=(pl.BlockSpec((tm, d), row), pl.BlockSpec((tm * ROW_PIECES, LANES), row),
                   pl.BlockSpec((tm, LANES), row), pl.BlockSpec((SUBLANES, LANES), const)),
        compiler_params=_params("arbitrary"),
        name="merge_router",
    )(x2, ya, yb, proj, proj, wa, wb, wo, g2, wr, br)


MOE_BLK = 512
MOE_SUB = 256
MOE_META_LANES = 256
MOE_ROUTE_TILE = 512
MOE_DMA_TILE = 256


def _route_kernel(route_ref, cnt_ref, pos_ref, meta_ref, run_scr, *, n_rows):
    i = pl.program_id(0)
    tp = route_ref.shape[0]

    @pl.when(i == 0)
    def _():
        cnt = cnt_ref[...]
        jr = lax.broadcasted_iota(jnp.int32, (LANES, LANES), 0)
        jc = lax.broadcasted_iota(jnp.int32, (LANES, LANES), 1)
        upper = jnp.where(jr < jc, 1.0, 0.0)
        hi_prec = dict(preferred_element_type=F32, precision=lax.Precision.HIGHEST)
        off = jnp.dot(cnt, upper, **hi_prec)
        run_scr[...] = off[0:1]
        blk = float(MOE_BLK)
        first_tile = jnp.floor(off / blk)
        last_tile = jnp.floor((off + cnt - 1.0) / blk)
        n_it = jnp.where(cnt > 0.0, last_tile - first_tile + 1.0, 0.0)
        it_start = jnp.dot(n_it, upper, **hi_prec)
        it_end = it_start + n_it
        lane8 = lax.broadcasted_iota(jnp.int32, cnt.shape, 1)
        e_max = jnp.max(jnp.where(cnt > 0.0, lane8, 0), axis=-1, keepdims=True).astype(F32)[0:1]
        sub8 = lax.broadcasted_iota(jnp.int32, cnt.shape, 0)
        table = jnp.where(sub8 == 0, first_tile, jnp.where(sub8 == 1, it_start, jnp.where(
            sub8 == 2, off, jnp.where(sub8 == 3, cnt, it_end))))
        cols = jnp.transpose(table)
        shape = (LANES, MOE_META_LANES)
        e_sub = lax.broadcasted_iota(jnp.int32, shape, 0)
        w_lane = lax.broadcasted_iota(jnp.int32, shape, 1).astype(F32)
        col = lambda k: jnp.broadcast_to(cols[:, k:k + 1], shape)
        e_w = jnp.sum(jnp.where((e_sub < MOE_EXPERTS) & (col(4) <= w_lane), 1.0, 0.0), axis=0, keepdims=True)
        valid = e_w < float(MOE_EXPERTS)
        e_w = jnp.minimum(e_w, e_max)
        sel = e_sub.astype(F32) == e_w
        pick = lambda k: jnp.sum(jnp.where(sel, col(k), 0.0), axis=0, keepdims=True)
        w_row = w_lane[0:1]
        tile_w = jnp.where(valid, pick(0) + (w_row - pick(1)), float(n_rows // MOE_BLK - 1))
        lo_w = jnp.maximum(pick(2) - tile_w * blk, 0.0)
        hi_w = jnp.minimum(pick(2) + pick(3) - tile_w * blk, blk)
        lo_w = jnp.where(valid, lo_w, 0.0)
        hi_w = jnp.where(valid, hi_w, 0.0)
        sub_m = lax.broadcasted_iota(jnp.int32, meta_ref.shape, 0)
        bc = lambda v: jnp.broadcast_to(v, meta_ref.shape)
        meta_ref[...] = jnp.where(sub_m == 0, bc(e_w), jnp.where(sub_m == 1, bc(tile_w), jnp.where(
            sub_m == 2, bc(lo_w), bc(hi_w)))).astype(jnp.int32)

    r = route_ref[...]
    lane = lax.broadcasted_iota(jnp.int32, r.shape, 1)
    lane_f = lane.astype(F32)
    oh1 = lane_f == r[:, 0:1]
    oh2 = lane_f == r[:, 1:2]
    hits = jnp.where(oh1 | oh2, 1.0, 0.0)
    tr = lax.broadcasted_iota(jnp.int32, (tp, tp), 0)
    tc = lax.broadcasted_iota(jnp.int32, (tp, tp), 1)
    earlier = jnp.where(tc < tr, 1.0, 0.0).astype(BF16)
    rank = jnp.dot(earlier, hits.astype(BF16), preferred_element_type=F32)
    base = run_scr[...] + rank
    p1 = jnp.sum(jnp.where(oh1, base, 0.0), axis=-1, keepdims=True)
    p2 = jnp.sum(jnp.where(oh2, base, 0.0), axis=-1, keepdims=True)
    pos_ref[...] = jnp.where(lane == 0, p1, jnp.where(lane == 1, p2, 0.0)).astype(jnp.int32)
    run_scr[...] += jnp.sum(hits, axis=0, keepdims=True)


def _route_positions(route, cnt):
    t = route.shape[0]
    tp = min(MOE_ROUTE_TILE, t)
    n_rows = 2 * t
    assert n_rows % MOE_BLK == 0 and n_rows // MOE_BLK + MOE_EXPERTS <= MOE_META_LANES and n_rows < 2 ** 24
    return pl.pallas_call(
        functools.partial(_route_kernel, n_rows=n_rows),
        out_shape=(jax.ShapeDtypeStruct((t, LANES), jnp.int32),
                   jax.ShapeDtypeStruct((SUBLANES, MOE_META_LANES), jnp.int32)),
        grid=(t // tp,),
        in_specs=[pl.BlockSpec((tp, LANES), lambda i: (i, 0)),
                  pl.BlockSpec((SUBLANES, LANES), lambda i: (0, 0))],
        out_specs=(pl.BlockSpec((tp, LANES), lambda i: (i, 0)),
                   pl.BlockSpec((SUBLANES, MOE_META_LANES), lambda i: (0, 0))),
        scratch_shapes=[pltpu.VMEM((1, LANES), F32)],
        compiler_params=_params("arbitrary"),
        name="route_positions",
    )(route, cnt)


def _scatter_kernel(pos_ref, h2_ref, xs_ref, sem):
    ts = h2_ref.shape[0] // ROW_PIECES

    def body(tok, carry):
        src = h2_ref.at[pl.ds(pl.multiple_of(tok * ROW_PIECES, ROW_PIECES), ROW_PIECES), :]
        for k in range(2):
            dst = xs_ref.at[pl.ds(pl.multiple_of(pos_ref[0, 0, 2 * tok + k], ROW_PIECES), ROW_PIECES), :]
            pltpu.make_async_copy(src, dst, sem).start(priority=k)
        return carry

    lax.fori_loop(0, ts, body, 0, unroll=8)
    for _ in range(2):
        pltpu.make_async_copy(h2_ref, xs_ref.at[pl.ds(0, ts * ROW_PIECES), :], sem).wait()


def _scatter_rows(pos3, h2t):
    t = h2t.shape[0] // ROW_PIECES
    ts = pos3.shape[2] // 2
    return pl.pallas_call(
        _scatter_kernel,
        out_shape=jax.ShapeDtypeStruct((2 * t * ROW_PIECES, LANES), F32),
        grid=(t // ts,),
        in_specs=[pl.BlockSpec((1, 1, 2 * ts), lambda i: (i, 0, 0), memory_space=pltpu.SMEM),
                  pl.BlockSpec((ts * ROW_PIECES, LANES), lambda i: (i, 0))],
        out_specs=pl.BlockSpec(memory_space=pl.ANY),
        scratch_shapes=[pltpu.SemaphoreType.DMA],
        compiler_params=pltpu.CompilerParams(dimension_semantics=("arbitrary",), vmem_limit_bytes=VMEM_LIMIT,
                                             disable_bounds_checks=True),
        name="scatter_rows",
    )(pos3, h2t)


def _expert_kernel(ie_ref, it_ref, lo_ref, hi_ref, xs_ref, wg_ref, wu_ref, wd_ref, y_ref,
                   wg_scr, wu_scr, wd_scr, acc_scr):
    w = pl.program_id(0)
    n_items = pl.num_programs(0)
    prev = jnp.maximum(w - 1, 0)
    nxt = jnp.minimum(w + 1, n_items - 1)

    @pl.when((w == 0) | (ie_ref[w] != ie_ref[prev]))
    def _():
        wg_scr[...] = wg_ref[0].astype(BF16)
        wu_scr[...] = wu_ref[0].astype(BF16)
        wd_scr[...] = wd_ref[0].astype(BF16)

    @pl.when((w == 0) | (it_ref[w] != it_ref[prev]))
    def _():
        acc_scr[...] = jnp.zeros(acc_scr.shape, F32)

    lo = lo_ref[w]
    hi = hi_ref[w]
    subs = [slice(r, r + MOE_SUB) for r in range(0, acc_scr.shape[0], MOE_SUB)]

    @pl.when(hi > lo)
    def _():
        xs = [_load_tile_rows(xs_ref, rs.start, MOE_SUB).astype(BF16) for rs in subs]
        hg = [jnp.dot(x, wg_scr[...], preferred_element_type=F32) for x in xs]
        hu = [jnp.dot(x, wu_scr[...], preferred_element_type=F32) for x in xs]
        act = [(_silu(g) * u).astype(BF16) for g, u in zip(hg, hu)]
        yp = [jnp.dot(a, wd_scr[...], preferred_element_type=F32) for a in act]
        for rs, v in zip(subs, yp):
            row = rs.start + lax.broadcasted_iota(jnp.int32, v.shape, 0)
            acc_scr[rs, :] += jnp.where((row >= lo) & (row < hi), v, 0.0)

    @pl.when((w == n_items - 1) | (it_ref[nxt] != it_ref[w]))
    def _():
        for rs in subs:
            _store_tile_rows(y_ref, rs.start, acc_scr[rs, :])


def _experts(meta, xs, wg, wu, wd):
    n_rows = xs.shape[0] // ROW_PIECES
    d = wg.shape[1]
    n_items = n_rows // MOE_BLK + MOE_EXPERTS
    ie, it, lo, hi = (meta[k, :n_items] for k in range(4))
    return pl.pallas_call(
        _expert_kernel,
        out_shape=jax.ShapeDtypeStruct((n_rows * ROW_PIECES, LANES), F32),
        grid_spec=pltpu.PrefetchScalarGridSpec(
            num_scalar_prefetch=4,
            grid=(n_items,),
            in_specs=[
                pl.BlockSpec((MOE_BLK * ROW_PIECES, LANES), lambda w, ie, it, lo, hi: (it[w], 0)),
                pl.BlockSpec((1, d, MOE_HIDDEN), lambda w, ie, it, lo, hi: (ie[w], 0, 0)),
                pl.BlockSpec((1, d, MOE_HIDDEN), lambda w, ie, it, lo, hi: (ie[w], 0, 0)),
                pl.BlockSpec((1, MOE_HIDDEN, d), lambda w, ie, it, lo, hi: (ie[w], 0, 0)),
            ],
            out_specs=pl.BlockSpec((MOE_BLK * ROW_PIECES, LANES), lambda w, ie, it, lo, hi: (it[w], 0)),
            scratch_shapes=[pltpu.VMEM((d, MOE_HIDDEN), BF16), pltpu.VMEM((d, MOE_HIDDEN), BF16),
                            pltpu.VMEM((MOE_HIDDEN, d), BF16), pltpu.VMEM((MOE_BLK, d), F32)],
        ),
        compiler_params=_params("arbitrary"),
        name="experts",
    )(ie, it, lo, hi, xs, wg, wu, wd)


def _combine_kernel(pos_ref, posn_ref, x1_ref, route_ref, gf_ref, y_ref, o_ref, ybuf, sem):
    i = pl.program_id(0)
    n = pl.num_programs(0)
    tc = x1_ref.shape[0]
    slot = i % 2

    def issue(p_ref, s):
        def body(tok, carry):
            for k in range(2):
                src = y_ref.at[pl.ds(pl.multiple_of(p_ref[0, 0, 2 * tok + k], ROW_PIECES), ROW_PIECES), :]
                dst = ybuf.at[s, k, pl.ds(pl.multiple_of(tok * ROW_PIECES, ROW_PIECES), ROW_PIECES), :]
                pltpu.make_async_copy(src, dst, sem.at[s]).start(priority=k)
            return carry

        lax.fori_loop(0, tc, body, 0, unroll=8)

    @pl.when(i == 0)
    def _():
        issue(pos_ref, 0)

    @pl.when(i + 1 < n)
    def _():
        issue(posn_ref, 1 - slot)

    for k in range(2):
        pltpu.make_async_copy(y_ref.at[pl.ds(0, tc * ROW_PIECES), :], ybuf.at[slot, k], sem.at[slot]).wait()
    r = route_ref[...]
    x2 = (x1_ref[...] + r[:, 2:3] * _load_tile_rows(ybuf.at[slot, 0], 0, tc)
          + r[:, 3:4] * _load_tile_rows(ybuf.at[slot, 1], 0, tc))
    ms = jnp.mean(x2 * x2, axis=-1, keepdims=True)
    o_ref[...] = x2 * lax.rsqrt(ms + EPS) * gf_ref[...]


def _combine(pos3, x1, route, gf, y):
    t, d = x1.shape
    tc = pos3.shape[2] // 2
    n = t // tc
    return pl.pallas_call(
        _combine_kernel,
        out_shape=jax.ShapeDtypeStruct((t, d), F32),
        grid=(n,),
        in_specs=[
            pl.BlockSpec((1, 1, 2 * tc), lambda i: (i, 0, 0), memory_space=pltpu.SMEM),
            pl.BlockSpec((1, 1, 2 * tc), lambda i: (jnp.minimum(i + 1, n - 1), 0, 0), memory_space=pltpu.SMEM),
            pl.BlockSpec((tc, d), lambda i: (i, 0)),
            pl.BlockSpec((tc, LANES), lambda i: (i, 0)),
            pl.BlockSpec((1, d), lambda i: (0, 0)),
            pl.BlockSpec(memory_space=pl.ANY),
        ],
        out_specs=pl.BlockSpec((tc, d), lambda i: (i, 0)),
        scratch_shapes=[pltpu.VMEM((2, 2, tc * ROW_PIECES, LANES), F32), pltpu.SemaphoreType.DMA((2,))],
        compiler_params=pltpu.CompilerParams(dimension_semantics=("arbitrary",), vmem_limit_bytes=VMEM_LIMIT,
                                             disable_bounds_checks=True),
        name="combine_norm",
    )(pos3, pos3, x1, route, gf, y)


def _pad_lanes(v, offset):
    return jnp.zeros((1, LANES), F32).at[0, offset:offset + v.shape[0]].set(v.astype(F32))


def kernel(x, norm_mix_gain, w_in, diff_lambda_q1, diff_lambda_k1, diff_lambda_q2, diff_lambda_k2, diff_subln_gain, gdn_conv_w, gdn_a_log, gdn_dt_bias, gdn_norm_gain, w_branch_attn, w_branch_gdn, w_out, norm_ffn_gain, moe_w_group, moe_b_group, moe_w_expert, moe_b_expert, moe_w_gate, moe_w_up, moe_w_down, norm_final_gain):
    b, s, d = x.shape
    t = b * s
    x2 = x.reshape(t, d)

    w = w_in[0]
    small_lo = PROJ_GDN + 3 * GDN_WIDTH
    small_hi = small_lo + 2 * GDN_HEADS
    gate_hi = small_hi + GDN_WIDTH
    w_main = jnp.concatenate([w[:, :small_lo], w[:, gate_hi:], w[:, small_hi:gate_hi]], axis=1).astype(BF16)
    w_small = jnp.concatenate([w[:, small_lo:small_hi], jnp.zeros((d, LANES - 2 * GDN_HEADS), w.dtype)],
                              axis=1).astype(BF16)
    gain1 = norm_mix_gain[0].reshape(1, d)

    proj2, small2 = _inproj(x2, gain1, w_main, w_small, min(1024, t), PROJ_WIDTH // 2)
    proj = proj2.reshape(b, s, PROJ_WIDTH)

    lam_params = jnp.stack([diff_lambda_q1[0], diff_lambda_k1[0], diff_lambda_q2[0], diff_lambda_k2[0]]).astype(F32)
    ya = _diff_attention(lam_params, proj, diff_subln_gain[0].reshape(1, DIFF_V_DIM))

    yb = _gdn(proj, small2.reshape(b, s, LANES), gdn_conv_w[0], _pad_lanes(gdn_a_log[0], GDN_HEADS),
              _pad_lanes(gdn_dt_bias[0], GDN_HEADS), gdn_norm_gain[0].reshape(1, GDN_DIM), tb=min(512, s))

    wr = jnp.concatenate([moe_w_group[0], moe_w_expert[0],
                          jnp.zeros((d, LANES - MOE_GROUPS - MOE_EXPERTS), F32)], axis=1)
    br = _pad_lanes(jnp.concatenate([moe_b_group[0], moe_b_expert[0]]), 0)
    x1, h2, route, cnt = _merge(x2, ya.reshape(t, DIFF_WIDTH), yb.reshape(t, GDN_WIDTH), proj2,
                                w_branch_attn[0].astype(BF16), w_branch_gdn[0].astype(BF16),
                                w_out[0].astype(BF16), norm_ffn_gain[0].reshape(1, d), wr, br, tm=min(512, t))

    pos, meta = _route_positions(route, cnt)
    ts = min(MOE_DMA_TILE, t)
    pos3 = (pos[:, :2] * ROW_PIECES).reshape(t // ts, 1, 2 * ts)
    xs = _scatter_rows(pos3, h2)
    y = _experts(meta, xs, moe_w_gate[0], moe_w_up[0], moe_w_down[0])
    out = _combine(pos3, x1, route, norm_final_gain.reshape(1, d), y)
    return out.reshape(b, s, d)
```

```python
import functools
import math

import jax
import jax.numpy as jnp
from jax import lax
from jax.experimental import pallas as pl
from jax.experimental.pallas import tpu as pltpu

F32 = jnp.float32
BF16 = jnp.bfloat16

D_MODEL = 1024
CHUNK = 64
EPS = 1e-6

DIFF_HEADS = 4
DIFF_HEAD_DIM = 64
DIFF_V_DIM = 2 * DIFF_HEAD_DIM
DIFF_WIDTH = DIFF_HEADS * DIFF_V_DIM

GDN_HEADS = 4
GDN_DIM = 128
GDN_WIDTH = GDN_HEADS * GDN_DIM
CONV_K = 4

MOE_GROUPS = 4
MOE_EXPERTS_PER_GROUP = 8
MOE_EXPERTS = MOE_GROUPS * MOE_EXPERTS_PER_GROUP
MOE_HIDDEN = 256

LANES = 128
SUBLANES = 8
BF16_SUBLANES = 16
VMEM_LIMIT = 56 * 1024 * 1024

LAMBDA_INIT = 0.8 - 0.6 * math.exp(-0.3 * 0)

PROJ_ATTN = 0
PROJ_GDN = 3 * DIFF_WIDTH
PROJ_MA = PROJ_GDN + 3 * GDN_WIDTH
PROJ_MB = PROJ_MA + D_MODEL
PROJ_GATE = PROJ_MB + D_MODEL
PROJ_WIDTH = PROJ_GATE + GDN_WIDTH

NEG_INF = float("-inf")
LOG2_E = math.log2(math.e)


def _params(*sem):
    return pltpu.CompilerParams(dimension_semantics=sem, vmem_limit_bytes=VMEM_LIMIT)


def _inproj_kernel(x_ref, g_ref, w_ref, ws_ref, o_ref, os_ref, h_scr):
    @pl.when(pl.program_id(1) == 0)
    def _():
        x = x_ref[...]
        ms = jnp.mean(x * x, axis=-1, keepdims=True)
        h_scr[...] = (x * lax.rsqrt(ms + EPS) * g_ref[...]).astype(BF16)
        os_ref[...] = jnp.dot(h_scr[...], ws_ref[...], preferred_element_type=F32)

    o_ref[...] = jnp.dot(h_scr[...], w_ref[...], preferred_element_type=F32).astype(o_ref.dtype)


def _inproj(x2, gain, w, w_small, tm, tn):
    t, d = x2.shape
    n = w.shape[1]
    return pl.pallas_call(
        _inproj_kernel,
        out_shape=(jax.ShapeDtypeStruct((t, n), BF16), jax.ShapeDtypeStruct((t, LANES), F32)),
        grid=(t // tm, n // tn),
        in_specs=[
            pl.BlockSpec((tm, d), lambda i, j: (i, 0)),
            pl.BlockSpec((1, d), lambda i, j: (0, 0)),
            pl.BlockSpec((d, tn), lambda i, j: (0, j)),
            pl.BlockSpec((d, LANES), lambda i, j: (0, 0)),
        ],
        out_specs=(pl.BlockSpec((tm, tn), lambda i, j: (i, j)), pl.BlockSpec((tm, LANES), lambda i, j: (i, 0))),
        scratch_shapes=[pltpu.VMEM((tm, d), BF16)],
        compiler_params=_params("parallel", "arbitrary"),
        name="inproj",
    )(x2, gain, w, w_small)


ATTN_TQ = 1024
ATTN_SPAN = 2
MASK_BIAS = -1e30
ATTN_V_ROWS = DIFF_V_DIM + BF16_SUBLANES
ATTN_COLS = 128


def _attn_kernel(lam_ref, q_ref, k_ref, v_ref, gain_ref, o_ref,
                 qt_scr, oh_scr, vt_scr, s0_scr, s1_scr, p0_scr, p1_scr, a0_scr, a1_scr, m_scr, acc_scr, *, tq, tk):
    i = pl.program_id(2)
    n_q = 2 * tq
    span = ATTN_SPAN
    n_slots = tk // CHUNK

    n_blk = n_q // ATTN_COLS
    half = tq // ATTN_COLS

    @pl.when(i == 0)
    def _():
        for c in range(v_ref.shape[1] // tk):
            v_t = jnp.transpose(v_ref[0, c * tk:(c + 1) * tk, :].astype(F32)).astype(BF16)
            for j in range(tk // LANES):
                vt_scr[c, j] = jnp.concatenate(
                    [v_t[:, j * LANES:(j + 1) * LANES], jnp.ones((BF16_SUBLANES, LANES), BF16)], axis=0)
        slot = lax.broadcasted_iota(jnp.int32, (LANES, ATTN_COLS), 0)
        lane = lax.broadcasted_iota(jnp.int32, (LANES, ATTN_COLS), 1)
        for ver in range(span + 1):
            for c in range(n_blk):
                rel = (lane + (c % half) * ATTN_COLS) // CHUNK - (ver - 1) * n_slots
                if ver == 0:
                    rel = rel * 0 + 2 * n_slots
                want = jnp.where(rel < 0, n_slots, jnp.where(rel < n_slots, rel, -1))
                oh_scr[ver, c] = jnp.where(slot == want, 1.0, 0.0).astype(BF16)

    q_t = jnp.transpose(q_ref[0].astype(F32) * (DIFF_HEAD_DIM ** -0.5 * LOG2_E))
    dim = lax.broadcasted_iota(jnp.int32, (LANES, tq), 0)
    q_comp = (jnp.where(dim < DIFF_HEAD_DIM, q_t, 0.0).astype(BF16),
              jnp.where(dim >= DIFF_HEAD_DIM, q_t, 0.0).astype(BF16))
    for c in range(n_blk):
        qt_scr[c] = q_comp[c // half][:, (c % half) * ATTN_COLS:(c % half + 1) * ATTN_COLS]

    k_lane = lax.broadcasted_iota(jnp.int32, (tk, LANES), 1)
    k_chunk = lax.broadcasted_iota(jnp.int32, (tk, LANES), 0) // CHUNK
    bias = jnp.where((k_lane <= n_slots) & ((k_chunk > k_lane) | (k_lane == n_slots)), MASK_BIAS, 0.0).astype(BF16)

    m_scr[...] = jnp.full(m_scr.shape, NEG_INF, F32)
    acc_scr[...] = jnp.zeros(acc_scr.shape, F32)

    all_blocks = tuple(range(n_blk))
    late_blocks = tuple(c for c in all_blocks if (c % half) * ATTN_COLS >= tk)

    def scores(t, s_scr, blocks=all_blocks):
        ver = jnp.clip(t - span * i + 1, 0, span)
        rows = pl.ds(pl.multiple_of(t * tk, tk), tk)
        k_aug = jnp.concatenate([k_ref[0, rows, :], bias], axis=1)
        q_all = jnp.concatenate(
            [jnp.concatenate([qt_scr[c], oh_scr[ver, c]], axis=0) for c in blocks], axis=1)
        s = jnp.dot(k_aug, q_all, preferred_element_type=F32)
        for n, c in enumerate(blocks):
            s_scr[c] = s[:, n * ATTN_COLS:(n + 1) * ATTN_COLS]

    def softmax(s_scr, p_scr, a_scr, blocks=all_blocks):
        for c in blocks:
            cs = slice(c * ATTN_COLS, (c + 1) * ATTN_COLS)
            s = s_scr[c]
            m_prev = m_scr[:, cs]
            m_new = jnp.maximum(m_prev, jnp.max(s, axis=0, keepdims=True))
            a_scr[:, cs] = jnp.exp2(m_prev - m_new)
            m_scr[:, cs] = m_new
            p_scr[c] = jnp.exp2(s - m_new).astype(BF16)

    def values(t, p_scr, a_scr, blocks=all_blocks):
        v_all = jnp.concatenate([vt_scr[t, j] for j in range(tk // LANES)], axis=1)
        p = jnp.concatenate([p_scr[c] for c in blocks], axis=1)
        pv = jnp.dot(v_all, p, preferred_element_type=F32)
        for n, c in enumerate(blocks):
            cs = slice(c * ATTN_COLS, (c + 1) * ATTN_COLS)
            acc_scr[c] = a_scr[:, cs] * acc_scr[c] + pv[:, n * ATTN_COLS:(n + 1) * ATTN_COLS]

    scores(0, s0_scr)
    scores(1, s1_scr)
    softmax(s0_scr, p0_scr, a0_scr)

    def pair_step(g, carry):
        t = 2 * g + 1
        scores(t + 1, s0_scr)
        softmax(s1_scr, p1_scr, a1_scr)
        values(t - 1, p0_scr, a0_scr)
        scores(t + 2, s1_scr)
        softmax(s0_scr, p0_scr, a0_scr)
        values(t, p1_scr, a1_scr)
        return carry

    lax.fori_loop(0, i, pair_step, 0)
    t_last = span * i + 1
    softmax(s1_scr, p1_scr, a1_scr, late_blocks)
    values(t_last - 1, p0_scr, a0_scr)
    values(t_last, p1_scr, a1_scr, late_blocks)

    lp = lam_ref[...]
    lam = (jnp.exp(jnp.sum(lp[0:1] * lp[1:2], axis=-1, keepdims=True))
           - jnp.exp(jnp.sum(lp[2:3] * lp[3:4], axis=-1, keepdims=True)) + LAMBDA_INIT)
    for c in range(half):
        o0_t = acc_scr[c, 0:DIFF_V_DIM, :] / acc_scr[c, DIFF_V_DIM:DIFF_V_DIM + 1, :]
        o1_t = acc_scr[half + c, 0:DIFF_V_DIM, :] / acc_scr[half + c, DIFF_V_DIM:DIFF_V_DIM + 1, :]
        o = jnp.transpose(o0_t - lam * o1_t)
        ms = jnp.mean(o * o, axis=-1, keepdims=True)
        y = (o * lax.rsqrt(ms + EPS) * gain_ref[...]) * (1.0 - LAMBDA_INIT)
        o_ref[0, c * ATTN_COLS:(c + 1) * ATTN_COLS, :] = y.astype(o_ref.dtype)


def _diff_attention(lam_params, qkv, subln_gain):
    b, s, _ = qkv.shape
    tq = min(ATTN_TQ, s)
    tk = tq // ATTN_SPAN
    assert s % tq == 0 and ATTN_SPAN == 2 and tk % LANES == 0 and tk // CHUNK < LANES
    n_q = 2 * tq
    return pl.pallas_call(
        functools.partial(_attn_kernel, tq=tq, tk=tk),
        out_shape=jax.ShapeDtypeStruct((b, s, DIFF_WIDTH), BF16),
        grid=(b, DIFF_HEADS, s // tq),
        in_specs=[
            pl.BlockSpec((4, DIFF_HEAD_DIM), lambda bi, h, i: (0, 0)),
            pl.BlockSpec((1, tq, DIFF_V_DIM), lambda bi, h, i: (bi, i, h)),
            pl.BlockSpec((1, s, DIFF_V_DIM), lambda bi, h, i: (bi, 0, DIFF_HEADS + h)),
            pl.BlockSpec((1, s, DIFF_V_DIM), lambda bi, h, i: (bi, 0, 2 * DIFF_HEADS + h)),
            pl.BlockSpec((1, DIFF_V_DIM), lambda bi, h, i: (0, 0)),
        ],
        out_specs=pl.BlockSpec((1, tq, DIFF_V_DIM), lambda bi, h, i: (bi, i, h)),
        scratch_shapes=[
            pltpu.VMEM((n_q // ATTN_COLS, LANES, ATTN_COLS), BF16),
            pltpu.VMEM((ATTN_SPAN + 1, n_q // ATTN_COLS, LANES, ATTN_COLS), BF16),
            pltpu.VMEM((s // tk, tk // LANES, ATTN_V_ROWS, LANES), BF16),
            pltpu.VMEM((n_q // ATTN_COLS, tk, ATTN_COLS), F32),
            pltpu.VMEM((n_q // ATTN_COLS, tk, ATTN_COLS), F32),
            pltpu.VMEM((n_q // ATTN_COLS, tk, ATTN_COLS), BF16),
            pltpu.VMEM((n_q // ATTN_COLS, tk, ATTN_COLS), BF16),
            pltpu.VMEM((1, n_q), F32),
            pltpu.VMEM((1, n_q), F32),
            pltpu.VMEM((1, n_q), F32),
            pltpu.VMEM((n_q // ATTN_COLS, ATTN_V_ROWS, ATTN_COLS), F32),
        ],
        compiler_params=_params("parallel", "parallel", "arbitrary"),
        name="diff_attn",
    )(lam_params, qkv, qkv, qkv, subln_gain)


def _silu(x):
    return x * (1.0 / (1.0 + jnp.exp(-x)))


def _sigmoid(x):
    return 1.0 / (1.0 + jnp.exp(-x))


def _softplus(x):
    return jnp.maximum(x, 0.0) + jnp.log(1.0 + jnp.exp(-jnp.abs(x)))


def _split_bf16(x):
    hi = x.astype(BF16)
    return hi, (x - hi.astype(F32)).astype(BF16)


def _dot_bf16x3(a_parts, b_parts):
    (a_hi, a_lo), (b_hi, b_lo) = a_parts, b_parts
    d = lambda x, y: jnp.dot(x, y, preferred_element_type=F32)
    return d(a_hi, b_hi) + (d(a_lo, b_hi) + d(a_hi, b_lo))


def _dot_exact_lhs(lhs, x):
    x1 = x.astype(BF16)
    r1 = x - x1.astype(F32)
    x2 = r1.astype(BF16)
    x3 = (r1 - x2.astype(F32)).astype(BF16)
    d = lambda y: jnp.dot(lhs, y, preferred_element_type=F32)
    return d(x1) + (d(x2) + d(x3))


def _dot(a, b):
    return jnp.dot(a.astype(BF16), b.astype(BF16), preferred_element_type=F32)


def _dot_nt(a, b):
    return lax.dot_general(a.astype(BF16), b.astype(BF16), (((1,), (1,)), ((), ())), preferred_element_type=F32)


GDN_UNIT = 2 * CHUNK


def _gdn_kernel(prev_ref, qkv_ref, gate_ref, small_ref, convw_ref, alog_ref, dtb_ref, ngain_ref, o_ref,
                xp_scr, q_scr, k_scr, v_scr, pq_scr, n_scr, oc_scr, gl_scr, oraw_scr, state_scr, *, tb):
    i = pl.program_id(1)

    @pl.when(i == 0)
    def _():
        state_scr[...] = jnp.zeros(state_scr.shape, F32)

    n_sec = 3 * GDN_HEADS
    prev = jnp.where(i == 0, 0.0, prev_ref[0, BF16_SUBLANES - SUBLANES:BF16_SUBLANES, :].astype(F32))
    for sec in range(n_sec):
        xp_scr[sec, 0:SUBLANES, :] = prev[:, sec * LANES:(sec + 1) * LANES]
    for r0 in range(0, tb, GDN_UNIT):
        blk = qkv_ref[0, r0:r0 + GDN_UNIT, :].astype(F32)
        for sec in range(n_sec):
            xp_scr[sec, SUBLANES + r0:SUBLANES + r0 + GDN_UNIT, :] = blk[:, sec * LANES:(sec + 1) * LANES]
    for sec in range(n_sec):
        cols = slice(sec * LANES, (sec + 1) * LANES)
        acc = None
        for jj in range(CONV_K):
            start = SUBLANES - (CONV_K - 1) + jj
            term = xp_scr[sec, start:start + tb, :] * convw_ref[jj:jj + 1, cols]
            acc = term if acc is None else acc + term
        y = _silu(acc)
        which, head = divmod(sec, GDN_HEADS)
        if which == 0:
            q_scr[head] = y * lax.rsqrt(jnp.sum(y * y, axis=-1, keepdims=True) + EPS) * (GDN_DIM ** -0.5)
        elif which == 1:
            k_scr[head] = y * lax.rsqrt(jnp.sum(y * y, axis=-1, keepdims=True) + EPS)
        else:
            v_scr[head] = y

    sm = small_ref[0]
    beta_all = _sigmoid(sm)
    g_all = -jnp.exp(alog_ref[...]) * _softplus(sm + dtb_ref[...])

    br = lax.broadcasted_iota(jnp.int32, (tb, tb), 0)
    bc = lax.broadcasted_iota(jnp.int32, (tb, tb), 1)
    block_tril = jnp.where((br // CHUNK == bc // CHUNK) & (bc <= br), 1.0, 0.0).astype(BF16)
    gc_all = _dot_exact_lhs(block_tril, g_all)
    gct_all = jnp.transpose(gc_all)

    ur = lax.broadcasted_iota(jnp.int32, (GDN_UNIT, GDN_UNIT), 0)
    uc = lax.broadcasted_iota(jnp.int32, (GDN_UNIT, GDN_UNIT), 1)
    same = (ur // CHUNK) == (uc // CHUNK)
    causal = same & (uc <= ur)
    strict = same & (uc < ur)
    eye = jnp.where(ur == uc, 1.0, 0.0)
    first_rows = ur < CHUNK
    first_cols = uc < CHUNK
    n_units = tb // GDN_UNIT
    units = [(h, u) for h in range(GDN_HEADS) for u in range(n_units)]

    def lane_bcast(x, lane):
        return jnp.broadcast_to(x[:, lane:lane + 1], (GDN_UNIT, GDN_UNIT))

    def row_bcast(x, r):
        return jnp.broadcast_to(x[r:r + 1, :], (GDN_UNIT, GDN_UNIT))

    def rows_of(u):
        return slice(u * GDN_UNIT, (u + 1) * GDN_UNIT)

    def cols_of(h):
        return slice(h * LANES, (h + 1) * LANES)

    qs = [q_scr[h, rows_of(u), :] for h, u in units]
    ks = [k_scr[h, rows_of(u), :] for h, u in units]
    vs = [v_scr[h, rows_of(u), :] for h, u in units]
    betas = [lane_bcast(beta_all[rows_of(u)], h) for h, u in units]
    gcb = [lane_bcast(gc_all[rows_of(u)], GDN_HEADS + h) for h, u in units]
    gcr = [row_bcast(gct_all[:, rows_of(u)], GDN_HEADS + h) for h, u in units]
    decay = [jnp.exp(jnp.where(causal, a - b, NEG_INF)) for a, b in zip(gcb, gcr)]
    eg = [jnp.exp(a) for a in gcb]
    g_last = [jnp.where(first_rows, row_bcast(a, CHUNK - 1), row_bcast(a, GDN_UNIT - 1)) for a in gcb]
    kb = [k * b for k, b in zip(ks, betas)]
    vb = [v * b for v, b in zip(vs, betas)]
    a_low = [jnp.where(strict, _dot_nt(x, k) * d, 0.0) for x, k, d in zip(kb, ks, decay)]
    tinv = [eye - a for a in a_low]
    pw = a_low
    for _ in range(5):
        pw = [_dot(x, x) for x in pw]
        tinv = [t + _dot(t, x) for t, x in zip(tinv, pw)]
    uw = [_dot(t, jnp.concatenate([v, x * e], axis=1)) for t, v, x, e in zip(tinv, vb, kb, eg)]
    intra = [_dot_nt(q, k) * d for q, k, d in zip(qs, ks, decay)]
    iuw = [_dot(a, x) for a, x in zip(intra, uw)]
    k_dec_t = [jnp.transpose(k * jnp.exp(gl - a)) for k, gl, a in zip(ks, g_last, gcb)]
    wu = [jnp.concatenate([x[:, GDN_DIM:], x[:, :GDN_DIM]], axis=1) for x in uw]
    pn0 = [_dot(jnp.where(first_cols, kt, 0.0), x) for kt, x in zip(k_dec_t, wu)]
    pn1 = [_dot(jnp.where(first_cols, 0.0, kt), x) for kt, x in zip(k_dec_t, wu)]
    for n in range(len(units)):
        q_eff = (qs[n] * eg[n] - iuw[n][:, GDN_DIM:]).astype(BF16)
        for half, pn in enumerate((pn0[n], pn1[n])):
            hr = slice(half * CHUNK, (half + 1) * CHUNK)
            pq_scr[n, half, 0:GDN_DIM, :] = pn[:, :GDN_DIM].astype(BF16)
            pq_scr[n, half, GDN_DIM:GDN_DIM + CHUNK, :] = q_eff[hr]
            n_scr[n, half] = pn[:, GDN_DIM:]
            oc_scr[n, half] = iuw[n][hr, :GDN_DIM]
            gl_scr[n, half] = jnp.exp(gcb[n][(half + 1) * CHUNK - 1:(half + 1) * CHUNK, :])

    def chunk_body(c, carry):
        u = c // 2
        half = c % 2
        rows = pl.ds(pl.multiple_of(c * CHUNK, CHUNK), CHUNK)
        states = [state_scr[h] for h in range(GDN_HEADS)]
        res = [jnp.dot(pq_scr[h * n_units + u, half], states[h].astype(BF16), preferred_element_type=F32)
               for h in range(GDN_HEADS)]
        for h in range(GDN_HEADS):
            n = h * n_units + u
            state_scr[h] = gl_scr[n, half] * states[h] - res[h][0:GDN_DIM] + n_scr[n, half]
            oraw_scr[h, rows, :] = res[h][GDN_DIM:GDN_DIM + CHUNK] + oc_scr[n, half]
        return carry

    lax.fori_loop(0, tb // CHUNK, chunk_body, 0)

    for h in range(GDN_HEADS):
        o = oraw_scr[h]
        ms = jnp.mean(o * o, axis=-1, keepdims=True)
        y = o * lax.rsqrt(ms + EPS) * ngain_ref[...] * _silu(gate_ref[0, :, cols_of(h)].astype(F32))
        o_ref[0, :, cols_of(h)] = y.astype(o_ref.dtype)


def _gdn(proj, small, conv_w, alog_pad, dtb_pad, norm_gain, tb):
    b, s, _ = proj.shape
    nb = s // tb
    w3 = 3 * GDN_WIDTH
    assert tb % GDN_UNIT == 0
    n_hu = GDN_HEADS * (tb // GDN_UNIT)
    return pl.pallas_call(
        functools.partial(_gdn_kernel, tb=tb),
        out_shape=jax.ShapeDtypeStruct((b, s, GDN_WIDTH), BF16),
        grid=(b, nb),
        in_specs=[
            pl.BlockSpec((1, BF16_SUBLANES, w3),
                         lambda bi, i: (bi, jnp.maximum(i * (tb // BF16_SUBLANES) - 1, 0), PROJ_GDN // w3)),
            pl.BlockSpec((1, tb, w3), lambda bi, i: (bi, i, PROJ_GDN // w3)),
            pl.BlockSpec((1, tb, GDN_WIDTH), lambda bi, i: (bi, i, PROJ_GATE // GDN_WIDTH)),
            pl.BlockSpec((1, tb, LANES), lambda bi, i: (bi, i, 0)),
            pl.BlockSpec((CONV_K, w3), lambda bi, i: (0, 0)),
            pl.BlockSpec((1, LANES), lambda bi, i: (0, 0)),
            pl.BlockSpec((1, LANES), lambda bi, i: (0, 0)),
            pl.BlockSpec((1, GDN_DIM), lambda bi, i: (0, 0)),
        ],
        out_specs=pl.BlockSpec((1, tb, GDN_WIDTH), lambda bi, i: (bi, i, 0)),
        scratch_shapes=[
            pltpu.VMEM((w3 // LANES, tb + SUBLANES, LANES), F32),
            pltpu.VMEM((GDN_HEADS, tb, GDN_DIM), F32),
            pltpu.VMEM((GDN_HEADS, tb, GDN_DIM), F32),
            pltpu.VMEM((GDN_HEADS, tb, GDN_DIM), F32),
            pltpu.VMEM((n_hu, 2, GDN_DIM + CHUNK, GDN_DIM), BF16),
            pltpu.VMEM((n_hu, 2, GDN_DIM, GDN_DIM), F32),
            pltpu.VMEM((n_hu, 2, CHUNK, GDN_DIM), F32),
            pltpu.VMEM((n_hu, 2, 1, GDN_DIM), F32),
            pltpu.VMEM((GDN_HEADS, tb, GDN_DIM), F32),
            pltpu.VMEM((GDN_HEADS, GDN_DIM, GDN_DIM), F32),
        ],
        compiler_params=_params("parallel", "arbitrary"),
        name="gdn",
    )(proj, proj, proj, small, conv_w, alog_pad, dtb_pad, norm_gain)


MERGE_SUB = 256

ROW_PIECES = D_MODEL // LANES


def _store_tile_rows(ref, r0, value):
    n = value.shape[0]
    for j in range(ROW_PIECES):
        ref[pl.ds(r0 * ROW_PIECES + j, n, stride=ROW_PIECES), :] = value[:, j * LANES:(j + 1) * LANES]


def _load_tile_rows(ref, r0, n):
    return jnp.concatenate(
        [ref[pl.ds(r0 * ROW_PIECES + j, n, stride=ROW_PIECES), :] for j in range(ROW_PIECES)], axis=1)


def _merge_kernel(x_ref, ya_ref, yb_ref, ma_ref, mb_ref, wa_ref, wb_ref, wo_ref, g2_ref, wr_ref, br_ref,
                  x1_ref, h2_ref, route_ref, cnt_ref):
    tm = x_ref.shape[0]
    subs = [slice(r, r + MERGE_SUB) for r in range(0, tm, MERGE_SUB)]
    pa = [jnp.dot(ya_ref[rs, :], wa_ref[...], preferred_element_type=F32) for rs in subs]
    pb = [jnp.dot(yb_ref[rs, :], wb_ref[...], preferred_element_type=F32) for rs in subs]
    merged = [(_sigmoid(ma_ref[rs, :].astype(F32)) * a + _sigmoid(mb_ref[rs, :].astype(F32)) * b).astype(BF16)
              for rs, a, b in zip(subs, pa, pb)]
    x1s = [x_ref[rs, :] + jnp.dot(m, wo_ref[...], preferred_element_type=F32) for rs, m in zip(subs, merged)]
    h2s = [v * lax.rsqrt(jnp.mean(v * v, axis=-1, keepdims=True) + EPS) * g2_ref[...] for v in x1s]
    w_parts = _split_bf16(wr_ref[...])
    logit_s = [_dot_bf16x3(_split_bf16(v), w_parts) for v in h2s]
    for rs, v, hh in zip(subs, x1s, h2s):
        x1_ref[rs, :] = v
        _store_tile_rows(h2_ref, rs.start, hh)
    logits = jnp.concatenate(logit_s, axis=0) + br_ref[...]
    lane = lax.broadcasted_iota(jnp.int32, logits.shape, 1)
    big = jnp.int32(4 * LANES)
    gl = jnp.where(lane < MOE_GROUPS, logits, NEG_INF)
    gmax = jnp.max(gl, axis=-1, keepdims=True)
    gidx = jnp.min(jnp.where(gl == gmax, lane, big), axis=-1, keepdims=True)
    grp_w = 1.0 / jnp.sum(jnp.exp(gl - gmax), axis=-1, keepdims=True)
    lo = MOE_GROUPS + gidx * MOE_EXPERTS_PER_GROUP
    el = jnp.where((lane >= lo) & (lane < lo + MOE_EXPERTS_PER_GROUP), logits, NEG_INF)
    v1 = jnp.max(el, axis=-1, keepdims=True)
    i1 = jnp.min(jnp.where(el == v1, lane, big), axis=-1, keepdims=True)
    el2 = jnp.where(lane == i1, NEG_INF, el)
    v2 = jnp.max(el2, axis=-1, keepdims=True)
    i2 = jnp.min(jnp.where(el2 == v2, lane, big), axis=-1, keepdims=True)
    e2 = jnp.exp(v2 - v1)
    w1 = grp_w / (1.0 + e2)
    w2 = w1 * e2
    id1 = i1 - MOE_GROUPS
    id2 = i2 - MOE_GROUPS
    route_ref[...] = jnp.where(lane == 0, id1.astype(F32), jnp.where(lane == 1, id2.astype(F32),
                               jnp.where(lane == 2, w1, jnp.where(lane == 3, w2, 0.0))))

    @pl.when(pl.program_id(0) == 0)
    def _():
        cnt_ref[...] = jnp.zeros(cnt_ref.shape, F32)

    hits = jnp.where((lane == id1) | (lane == id2), 1.0, 0.0)
    cnt_ref[...] += jnp.broadcast_to(jnp.sum(hits, axis=0, keepdims=True), cnt_ref.shape)


def _merge(x2, ya, yb, proj, wa, wb, wo, g2, wr, br, tm):
    t, d = x2.shape
    row = lambda i: (i, 0)
    const = lambda i: (0, 0)
    return pl.pallas_call(
        _merge_kernel,
        out_shape=(
            jax.ShapeDtypeStruct((t, d), F32),
            jax.ShapeDtypeStruct((t * ROW_PIECES, LANES), F32),
            jax.ShapeDtypeStruct((t, LANES), F32),
            jax.ShapeDtypeStruct((SUBLANES, LANES), F32),
        ),
        grid=(t // tm,),
        in_specs=[
            pl.BlockSpec((tm, d), row),
            pl.BlockSpec((tm, DIFF_WIDTH), row),
            pl.BlockSpec((tm, GDN_WIDTH), row),
            pl.BlockSpec((tm, d), lambda i: (i, PROJ_MA // D_MODEL)),
            pl.BlockSpec((tm, d), lambda i: (i, PROJ_MB // D_MODEL)),
            pl.BlockSpec((DIFF_WIDTH, d), const),
            pl.BlockSpec((GDN_WIDTH, d), const),
            pl.BlockSpec((d, d), const),
            pl.BlockSpec((1, d), const),
            pl.BlockSpec((d, LANES), const),
            pl.BlockSpec((1, LANES), const),
        ],
        out_specs=(pl.BlockSpec((tm, d), row), pl.BlockSpec((tm * ROW_PIECES, LANES), row),
                   pl.BlockSpec((tm, LANES), row), pl.BlockSpec((SUBLANES, LANES), const)),
        compiler_params=_params("arbitrary"),
        name="merge_router",
    )(x2, ya, yb, proj, proj, wa, wb, wo, g2, wr, br)


MOE_BLK = 512
MOE_SUB = 256
MOE_META_LANES = 256
MOE_ROUTE_TILE = 512
MOE_DMA_TILE = 512


def _route_kernel(route_ref, cnt_ref, pos_ref, meta_ref, run_scr, *, n_rows):
    i = pl.program_id(0)
    tp = route_ref.shape[0]

    @pl.when(i == 0)
    def _():
        cnt = cnt_ref[...]
        jr = lax.broadcasted_iota(jnp.int32, (LANES, LANES), 0)
        jc = lax.broadcasted_iota(jnp.int32, (LANES, LANES), 1)
        upper = jnp.where(jr < jc, 1.0, 0.0)
        hi_prec = dict(preferred_element_type=F32, precision=lax.Precision.HIGHEST)
        off = jnp.dot(cnt, upper, **hi_prec)
        run_scr[...] = off[0:1]
        blk = float(MOE_BLK)
        first_tile = jnp.floor(off / blk)
        last_tile = jnp.floor((off + cnt - 1.0) / blk)
        n_it = jnp.where(cnt > 0.0, last_tile - first_tile + 1.0, 0.0)
        it_start = jnp.dot(n_it, upper, **hi_prec)
        it_end = it_start + n_it
        lane8 = lax.broadcasted_iota(jnp.int32, cnt.shape, 1)
        e_max = jnp.max(jnp.where(cnt > 0.0, lane8, 0), axis=-1, keepdims=True).astype(F32)[0:1]
        sub8 = lax.broadcasted_iota(jnp.int32, cnt.shape, 0)
        table = jnp.where(sub8 == 0, first_tile, jnp.where(sub8 == 1, it_start, jnp.where(
            sub8 == 2, off, jnp.where(sub8 == 3, cnt, it_end))))
        cols = jnp.transpose(table)
        shape = (LANES, MOE_META_LANES)
        e_sub = lax.broadcasted_iota(jnp.int32, shape, 0)
        w_lane = lax.broadcasted_iota(jnp.int32, shape, 1).astype(F32)
        col = lambda k: jnp.broadcast_to(cols[:, k:k + 1], shape)
        e_w = jnp.sum(jnp.where((e_sub < MOE_EXPERTS) & (col(4) <= w_lane), 1.0, 0.0), axis=0, keepdims=True)
        valid = e_w < float(MOE_EXPERTS)
        e_w = jnp.minimum(e_w, e_max)
        sel = e_sub.astype(F32) == e_w
        pick = lambda k: jnp.sum(jnp.where(sel, col(k), 0.0), axis=0, keepdims=True)
        w_row = w_lane[0:1]
        tile_w = jnp.where(valid, pick(0) + (w_row - pick(1)), float(n_rows // MOE_BLK - 1))
        lo_w = jnp.maximum(pick(2) - tile_w * blk, 0.0)
        hi_w = jnp.minimum(pick(2) + pick(3) - tile_w * blk, blk)
        lo_w = jnp.where(valid, lo_w, 0.0)
        hi_w = jnp.where(valid, hi_w, 0.0)
        sub_m = lax.broadcasted_iota(jnp.int32, meta_ref.shape, 0)
        bc = lambda v: jnp.broadcast_to(v, meta_ref.shape)
        meta_ref[...] = jnp.where(sub_m == 0, bc(e_w), jnp.where(sub_m == 1, bc(tile_w), jnp.where(
            sub_m == 2, bc(lo_w), bc(hi_w)))).astype(jnp.int32)

    r = route_ref[...]
    lane = lax.broadcasted_iota(jnp.int32, r.shape, 1)
    lane_f = lane.astype(F32)
    oh1 = lane_f == r[:, 0:1]
    oh2 = lane_f == r[:, 1:2]
    hits = jnp.where(oh1 | oh2, 1.0, 0.0)
    tr = lax.broadcasted_iota(jnp.int32, (tp, tp), 0)
    tc = lax.broadcasted_iota(jnp.int32, (tp, tp), 1)
    earlier = jnp.where(tc < tr, 1.0, 0.0).astype(BF16)
    rank = jnp.dot(earlier, hits.astype(BF16), preferred_element_type=F32)
    base = run_scr[...] + rank
    p1 = jnp.sum(jnp.where(oh1, base, 0.0), axis=-1, keepdims=True)
    p2 = jnp.sum(jnp.where(oh2, base, 0.0), axis=-1, keepdims=True)
    pos_ref[...] = jnp.where(lane == 0, p1, jnp.where(lane == 1, p2, 0.0)).astype(jnp.int32)
    run_scr[...] += jnp.sum(hits, axis=0, keepdims=True)


def _route_positions(route, cnt):
    t = route.shape[0]
    tp = min(MOE_ROUTE_TILE, t)
    n_rows = 2 * t
    assert n_rows % MOE_BLK == 0 and n_rows // MOE_BLK + MOE_EXPERTS <= MOE_META_LANES and n_rows < 2 ** 24
    return pl.pallas_call(
        functools.partial(_route_kernel, n_rows=n_rows),
        out_shape=(jax.ShapeDtypeStruct((t, LANES), jnp.int32),
                   jax.ShapeDtypeStruct((SUBLANES, MOE_META_LANES), jnp.int32)),
        grid=(t // tp,),
        in_specs=[pl.BlockSpec((tp, LANES), lambda i: (i, 0)),
                  pl.BlockSpec((SUBLANES, LANES), lambda i: (0, 0))],
        out_specs=(pl.BlockSpec((tp, LANES), lambda i: (i, 0)),
                   pl.BlockSpec((SUBLANES, MOE_META_LANES), lambda i: (0, 0))),
        scratch_shapes=[pltpu.VMEM((1, LANES), F32)],
        compiler_params=_params("arbitrary"),
        name="route_positions",
    )(route, cnt)


def _scatter_kernel(pos_ref, h2_ref, xs_ref, sem):
    ts = h2_ref.shape[0] // ROW_PIECES

    def body(tok, carry):
        src = h2_ref.at[pl.ds(pl.multiple_of(tok * ROW_PIECES, ROW_PIECES), ROW_PIECES), :]
        for k in range(2):
            dst = xs_ref.at[pl.ds(pl.multiple_of(pos_ref[0, 0, 2 * tok + k], ROW_PIECES), ROW_PIECES), :]
            pltpu.make_async_copy(src, dst, sem).start(priority=k)
        return carry

    lax.fori_loop(0, ts, body, 0, unroll=8)
    for _ in range(2):
        pltpu.make_async_copy(h2_ref, xs_ref.at[pl.ds(0, ts * ROW_PIECES), :], sem).wait()


def _scatter_rows(pos3, h2t):
    t = h2t.shape[0] // ROW_PIECES
    ts = pos3.shape[2] // 2
    return pl.pallas_call(
        _scatter_kernel,
        out_shape=jax.ShapeDtypeStruct((2 * t * ROW_PIECES, LANES), F32),
        grid=(t // ts,),
        in_specs=[pl.BlockSpec((1, 1, 2 * ts), lambda i: (i, 0, 0), memory_space=pltpu.SMEM),
                  pl.BlockSpec((ts * ROW_PIECES, LANES), lambda i: (i, 0))],
        out_specs=pl.BlockSpec(memory_space=pl.ANY),
        scratch_shapes=[pltpu.SemaphoreType.DMA],
        compiler_params=pltpu.CompilerParams(dimension_semantics=("arbitrary",), vmem_limit_bytes=VMEM_LIMIT,
                                             disable_bounds_checks=True),
        name="scatter_rows",
    )(pos3, h2t)


def _expert_kernel(ie_ref, it_ref, lo_ref, hi_ref, xs_ref, wg_ref, wu_ref, wd_ref, y_ref,
                   wg_scr, wu_scr, wd_scr, acc_scr):
    w = pl.program_id(0)
    n_items = pl.num_programs(0)
    prev = jnp.maximum(w - 1, 0)
    nxt = jnp.minimum(w + 1, n_items - 1)

    @pl.when((w == 0) | (ie_ref[w] != ie_ref[prev]))
    def _():
        wg_scr[...] = wg_ref[0].astype(BF16)
        wu_scr[...] = wu_ref[0].astype(BF16)
        wd_scr[...] = wd_ref[0].astype(BF16)

    @pl.when((w == 0) | (it_ref[w] != it_ref[prev]))
    def _():
        acc_scr[...] = jnp.zeros(acc_scr.shape, F32)

    lo = lo_ref[w]
    hi = hi_ref[w]
    subs = [slice(r, r + MOE_SUB) for r in range(0, acc_scr.shape[0], MOE_SUB)]

    @pl.when(hi > lo)
    def _():
        xs = [_load_tile_rows(xs_ref, rs.start, MOE_SUB).astype(BF16) for rs in subs]
        hg = [jnp.dot(x, wg_scr[...], preferred_element_type=F32) for x in xs]
        hu = [jnp.dot(x, wu_scr[...], preferred_element_type=F32) for x in xs]
        act = [(_silu(g) * u).astype(BF16) for g, u in zip(hg, hu)]
        yp = [jnp.dot(a, wd_scr[...], preferred_element_type=F32) for a in act]
        for rs, v in zip(subs, yp):
            row = rs.start + lax.broadcasted_iota(jnp.int32, v.shape, 0)
            acc_scr[rs, :] += jnp.where((row >= lo) & (row < hi), v, 0.0)

    @pl.when((w == n_items - 1) | (it_ref[nxt] != it_ref[w]))
    def _():
        for rs in subs:
            _store_tile_rows(y_ref, rs.start, acc_scr[rs, :])


def _experts(meta, xs, wg, wu, wd):
    n_rows = xs.shape[0] // ROW_PIECES
    d = wg.shape[1]
    n_items = n_rows // MOE_BLK + MOE_EXPERTS
    ie, it, lo, hi = (meta[k, :n_items] for k in range(4))
    return pl.pallas_call(
        _expert_kernel,
        out_shape=jax.ShapeDtypeStruct((n_rows * ROW_PIECES, LANES), F32),
        grid_spec=pltpu.PrefetchScalarGridSpec(
            num_scalar_prefetch=4,
            grid=(n_items,),
            in_specs=[
                pl.BlockSpec((MOE_BLK * ROW_PIECES, LANES), lambda w, ie, it, lo, hi: (it[w], 0)),
                pl.BlockSpec((1, d, MOE_HIDDEN), lambda w, ie, it, lo, hi: (ie[w], 0, 0)),
                pl.BlockSpec((1, d, MOE_HIDDEN), lambda w, ie, it, lo, hi: (ie[w], 0, 0)),
                pl.BlockSpec((1, MOE_HIDDEN, d), lambda w, ie, it, lo, hi: (ie[w], 0, 0)),
            ],
            out_specs=pl.BlockSpec((MOE_BLK * ROW_PIECES, LANES), lambda w, ie, it, lo, hi: (it[w], 0)),
            scratch_shapes=[pltpu.VMEM((d, MOE_HIDDEN), BF16), pltpu.VMEM((d, MOE_HIDDEN), BF16),
                            pltpu.VMEM((MOE_HIDDEN, d), BF16), pltpu.VMEM((MOE_BLK, d), F32)],
        ),
        compiler_params=_params("arbitrary"),
        name="experts",
    )(ie, it, lo, hi, xs, wg, wu, wd)


def _combine_kernel(pos_ref, posn_ref, x1_ref, route_ref, gf_ref, y_ref, o_ref, ybuf, sem):
    i = pl.program_id(0)
    n = pl.num_programs(0)
    tc = x1_ref.shape[0]
    slot = i % 2

    def issue(p_ref, s):
        def body(tok, carry):
            for k in range(2):
                src = y_ref.at[pl.ds(pl.multiple_of(p_ref[0, 0, 2 * tok + k], ROW_PIECES), ROW_PIECES), :]
                dst = ybuf.at[s, k, pl.ds(pl.multiple_of(tok * ROW_PIECES, ROW_PIECES), ROW_PIECES), :]
                pltpu.make_async_copy(src, dst, sem.at[s]).start(priority=k)
            return carry

        lax.fori_loop(0, tc, body, 0, unroll=8)

    @pl.when(i == 0)
    def _():
        issue(pos_ref, 0)

    @pl.when(i + 1 < n)
    def _():
        issue(posn_ref, 1 - slot)

    for k in range(2):
        pltpu.make_async_copy(y_ref.at[pl.ds(0, tc * ROW_PIECES), :], ybuf.at[slot, k], sem.at[slot]).wait()
    r = route_ref[...]
    x2 = (x1_ref[...] + r[:, 2:3] * _load_tile_rows(ybuf.at[slot, 0], 0, tc)
          + r[:, 3:4] * _load_tile_rows(ybuf.at[slot, 1], 0, tc))
    ms = jnp.mean(x2 * x2, axis=-1, keepdims=True)
    o_ref[...] = x2 * lax.rsqrt(ms + EPS) * gf_ref[...]


def _combine(pos3, x1, route, gf, y):
    t, d = x1.shape
    tc = pos3.shape[2] // 2
    n = t // tc
    return pl.pallas_call(
        _combine_kernel,
        out_shape=jax.ShapeDtypeStruct((t, d), F32),
        grid=(n,),
        in_specs=[
            pl.BlockSpec((1, 1, 2 * tc), lambda i: (i, 0, 0), memory_space=pltpu.SMEM),
            pl.BlockSpec((1, 1, 2 * tc), lambda i: (jnp.minimum(i + 1, n - 1), 0, 0), memory_space=pltpu.SMEM),
            pl.BlockSpec((tc, d), lambda i: (i, 0)),
            pl.BlockSpec((tc, LANES), lambda i: (i, 0)),
            pl.BlockSpec((1, d), lambda i: (0, 0)),
            pl.BlockSpec(memory_space=pl.ANY),
        ],
        out_specs=pl.BlockSpec((tc, d), lambda i: (i, 0)),
        scratch_shapes=[pltpu.VMEM((2, 2, tc * ROW_PIECES, LANES), F32), pltpu.SemaphoreType.DMA((2,))],
        compiler_params=pltpu.CompilerParams(dimension_semantics=("arbitrary",), vmem_limit_bytes=VMEM_LIMIT,
                                             disable_bounds_checks=True),
        name="combine_norm",
    )(pos3, pos3, x1, route, gf, y)


def _pad_lanes(v, offset):
    return jnp.zeros((1, LANES), F32).at[0, offset:offset + v.shape[0]].set(v.astype(F32))


def kernel(x, norm_mix_gain, w_in, diff_lambda_q1, diff_lambda_k1, diff_lambda_q2, diff_lambda_k2, diff_subln_gain, gdn_conv_w, gdn_a_log, gdn_dt_bias, gdn_norm_gain, w_branch_attn, w_branch_gdn, w_out, norm_ffn_gain, moe_w_group, moe_b_group, moe_w_expert, moe_b_expert, moe_w_gate, moe_w_up, moe_w_down, norm_final_gain):
    b, s, d = x.shape
    t = b * s
    x2 = x.reshape(t, d)

    w = w_in[0]
    small_lo = PROJ_GDN + 3 * GDN_WIDTH
    small_hi = small_lo + 2 * GDN_HEADS
    gate_hi = small_hi + GDN_WIDTH
    w_main = jnp.concatenate([w[:, :small_lo], w[:, gate_hi:], w[:, small_hi:gate_hi]], axis=1).astype(BF16)
    w_small = jnp.concatenate([w[:, small_lo:small_hi], jnp.zeros((d, LANES - 2 * GDN_HEADS), w.dtype)],
                              axis=1).astype(BF16)
    gain1 = norm_mix_gain[0].reshape(1, d)

    proj2, small2 = _inproj(x2, gain1, w_main, w_small, min(1024, t), PROJ_WIDTH // 2)
    proj = proj2.reshape(b, s, PROJ_WIDTH)

    lam_params = jnp.stack([diff_lambda_q1[0], diff_lambda_k1[0], diff_lambda_q2[0], diff_lambda_k2[0]]).astype(F32)
    ya = _diff_attention(lam_params, proj, diff_subln_gain[0].reshape(1, DIFF_V_DIM))

    yb = _gdn(proj, small2.reshape(b, s, LANES), gdn_conv_w[0], _pad_lanes(gdn_a_log[0], GDN_HEADS),
              _pad_lanes(gdn_dt_bias[0], GDN_HEADS), gdn_norm_gain[0].reshape(1, GDN_DIM), tb=min(512, s))

    wr = jnp.concatenate([moe_w_group[0], moe_w_expert[0],
                          jnp.zeros((d, LANES - MOE_GROUPS - MOE_EXPERTS), F32)], axis=1)
    br = _pad_lanes(jnp.concatenate([moe_b_group[0], moe_b_expert[0]]), 0)
    x1, h2, route, cnt = _merge(x2, ya.reshape(t, DIFF_WIDTH), yb.reshape(t, GDN_WIDTH), proj2,
                                w_branch_attn[0].astype(BF16), w_branch_gdn[0].astype(BF16),
                                w_out[0].astype(BF16), norm_ffn_gain[0].reshape(1, d), wr, br, tm=min(512, t))

    pos, meta = _route_positions(route, cnt)
    ts = min(MOE_DMA_TILE, t)
    pos3 = (pos[:, :2] * ROW_PIECES).reshape(t // ts, 1, 2 * ts)
    xs = _scatter_rows(pos3, h2)
    y = _experts(meta, xs, moe_w_gate[0], moe_w_up[0], moe_w_down[0])
    out = _combine(pos3, x1, route, norm_final_gain.reshape(1, d), y)
    return out.reshape(b, s, d)
```

```python
import functools
import math

import jax
import jax.numpy as jnp
from jax import lax
from jax.experimental import pallas as pl
from jax.experimental.pallas import tpu as pltpu

F32 = jnp.float32
BF16 = jnp.bfloat16

D_MODEL = 1024
CHUNK = 64
EPS = 1e-6

DIFF_HEADS = 4
DIFF_HEAD_DIM = 64
DIFF_V_DIM = 2 * DIFF_HEAD_DIM
DIFF_WIDTH = DIFF_HEADS * DIFF_V_DIM

GDN_HEADS = 4
GDN_DIM = 128
GDN_WIDTH = GDN_HEADS * GDN_DIM
CONV_K = 4

MOE_GROUPS = 4
MOE_EXPERTS_PER_GROUP = 8
MOE_EXPERTS = MOE_GROUPS * MOE_EXPERTS_PER_GROUP
MOE_HIDDEN = 256

LANES = 128
SUBLANES = 8
BF16_SUBLANES = 16
VMEM_LIMIT = 56 * 1024 * 1024

LAMBDA_INIT = 0.8 - 0.6 * math.exp(-0.3 * 0)

PROJ_ATTN = 0
PROJ_GDN = 3 * DIFF_WIDTH
PROJ_MA = PROJ_GDN + 3 * GDN_WIDTH
PROJ_MB = PROJ_MA + D_MODEL
PROJ_GATE = PROJ_MB + D_MODEL
PROJ_WIDTH = PROJ_GATE + GDN_WIDTH

NEG_INF = float("-inf")
LOG2_E = math.log2(math.e)


def _params(*sem):
    return pltpu.CompilerParams(dimension_semantics=sem, vmem_limit_bytes=VMEM_LIMIT)


def _inproj_kernel(x_ref, g_ref, w_ref, ws_ref, o_ref, os_ref, h_scr):
    @pl.when(pl.program_id(1) == 0)
    def _():
        x = x_ref[...]
        ms = jnp.mean(x * x, axis=-1, keepdims=True)
        h_scr[...] = (x * lax.rsqrt(ms + EPS) * g_ref[...]).astype(BF16)
        os_ref[...] = jnp.dot(h_scr[...], ws_ref[...], preferred_element_type=F32)

    o_ref[...] = jnp.dot(h_scr[...], w_ref[...], preferred_element_type=F32).astype(o_ref.dtype)


def _inproj(x2, gain, w, w_small, tm, tn):
    t, d = x2.shape
    n = w.shape[1]
    return pl.pallas_call(
        _inproj_kernel,
        out_shape=(jax.ShapeDtypeStruct((t, n), BF16), jax.ShapeDtypeStruct((t, LANES), F32)),
        grid=(t // tm, n // tn),
        in_specs=[
            pl.BlockSpec((tm, d), lambda i, j: (i, 0)),
            pl.BlockSpec((1, d), lambda i, j: (0, 0)),
            pl.BlockSpec((d, tn), lambda i, j: (0, j)),
            pl.BlockSpec((d, LANES), lambda i, j: (0, 0)),
        ],
        out_specs=(pl.BlockSpec((tm, tn), lambda i, j: (i, j)), pl.BlockSpec((tm, LANES), lambda i, j: (i, 0))),
        scratch_shapes=[pltpu.VMEM((tm, d), BF16)],
        compiler_params=_params("parallel", "arbitrary"),
        name="inproj",
    )(x2, gain, w, w_small)


ATTN_TQ = 1024
ATTN_SPAN = 2
MASK_BIAS = -1e30
ATTN_V_ROWS = DIFF_V_DIM + BF16_SUBLANES
ATTN_COLS = 128


def _attn_kernel(lam_ref, q_ref, k_ref, v_ref, gain_ref, o_ref,
                 qt_scr, oh_scr, vt_scr, s0_scr, s1_scr, p0_scr, p1_scr, a0_scr, a1_scr, m_scr, acc_scr, *, tq, tk):
    i = pl.program_id(2)
    n_q = 2 * tq
    span = ATTN_SPAN
    n_slots = tk // CHUNK

    n_blk = n_q // ATTN_COLS
    half = tq // ATTN_COLS

    @pl.when(i == 0)
    def _():
        for c in range(v_ref.shape[1] // tk):
            v_t = jnp.transpose(v_ref[0, c * tk:(c + 1) * tk, :].astype(F32)).astype(BF16)
            for j in range(tk // LANES):
                vt_scr[c, j] = jnp.concatenate(
                    [v_t[:, j * LANES:(j + 1) * LANES], jnp.ones((BF16_SUBLANES, LANES), BF16)], axis=0)
        slot = lax.broadcasted_iota(jnp.int32, (LANES, ATTN_COLS), 0)
        lane = lax.broadcasted_iota(jnp.int32, (LANES, ATTN_COLS), 1)
        for ver in range(span + 1):
            for c in range(n_blk):
                rel = (lane + (c % half) * ATTN_COLS) // CHUNK - (ver - 1) * n_slots
                if ver == 0:
                    rel = rel * 0 + 2 * n_slots
                want = jnp.where(rel < 0, n_slots, jnp.where(rel < n_slots, rel, -1))
                oh_scr[ver, c] = jnp.where(slot == want, 1.0, 0.0).astype(BF16)

    q_t = jnp.transpose(q_ref[0].astype(F32) * (DIFF_HEAD_DIM ** -0.5 * LOG2_E))
    dim = lax.broadcasted_iota(jnp.int32, (LANES, tq), 0)
    q_comp = (jnp.where(dim < DIFF_HEAD_DIM, q_t, 0.0).astype(BF16),
              jnp.where(dim >= DIFF_HEAD_DIM, q_t, 0.0).astype(BF16))
    for c in range(n_blk):
        qt_scr[c] = q_comp[c // half][:, (c % half) * ATTN_COLS:(c % half + 1) * ATTN_COLS]

    k_lane = lax.broadcasted_iota(jnp.int32, (tk, LANES), 1)
    k_chunk = lax.broadcasted_iota(jnp.int32, (tk, LANES), 0) // CHUNK
    bias = jnp.where((k_lane <= n_slots) & ((k_chunk > k_lane) | (k_lane == n_slots)), MASK_BIAS, 0.0).astype(BF16)

    m_scr[...] = jnp.full(m_scr.shape, NEG_INF, F32)
    acc_scr[...] = jnp.zeros(acc_scr.shape, F32)

    all_blocks = tuple(range(n_blk))
    late_blocks = tuple(c for c in all_blocks if (c % half) * ATTN_COLS >= tk)

    def scores(t, s_scr, blocks=all_blocks):
        ver = jnp.clip(t - span * i + 1, 0, span)
        rows = pl.ds(pl.multiple_of(t * tk, tk), tk)
        k_aug = jnp.concatenate([k_ref[0, rows, :], bias], axis=1)
        q_all = jnp.concatenate(
            [jnp.concatenate([qt_scr[c], oh_scr[ver, c]], axis=0) for c in blocks], axis=1)
        s = jnp.dot(k_aug, q_all, preferred_element_type=F32)
        for n, c in enumerate(blocks):
            s_scr[c] = s[:, n * ATTN_COLS:(n + 1) * ATTN_COLS]

    def softmax(s_scr, p_scr, a_scr, blocks=all_blocks):
        for c in blocks:
            cs = slice(c * ATTN_COLS, (c + 1) * ATTN_COLS)
            s = s_scr[c]
            m_prev = m_scr[:, cs]
            m_new = jnp.maximum(m_prev, jnp.max(s, axis=0, keepdims=True))
            a_scr[:, cs] = jnp.exp2(m_prev - m_new)
            m_scr[:, cs] = m_new
            p_scr[c] = jnp.exp2(s - m_new).astype(BF16)

    def values(t, p_scr, a_scr, blocks=all_blocks):
        v_all = jnp.concatenate([vt_scr[t, j] for j in range(tk // LANES)], axis=1)
        p = jnp.concatenate([p_scr[c] for c in blocks], axis=1)
        pv = jnp.dot(v_all, p, preferred_element_type=F32)
        for n, c in enumerate(blocks):
            cs = slice(c * ATTN_COLS, (c + 1) * ATTN_COLS)
            acc_scr[c] = a_scr[:, cs] * acc_scr[c] + pv[:, n * ATTN_COLS:(n + 1) * ATTN_COLS]

    scores(0, s0_scr)
    scores(1, s1_scr)
    softmax(s0_scr, p0_scr, a0_scr)

    def pair_step(g, carry):
        t = 2 * g + 1
        scores(t + 1, s0_scr)
        softmax(s1_scr, p1_scr, a1_scr)
        values(t - 1, p0_scr, a0_scr)
        scores(t + 2, s1_scr)
        softmax(s0_scr, p0_scr, a0_scr)
        values(t, p1_scr, a1_scr)
        return carry

    lax.fori_loop(0, i, pair_step, 0)
    t_last = span * i + 1
    softmax(s1_scr, p1_scr, a1_scr, late_blocks)
    values(t_last - 1, p0_scr, a0_scr)
    values(t_last, p1_scr, a1_scr, late_blocks)

    lp = lam_ref[...]
    lam = (jnp.exp(jnp.sum(lp[0:1] * lp[1:2], axis=-1, keepdims=True))
           - jnp.exp(jnp.sum(lp[2:3] * lp[3:4], axis=-1, keepdims=True)) + LAMBDA_INIT)
    for c in range(half):
        o0_t = acc_scr[c, 0:DIFF_V_DIM, :] / acc_scr[c, DIFF_V_DIM:DIFF_V_DIM + 1, :]
        o1_t = acc_scr[half + c, 0:DIFF_V_DIM, :] / acc_scr[half + c, DIFF_V_DIM:DIFF_V_DIM + 1, :]
        o = jnp.transpose(o0_t - lam * o1_t)
        ms = jnp.mean(o * o, axis=-1, keepdims=True)
        y = (o * lax.rsqrt(ms + EPS) * gain_ref[...]) * (1.0 - LAMBDA_INIT)
        o_ref[0, c * ATTN_COLS:(c + 1) * ATTN_COLS, :] = y.astype(o_ref.dtype)


def _diff_attention(lam_params, qkv, subln_gain):
    b, s, _ = qkv.shape
    tq = min(ATTN_TQ, s)
    tk = tq // ATTN_SPAN
    assert s % tq == 0 and ATTN_SPAN == 2 and tk % LANES == 0 and tk // CHUNK < LANES
    n_q = 2 * tq
    return pl.pallas_call(
        functools.partial(_attn_kernel, tq=tq, tk=tk),
        out_shape=jax.ShapeDtypeStruct((b, s, DIFF_WIDTH), BF16),
        grid=(b, DIFF_HEADS, s // tq),
        in_specs=[
            pl.BlockSpec((4, DIFF_HEAD_DIM), lambda bi, h, i: (0, 0)),
            pl.BlockSpec((1, tq, DIFF_V_DIM), lambda bi, h, i: (bi, i, h)),
            pl.BlockSpec((1, s, DIFF_V_DIM), lambda bi, h, i: (bi, 0, DIFF_HEADS + h)),
            pl.BlockSpec((1, s, DIFF_V_DIM), lambda bi, h, i: (bi, 0, 2 * DIFF_HEADS + h)),
            pl.BlockSpec((1, DIFF_V_DIM), lambda bi, h, i: (0, 0)),
        ],
        out_specs=pl.BlockSpec((1, tq, DIFF_V_DIM), lambda bi, h, i: (bi, i, h)),
        scratch_shapes=[
            pltpu.VMEM((n_q // ATTN_COLS, LANES, ATTN_COLS), BF16),
            pltpu.VMEM((ATTN_SPAN + 1, n_q // ATTN_COLS, LANES, ATTN_COLS), BF16),
            pltpu.VMEM((s // tk, tk // LANES, ATTN_V_ROWS, LANES), BF16),
            pltpu.VMEM((n_q // ATTN_COLS, tk, ATTN_COLS), F32),
            pltpu.VMEM((n_q // ATTN_COLS, tk, ATTN_COLS), F32),
            pltpu.VMEM((n_q // ATTN_COLS, tk, ATTN_COLS), BF16),
            pltpu.VMEM((n_q // ATTN_COLS, tk, ATTN_COLS), BF16),
            pltpu.VMEM((1, n_q), F32),
            pltpu.VMEM((1, n_q), F32),
            pltpu.VMEM((1, n_q), F32),
            pltpu.VMEM((n_q // ATTN_COLS, ATTN_V_ROWS, ATTN_COLS), F32),
        ],
        compiler_params=_params("parallel", "parallel", "arbitrary"),
        name="diff_attn",
    )(lam_params, qkv, qkv, qkv, subln_gain)


def _silu(x):
    return x * (1.0 / (1.0 + jnp.exp(-x)))


def _sigmoid(x):
    return 1.0 / (1.0 + jnp.exp(-x))


def _softplus(x):
    return jnp.maximum(x, 0.0) + jnp.log(1.0 + jnp.exp(-jnp.abs(x)))


def _split_bf16(x):
    hi = x.astype(BF16)
    return hi, (x - hi.astype(F32)).astype(BF16)


def _dot_bf16x3(a_parts, b_parts):
    (a_hi, a_lo), (b_hi, b_lo) = a_parts, b_parts
    d = lambda x, y: jnp.dot(x, y, preferred_element_type=F32)
    return d(a_hi, b_hi) + (d(a_lo, b_hi) + d(a_hi, b_lo))


def _dot_exact_lhs(lhs, x):
    x1 = x.astype(BF16)
    r1 = x - x1.astype(F32)
    x2 = r1.astype(BF16)
    x3 = (r1 - x2.astype(F32)).astype(BF16)
    d = lambda y: jnp.dot(lhs, y, preferred_element_type=F32)
    return d(x1) + (d(x2) + d(x3))


def _dot(a, b):
    return jnp.dot(a.astype(BF16), b.astype(BF16), preferred_element_type=F32)


def _dot_nt(a, b):
    return lax.dot_general(a.astype(BF16), b.astype(BF16), (((1,), (1,)), ((), ())), preferred_element_type=F32)


GDN_UNIT = 2 * CHUNK


def _gdn_kernel(prev_ref, qkv_ref, gate_ref, small_ref, convw_ref, alog_ref, dtb_ref, ngain_ref, o_ref,
                xp_scr, q_scr, k_scr, v_scr, pq_scr, n_scr, oc_scr, gl_scr, oraw_scr, state_scr, *, tb):
    i = pl.program_id(1)

    @pl.when(i == 0)
    def _():
        state_scr[...] = jnp.zeros(state_scr.shape, F32)

    n_sec = 3 * GDN_HEADS
    prev = jnp.where(i == 0, 0.0, prev_ref[0, BF16_SUBLANES - SUBLANES:BF16_SUBLANES, :].astype(F32))
    for sec in range(n_sec):
        xp_scr[sec, 0:SUBLANES, :] = prev[:, sec * LANES:(sec + 1) * LANES]
    for r0 in range(0, tb, GDN_UNIT):
        blk = qkv_ref[0, r0:r0 + GDN_UNIT, :].astype(F32)
        for sec in range(n_sec):
            xp_scr[sec, SUBLANES + r0:SUBLANES + r0 + GDN_UNIT, :] = blk[:, sec * LANES:(sec + 1) * LANES]
    for sec in range(n_sec):
        cols = slice(sec * LANES, (sec + 1) * LANES)
        acc = None
        for jj in range(CONV_K):
            start = SUBLANES - (CONV_K - 1) + jj
            term = xp_scr[sec, start:start + tb, :] * convw_ref[jj:jj + 1, cols]
            acc = term if acc is None else acc + term
        y = _silu(acc)
        which, head = divmod(sec, GDN_HEADS)
        if which == 0:
            q_scr[head] = y * lax.rsqrt(jnp.sum(y * y, axis=-1, keepdims=True) + EPS) * (GDN_DIM ** -0.5)
        elif which == 1:
            k_scr[head] = y * lax.rsqrt(jnp.sum(y * y, axis=-1, keepdims=True) + EPS)
        else:
            v_scr[head] = y

    sm = small_ref[0]
    beta_all = _sigmoid(sm)
    g_all = -jnp.exp(alog_ref[...]) * _softplus(sm + dtb_ref[...])

    br = lax.broadcasted_iota(jnp.int32, (tb, tb), 0)
    bc = lax.broadcasted_iota(jnp.int32, (tb, tb), 1)
    block_tril = jnp.where((br // CHUNK == bc // CHUNK) & (bc <= br), 1.0, 0.0).astype(BF16)
    gc_all = _dot_exact_lhs(block_tril, g_all)
    gct_all = jnp.transpose(gc_all)

    ur = lax.broadcasted_iota(jnp.int32, (GDN_UNIT, GDN_UNIT), 0)
    uc = lax.broadcasted_iota(jnp.int32, (GDN_UNIT, GDN_UNIT), 1)
    same = (ur // CHUNK) == (uc // CHUNK)
    causal = same & (uc <= ur)
    strict = same & (uc < ur)
    eye = jnp.where(ur == uc, 1.0, 0.0)
    first_rows = ur < CHUNK
    first_cols = uc < CHUNK
    n_units = tb // GDN_UNIT
    units = [(h, u) for h in range(GDN_HEADS) for u in range(n_units)]

    def lane_bcast(x, lane):
        return jnp.broadcast_to(x[:, lane:lane + 1], (GDN_UNIT, GDN_UNIT))

    def row_bcast(x, r):
        return jnp.broadcast_to(x[r:r + 1, :], (GDN_UNIT, GDN_UNIT))

    def rows_of(u):
        return slice(u * GDN_UNIT, (u + 1) * GDN_UNIT)

    def cols_of(h):
        return slice(h * LANES, (h + 1) * LANES)

    qs = [q_scr[h, rows_of(u), :] for h, u in units]
    ks = [k_scr[h, rows_of(u), :] for h, u in units]
    vs = [v_scr[h, rows_of(u), :] for h, u in units]
    betas = [lane_bcast(beta_all[rows_of(u)], h) for h, u in units]
    gcb = [lane_bcast(gc_all[rows_of(u)], GDN_HEADS + h) for h, u in units]
    gcr = [row_bcast(gct_all[:, rows_of(u)], GDN_HEADS + h) for h, u in units]
    decay = [jnp.exp(jnp.where(causal, a - b, NEG_INF)) for a, b in zip(gcb, gcr)]
    eg = [jnp.exp(a) for a in gcb]
    g_last = [jnp.where(first_rows, row_bcast(a, CHUNK - 1), row_bcast(a, GDN_UNIT - 1)) for a in gcb]
    kb = [k * b for k, b in zip(ks, betas)]
    vb = [v * b for v, b in zip(vs, betas)]
    a_low = [jnp.where(strict, _dot_nt(x, k) * d, 0.0) for x, k, d in zip(kb, ks, decay)]
    tinv = [eye - a for a in a_low]
    pw = a_low
    for _ in range(5):
        pw = [_dot(x, x) for x in pw]
        tinv = [t + _dot(t, x) for t, x in zip(tinv, pw)]
    uw = [_dot(t, jnp.concatenate([v, x * e], axis=1)) for t, v, x, e in zip(tinv, vb, kb, eg)]
    intra = [_dot_nt(q, k) * d for q, k, d in zip(qs, ks, decay)]
    iuw = [_dot(a, x) for a, x in zip(intra, uw)]
    k_dec_t = [jnp.transpose(k * jnp.exp(gl - a)) for k, gl, a in zip(ks, g_last, gcb)]
    wu = [jnp.concatenate([x[:, GDN_DIM:], x[:, :GDN_DIM]], axis=1) for x in uw]
    pn0 = [_dot(jnp.where(first_cols, kt, 0.0), x) for kt, x in zip(k_dec_t, wu)]
    pn1 = [_dot(jnp.where(first_cols, 0.0, kt), x) for kt, x in zip(k_dec_t, wu)]
    for n in range(len(units)):
        q_eff = (qs[n] * eg[n] - iuw[n][:, GDN_DIM:]).astype(BF16)
        for half, pn in enumerate((pn0[n], pn1[n])):
            hr = slice(half * CHUNK, (half + 1) * CHUNK)
            pq_scr[n, half, 0:GDN_DIM, :] = pn[:, :GDN_DIM].astype(BF16)
            pq_scr[n, half, GDN_DIM:GDN_DIM + CHUNK, :] = q_eff[hr]
            n_scr[n, half] = pn[:, GDN_DIM:]
            oc_scr[n, half] = iuw[n][hr, :GDN_DIM]
            gl_scr[n, half] = jnp.exp(gcb[n][(half + 1) * CHUNK - 1:(half + 1) * CHUNK, :])

    def chunk_body(c, carry):
        u = c // 2
        half = c % 2
        rows = pl.ds(pl.multiple_of(c * CHUNK, CHUNK), CHUNK)
        states = [state_scr[h] for h in range(GDN_HEADS)]
        res = [jnp.dot(pq_scr[h * n_units + u, half], states[h].astype(BF16), preferred_element_type=F32)
               for h in range(GDN_HEADS)]
        for h in range(GDN_HEADS):
            n = h * n_units + u
            state_scr[h] = gl_scr[n, half] * states[h] - res[h][0:GDN_DIM] + n_scr[n, half]
            oraw_scr[h, rows, :] = res[h][GDN_DIM:GDN_DIM + CHUNK] + oc_scr[n, half]
        return carry

    lax.fori_loop(0, tb // CHUNK, chunk_body, 0)

    for h in range(GDN_HEADS):
        o = oraw_scr[h]
        ms = jnp.mean(o * o, axis=-1, keepdims=True)
        y = o * lax.rsqrt(ms + EPS) * ngain_ref[...] * _silu(gate_ref[0, :, cols_of(h)].astype(F32))
        o_ref[0, :, cols_of(h)] = y.astype(o_ref.dtype)


def _gdn(proj, small, conv_w, alog_pad, dtb_pad, norm_gain, tb):
    b, s, _ = proj.shape
    nb = s // tb
    w3 = 3 * GDN_WIDTH
    assert tb % GDN_UNIT == 0
    n_hu = GDN_HEADS * (tb // GDN_UNIT)
    return pl.pallas_call(
        functools.partial(_gdn_kernel, tb=tb),
        out_shape=jax.ShapeDtypeStruct((b, s, GDN_WIDTH), BF16),
        grid=(b, nb),
        in_specs=[
            pl.BlockSpec((1, BF16_SUBLANES, w3),
                         lambda bi, i: (bi, jnp.maximum(i * (tb // BF16_SUBLANES) - 1, 0), PROJ_GDN // w3)),
            pl.BlockSpec((1, tb, w3), lambda bi, i: (bi, i, PROJ_GDN // w3)),
            pl.BlockSpec((1, tb, GDN_WIDTH), lambda bi, i: (bi, i, PROJ_GATE // GDN_WIDTH)),
            pl.BlockSpec((1, tb, LANES), lambda bi, i: (bi, i, 0)),
            pl.BlockSpec((CONV_K, w3), lambda bi, i: (0, 0)),
            pl.BlockSpec((1, LANES), lambda bi, i: (0, 0)),
            pl.BlockSpec((1, LANES), lambda bi, i: (0, 0)),
            pl.BlockSpec((1, GDN_DIM), lambda bi, i: (0, 0)),
        ],
        out_specs=pl.BlockSpec((1, tb, GDN_WIDTH), lambda bi, i: (bi, i, 0)),
        scratch_shapes=[
            pltpu.VMEM((w3 // LANES, tb + SUBLANES, LANES), F32),
            pltpu.VMEM((GDN_HEADS, tb, GDN_DIM), F32),
            pltpu.VMEM((GDN_HEADS, tb, GDN_DIM), F32),
            pltpu.VMEM((GDN_HEADS, tb, GDN_DIM), F32),
            pltpu.VMEM((n_hu, 2, GDN_DIM + CHUNK, GDN_DIM), BF16),
            pltpu.VMEM((n_hu, 2, GDN_DIM, GDN_DIM), F32),
            pltpu.VMEM((n_hu, 2, CHUNK, GDN_DIM), F32),
            pltpu.VMEM((n_hu, 2, 1, GDN_DIM), F32),
            pltpu.VMEM((GDN_HEADS, tb, GDN_DIM), F32),
            pltpu.VMEM((GDN_HEADS, GDN_DIM, GDN_DIM), F32),
        ],
        compiler_params=_params("parallel", "arbitrary"),
        name="gdn",
    )(proj, proj, proj, small, conv_w, alog_pad, dtb_pad, norm_gain)


MERGE_SUB = 256

ROW_PIECES = D_MODEL // LANES


def _store_tile_rows(ref, r0, value):
    n = value.shape[0]
    for j in range(ROW_PIECES):
        ref[pl.ds(r0 * ROW_PIECES + j, n, stride=ROW_PIECES), :] = value[:, j * LANES:(j + 1) * LANES]


def _load_tile_rows(ref, r0, n):
    return jnp.concatenate(
        [ref[pl.ds(r0 * ROW_PIECES + j, n, stride=ROW_PIECES), :] for j in range(ROW_PIECES)], axis=1)


def _merge_kernel(x_ref, ya_ref, yb_ref, ma_ref, mb_ref, wa_ref, wb_ref, wo_ref, g2_ref, wr_ref, br_ref,
                  x1_ref, h2_ref, route_ref, cnt_ref):
    tm = x_ref.shape[0]
    subs = [slice(r, r + MERGE_SUB) for r in range(0, tm, MERGE_SUB)]
    pa = [jnp.dot(ya_ref[rs, :], wa_ref[...], preferred_element_type=F32) for rs in subs]
    pb = [jnp.dot(yb_ref[rs, :], wb_ref[...], preferred_element_type=F32) for rs in subs]
    merged = [(_sigmoid(ma_ref[rs, :].astype(F32)) * a + _sigmoid(mb_ref[rs, :].astype(F32)) * b).astype(BF16)
              for rs, a, b in zip(subs, pa, pb)]
    x1s = [x_ref[rs, :] + jnp.dot(m, wo_ref[...], preferred_element_type=F32) for rs, m in zip(subs, merged)]
    h2s = [v * lax.rsqrt(jnp.mean(v * v, axis=-1, keepdims=True) + EPS) * g2_ref[...] for v in x1s]
    w_parts = _split_bf16(wr_ref[...])
    logit_s = [_dot_bf16x3(_split_bf16(v), w_parts) for v in h2s]
    for rs, v, hh in zip(subs, x1s, h2s):
        x1_ref[rs, :] = v
        _store_tile_rows(h2_ref, rs.start, hh)
    logits = jnp.concatenate(logit_s, axis=0) + br_ref[...]
    lane = lax.broadcasted_iota(jnp.int32, logits.shape, 1)
    big = jnp.int32(4 * LANES)
    gl = jnp.where(lane < MOE_GROUPS, logits, NEG_INF)
    gmax = jnp.max(gl, axis=-1, keepdims=True)
    gidx = jnp.min(jnp.where(gl == gmax, lane, big), axis=-1, keepdims=True)
    grp_w = 1.0 / jnp.sum(jnp.exp(gl - gmax), axis=-1, keepdims=True)
    lo = MOE_GROUPS + gidx * MOE_EXPERTS_PER_GROUP
    el = jnp.where((lane >= lo) & (lane < lo + MOE_EXPERTS_PER_GROUP), logits, NEG_INF)
    v1 = jnp.max(el, axis=-1, keepdims=True)
    i1 = jnp.min(jnp.where(el == v1, lane, big), axis=-1, keepdims=True)
    el2 = jnp.where(lane == i1, NEG_INF, el)
    v2 = jnp.max(el2, axis=-1, keepdims=True)
    i2 = jnp.min(jnp.where(el2 == v2, lane, big), axis=-1, keepdims=True)
    e2 = jnp.exp(v2 - v1)
    w1 = grp_w / (1.0 + e2)
    w2 = w1 * e2
    id1 = i1 - MOE_GROUPS
    id2 = i2 - MOE_GROUPS
    route_ref[...] = jnp.where(lane == 0, id1.astype(F32), jnp.where(lane == 1, id2.astype(F32),
                               jnp.where(lane == 2, w1, jnp.where(lane == 3, w2, 0.0))))

    @pl.when(pl.program_id(0) == 0)
    def _():
        cnt_ref[...] = jnp.zeros(cnt_ref.shape, F32)

    hits = jnp.where((lane == id1) | (lane == id2), 1.0, 0.0)
    cnt_ref[...] += jnp.broadcast_to(jnp.sum(hits, axis=0, keepdims=True), cnt_ref.shape)


def _merge(x2, ya, yb, proj, wa, wb, wo, g2, wr, br, tm):
    t, d = x2.shape
    row = lambda i: (i, 0)
    const = lambda i: (0, 0)
    return pl.pallas_call(
        _merge_kernel,
        out_shape=(
            jax.ShapeDtypeStruct((t, d), F32),
            jax.ShapeDtypeStruct((t * ROW_PIECES, LANES), F32),
            jax.ShapeDtypeStruct((t, LANES), F32),
            jax.ShapeDtypeStruct((SUBLANES, LANES), F32),
        ),
        grid=(t // tm,),
        in_specs=[
            pl.BlockSpec((tm, d), row),
            pl.BlockSpec((tm, DIFF_WIDTH), row),
            pl.BlockSpec((tm, GDN_WIDTH), row),
            pl.BlockSpec((tm, d), lambda i: (i, PROJ_MA // D_MODEL)),
            pl.BlockSpec((tm, d), lambda i: (i, PROJ_MB // D_MODEL)),
            pl.BlockSpec((DIFF_WIDTH, d), const),
            pl.BlockSpec((GDN_WIDTH, d), const),
            pl.BlockSpec((d, d), const),
            pl.BlockSpec((1, d), const),
            pl.BlockSpec((d, LANES), const),
            pl.BlockSpec((1, LANES), const),
        ],
        out_specs=(pl.BlockSpec((tm, d), row), pl.BlockSpec((tm * ROW_PIECES, LANES), row),
                   pl.BlockSpec((tm, LANES), row), pl.BlockSpec((SUBLANES, LANES), const)),
        compiler_params=_params("arbitrary"),
        name="merge_router",
    )(x2, ya, yb, proj, proj, wa, wb, wo, g2, wr, br)


MOE_BLK = 512
MOE_SUB = 256
MOE_META_LANES = 256
MOE_ROUTE_TILE = 512
MOE_SCATTER_TILE = 1024
MOE_GATHER_TILE = 512


def _route_kernel(route_ref, cnt_ref, pos_ref, meta_ref, run_scr, *, n_rows):
    i = pl.program_id(0)
    tp = route_ref.shape[0]

    @pl.when(i == 0)
    def _():
        cnt = cnt_ref[...]
        jr = lax.broadcasted_iota(jnp.int32, (LANES, LANES), 0)
        jc = lax.broadcasted_iota(jnp.int32, (LANES, LANES), 1)
        upper = jnp.where(jr < jc, 1.0, 0.0)
        hi_prec = dict(preferred_element_type=F32, precision=lax.Precision.HIGHEST)
        off = jnp.dot(cnt, upper, **hi_prec)
        run_scr[...] = off[0:1]
        blk = float(MOE_BLK)
        first_tile = jnp.floor(off / blk)
        last_tile = jnp.floor((off + cnt - 1.0) / blk)
        n_it = jnp.where(cnt > 0.0, last_tile - first_tile + 1.0, 0.0)
        it_start = jnp.dot(n_it, upper, **hi_prec)
        it_end = it_start + n_it
        lane8 = lax.broadcasted_iota(jnp.int32, cnt.shape, 1)
        e_max = jnp.max(jnp.where(cnt > 0.0, lane8, 0), axis=-1, keepdims=True).astype(F32)[0:1]
        sub8 = lax.broadcasted_iota(jnp.int32, cnt.shape, 0)
        table = jnp.where(sub8 == 0, first_tile, jnp.where(sub8 == 1, it_start, jnp.where(
            sub8 == 2, off, jnp.where(sub8 == 3, cnt, it_end))))
        cols = jnp.transpose(table)
        shape = (LANES, MOE_META_LANES)
        e_sub = lax.broadcasted_iota(jnp.int32, shape, 0)
        w_lane = lax.broadcasted_iota(jnp.int32, shape, 1).astype(F32)
        col = lambda k: jnp.broadcast_to(cols[:, k:k + 1], shape)
        e_w = jnp.sum(jnp.where((e_sub < MOE_EXPERTS) & (col(4) <= w_lane), 1.0, 0.0), axis=0, keepdims=True)
        valid = e_w < float(MOE_EXPERTS)
        e_w = jnp.minimum(e_w, e_max)
        sel = e_sub.astype(F32) == e_w
        pick = lambda k: jnp.sum(jnp.where(sel, col(k), 0.0), axis=0, keepdims=True)
        w_row = w_lane[0:1]
        tile_w = jnp.where(valid, pick(0) + (w_row - pick(1)), float(n_rows // MOE_BLK - 1))
        lo_w = jnp.maximum(pick(2) - tile_w * blk, 0.0)
        hi_w = jnp.minimum(pick(2) + pick(3) - tile_w * blk, blk)
        lo_w = jnp.where(valid, lo_w, 0.0)
        hi_w = jnp.where(valid, hi_w, 0.0)
        sub_m = lax.broadcasted_iota(jnp.int32, meta_ref.shape, 0)
        bc = lambda v: jnp.broadcast_to(v, meta_ref.shape)
        meta_ref[...] = jnp.where(sub_m == 0, bc(e_w), jnp.where(sub_m == 1, bc(tile_w), jnp.where(
            sub_m == 2, bc(lo_w), bc(hi_w)))).astype(jnp.int32)

    r = route_ref[...]
    lane = lax.broadcasted_iota(jnp.int32, r.shape, 1)
    lane_f = lane.astype(F32)
    oh1 = lane_f == r[:, 0:1]
    oh2 = lane_f == r[:, 1:2]
    hits = jnp.where(oh1 | oh2, 1.0, 0.0)
    tr = lax.broadcasted_iota(jnp.int32, (tp, tp), 0)
    tc = lax.broadcasted_iota(jnp.int32, (tp, tp), 1)
    earlier = jnp.where(tc < tr, 1.0, 0.0).astype(BF16)
    rank = jnp.dot(earlier, hits.astype(BF16), preferred_element_type=F32)
    base = run_scr[...] + rank
    p1 = jnp.sum(jnp.where(oh1, base, 0.0), axis=-1, keepdims=True)
    p2 = jnp.sum(jnp.where(oh2, base, 0.0), axis=-1, keepdims=True)
    pos_ref[...] = jnp.where(lane == 0, p1, jnp.where(lane == 1, p2, 0.0)).astype(jnp.int32)
    run_scr[...] += jnp.sum(hits, axis=0, keepdims=True)


def _route_positions(route, cnt):
    t = route.shape[0]
    tp = min(MOE_ROUTE_TILE, t)
    n_rows = 2 * t
    assert n_rows % MOE_BLK == 0 and n_rows // MOE_BLK + MOE_EXPERTS <= MOE_META_LANES and n_rows < 2 ** 24
    return pl.pallas_call(
        functools.partial(_route_kernel, n_rows=n_rows),
        out_shape=(jax.ShapeDtypeStruct((t, LANES), jnp.int32),
                   jax.ShapeDtypeStruct((SUBLANES, MOE_META_LANES), jnp.int32)),
        grid=(t // tp,),
        in_specs=[pl.BlockSpec((tp, LANES), lambda i: (i, 0)),
                  pl.BlockSpec((SUBLANES, LANES), lambda i: (0, 0))],
        out_specs=(pl.BlockSpec((tp, LANES), lambda i: (i, 0)),
                   pl.BlockSpec((SUBLANES, MOE_META_LANES), lambda i: (0, 0))),
        scratch_shapes=[pltpu.VMEM((1, LANES), F32)],
        compiler_params=_params("arbitrary"),
        name="route_positions",
    )(route, cnt)


def _scatter_kernel(pos_ref, h2_ref, xs_ref, sem):
    ts = h2_ref.shape[0] // ROW_PIECES

    def body(tok, carry):
        src = h2_ref.at[pl.ds(pl.multiple_of(tok * ROW_PIECES, ROW_PIECES), ROW_PIECES), :]
        for k in range(2):
            dst = xs_ref.at[pl.ds(pl.multiple_of(pos_ref[0, 0, 2 * tok + k], ROW_PIECES), ROW_PIECES), :]
            pltpu.make_async_copy(src, dst, sem).start(priority=k)
        return carry

    lax.fori_loop(0, ts, body, 0, unroll=8)
    for _ in range(2):
        pltpu.make_async_copy(h2_ref, xs_ref.at[pl.ds(0, ts * ROW_PIECES), :], sem).wait()


def _scatter_rows(pos3, h2t):
    t = h2t.shape[0] // ROW_PIECES
    ts = pos3.shape[2] // 2
    return pl.pallas_call(
        _scatter_kernel,
        out_shape=jax.ShapeDtypeStruct((2 * t * ROW_PIECES, LANES), F32),
        grid=(t // ts,),
        in_specs=[pl.BlockSpec((1, 1, 2 * ts), lambda i: (i, 0, 0), memory_space=pltpu.SMEM),
                  pl.BlockSpec((ts * ROW_PIECES, LANES), lambda i: (i, 0))],
        out_specs=pl.BlockSpec(memory_space=pl.ANY),
        scratch_shapes=[pltpu.SemaphoreType.DMA],
        compiler_params=pltpu.CompilerParams(dimension_semantics=("arbitrary",), vmem_limit_bytes=VMEM_LIMIT,
                                             disable_bounds_checks=True),
        name="scatter_rows",
    )(pos3, h2t)


def _expert_kernel(ie_ref, it_ref, lo_ref, hi_ref, xs_ref, wg_ref, wu_ref, wd_ref, y_ref,
                   wg_scr, wu_scr, wd_scr, acc_scr):
    w = pl.program_id(0)
    n_items = pl.num_programs(0)
    prev = jnp.maximum(w - 1, 0)
    nxt = jnp.minimum(w + 1, n_items - 1)

    @pl.when((w == 0) | (ie_ref[w] != ie_ref[prev]))
    def _():
        wg_scr[...] = wg_ref[0].astype(BF16)
        wu_scr[...] = wu_ref[0].astype(BF16)
        wd_scr[...] = wd_ref[0].astype(BF16)

    first = (w == 0) | (it_ref[w] != it_ref[prev])
    last = (w == n_items - 1) | (it_ref[nxt] != it_ref[w])
    only = first & last

    @pl.when(first & jnp.logical_not(only))
    def _():
        acc_scr[...] = jnp.zeros(acc_scr.shape, F32)

    lo = lo_ref[w]
    hi = hi_ref[w]
    subs = [slice(r, r + MOE_SUB) for r in range(0, acc_scr.shape[0], MOE_SUB)]

    @pl.when(hi > lo)
    def _():
        xs = [_load_tile_rows(xs_ref, rs.start, MOE_SUB).astype(BF16) for rs in subs]
        hg = [jnp.dot(x, wg_scr[...], preferred_element_type=F32) for x in xs]
        hu = [jnp.dot(x, wu_scr[...], preferred_element_type=F32) for x in xs]
        act = [(_silu(g) * u).astype(BF16) for g, u in zip(hg, hu)]
        yp = [jnp.dot(a, wd_scr[...], preferred_element_type=F32) for a in act]

        @pl.when(only)
        def _():
            for rs, v in zip(subs, yp):
                _store_tile_rows(y_ref, rs.start, v)

        @pl.when(jnp.logical_not(only))
        def _():
            for rs, v in zip(subs, yp):
                row = rs.start + lax.broadcasted_iota(jnp.int32, v.shape, 0)
                acc_scr[rs, :] += jnp.where((row >= lo) & (row < hi), v, 0.0)

    @pl.when(last & jnp.logical_not(only))
    def _():
        for rs in subs:
            _store_tile_rows(y_ref, rs.start, acc_scr[rs, :])


def _experts(meta, xs, wg, wu, wd):
    n_rows = xs.shape[0] // ROW_PIECES
    d = wg.shape[1]
    n_items = n_rows // MOE_BLK + MOE_EXPERTS
    ie, it, lo, hi = (meta[k, :n_items] for k in range(4))
    return pl.pallas_call(
        _expert_kernel,
        out_shape=jax.ShapeDtypeStruct((n_rows * ROW_PIECES, LANES), F32),
        grid_spec=pltpu.PrefetchScalarGridSpec(
            num_scalar_prefetch=4,
            grid=(n_items,),
            in_specs=[
                pl.BlockSpec((MOE_BLK * ROW_PIECES, LANES), lambda w, ie, it, lo, hi: (it[w], 0)),
                pl.BlockSpec((1, d, MOE_HIDDEN), lambda w, ie, it, lo, hi: (ie[w], 0, 0)),
                pl.BlockSpec((1, d, MOE_HIDDEN), lambda w, ie, it, lo, hi: (ie[w], 0, 0)),
                pl.BlockSpec((1, MOE_HIDDEN, d), lambda w, ie, it, lo, hi: (ie[w], 0, 0)),
            ],
            out_specs=pl.BlockSpec((MOE_BLK * ROW_PIECES, LANES), lambda w, ie, it, lo, hi: (it[w], 0)),
            scratch_shapes=[pltpu.VMEM((d, MOE_HIDDEN), BF16), pltpu.VMEM((d, MOE_HIDDEN), BF16),
                            pltpu.VMEM((MOE_HIDDEN, d), BF16), pltpu.VMEM((MOE_BLK, d), F32)],
        ),
        compiler_params=_params("arbitrary"),
        name="experts",
    )(ie, it, lo, hi, xs, wg, wu, wd)


def _combine_kernel(pos_ref, posn_ref, x1_ref, route_ref, gf_ref, y_ref, o_ref, ybuf, sem):
    i = pl.program_id(0)
    n = pl.num_programs(0)
    tc = x1_ref.shape[0]
    slot = i % 2

    def issue(p_ref, s):
        def body(tok, carry):
            for k in range(2):
                src = y_ref.at[pl.ds(pl.multiple_of(p_ref[0, 0, 2 * tok + k], ROW_PIECES), ROW_PIECES), :]
                dst = ybuf.at[s, k, pl.ds(pl.multiple_of(tok * ROW_PIECES, ROW_PIECES), ROW_PIECES), :]
                pltpu.make_async_copy(src, dst, sem.at[s]).start(priority=k)
            return carry

        lax.fori_loop(0, tc, body, 0, unroll=8)

    @pl.when(i == 0)
    def _():
        issue(pos_ref, 0)

    @pl.when(i + 1 < n)
    def _():
        issue(posn_ref, 1 - slot)

    for k in range(2):
        pltpu.make_async_copy(y_ref.at[pl.ds(0, tc * ROW_PIECES), :], ybuf.at[slot, k], sem.at[slot]).wait()
    r = route_ref[...]
    x2 = (x1_ref[...] + r[:, 2:3] * _load_tile_rows(ybuf.at[slot, 0], 0, tc)
          + r[:, 3:4] * _load_tile_rows(ybuf.at[slot, 1], 0, tc))
    ms = jnp.mean(x2 * x2, axis=-1, keepdims=True)
    o_ref[...] = x2 * lax.rsqrt(ms + EPS) * gf_ref[...]


def _combine(pos3, x1, route, gf, y):
    t, d = x1.shape
    tc = pos3.shape[2] // 2
    n = t // tc
    return pl.pallas_call(
        _combine_kernel,
        out_shape=jax.ShapeDtypeStruct((t, d), F32),
        grid=(n,),
        in_specs=[
            pl.BlockSpec((1, 1, 2 * tc), lambda i: (i, 0, 0), memory_space=pltpu.SMEM),
            pl.BlockSpec((1, 1, 2 * tc), lambda i: (jnp.minimum(i + 1, n - 1), 0, 0), memory_space=pltpu.SMEM),
            pl.BlockSpec((tc, d), lambda i: (i, 0)),
            pl.BlockSpec((tc, LANES), lambda i: (i, 0)),
            pl.BlockSpec((1, d), lambda i: (0, 0)),
            pl.BlockSpec(memory_space=pl.ANY),
        ],
        out_specs=pl.BlockSpec((tc, d), lambda i: (i, 0)),
        scratch_shapes=[pltpu.VMEM((2, 2, tc * ROW_PIECES, LANES), F32), pltpu.SemaphoreType.DMA((2,))],
        compiler_params=pltpu.CompilerParams(dimension_semantics=("arbitrary",), vmem_limit_bytes=VMEM_LIMIT,
                                             disable_bounds_checks=True),
        name="combine_norm",
    )(pos3, pos3, x1, route, gf, y)


def _pad_lanes(v, offset):
    return jnp.zeros((1, LANES), F32).at[0, offset:offset + v.shape[0]].set(v.astype(F32))


def kernel(x, norm_mix_gain, w_in, diff_lambda_q1, diff_lambda_k1, diff_lambda_q2, diff_lambda_k2, diff_subln_gain, gdn_conv_w, gdn_a_log, gdn_dt_bias, gdn_norm_gain, w_branch_attn, w_branch_gdn, w_out, norm_ffn_gain, moe_w_group, moe_b_group, moe_w_expert, moe_b_expert, moe_w_gate, moe_w_up, moe_w_down, norm_final_gain):
    b, s, d = x.shape
    t = b * s
    x2 = x.reshape(t, d)

    w = w_in[0]
    small_lo = PROJ_GDN + 3 * GDN_WIDTH
    small_hi = small_lo + 2 * GDN_HEADS
    gate_hi = small_hi + GDN_WIDTH
    w_main = jnp.concatenate([w[:, :small_lo], w[:, gate_hi:], w[:, small_hi:gate_hi]], axis=1).astype(BF16)
    w_small = jnp.concatenate([w[:, small_lo:small_hi], jnp.zeros((d, LANES - 2 * GDN_HEADS), w.dtype)],
                              axis=1).astype(BF16)
    gain1 = norm_mix_gain[0].reshape(1, d)

    proj2, small2 = _inproj(x2, gain1, w_main, w_small, min(1024, t), PROJ_WIDTH // 2)
    proj = proj2.reshape(b, s, PROJ_WIDTH)

    lam_params = jnp.stack([diff_lambda_q1[0], diff_lambda_k1[0], diff_lambda_q2[0], diff_lambda_k2[0]]).astype(F32)
    ya = _diff_attention(lam_params, proj, diff_subln_gain[0].reshape(1, DIFF_V_DIM))

    yb = _gdn(proj, small2.reshape(b, s, LANES), gdn_conv_w[0], _pad_lanes(gdn_a_log[0], GDN_HEADS),
              _pad_lanes(gdn_dt_bias[0], GDN_HEADS), gdn_norm_gain[0].reshape(1, GDN_DIM), tb=min(512, s))

    wr = jnp.concatenate([moe_w_group[0], moe_w_expert[0],
                          jnp.zeros((d, LANES - MOE_GROUPS - MOE_EXPERTS), F32)], axis=1)
    br = _pad_lanes(jnp.concatenate([moe_b_group[0], moe_b_expert[0]]), 0)
    x1, h2, route, cnt = _merge(x2, ya.reshape(t, DIFF_WIDTH), yb.reshape(t, GDN_WIDTH), proj2,
                                w_branch_attn[0].astype(BF16), w_branch_gdn[0].astype(BF16),
                                w_out[0].astype(BF16), norm_ffn_gain[0].reshape(1, d), wr, br, tm=min(512, t))

    pos, meta = _route_positions(route, cnt)
    lines = pos[:, :2] * ROW_PIECES
    ts, tc = min(MOE_SCATTER_TILE, t), min(MOE_GATHER_TILE, t)
    xs = _scatter_rows(lines.reshape(t // ts, 1, 2 * ts), h2)
    y = _experts(meta, xs, moe_w_gate[0], moe_w_up[0], moe_w_down[0])
    out = _combine(lines.reshape(t // tc, 1, 2 * tc), x1, route, norm_final_gain.reshape(1, d), y)
    return out.reshape(b, s, d)
```

```python
import functools
import math

import jax
import jax.numpy as jnp
from jax import lax
from jax.experimental import pallas as pl
from jax.experimental.pallas import tpu as pltpu

F32 = jnp.float32
BF16 = jnp.bfloat16

D_MODEL = 1024
CHUNK = 64
EPS = 1e-6

DIFF_HEADS = 4
DIFF_HEAD_DIM = 64
DIFF_V_DIM = 2 * DIFF_HEAD_DIM
DIFF_WIDTH = DIFF_HEADS * DIFF_V_DIM

GDN_HEADS = 4
GDN_DIM = 128
GDN_WIDTH = GDN_HEADS * GDN_DIM
CONV_K = 4

MOE_GROUPS = 4
MOE_EXPERTS_PER_GROUP = 8
MOE_EXPERTS = MOE_GROUPS * MOE_EXPERTS_PER_GROUP
MOE_HIDDEN = 256

LANES = 128
SUBLANES = 8
BF16_SUBLANES = 16
VMEM_LIMIT = 56 * 1024 * 1024

LAMBDA_INIT = 0.8 - 0.6 * math.exp(-0.3 * 0)

PROJ_ATTN = 0
PROJ_GDN = 3 * DIFF_WIDTH
PROJ_MA = PROJ_GDN + 3 * GDN_WIDTH
PROJ_MB = PROJ_MA + D_MODEL
PROJ_GATE = PROJ_MB + D_MODEL
PROJ_WIDTH = PROJ_GATE + GDN_WIDTH

NEG_INF = float("-inf")
LOG2_E = math.log2(math.e)


def _params(*sem):
    return pltpu.CompilerParams(dimension_semantics=sem, vmem_limit_bytes=VMEM_LIMIT)


def _inproj_kernel(x_ref, g_ref, w_ref, ws_ref, o_ref, os_ref, h_scr):
    @pl.when(pl.program_id(1) == 0)
    def _():
        x = x_ref[...]
        ms = jnp.mean(x * x, axis=-1, keepdims=True)
        h_scr[...] = (x * lax.rsqrt(ms + EPS) * g_ref[...]).astype(BF16)
        os_ref[...] = jnp.dot(h_scr[...], ws_ref[...], preferred_element_type=F32)

    o_ref[...] = jnp.dot(h_scr[...], w_ref[...], preferred_element_type=F32).astype(o_ref.dtype)


def _inproj(x2, gain, w, w_small, tm, tn):
    t, d = x2.shape
    n = w.shape[1]
    return pl.pallas_call(
        _inproj_kernel,
        out_shape=(jax.ShapeDtypeStruct((t, n), BF16), jax.ShapeDtypeStruct((t, LANES), F32)),
        grid=(t // tm, n // tn),
        in_specs=[
            pl.BlockSpec((tm, d), lambda i, j: (i, 0)),
            pl.BlockSpec((1, d), lambda i, j: (0, 0)),
            pl.BlockSpec((d, tn), lambda i, j: (0, j)),
            pl.BlockSpec((d, LANES), lambda i, j: (0, 0)),
        ],
        out_specs=(pl.BlockSpec((tm, tn), lambda i, j: (i, j)), pl.BlockSpec((tm, LANES), lambda i, j: (i, 0))),
        scratch_shapes=[pltpu.VMEM((tm, d), BF16)],
        compiler_params=_params("parallel", "arbitrary"),
        name="inproj",
    )(x2, gain, w, w_small)


ATTN_TQ = 1024
ATTN_SPAN = 2
MASK_BIAS = -1e30
ATTN_V_ROWS = DIFF_V_DIM + BF16_SUBLANES
ATTN_COLS = 128


def _attn_kernel(lam_ref, q_ref, k_ref, v_ref, gain_ref, o_ref,
                 qt_scr, oh_scr, vt_scr, s0_scr, s1_scr, p0_scr, p1_scr, a0_scr, a1_scr, m_scr, acc_scr, *, tq, tk):
    i = pl.program_id(2)
    n_q = 2 * tq
    span = ATTN_SPAN
    n_slots = tk // CHUNK

    n_blk = n_q // ATTN_COLS
    half = tq // ATTN_COLS

    @pl.when(i == 0)
    def _():
        for c in range(v_ref.shape[1] // tk):
            v_t = jnp.transpose(v_ref[0, c * tk:(c + 1) * tk, :].astype(F32)).astype(BF16)
            for j in range(tk // LANES):
                vt_scr[c, j] = jnp.concatenate(
                    [v_t[:, j * LANES:(j + 1) * LANES], jnp.ones((BF16_SUBLANES, LANES), BF16)], axis=0)
        slot = lax.broadcasted_iota(jnp.int32, (LANES, ATTN_COLS), 0)
        lane = lax.broadcasted_iota(jnp.int32, (LANES, ATTN_COLS), 1)
        for ver in range(span + 1):
            for c in range(n_blk):
                rel = (lane + (c % half) * ATTN_COLS) // CHUNK - (ver - 1) * n_slots
                if ver == 0:
                    rel = rel * 0 + 2 * n_slots
                want = jnp.where(rel < 0, n_slots, jnp.where(rel < n_slots, rel, -1))
                oh_scr[ver, c] = jnp.where(slot == want, 1.0, 0.0).astype(BF16)

    q_t = jnp.transpose(q_ref[0].astype(F32) * (DIFF_HEAD_DIM ** -0.5 * LOG2_E))
    dim = lax.broadcasted_iota(jnp.int32, (LANES, tq), 0)
    q_comp = (jnp.where(dim < DIFF_HEAD_DIM, q_t, 0.0).astype(BF16),
              jnp.where(dim >= DIFF_HEAD_DIM, q_t, 0.0).astype(BF16))
    for c in range(n_blk):
        qt_scr[c] = q_comp[c // half][:, (c % half) * ATTN_COLS:(c % half + 1) * ATTN_COLS]

    k_lane = lax.broadcasted_iota(jnp.int32, (tk, LANES), 1)
    k_chunk = lax.broadcasted_iota(jnp.int32, (tk, LANES), 0) // CHUNK
    bias = jnp.where((k_lane <= n_slots) & ((k_chunk > k_lane) | (k_lane == n_slots)), MASK_BIAS, 0.0).astype(BF16)

    m_scr[...] = jnp.full(m_scr.shape, NEG_INF, F32)
    acc_scr[...] = jnp.zeros(acc_scr.shape, F32)

    all_blocks = tuple(range(n_blk))
    late_blocks = tuple(c for c in all_blocks if (c % half) * ATTN_COLS >= tk)

    def scores(t, s_scr, blocks=all_blocks):
        ver = jnp.clip(t - span * i + 1, 0, span)
        rows = pl.ds(pl.multiple_of(t * tk, tk), tk)
        k_aug = jnp.concatenate([k_ref[0, rows, :], bias], axis=1)
        q_all = jnp.concatenate(
            [jnp.concatenate([qt_scr[c], oh_scr[ver, c]], axis=0) for c in blocks], axis=1)
        s = jnp.dot(k_aug, q_all, preferred_element_type=F32)
        for n, c in enumerate(blocks):
            s_scr[c] = s[:, n * ATTN_COLS:(n + 1) * ATTN_COLS]

    def softmax(s_scr, p_scr, a_scr, blocks=all_blocks):
        for c in blocks:
            cs = slice(c * ATTN_COLS, (c + 1) * ATTN_COLS)
            s = s_scr[c]
            m_prev = m_scr[:, cs]
            m_new = jnp.maximum(m_prev, jnp.max(s, axis=0, keepdims=True))
            a_scr[:, cs] = jnp.exp2(m_prev - m_new)
            m_scr[:, cs] = m_new
            p_scr[c] = jnp.exp2(s - m_new).astype(BF16)

    def values(t, p_scr, a_scr, blocks=all_blocks):
        v_all = jnp.concatenate([vt_scr[t, j] for j in range(tk // LANES)], axis=1)
        p = jnp.concatenate([p_scr[c] for c in blocks], axis=1)
        pv = jnp.dot(v_all, p, preferred_element_type=F32)
        for n, c in enumerate(blocks):
            cs = slice(c * ATTN_COLS, (c + 1) * ATTN_COLS)
            acc_scr[c] = a_scr[:, cs] * acc_scr[c] + pv[:, n * ATTN_COLS:(n + 1) * ATTN_COLS]

    scores(0, s0_scr)
    scores(1, s1_scr)
    softmax(s0_scr, p0_scr, a0_scr)

    def pair_step(g, carry):
        t = 2 * g + 1
        scores(t + 1, s0_scr)
        softmax(s1_scr, p1_scr, a1_scr)
        values(t - 1, p0_scr, a0_scr)
        scores(t + 2, s1_scr)
        softmax(s0_scr, p0_scr, a0_scr)
        values(t, p1_scr, a1_scr)
        return carry

    lax.fori_loop(0, i, pair_step, 0)
    t_last = span * i + 1
    softmax(s1_scr, p1_scr, a1_scr, late_blocks)
    values(t_last - 1, p0_scr, a0_scr)
    values(t_last, p1_scr, a1_scr, late_blocks)

    lp = lam_ref[...]
    lam = (jnp.exp(jnp.sum(lp[0:1] * lp[1:2], axis=-1, keepdims=True))
           - jnp.exp(jnp.sum(lp[2:3] * lp[3:4], axis=-1, keepdims=True)) + LAMBDA_INIT)
    for c in range(half):
        o0_t = acc_scr[c, 0:DIFF_V_DIM, :] / acc_scr[c, DIFF_V_DIM:DIFF_V_DIM + 1, :]
        o1_t = acc_scr[half + c, 0:DIFF_V_DIM, :] / acc_scr[half + c, DIFF_V_DIM:DIFF_V_DIM + 1, :]
        o = jnp.transpose(o0_t - lam * o1_t)
        ms = jnp.mean(o * o, axis=-1, keepdims=True)
        y = (o * lax.rsqrt(ms + EPS) * gain_ref[...]) * (1.0 - LAMBDA_INIT)
        o_ref[0, c * ATTN_COLS:(c + 1) * ATTN_COLS, :] = y.astype(o_ref.dtype)


def _diff_attention(lam_params, qkv, subln_gain):
    b, s, _ = qkv.shape
    tq = min(ATTN_TQ, s)
    tk = tq // ATTN_SPAN
    assert s % tq == 0 and ATTN_SPAN == 2 and tk % LANES == 0 and tk // CHUNK < LANES
    n_q = 2 * tq
    return pl.pallas_call(
        functools.partial(_attn_kernel, tq=tq, tk=tk),
        out_shape=jax.ShapeDtypeStruct((b, s, DIFF_WIDTH), BF16),
        grid=(b, DIFF_HEADS, s // tq),
        in_specs=[
            pl.BlockSpec((4, DIFF_HEAD_DIM), lambda bi, h, i: (0, 0)),
            pl.BlockSpec((1, tq, DIFF_V_DIM), lambda bi, h, i: (bi, i, h)),
            pl.BlockSpec((1, s, DIFF_V_DIM), lambda bi, h, i: (bi, 0, DIFF_HEADS + h)),
            pl.BlockSpec((1, s, DIFF_V_DIM), lambda bi, h, i: (bi, 0, 2 * DIFF_HEADS + h)),
            pl.BlockSpec((1, DIFF_V_DIM), lambda bi, h, i: (0, 0)),
        ],
        out_specs=pl.BlockSpec((1, tq, DIFF_V_DIM), lambda bi, h, i: (bi, i, h)),
        scratch_shapes=[
            pltpu.VMEM((n_q // ATTN_COLS, LANES, ATTN_COLS), BF16),
            pltpu.VMEM((ATTN_SPAN + 1, n_q // ATTN_COLS, LANES, ATTN_COLS), BF16),
            pltpu.VMEM((s // tk, tk // LANES, ATTN_V_ROWS, LANES), BF16),
            pltpu.VMEM((n_q // ATTN_COLS, tk, ATTN_COLS), F32),
            pltpu.VMEM((n_q // ATTN_COLS, tk, ATTN_COLS), F32),
            pltpu.VMEM((n_q // ATTN_COLS, tk, ATTN_COLS), BF16),
            pltpu.VMEM((n_q // ATTN_COLS, tk, ATTN_COLS), BF16),
            pltpu.VMEM((1, n_q), F32),
            pltpu.VMEM((1, n_q), F32),
            pltpu.VMEM((1, n_q), F32),
            pltpu.VMEM((n_q // ATTN_COLS, ATTN_V_ROWS, ATTN_COLS), F32),
        ],
        compiler_params=_params("parallel", "parallel", "arbitrary"),
        name="diff_attn",
    )(lam_params, qkv, qkv, qkv, subln_gain)


def _silu(x):
    return x * (1.0 / (1.0 + jnp.exp(-x)))


def _sigmoid(x):
    return 1.0 / (1.0 + jnp.exp(-x))


def _softplus(x):
    return jnp.maximum(x, 0.0) + jnp.log(1.0 + jnp.exp(-jnp.abs(x)))


def _split_bf16(x):
    hi = x.astype(BF16)
    return hi, (x - hi.astype(F32)).astype(BF16)


def _dot_bf16x3(a_parts, b_parts):
    (a_hi, a_lo), (b_hi, b_lo) = a_parts, b_parts
    d = lambda x, y: jnp.dot(x, y, preferred_element_type=F32)
    return d(a_hi, b_hi) + (d(a_lo, b_hi) + d(a_hi, b_lo))


def _dot_exact_lhs(lhs, x):
    x1 = x.astype(BF16)
    r1 = x - x1.astype(F32)
    x2 = r1.astype(BF16)
    x3 = (r1 - x2.astype(F32)).astype(BF16)
    d = lambda y: jnp.dot(lhs, y, preferred_element_type=F32)
    return d(x1) + (d(x2) + d(x3))


def _dot(a, b):
    return jnp.dot(a.astype(BF16), b.astype(BF16), preferred_element_type=F32)


def _dot_nt(a, b):
    return lax.dot_general(a.astype(BF16), b.astype(BF16), (((1,), (1,)), ((), ())), preferred_element_type=F32)


GDN_UNIT = 2 * CHUNK


def _gdn_kernel(prev_ref, qkv_ref, gate_ref, small_ref, convw_ref, alog_ref, dtb_ref, ngain_ref, o_ref,
                xp_scr, q_scr, k_scr, v_scr, pq_scr, n_scr, oc_scr, gl_scr, oraw_scr, state_scr, *, tb):
    i = pl.program_id(1)

    @pl.when(i == 0)
    def _():
        state_scr[...] = jnp.zeros(state_scr.shape, F32)

    n_sec = 3 * GDN_HEADS
    prev = jnp.where(i == 0, 0.0, prev_ref[0, BF16_SUBLANES - SUBLANES:BF16_SUBLANES, :].astype(F32))
    for sec in range(n_sec):
        xp_scr[sec, 0:SUBLANES, :] = prev[:, sec * LANES:(sec + 1) * LANES]
    for r0 in range(0, tb, GDN_UNIT):
        blk = qkv_ref[0, r0:r0 + GDN_UNIT, :].astype(F32)
        for sec in range(n_sec):
            xp_scr[sec, SUBLANES + r0:SUBLANES + r0 + GDN_UNIT, :] = blk[:, sec * LANES:(sec + 1) * LANES]
    for sec in range(n_sec):
        cols = slice(sec * LANES, (sec + 1) * LANES)
        acc = None
        for jj in range(CONV_K):
            start = SUBLANES - (CONV_K - 1) + jj
            term = xp_scr[sec, start:start + tb, :] * convw_ref[jj:jj + 1, cols]
            acc = term if acc is None else acc + term
        y = _silu(acc)
        which, head = divmod(sec, GDN_HEADS)
        if which == 0:
            q_scr[head] = y * lax.rsqrt(jnp.sum(y * y, axis=-1, keepdims=True) + EPS) * (GDN_DIM ** -0.5)
        elif which == 1:
            k_scr[head] = y * lax.rsqrt(jnp.sum(y * y, axis=-1, keepdims=True) + EPS)
        else:
            v_scr[head] = y

    sm = small_ref[0]
    beta_all = _sigmoid(sm)
    g_all = -jnp.exp(alog_ref[...]) * _softplus(sm + dtb_ref[...])

    br = lax.broadcasted_iota(jnp.int32, (tb, tb), 0)
    bc = lax.broadcasted_iota(jnp.int32, (tb, tb), 1)
    block_tril = jnp.where((br // CHUNK == bc // CHUNK) & (bc <= br), 1.0, 0.0).astype(BF16)
    gc_all = _dot_exact_lhs(block_tril, g_all)
    gct_all = jnp.transpose(gc_all)

    ur = lax.broadcasted_iota(jnp.int32, (GDN_UNIT, GDN_UNIT), 0)
    uc = lax.broadcasted_iota(jnp.int32, (GDN_UNIT, GDN_UNIT), 1)
    same = (ur // CHUNK) == (uc // CHUNK)
    causal = same & (uc <= ur)
    strict = same & (uc < ur)
    eye = jnp.where(ur == uc, 1.0, 0.0)
    first_rows = ur < CHUNK
    first_cols = uc < CHUNK
    n_units = tb // GDN_UNIT
    units = [(h, u) for h in range(GDN_HEADS) for u in range(n_units)]

    def lane_bcast(x, lane):
        return jnp.broadcast_to(x[:, lane:lane + 1], (GDN_UNIT, GDN_UNIT))

    def row_bcast(x, r):
        return jnp.broadcast_to(x[r:r + 1, :], (GDN_UNIT, GDN_UNIT))

    def rows_of(u):
        return slice(u * GDN_UNIT, (u + 1) * GDN_UNIT)

    def cols_of(h):
        return slice(h * LANES, (h + 1) * LANES)

    qs = [q_scr[h, rows_of(u), :] for h, u in units]
    ks = [k_scr[h, rows_of(u), :] for h, u in units]
    vs = [v_scr[h, rows_of(u), :] for h, u in units]
    betas = [lane_bcast(beta_all[rows_of(u)], h) for h, u in units]
    gcb = [lane_bcast(gc_all[rows_of(u)], GDN_HEADS + h) for h, u in units]
    gcr = [row_bcast(gct_all[:, rows_of(u)], GDN_HEADS + h) for h, u in units]
    decay = [jnp.exp(jnp.where(causal, a - b, NEG_INF)) for a, b in zip(gcb, gcr)]
    eg = [jnp.exp(a) for a in gcb]
    g_last = [jnp.where(first_rows, row_bcast(a, CHUNK - 1), row_bcast(a, GDN_UNIT - 1)) for a in gcb]
    kb = [k * b for k, b in zip(ks, betas)]
    vb = [v * b for v, b in zip(vs, betas)]
    a_low = [jnp.where(strict, _dot_nt(x, k) * d, 0.0) for x, k, d in zip(kb, ks, decay)]
    tinv = [eye - a for a in a_low]
    pw = a_low
    for _ in range(5):
        pw = [_dot(x, x) for x in pw]
        tinv = [t + _dot(t, x) for t, x in zip(tinv, pw)]
    uw = [_dot(t, jnp.concatenate([v, x * e], axis=1)) for t, v, x, e in zip(tinv, vb, kb, eg)]
    intra = [_dot_nt(q, k) * d for q, k, d in zip(qs, ks, decay)]
    iuw = [_dot(a, x) for a, x in zip(intra, uw)]
    k_dec_t = [jnp.transpose(k * jnp.exp(gl - a)) for k, gl, a in zip(ks, g_last, gcb)]
    wu = [jnp.concatenate([x[:, GDN_DIM:], x[:, :GDN_DIM]], axis=1) for x in uw]
    pn0 = [_dot(jnp.where(first_cols, kt, 0.0), x) for kt, x in zip(k_dec_t, wu)]
    pn1 = [_dot(jnp.where(first_cols, 0.0, kt), x) for kt, x in zip(k_dec_t, wu)]
    for n in range(len(units)):
        q_eff = (qs[n] * eg[n] - iuw[n][:, GDN_DIM:]).astype(BF16)
        for half, pn in enumerate((pn0[n], pn1[n])):
            hr = slice(half * CHUNK, (half + 1) * CHUNK)
            pq_scr[n, half, 0:GDN_DIM, :] = pn[:, :GDN_DIM].astype(BF16)
            pq_scr[n, half, GDN_DIM:GDN_DIM + CHUNK, :] = q_eff[hr]
            n_scr[n, half] = pn[:, GDN_DIM:]
            oc_scr[n, half] = iuw[n][hr, :GDN_DIM]
            gl_scr[n, half] = jnp.exp(gcb[n][(half + 1) * CHUNK - 1:(half + 1) * CHUNK, :])

    def chunk_body(c, carry):
        u = c // 2
        half = c % 2
        rows = pl.ds(pl.multiple_of(c * CHUNK, CHUNK), CHUNK)
        states = [state_scr[h] for h in range(GDN_HEADS)]
        res = [jnp.dot(pq_scr[h * n_units + u, half], states[h].astype(BF16), preferred_element_type=F32)
               for h in range(GDN_HEADS)]
        for h in range(GDN_HEADS):
            n = h * n_units + u
            state_scr[h] = gl_scr[n, half] * states[h] - res[h][0:GDN_DIM] + n_scr[n, half]
            oraw_scr[h, rows, :] = res[h][GDN_DIM:GDN_DIM + CHUNK] + oc_scr[n, half]
        return carry

    lax.fori_loop(0, tb // CHUNK, chunk_body, 0)

    for h in range(GDN_HEADS):
        o = oraw_scr[h]
        ms = jnp.mean(o * o, axis=-1, keepdims=True)
        y = o * lax.rsqrt(ms + EPS) * ngain_ref[...] * _silu(gate_ref[0, :, cols_of(h)].astype(F32))
        o_ref[0, :, cols_of(h)] = y.astype(o_ref.dtype)


def _gdn(proj, small, conv_w, alog_pad, dtb_pad, norm_gain, tb):
    b, s, _ = proj.shape
    nb = s // tb
    w3 = 3 * GDN_WIDTH
    assert tb % GDN_UNIT == 0
    n_hu = GDN_HEADS * (tb // GDN_UNIT)
    return pl.pallas_call(
        functools.partial(_gdn_kernel, tb=tb),
        out_shape=jax.ShapeDtypeStruct((b, s, GDN_WIDTH), BF16),
        grid=(b, nb),
        in_specs=[
            pl.BlockSpec((1, BF16_SUBLANES, w3),
                         lambda bi, i: (bi, jnp.maximum(i * (tb // BF16_SUBLANES) - 1, 0), PROJ_GDN // w3)),
            pl.BlockSpec((1, tb, w3), lambda bi, i: (bi, i, PROJ_GDN // w3)),
            pl.BlockSpec((1, tb, GDN_WIDTH), lambda bi, i: (bi, i, PROJ_GATE // GDN_WIDTH)),
            pl.BlockSpec((1, tb, LANES), lambda bi, i: (bi, i, 0)),
            pl.BlockSpec((CONV_K, w3), lambda bi, i: (0, 0)),
            pl.BlockSpec((1, LANES), lambda bi, i: (0, 0)),
            pl.BlockSpec((1, LANES), lambda bi, i: (0, 0)),
            pl.BlockSpec((1, GDN_DIM), lambda bi, i: (0, 0)),
        ],
        out_specs=pl.BlockSpec((1, tb, GDN_WIDTH), lambda bi, i: (bi, i, 0)),
        scratch_shapes=[
            pltpu.VMEM((w3 // LANES, tb + SUBLANES, LANES), F32),
            pltpu.VMEM((GDN_HEADS, tb, GDN_DIM), F32),
            pltpu.VMEM((GDN_HEADS, tb, GDN_DIM), F32),
            pltpu.VMEM((GDN_HEADS, tb, GDN_DIM), F32),
            pltpu.VMEM((n_hu, 2, GDN_DIM + CHUNK, GDN_DIM), BF16),
            pltpu.VMEM((n_hu, 2, GDN_DIM, GDN_DIM), F32),
            pltpu.VMEM((n_hu, 2, CHUNK, GDN_DIM), F32),
            pltpu.VMEM((n_hu, 2, 1, GDN_DIM), F32),
            pltpu.VMEM((GDN_HEADS, tb, GDN_DIM), F32),
            pltpu.VMEM((GDN_HEADS, GDN_DIM, GDN_DIM), F32),
        ],
        compiler_params=_params("parallel", "arbitrary"),
        name="gdn",
    )(proj, proj, proj, small, conv_w, alog_pad, dtb_pad, norm_gain)


MERGE_SUB = 256

ROW_PIECES = D_MODEL // LANES


def _store_tile_rows(ref, r0, value):
    n = value.shape[0]
    for j in range(ROW_PIECES):
        ref[pl.ds(r0 * ROW_PIECES + j, n, stride=ROW_PIECES), :] = value[:, j * LANES:(j + 1) * LANES]


def _load_tile_rows(ref, r0, n):
    return jnp.concatenate(
        [ref[pl.ds(r0 * ROW_PIECES + j, n, stride=ROW_PIECES), :] for j in range(ROW_PIECES)], axis=1)


def _merge_kernel(x_ref, ya_ref, yb_ref, ma_ref, mb_ref, wa_ref, wb_ref, wo_ref, g2_ref, wr_ref, br_ref,
                  x1_ref, h2_ref, route_ref, cnt_ref):
    tm = x_ref.shape[0]
    subs = [slice(r, r + MERGE_SUB) for r in range(0, tm, MERGE_SUB)]
    pa = [jnp.dot(ya_ref[rs, :], wa_ref[...], preferred_element_type=F32) for rs in subs]
    pb = [jnp.dot(yb_ref[rs, :], wb_ref[...], preferred_element_type=F32) for rs in subs]
    merged = [(_sigmoid(ma_ref[rs, :].astype(F32)) * a + _sigmoid(mb_ref[rs, :].astype(F32)) * b).astype(BF16)
              for rs, a, b in zip(subs, pa, pb)]
    x1s = [x_ref[rs, :] + jnp.dot(m, wo_ref[...], preferred_element_type=F32) for rs, m in zip(subs, merged)]
    h2s = [v * lax.rsqrt(jnp.mean(v * v, axis=-1, keepdims=True) + EPS) * g2_ref[...] for v in x1s]
    w_parts = _split_bf16(wr_ref[...])
    logit_s = [_dot_bf16x3(_split_bf16(v), w_parts) for v in h2s]
    for rs, v, hh in zip(subs, x1s, h2s):
        x1_ref[rs, :] = v
        _store_tile_rows(h2_ref, rs.start, hh)
    logits = jnp.concatenate(logit_s, axis=0) + br_ref[...]
    lane = lax.broadcasted_iota(jnp.int32, logits.shape, 1)
    big = jnp.int32(4 * LANES)
    gl = jnp.where(lane < MOE_GROUPS, logits, NEG_INF)
    gmax = jnp.max(gl, axis=-1, keepdims=True)
    gidx = jnp.min(jnp.where(gl == gmax, lane, big), axis=-1, keepdims=True)
    grp_w = 1.0 / jnp.sum(jnp.exp(gl - gmax), axis=-1, keepdims=True)
    lo = MOE_GROUPS + gidx * MOE_EXPERTS_PER_GROUP
    el = jnp.where((lane >= lo) & (lane < lo + MOE_EXPERTS_PER_GROUP), logits, NEG_INF)
    v1 = jnp.max(el, axis=-1, keepdims=True)
    i1 = jnp.min(jnp.where(el == v1, lane, big), axis=-1, keepdims=True)
    el2 = jnp.where(lane == i1, NEG_INF, el)
    v2 = jnp.max(el2, axis=-1, keepdims=True)
    i2 = jnp.min(jnp.where(el2 == v2, lane, big), axis=-1, keepdims=True)
    e2 = jnp.exp(v2 - v1)
    w1 = grp_w / (1.0 + e2)
    w2 = w1 * e2
    id1 = i1 - MOE_GROUPS
    id2 = i2 - MOE_GROUPS
    route_ref[...] = jnp.where(lane == 0, id1.astype(F32), jnp.where(lane == 1, id2.astype(F32),
                               jnp.where(lane == 2, w1, jnp.where(lane == 3, w2, 0.0))))

    @pl.when(pl.program_id(0) == 0)
    def _():
        cnt_ref[...] = jnp.zeros(cnt_ref.shape, F32)

    hits = jnp.where((lane == id1) | (lane == id2), 1.0, 0.0)
    cnt_ref[...] += jnp.broadcast_to(jnp.sum(hits, axis=0, keepdims=True), cnt_ref.shape)


def _merge(x2, ya, yb, proj, wa, wb, wo, g2, wr, br, tm):
    t, d = x2.shape
    row = lambda i: (i, 0)
    const = lambda i: (0, 0)
    return pl.pallas_call(
        _merge_kernel,
        out_shape=(
            jax.ShapeDtypeStruct((t, d), F32),
            jax.ShapeDtypeStruct((t * ROW_PIECES, LANES), F32),
            jax.ShapeDtypeStruct((t, LANES), F32),
            jax.ShapeDtypeStruct((SUBLANES, LANES), F32),
        ),
        grid=(t // tm,),
        in_specs=[
            pl.BlockSpec((tm, d), row),
            pl.BlockSpec((tm, DIFF_WIDTH), row),
            pl.BlockSpec((tm, GDN_WIDTH), row),
            pl.BlockSpec((tm, d), lambda i: (i, PROJ_MA // D_MODEL)),
            pl.BlockSpec((tm, d), lambda i: (i, PROJ_MB // D_MODEL)),
            pl.BlockSpec((DIFF_WIDTH, d), const),
            pl.BlockSpec((GDN_WIDTH, d), const),
            pl.BlockSpec((d, d), const),
            pl.BlockSpec((1, d), const),
            pl.BlockSpec((d, LANES), const),
            pl.BlockSpec((1, LANES), const),
        ],
        out_specs=(pl.BlockSpec((tm, d), row), pl.BlockSpec((tm * ROW_PIECES, LANES), row),
                   pl.BlockSpec((tm, LANES), row), pl.BlockSpec((SUBLANES, LANES), const)),
        compiler_params=_params("arbitrary"),
        name="merge_router",
    )(x2, ya, yb, proj, proj, wa, wb, wo, g2, wr, br)


MOE_BLK = 1024
MOE_SUB = 256
MOE_META_LANES = 256
MOE_ROUTE_TILE = 512
MOE_SCATTER_TILE = 1024
MOE_GATHER_TILE = 512


def _route_kernel(route_ref, cnt_ref, pos_ref, meta_ref, run_scr, *, n_rows):
    i = pl.program_id(0)
    tp = route_ref.shape[0]

    @pl.when(i == 0)
    def _():
        cnt = cnt_ref[...]
        jr = lax.broadcasted_iota(jnp.int32, (LANES, LANES), 0)
        jc = lax.broadcasted_iota(jnp.int32, (LANES, LANES), 1)
        upper = jnp.where(jr < jc, 1.0, 0.0)
        hi_prec = dict(preferred_element_type=F32, precision=lax.Precision.HIGHEST)
        off = jnp.dot(cnt, upper, **hi_prec)
        run_scr[...] = off[0:1]
        blk = float(MOE_BLK)
        first_tile = jnp.floor(off / blk)
        last_tile = jnp.floor((off + cnt - 1.0) / blk)
        n_it = jnp.where(cnt > 0.0, last_tile - first_tile + 1.0, 0.0)
        it_start = jnp.dot(n_it, upper, **hi_prec)
        it_end = it_start + n_it
        lane8 = lax.broadcasted_iota(jnp.int32, cnt.shape, 1)
        e_max = jnp.max(jnp.where(cnt > 0.0, lane8, 0), axis=-1, keepdims=True).astype(F32)[0:1]
        sub8 = lax.broadcasted_iota(jnp.int32, cnt.shape, 0)
        table = jnp.where(sub8 == 0, first_tile, jnp.where(sub8 == 1, it_start, jnp.where(
            sub8 == 2, off, jnp.where(sub8 == 3, cnt, it_end))))
        cols = jnp.transpose(table)
        shape = (LANES, MOE_META_LANES)
        e_sub = lax.broadcasted_iota(jnp.int32, shape, 0)
        w_lane = lax.broadcasted_iota(jnp.int32, shape, 1).astype(F32)
        col = lambda k: jnp.broadcast_to(cols[:, k:k + 1], shape)
        e_w = jnp.sum(jnp.where((e_sub < MOE_EXPERTS) & (col(4) <= w_lane), 1.0, 0.0), axis=0, keepdims=True)
        valid = e_w < float(MOE_EXPERTS)
        e_w = jnp.minimum(e_w, e_max)
        sel = e_sub.astype(F32) == e_w
        pick = lambda k: jnp.sum(jnp.where(sel, col(k), 0.0), axis=0, keepdims=True)
        w_row = w_lane[0:1]
        tile_w = jnp.where(valid, pick(0) + (w_row - pick(1)), float(n_rows // MOE_BLK - 1))
        lo_w = jnp.maximum(pick(2) - tile_w * blk, 0.0)
        hi_w = jnp.minimum(pick(2) + pick(3) - tile_w * blk, blk)
        lo_w = jnp.where(valid, lo_w, 0.0)
        hi_w = jnp.where(valid, hi_w, 0.0)
        sub_m = lax.broadcasted_iota(jnp.int32, meta_ref.shape, 0)
        bc = lambda v: jnp.broadcast_to(v, meta_ref.shape)
        meta_ref[...] = jnp.where(sub_m == 0, bc(e_w), jnp.where(sub_m == 1, bc(tile_w), jnp.where(
            sub_m == 2, bc(lo_w), bc(hi_w)))).astype(jnp.int32)

    r = route_ref[...]
    lane = lax.broadcasted_iota(jnp.int32, r.shape, 1)
    lane_f = lane.astype(F32)
    oh1 = lane_f == r[:, 0:1]
    oh2 = lane_f == r[:, 1:2]
    hits = jnp.where(oh1 | oh2, 1.0, 0.0)
    tr = lax.broadcasted_iota(jnp.int32, (tp, tp), 0)
    tc = lax.broadcasted_iota(jnp.int32, (tp, tp), 1)
    earlier = jnp.where(tc < tr, 1.0, 0.0).astype(BF16)
    rank = jnp.dot(earlier, hits.astype(BF16), preferred_element_type=F32)
    base = run_scr[...] + rank
    p1 = jnp.sum(jnp.where(oh1, base, 0.0), axis=-1, keepdims=True)
    p2 = jnp.sum(jnp.where(oh2, base, 0.0), axis=-1, keepdims=True)
    pos_ref[...] = jnp.where(lane == 0, p1, jnp.where(lane == 1, p2, 0.0)).astype(jnp.int32)
    run_scr[...] += jnp.sum(hits, axis=0, keepdims=True)


def _route_positions(route, cnt):
    t = route.shape[0]
    tp = min(MOE_ROUTE_TILE, t)
    n_rows = 2 * t
    assert n_rows % MOE_BLK == 0 and n_rows // MOE_BLK + MOE_EXPERTS <= MOE_META_LANES and n_rows < 2 ** 24
    return pl.pallas_call(
        functools.partial(_route_kernel, n_rows=n_rows),
        out_shape=(jax.ShapeDtypeStruct((t, LANES), jnp.int32),
                   jax.ShapeDtypeStruct((SUBLANES, MOE_META_LANES), jnp.int32)),
        grid=(t // tp,),
        in_specs=[pl.BlockSpec((tp, LANES), lambda i: (i, 0)),
                  pl.BlockSpec((SUBLANES, LANES), lambda i: (0, 0))],
        out_specs=(pl.BlockSpec((tp, LANES), lambda i: (i, 0)),
                   pl.BlockSpec((SUBLANES, MOE_META_LANES), lambda i: (0, 0))),
        scratch_shapes=[pltpu.VMEM((1, LANES), F32)],
        compiler_params=_params("arbitrary"),
        name="route_positions",
    )(route, cnt)


def _scatter_kernel(pos_ref, h2_ref, xs_ref, sem):
    ts = h2_ref.shape[0] // ROW_PIECES

    def body(tok, carry):
        src = h2_ref.at[pl.ds(pl.multiple_of(tok * ROW_PIECES, ROW_PIECES), ROW_PIECES), :]
        for k in range(2):
            dst = xs_ref.at[pl.ds(pl.multiple_of(pos_ref[0, 0, 2 * tok + k], ROW_PIECES), ROW_PIECES), :]
            pltpu.make_async_copy(src, dst, sem).start(priority=k)
        return carry

    lax.fori_loop(0, ts, body, 0, unroll=8)
    for _ in range(2):
        pltpu.make_async_copy(h2_ref, xs_ref.at[pl.ds(0, ts * ROW_PIECES), :], sem).wait()


def _scatter_rows(pos3, h2t):
    t = h2t.shape[0] // ROW_PIECES
    ts = pos3.shape[2] // 2
    return pl.pallas_call(
        _scatter_kernel,
        out_shape=jax.ShapeDtypeStruct((2 * t * ROW_PIECES, LANES), F32),
        grid=(t // ts,),
        in_specs=[pl.BlockSpec((1, 1, 2 * ts), lambda i: (i, 0, 0), memory_space=pltpu.SMEM),
                  pl.BlockSpec((ts * ROW_PIECES, LANES), lambda i: (i, 0))],
        out_specs=pl.BlockSpec(memory_space=pl.ANY),
        scratch_shapes=[pltpu.SemaphoreType.DMA],
        compiler_params=pltpu.CompilerParams(dimension_semantics=("arbitrary",), vmem_limit_bytes=VMEM_LIMIT,
                                             disable_bounds_checks=True),
        name="scatter_rows",
    )(pos3, h2t)


def _expert_kernel(ie_ref, it_ref, lo_ref, hi_ref, xs_ref, wg_ref, wu_ref, wd_ref, y_ref,
                   wg_scr, wu_scr, wd_scr, acc_scr):
    w = pl.program_id(0)
    n_items = pl.num_programs(0)
    prev = jnp.maximum(w - 1, 0)
    nxt = jnp.minimum(w + 1, n_items - 1)

    @pl.when((w == 0) | (ie_ref[w] != ie_ref[prev]))
    def _():
        wg_scr[...] = wg_ref[0].astype(BF16)
        wu_scr[...] = wu_ref[0].astype(BF16)
        wd_scr[...] = wd_ref[0].astype(BF16)

    first = (w == 0) | (it_ref[w] != it_ref[prev])
    last = (w == n_items - 1) | (it_ref[nxt] != it_ref[w])
    only = first & last

    @pl.when(first & jnp.logical_not(only))
    def _():
        acc_scr[...] = jnp.zeros(acc_scr.shape, F32)

    lo = lo_ref[w]
    hi = hi_ref[w]
    subs = [slice(r, r + MOE_SUB) for r in range(0, acc_scr.shape[0], MOE_SUB)]

    @pl.when(hi > lo)
    def _():
        xs = [_load_tile_rows(xs_ref, rs.start, MOE_SUB).astype(BF16) for rs in subs]
        hg = [jnp.dot(x, wg_scr[...], preferred_element_type=F32) for x in xs]
        hu = [jnp.dot(x, wu_scr[...], preferred_element_type=F32) for x in xs]
        act = [(_silu(g) * u).astype(BF16) for g, u in zip(hg, hu)]
        yp = [jnp.dot(a, wd_scr[...], preferred_element_type=F32) for a in act]

        @pl.when(only)
        def _():
            for rs, v in zip(subs, yp):
                _store_tile_rows(y_ref, rs.start, v)

        @pl.when(jnp.logical_not(only))
        def _():
            for rs, v in zip(subs, yp):
                row = rs.start + lax.broadcasted_iota(jnp.int32, v.shape, 0)
                acc_scr[rs, :] += jnp.where((row >= lo) & (row < hi), v, 0.0)

    @pl.when(last & jnp.logical_not(only))
    def _():
        for rs in subs:
            _store_tile_rows(y_ref, rs.start, acc_scr[rs, :])


def _experts(meta, xs, wg, wu, wd):
    n_rows = xs.shape[0] // ROW_PIECES
    d = wg.shape[1]
    n_items = n_rows // MOE_BLK + MOE_EXPERTS
    ie, it, lo, hi = (meta[k, :n_items] for k in range(4))
    return pl.pallas_call(
        _expert_kernel,
        out_shape=jax.ShapeDtypeStruct((n_rows * ROW_PIECES, LANES), F32),
        grid_spec=pltpu.PrefetchScalarGridSpec(
            num_scalar_prefetch=4,
            grid=(n_items,),
            in_specs=[
                pl.BlockSpec((MOE_BLK * ROW_PIECES, LANES), lambda w, ie, it, lo, hi: (it[w], 0)),
                pl.BlockSpec((1, d, MOE_HIDDEN), lambda w, ie, it, lo, hi: (ie[w], 0, 0)),
                pl.BlockSpec((1, d, MOE_HIDDEN), lambda w, ie, it, lo, hi: (ie[w], 0, 0)),
                pl.BlockSpec((1, MOE_HIDDEN, d), lambda w, ie, it, lo, hi: (ie[w], 0, 0)),
            ],
            out_specs=pl.BlockSpec((MOE_BLK * ROW_PIECES, LANES), lambda w, ie, it, lo, hi: (it[w], 0)),
            scratch_shapes=[pltpu.VMEM((d, MOE_HIDDEN), BF16), pltpu.VMEM((d, MOE_HIDDEN), BF16),
                            pltpu.VMEM((MOE_HIDDEN, d), BF16), pltpu.VMEM((MOE_BLK, d), F32)],
        ),
        compiler_params=_params("arbitrary"),
        name="experts",
    )(ie, it, lo, hi, xs, wg, wu, wd)


def _combine_kernel(pos_ref, posn_ref, x1_ref, route_ref, gf_ref, y_ref, o_ref, ybuf, sem):
    i = pl.program_id(0)
    n = pl.num_programs(0)
    tc = x1_ref.shape[0]
    slot = i % 2

    def issue(p_ref, s):
        def body(tok, carry):
            for k in range(2):
                src = y_ref.at[pl.ds(pl.multiple_of(p_ref[0, 0, 2 * tok + k], ROW_PIECES), ROW_PIECES), :]
                dst = ybuf.at[s, k, pl.ds(pl.multiple_of(tok * ROW_PIECES, ROW_PIECES), ROW_PIECES), :]
                pltpu.make_async_copy(src, dst, sem.at[s]).start(priority=k)
            return carry

        lax.fori_loop(0, tc, body, 0, unroll=8)

    @pl.when(i == 0)
    def _():
        issue(pos_ref, 0)

    @pl.when(i + 1 < n)
    def _():
        issue(posn_ref, 1 - slot)

    for k in range(2):
        pltpu.make_async_copy(y_ref.at[pl.ds(0, tc * ROW_PIECES), :], ybuf.at[slot, k], sem.at[slot]).wait()
    r = route_ref[...]
    x2 = (x1_ref[...] + r[:, 2:3] * _load_tile_rows(ybuf.at[slot, 0], 0, tc)
          + r[:, 3:4] * _load_tile_rows(ybuf.at[slot, 1], 0, tc))
    ms = jnp.mean(x2 * x2, axis=-1, keepdims=True)
    o_ref[...] = x2 * lax.rsqrt(ms + EPS) * gf_ref[...]


def _combine(pos3, x1, route, gf, y):
    t, d = x1.shape
    tc = pos3.shape[2] // 2
    n = t // tc
    return pl.pallas_call(
        _combine_kernel,
        out_shape=jax.ShapeDtypeStruct((t, d), F32),
        grid=(n,),
        in_specs=[
            pl.BlockSpec((1, 1, 2 * tc), lambda i: (i, 0, 0), memory_space=pltpu.SMEM),
            pl.BlockSpec((1, 1, 2 * tc), lambda i: (jnp.minimum(i + 1, n - 1), 0, 0), memory_space=pltpu.SMEM),
            pl.BlockSpec((tc, d), lambda i: (i, 0)),
            pl.BlockSpec((tc, LANES), lambda i: (i, 0)),
            pl.BlockSpec((1, d), lambda i: (0, 0)),
            pl.BlockSpec(memory_space=pl.ANY),
        ],
        out_specs=pl.BlockSpec((tc, d), lambda i: (i, 0)),
        scratch_shapes=[pltpu.VMEM((2, 2, tc * ROW_PIECES, LANES), F32), pltpu.SemaphoreType.DMA((2,))],
        compiler_params=pltpu.CompilerParams(dimension_semantics=("arbitrary",), vmem_limit_bytes=VMEM_LIMIT,
                                             disable_bounds_checks=True),
        name="combine_norm",
    )(pos3, pos3, x1, route, gf, y)


def _pad_lanes(v, offset):
    return jnp.zeros((1, LANES), F32).at[0, offset:offset + v.shape[0]].set(v.astype(F32))


def kernel(x, norm_mix_gain, w_in, diff_lambda_q1, diff_lambda_k1, diff_lambda_q2, diff_lambda_k2, diff_subln_gain, gdn_conv_w, gdn_a_log, gdn_dt_bias, gdn_norm_gain, w_branch_attn, w_branch_gdn, w_out, norm_ffn_gain, moe_w_group, moe_b_group, moe_w_expert, moe_b_expert, moe_w_gate, moe_w_up, moe_w_down, norm_final_gain):
    b, s, d = x.shape
    t = b * s
    x2 = x.reshape(t, d)

    w = w_in[0]
    small_lo = PROJ_GDN + 3 * GDN_WIDTH
    small_hi = small_lo + 2 * GDN_HEADS
    gate_hi = small_hi + GDN_WIDTH
    w_main = jnp.concatenate([w[:, :small_lo], w[:, gate_hi:], w[:, small_hi:gate_hi]], axis=1).astype(BF16)
    w_small = jnp.concatenate([w[:, small_lo:small_hi], jnp.zeros((d, LANES - 2 * GDN_HEADS), w.dtype)],
                              axis=1).astype(BF16)
    gain1 = norm_mix_gain[0].reshape(1, d)

    proj2, small2 = _inproj(x2, gain1, w_main, w_small, min(1024, t), PROJ_WIDTH // 2)
    proj = proj2.reshape(b, s, PROJ_WIDTH)

    lam_params = jnp.stack([diff_lambda_q1[0], diff_lambda_k1[0], diff_lambda_q2[0], diff_lambda_k2[0]]).astype(F32)
    ya = _diff_attention(lam_params, proj, diff_subln_gain[0].reshape(1, DIFF_V_DIM))

    yb = _gdn(proj, small2.reshape(b, s, LANES), gdn_conv_w[0], _pad_lanes(gdn_a_log[0], GDN_HEADS),
              _pad_lanes(gdn_dt_bias[0], GDN_HEADS), gdn_norm_gain[0].reshape(1, GDN_DIM), tb=min(512, s))

    wr = jnp.concatenate([moe_w_group[0], moe_w_expert[0],
                          jnp.zeros((d, LANES - MOE_GROUPS - MOE_EXPERTS), F32)], axis=1)
    br = _pad_lanes(jnp.concatenate([moe_b_group[0], moe_b_expert[0]]), 0)
    x1, h2, route, cnt = _merge(x2, ya.reshape(t, DIFF_WIDTH), yb.reshape(t, GDN_WIDTH), proj2,
                                w_branch_attn[0].astype(BF16), w_branch_gdn[0].astype(BF16),
                                w_out[0].astype(BF16), norm_ffn_gain[0].reshape(1, d), wr, br, tm=min(512, t))

    pos, meta = _route_positions(route, cnt)
    lines = pos[:, :2] * ROW_PIECES
    ts, tc = min(MOE_SCATTER_TILE, t), min(MOE_GATHER_TILE, t)
    xs = _scatter_rows(lines.reshape(t // ts, 1, 2 * ts), h2)
    y = _experts(meta, xs, moe_w_gate[0], moe_w_up[0], moe_w_down[0])
    out = _combine(lines.reshape(t // tc, 1, 2 * tc), x1, route, norm_final_gain.reshape(1, d), y)
    return out.reshape(b, s, d)
```
